```python
import math
import jax, jax.numpy as jnp
from jax import lax
import numpy as np

D_MODEL = 1024
BATCH = 4
SEQ = 8192
DEPTH = 1

CHUNK = 64
D_SSM = D_MODEL // 2
SSM_GROUP = 16
N_GROUPS = D_SSM // SSM_GROUP
STATE = 64
N_HEADS = 8
HEAD_DIM = 64
D_ATTN = N_HEADS * HEAD_DIM
D_IN = D_SSM + 3 * D_ATTN
D_FF = 4 * D_MODEL
Q_BLOCK = 128
EPS = 1e-6
DT_MIN = 1e-3
DT_MAX = 1e-1

kernel_name = "hybrid_s5_stickbreaking_gated_block"


def rmsnorm(x, g):
    xf = x.astype(jnp.float32)
    y = xf * lax.rsqrt(jnp.mean(xf * xf, axis=-1, keepdims=True) + EPS)
    return (y * g.astype(jnp.float32)).astype(x.dtype)


def s5_branch(u, A_re, A_im, log_dt, B_re, B_im, C_re, C_im, D_skip, w_glu, b_glu):
    f32 = jnp.float32
    bsz, seq, _ = u.shape
    uf = u.astype(f32)
    ug = uf.reshape(bsz, seq, N_GROUPS, SSM_GROUP)
    lam = lax.complex(A_re.astype(f32), A_im.astype(f32))
    dt = jnp.exp(log_dt.astype(f32))[:, None]
    a_bar = jnp.exp(lam * dt)
    b_bar = ((a_bar - 1.0) / lam)[..., None] * lax.complex(B_re.astype(f32), B_im.astype(f32))
    bu = lax.complex(jnp.einsum('bsgc,gpc->bsgp', ug, b_bar.real),
                     jnp.einsum('bsgc,gpc->bsgp', ug, b_bar.imag))
    a = jnp.broadcast_to(a_bar, (1, seq, N_GROUPS, STATE))

    def combine(left, right):
        a_l, b_l = left
        a_r, b_r = right
        return a_r * a_l, a_r * b_l + b_r

    _, states = lax.associative_scan(combine, (a, bu), axis=1)
    y = (jnp.einsum('bsgp,gcp->bsgc', states.real, C_re.astype(f32))
         - jnp.einsum('bsgp,gcp->bsgc', states.imag, C_im.astype(f32)))
    y = y.reshape(bsz, seq, D_SSM) + D_skip.astype(f32) * uf
    y = jax.nn.gelu(y)
    y = y * jax.nn.sigmoid(y @ w_glu.astype(f32) + b_glu.astype(f32))
    return y.astype(u.dtype)


def stick_breaking_attention(q, k, v):
    f32 = jnp.float32
    bsz, seq, nh, dh = q.shape
    nb = seq // Q_BLOCK
    qb = q.astype(f32).reshape(bsz, nb, Q_BLOCK, nh, dh).transpose(1, 0, 2, 3, 4)
    kf = k.astype(f32)
    vf = v.astype(f32)
    key_pos = jnp.arange(seq)
    scale = HEAD_DIM ** -0.5

    def one_block(args):
        blk, q_blk = args
        q_pos = blk * Q_BLOCK + jnp.arange(Q_BLOCK)
        z = jnp.einsum('bqhd,bkhd->bhqk', q_blk, kf) * scale
        mask = key_pos[None, :] < q_pos[:, None]
        log_fail = jnp.where(mask, jax.nn.log_sigmoid(-z), 0.0)
        after = lax.cumsum(log_fail, axis=3, reverse=True) - log_fail
        log_w = jax.nn.log_sigmoid(z) + after
        w = jnp.where(mask, jnp.exp(log_w), 0.0)
        return jnp.einsum('bhqk,bkhd->bqhd', w, vf)

    out = lax.map(one_block, (jnp.arange(nb), qb))
    return out.transpose(1, 0, 2, 3, 4).reshape(bsz, seq, nh * dh).astype(q.dtype)


def setup_inputs(seed: int = 0) -> dict:
    key = jax.random.key(seed)
    ks = jax.random.split(key, 22)
    f32 = jnp.float32

    def nrm(k, shape, scale):
        return jax.random.normal(k, shape, f32) * scale

    x = nrm(ks[0], (BATCH, SEQ, D_MODEL), 1.0)
    norm_mix = 1.0 + nrm(ks[1], (DEPTH, D_MODEL), 0.02)
    w_in = nrm(ks[2], (DEPTH, D_MODEL, D_IN), D_MODEL ** -0.5)
    A_re = -0.5 + nrm(ks[3], (DEPTH, N_GROUPS, STATE), 0.01)
    A_im = jnp.pi * jnp.arange(STATE, dtype=f32) + nrm(ks[4], (DEPTH, N_GROUPS, STATE), 0.01)
    log_dt = jax.random.uniform(ks[5], (DEPTH, N_GROUPS), f32, math.log(DT_MIN), math.log(DT_MAX))
    B_re = nrm(ks[6], (DEPTH, N_GROUPS, STATE, SSM_GROUP), (2 * SSM_GROUP) ** -0.5)
    B_im = nrm(ks[7], (DEPTH, N_GROUPS, STATE, SSM_GROUP), (2 * SSM_GROUP) ** -0.5)
    C_re = nrm(ks[8], (DEPTH, N_GROUPS, SSM_GROUP, STATE), STATE ** -0.5)
    C_im = nrm(ks[9], (DEPTH, N_GROUPS, SSM_GROUP, STATE), STATE ** -0.5)
    D_skip = nrm(ks[10], (DEPTH, D_SSM), 1.0)
    w_glu = nrm(ks[11], (DEPTH, D_SSM, D_SSM), D_SSM ** -0.5)
    b_glu = nrm(ks[12], (DEPTH, D_SSM), 0.02)
    w_up_ssm = nrm(ks[13], (DEPTH, D_SSM, D_MODEL), D_SSM ** -0.5)
    w_up_attn = nrm(ks[14], (DEPTH, D_ATTN, D_MODEL), D_ATTN ** -0.5)
    w_gate = nrm(ks[15], (DEPTH, D_MODEL, 2 * D_MODEL), D_MODEL ** -0.5)
    b_gate = nrm(ks[16], (DEPTH, 2 * D_MODEL), 0.02)
    w_out = nrm(ks[17], (DEPTH, D_MODEL, D_MODEL), D_MODEL ** -0.5)
    norm_mlp = 1.0 + nrm(ks[18], (DEPTH, D_MODEL), 0.02)
    w_ff1 = nrm(ks[19], (DEPTH, D_MODEL, D_FF), D_MODEL ** -0.5)
    w_ff2 = nrm(ks[20], (DEPTH, D_FF, D_MODEL), D_FF ** -0.5)
    norm_final = 1.0 + nrm(ks[21], (D_MODEL,), 0.02)
    return {"x": x, "norm_mix": norm_mix, "w_in": w_in, "A_re": A_re, "A_im": A_im,
            "log_dt": log_dt, "B_re": B_re, "B_im": B_im, "C_re": C_re, "C_im": C_im,
            "D_skip": D_skip, "w_glu": w_glu, "b_glu": b_glu, "w_up_ssm": w_up_ssm,
            "w_up_attn": w_up_attn, "w_gate": w_gate, "b_gate": b_gate, "w_out": w_out,
            "norm_mlp": norm_mlp, "w_ff1": w_ff1, "w_ff2": w_ff2, "norm_final": norm_final}


def reference(x, norm_mix, w_in, A_re, A_im, log_dt, B_re, B_im, C_re, C_im, D_skip, w_glu, b_glu,
              w_up_ssm, w_up_attn, w_gate, b_gate, w_out, norm_mlp, w_ff1, w_ff2, norm_final):
    h = x
    bsz, seq, _ = x.shape
    for l in range(DEPTH):
        u = rmsnorm(h, norm_mix[l])
        proj = u @ w_in[l]
        u_ssm, q, k, v = jnp.split(proj, [D_SSM, D_SSM + D_ATTN, D_SSM + 2 * D_ATTN], axis=-1)
        y_ssm = s5_branch(u_ssm, A_re[l], A_im[l], log_dt[l], B_re[l], B_im[l], C_re[l], C_im[l],
                          D_skip[l], w_glu[l], b_glu[l])
        y_attn = stick_breaking_attention(q.reshape(bsz, seq, N_HEADS, HEAD_DIM),
                                          k.reshape(bsz, seq, N_HEADS, HEAD_DIM),
                                          v.reshape(bsz, seq, N_HEADS, HEAD_DIM))
        gates = jax.nn.sigmoid((u @ w_gate[l] + b_gate[l]).astype(jnp.float32)).astype(u.dtype)
        g_ssm, g_attn = jnp.split(gates, 2, axis=-1)
        merged = g_ssm * (y_ssm @ w_up_ssm[l]) + g_attn * (y_attn @ w_up_attn[l])
        h = h + merged @ w_out[l]
        hid = jax.nn.relu(rmsnorm(h, norm_mlp[l]) @ w_ff1[l])
        h = h + (hid * hid) @ w_ff2[l]
    return rmsnorm(h, norm_final)
```

```python
import functools

import jax
import jax.numpy as jnp
from jax import lax
from jax.experimental import pallas as pl
from jax.experimental.pallas import tpu as pltpu

D_MODEL = 1024
D_SSM = 512
SSM_GROUP = 16
N_GROUPS = 32
STATE = 64
N_HEADS = 8
HEAD_DIM = 64
D_ATTN = 512
D_FF = 4096
EPS = 1e-6

LANES = 128
VMEM_LIMIT = 52 * 1024 * 1024

SSM_CHUNK = 16
N_LANE_BLOCKS = D_SSM // LANES
GROUPS_PER_BLOCK = LANES // SSM_GROUP
STATE_COLS = 2 * GROUPS_PER_BLOCK * STATE
CHUNK_COLS = SSM_CHUNK * LANES
MXU_TILE = 256
ATT_BLOCK = 128
ATT_SKIP_LOG = -64.0
PROJ_ROWS = 512
MERGE_ROWS = 256
FF_CHUNK = 1024


def _rmsnorm_f32(x, g):
    ms = jnp.mean(x * x, axis=-1, keepdims=True)
    return x * lax.rsqrt(ms + EPS) * g


def _inproj_kernel(x_ref, g_ref, w_ref, wkt_ref, bg_ref,
                   ussm_ref, q_ref, kt_ref, v_ref, gates_ref):
    x = x_ref[...]
    u = _rmsnorm_f32(x, g_ref[...]).astype(jnp.bfloat16)
    rows = x.shape[0]

    p_ssm = jnp.dot(u, w_ref[:, 0:D_SSM], preferred_element_type=jnp.float32)
    for blk in range(N_LANE_BLOCKS):
        ussm_ref[blk] = p_ssm[:, blk * LANES:(blk + 1) * LANES].astype(jnp.bfloat16)

    p_q = jnp.dot(u, w_ref[:, D_SSM:D_SSM + D_ATTN], preferred_element_type=jnp.float32)
    q_ref[...] = (p_q * (HEAD_DIM ** -0.5)).astype(jnp.bfloat16)

    kt = lax.dot_general(wkt_ref[...], u, (((1,), (1,)), ((), ())),
                         preferred_element_type=jnp.float32).astype(jnp.bfloat16)
    for c in range(rows // ATT_BLOCK):
        kt_ref[0, c] = kt[:, c * ATT_BLOCK:(c + 1) * ATT_BLOCK]

    p_v = jnp.dot(u, w_ref[:, D_SSM + 2 * D_ATTN:D_SSM + 3 * D_ATTN],
                  preferred_element_type=jnp.float32)
    v_ref[...] = p_v.astype(jnp.bfloat16)

    gate_off = D_SSM + 3 * D_ATTN
    for c in range(2 * D_MODEL // 512):
        pg = jnp.dot(u, w_ref[:, gate_off + c * 512:gate_off + (c + 1) * 512],
                     preferred_element_type=jnp.float32)
        pg = pg + bg_ref[:, c * 512:(c + 1) * 512]
        gates_ref[:, c * 512:(c + 1) * 512] = jax.nn.sigmoid(pg).astype(jnp.bfloat16)


def _inproj_call(x2, g, w_cat, wkt, bg, bsz, seq):
    tokens = bsz * seq
    rows = min(PROJ_ROWS, seq)
    tiles_per_seq = seq // rows
    n_kb = rows // ATT_BLOCK
    const = lambda i: (0, 0)
    return pl.pallas_call(
        _inproj_kernel,
        grid=(tokens // rows,),
        in_specs=[
            pl.BlockSpec((rows, D_MODEL), lambda i: (i, 0)),
            pl.BlockSpec((1, D_MODEL), const),
            pl.BlockSpec(w_cat.shape, const),
            pl.BlockSpec(wkt.shape, const),
            pl.BlockSpec((1, 2 * D_MODEL), const),
        ],
        out_specs=[
            pl.BlockSpec((N_LANE_BLOCKS, rows, LANES), lambda i: (0, i, 0)),
            pl.BlockSpec((rows, D_ATTN), lambda i: (i, 0)),
            pl.BlockSpec((1, n_kb, D_ATTN, ATT_BLOCK),
                         lambda i: (i // tiles_per_seq, i % tiles_per_seq, 0, 0)),
            pl.BlockSpec((rows, D_ATTN), lambda i: (i, 0)),
            pl.BlockSpec((rows, 2 * D_MODEL), lambda i: (i, 0)),
        ],
        out_shape=[
            jax.ShapeDtypeStruct((N_LANE_BLOCKS, tokens, LANES), jnp.bfloat16),
            jax.ShapeDtypeStruct((tokens, D_ATTN), jnp.bfloat16),
            jax.ShapeDtypeStruct((bsz, seq // ATT_BLOCK, D_ATTN, ATT_BLOCK), jnp.bfloat16),
            jax.ShapeDtypeStruct((tokens, D_ATTN), jnp.bfloat16),
            jax.ShapeDtypeStruct((tokens, 2 * D_MODEL), jnp.bfloat16),
        ],
        compiler_params=pltpu.CompilerParams(
            dimension_semantics=("arbitrary",), vmem_limit_bytes=VMEM_LIMIT),
        name="inproj",
    )(x2, g, w_cat, wkt, bg)


def _ssm_tables(A_re, A_im, log_dt, B_re, B_im, C_re, C_im, D_skip):
    f32 = jnp.float32
    L = SSM_CHUNK
    nb, gb = N_LANE_BLOCKS, GROUPS_PER_BLOCK
    lam = lax.complex(A_re.astype(f32), A_im.astype(f32))
    dt = jnp.exp(log_dt.astype(f32))[:, None]
    a_bar = jnp.exp(lam * dt)
    b_bar = ((a_bar - 1.0) / lam)[..., None] * lax.complex(B_re.astype(f32), B_im.astype(f32))
    c_cplx = lax.complex(C_re.astype(f32), C_im.astype(f32))
    tau = jnp.arange(L + 1, dtype=f32)[:, None, None]
    a_pow = jnp.exp(lam[None] * dt[None] * tau)
    eye = jnp.eye(gb, dtype=f32)

    kern = jnp.einsum('gdp,tgp,gpc->tgdc', c_cplx, a_pow[:L], b_bar).real
    kern = kern.reshape(L, nb, gb, SSM_GROUP, SSM_GROUP)
    toep = jnp.einsum('tqgdc,gh->tqgchd', kern, eye)
    toep = toep.reshape(L, nb, LANES, LANES).transpose(1, 0, 2, 3)

    inj = a_pow[:L][::-1][:, :, :, None] * b_bar[None]
    inj = inj.reshape(L, nb, gb, STATE, SSM_GROUP)
    inj_ri = jnp.stack([inj.real, inj.imag], axis=0)
    p_mat = jnp.einsum('rsqgpc,gh->qsgcrhp', inj_ri, eye)
    p_mat = p_mat.reshape(nb, CHUNK_COLS, STATE_COLS)

    ro = c_cplx[None] * a_pow[1:L + 1][:, :, None, :]
    ro = ro.reshape(L, nb, gb, SSM_GROUP, STATE)
    ro_ri = jnp.stack([ro.real, -ro.imag], axis=0)
    q_mat = jnp.einsum('rtqgdp,gh->qrgpthd', ro_ri, eye)
    q_mat = q_mat.reshape(nb, STATE_COLS, CHUNK_COLS)

    a_chunk = a_pow[L].reshape(nb, 1, gb * STATE)
    a_tab = jnp.concatenate([a_chunk.real, a_chunk.imag], axis=1)
    d_tab = jnp.tile(D_skip.astype(f32).reshape(nb, 1, LANES), (1, 1, L))
    bf = jnp.bfloat16
    return toep.astype(bf), p_mat.astype(bf), q_mat.astype(bf), a_tab, d_tab


def _ssm_kernel(x_ref, toep_ref, p_ref, q_ref, a_ref, d_ref, y_ref,
                m_scr, z_scr, hp_scr):
    L = SSM_CHUNK
    half = STATE_COLS // 2
    n_rows = x_ref.shape[2]

    @pl.when(pl.program_id(1) == 0)
    def _():
        m_scr[...] = jnp.zeros_like(m_scr)
        for s in range(L):
            for t in range(s, L):
                m_scr[s * LANES:(s + 1) * LANES, t * LANES:(t + 1) * LANES] = toep_ref[0, t - s]

    x = x_ref[0, 0]
    z_scr[...] = jnp.dot(x, p_ref[0], preferred_element_type=jnp.float32)

    a_re = a_ref[0, 0:1, :]
    a_im = a_ref[0, 1:2, :]

    def step(k, h):
        h_re, h_im = h
        hp_scr[pl.ds(k, 1), 0:half] = h_re
        hp_scr[pl.ds(k, 1), half:STATE_COLS] = h_im
        z_re = z_scr[pl.ds(k, 1), 0:half]
        z_im = z_scr[pl.ds(k, 1), half:STATE_COLS]
        return (a_re * h_re - a_im * h_im + z_re, a_re * h_im + a_im * h_re + z_im)

    zero = jnp.zeros((1, half), jnp.float32)
    lax.fori_loop(0, n_rows, step, (zero, zero))

    hp = hp_scr[...].astype(jnp.bfloat16)
    for n in range(CHUNK_COLS // MXU_TILE):
        lo, hi = n * MXU_TILE, (n + 1) * MXU_TILE
        y = jnp.dot(x[:, 0:hi], m_scr[0:hi, lo:hi], preferred_element_type=jnp.float32)
        y = y + jnp.dot(hp, q_ref[0, :, lo:hi], preferred_element_type=jnp.float32)
        y = y + d_ref[0, :, lo:hi] * x[:, lo:hi].astype(jnp.float32)
        y_ref[0, 0, :, lo:hi] = jax.nn.gelu(y).astype(jnp.bfloat16)


def _ssm_call(xc, toep, p_mat, q_mat, a_tab, d_tab):
    nb, bsz, n_rows, _ = xc.shape
    per_q = lambda q, b: (q, 0, 0)
    return pl.pallas_call(
        _ssm_kernel,
        grid=(nb, bsz),
        in_specs=[
            pl.BlockSpec((1, 1, n_rows, CHUNK_COLS), lambda q, b: (q, b, 0, 0)),
            pl.BlockSpec((1, SSM_CHUNK, LANES, LANES), lambda q, b: (q, 0, 0, 0)),
            pl.BlockSpec((1, CHUNK_COLS, STATE_COLS), per_q),
            pl.BlockSpec((1, STATE_COLS, CHUNK_COLS), per_q),
            pl.BlockSpec((1, 2, STATE_COLS // 2), per_q),
            pl.BlockSpec((1, 1, CHUNK_COLS), per_q),
        ],
        out_specs=pl.BlockSpec((1, 1, n_rows, CHUNK_COLS), lambda q, b: (q, b, 0, 0)),
        out_shape=jax.ShapeDtypeStruct(xc.shape, jnp.bfloat16),
        scratch_shapes=[
            pltpu.VMEM((CHUNK_COLS, CHUNK_COLS), jnp.bfloat16),
            pltpu.VMEM((n_rows, STATE_COLS), jnp.float32),
            pltpu.VMEM((n_rows, STATE_COLS), jnp.float32),
        ],
        compiler_params=pltpu.CompilerParams(
            dimension_semantics=("arbitrary", "arbitrary"), vmem_limit_bytes=VMEM_LIMIT),
        name="ssm",
    )(xc, toep, p_mat, q_mat, a_tab, d_tab)


def _attn_kernel(q_ref, kt_ref, v_ref, tri_ref, o_ref, carry_scr, acc_scr):
    blk = ATT_BLOCK
    i = pl.program_id(1)
    n_pairs = N_HEADS // 2

    carry_scr[...] = jnp.zeros_like(carry_scr)
    acc_scr[...] = jnp.zeros_like(acc_scr)

    lane = lax.broadcasted_iota(jnp.int32, (blk, LANES), 1)
    first_head = lane < HEAD_DIM
    row = lax.broadcasted_iota(jnp.int32, (blk, blk), 0)
    col = lax.broadcasted_iota(jnp.int32, (blk, blk), 1)
    tri = tri_ref[...]

    def body(state):
        j, _ = state
        valid = (col < row) | (j < i)
        ks = pl.multiple_of(j * blk, blk)
        worst = jnp.full((blk, LANES), -jnp.inf, jnp.float32)
        for pair in range(n_pairs):
            lanes = slice(pair * LANES, (pair + 1) * LANES)
            q_pair = q_ref[0, :, lanes]
            kt_pair = kt_ref[0, j, lanes, :]
            v_pair = v_ref[0, pl.ds(ks, blk), lanes]
            outs = []
            for sub in range(2):
                head = 2 * pair + sub
                keep = first_head if sub == 0 else jnp.logical_not(first_head)
                q_h = jnp.where(keep, q_pair, jnp.zeros_like(q_pair))
                z = jnp.dot(q_h, kt_pair, preferred_element_type=jnp.float32)
                lf = jnp.minimum(-z, 0.0) - jnp.log1p(jnp.exp(-jnp.abs(z)))
                lf = jnp.where(valid, lf, 0.0)
                lf_hi = lf.astype(jnp.bfloat16)
                lf_lo = (lf - lf_hi.astype(jnp.float32)).astype(jnp.bfloat16)
                cs = (jnp.dot(lf_hi, tri, preferred_element_type=jnp.float32)
                      + jnp.dot(lf_lo, tri, preferred_element_type=jnp.float32))
                carry = carry_scr[head]
                after = carry + cs[:, 0:blk]
                w = jnp.where(valid, jnp.exp(z + lf + after), 0.0)
                outs.append(jnp.dot(w.astype(jnp.bfloat16), v_pair,
                                    preferred_element_type=jnp.float32))
                carry = carry + cs[:, blk:blk + LANES]
                carry_scr[head] = carry
                worst = jnp.maximum(worst, carry)
            acc_scr[:, lanes] += jnp.where(first_head, outs[0], outs[1])
        return j - 1, jnp.max(worst)

    def cond(state):
        j, worst = state
        return jnp.logical_and(j >= 0, worst > ATT_SKIP_LOG)

    lax.while_loop(cond, body, (i, jnp.float32(0.0)))
    o_ref[0] = acc_scr[...].astype(o_ref.dtype)


def _attn_call(q3, kt4, v3, tri):
    bsz, seq, _ = q3.shape
    blk = ATT_BLOCK
    return pl.pallas_call(
        _attn_kernel,
        grid=(bsz, seq // blk),
        in_specs=[
            pl.BlockSpec((1, blk, D_ATTN), lambda b, i: (b, i, 0)),
            pl.BlockSpec((1, seq // blk, D_ATTN, blk), lambda b, i: (b, 0, 0, 0)),
            pl.BlockSpec((1, seq, D_ATTN), lambda b, i: (b, 0, 0)),
            pl.BlockSpec(tri.shape, lambda b, i: (0, 0)),
        ],
        out_specs=pl.BlockSpec((1, blk, D_ATTN), lambda b, i: (b, i, 0)),
        out_shape=jax.ShapeDtypeStruct((bsz, seq, D_ATTN), jnp.bfloat16),
        scratch_shapes=[
            pltpu.VMEM((N_HEADS, blk, LANES), jnp.float32),
            pltpu.VMEM((blk, D_ATTN), jnp.float32),
        ],
        compiler_params=pltpu.CompilerParams(
            dimension_semantics=("arbitrary", "arbitrary"), vmem_limit_bytes=VMEM_LIMIT),
        name="attn",
    )(q3, kt4, v3, tri)


def _merge_kernel(x_ref, ys_ref, ya_ref, gates_ref, wglu_ref, bglu_ref, wus_ref, wua_ref,
                  wout_ref, gm_ref, w1_ref, w2_ref, gf_ref, o_ref):
    f32, bf = jnp.float32, jnp.bfloat16
    y = jnp.concatenate([ys_ref[b] for b in range(N_LANE_BLOCKS)], axis=-1)
    glu_arg = jnp.dot(y, wglu_ref[...], preferred_element_type=f32) + bglu_ref[...]
    y_ssm = (y.astype(f32) * jax.nn.sigmoid(glu_arg)).astype(bf)
    up_s = jnp.dot(y_ssm, wus_ref[...], preferred_element_type=f32)
    up_a = jnp.dot(ya_ref[...], wua_ref[...], preferred_element_type=f32)
    g_s = gates_ref[:, 0:D_MODEL].astype(f32)
    g_a = gates_ref[:, D_MODEL:2 * D_MODEL].astype(f32)
    merged = (g_s * up_s + g_a * up_a).astype(bf)
    h = x_ref[...] + jnp.dot(merged, wout_ref[...], preferred_element_type=f32)
    n = _rmsnorm_f32(h, gm_ref[...]).astype(bf)
    for c in range(D_FF // FF_CHUNK):
        cols = slice(c * FF_CHUNK, (c + 1) * FF_CHUNK)
        hid = jnp.maximum(jnp.dot(n, w1_ref[:, cols], preferred_element_type=f32), 0.0)
        h = h + jnp.dot((hid * hid).astype(bf), w2_ref[cols, :], preferred_element_type=f32)
    o_ref[...] = _rmsnorm_f32(h, gf_ref[...])


def _merge_call(x2, ys, ya, gates, wglu, bglu, wus, wua, wout, gm, w1, w2, gf):
    tokens = x2.shape[0]
    rows = MERGE_ROWS
    const = lambda i: (0, 0)

    def resident(arr):
        return pl.BlockSpec(arr.shape, const, pipeline_mode=pl.Buffered(1))

    return pl.pallas_call(
        _merge_kernel,
        grid=(tokens // rows,),
        in_specs=[
            pl.BlockSpec((rows, D_MODEL), lambda i: (i, 0)),
            pl.BlockSpec((N_LANE_BLOCKS, rows, LANES), lambda i: (0, i, 0)),
            pl.BlockSpec((rows, D_ATTN), lambda i: (i, 0)),
            pl.BlockSpec((rows, 2 * D_MODEL), lambda i: (i, 0)),
            resident(wglu), resident(bglu), resident(wus), resident(wua), resident(wout),
            resident(gm), resident(w1), resident(w2), resident(gf),
        ],
        out_specs=pl.BlockSpec((rows, D_MODEL), lambda i: (i, 0)),
        out_shape=jax.ShapeDtypeStruct((tokens, D_MODEL), jnp.float32),
        compiler_params=pltpu.CompilerParams(
            dimension_semantics=("arbitrary",), vmem_limit_bytes=VMEM_LIMIT),
        name="merge_mlp",
    )(x2, ys, ya, gates, wglu, bglu, wus, wua, wout, gm, w1, w2, gf)


def _suffix_sum_matrix():
    blk = ATT_BLOCK
    r = jnp.arange(blk)[:, None]
    c = jnp.arange(blk + LANES)[None, :]
    return jnp.where((c >= blk) | (r > c), 1.0, 0.0).astype(jnp.bfloat16)


def kernel(x, norm_mix, w_in, A_re, A_im, log_dt, B_re, B_im, C_re, C_im, D_skip, w_glu, b_glu,
           w_up_ssm, w_up_attn, w_gate, b_gate, w_out, norm_mlp, w_ff1, w_ff2, norm_final):
    bsz, seq, _ = x.shape
    tokens = bsz * seq
    bf = jnp.bfloat16
    assert norm_mix.shape[0] == 1, "single layer"
    assert seq % (SSM_CHUNK * 8) == 0 and seq % ATT_BLOCK == 0 and tokens % PROJ_ROWS == 0

    x2 = x.reshape(tokens, D_MODEL)
    w_cat = jnp.concatenate([w_in[0], w_gate[0]], axis=1).astype(bf)
    wkt = w_in[0][:, D_SSM + D_ATTN:D_SSM + 2 * D_ATTN].T.astype(bf)
    ussm, q, kt, v, gates = _inproj_call(x2, norm_mix, w_cat, wkt, b_gate, bsz, seq)

    toep, p_mat, q_mat, a_tab, d_tab = _ssm_tables(
        A_re[0], A_im[0], log_dt[0], B_re[0], B_im[0], C_re[0], C_im[0], D_skip[0])
    xc = ussm.reshape(N_LANE_BLOCKS, bsz, seq // SSM_CHUNK, CHUNK_COLS)
    ys = _ssm_call(xc, toep, p_mat, q_mat, a_tab, d_tab).reshape(N_LANE_BLOCKS, tokens, LANES)

    ya = _attn_call(q.reshape(bsz, seq, D_ATTN), kt, v.reshape(bsz, seq, D_ATTN),
                    _suffix_sum_matrix()).reshape(tokens, D_ATTN)

    out = _merge_call(x2, ys, ya, gates, w_glu[0].astype(bf), b_glu, w_up_ssm[0].astype(bf),
                      w_up_attn[0].astype(bf), w_out[0].astype(bf), norm_mlp,
                      w_ff1[0].astype(bf), w_ff2[0].astype(bf), norm_final.reshape(1, D_MODEL))
    return out.reshape(bsz, seq, D_MODEL)
```

```python
import jax
import jax.numpy as jnp
from jax import lax
from jax.experimental import pallas as pl
from jax.experimental.pallas import tpu as pltpu

D_MODEL = 1024
D_SSM = 512
SSM_GROUP = 16
N_GROUPS = 32
STATE = 64
N_HEADS = 8
HEAD_DIM = 64
D_ATTN = 512
D_FF = 4096
EPS = 1e-6

LANES = 128
MXU_TILE = 256
VMEM_LIMIT = 52 * 1024 * 1024

SSM_CHUNK = 16
N_LANE_BLOCKS = D_SSM // LANES
GROUPS_PER_BLOCK = LANES // SSM_GROUP
STATE_COLS = 2 * GROUPS_PER_BLOCK * STATE
CHUNK_COLS = SSM_CHUNK * LANES
ATT_BLOCK = 128
ATT_SKIP_LOG = -64.0
PROJ_ROWS = 512
MERGE_ROWS = 256
FF_CHUNK = 1024


def _rmsnorm_f32(x, g):
    ms = jnp.mean(x * x, axis=-1, keepdims=True)
    return x * lax.rsqrt(ms + EPS) * g


def _inproj_kernel(x_ref, g_ref, w_ref, wkt_ref, bg_ref,
                   ussm_ref, q_ref, kt_ref, v_ref, gates_ref, pssm_scr):
    x = x_ref[...]
    u = _rmsnorm_f32(x, g_ref[...]).astype(jnp.bfloat16)
    rows = x.shape[0]
    chunk_rows = rows // SSM_CHUNK

    p_ssm = jnp.dot(u, w_ref[:, 0:D_SSM], preferred_element_type=jnp.float32)
    for blk in range(N_LANE_BLOCKS):
        pssm_scr[blk] = p_ssm[:, blk * LANES:(blk + 1) * LANES]
    for s in range(SSM_CHUNK):
        for blk in range(N_LANE_BLOCKS):
            piece = pssm_scr[blk, pl.ds(s, chunk_rows, stride=SSM_CHUNK), :]
            ussm_ref[blk, :, s * LANES:(s + 1) * LANES] = piece.astype(jnp.bfloat16)

    p_q = jnp.dot(u, w_ref[:, D_SSM:D_SSM + D_ATTN], preferred_element_type=jnp.float32)
    q_ref[...] = (p_q * (HEAD_DIM ** -0.5)).astype(jnp.bfloat16)

    kt = lax.dot_general(wkt_ref[...], u, (((1,), (1,)), ((), ())),
                         preferred_element_type=jnp.float32).astype(jnp.bfloat16)
    for c in range(rows // ATT_BLOCK):
        kt_ref[0, c] = kt[:, c * ATT_BLOCK:(c + 1) * ATT_BLOCK]

    p_v = jnp.dot(u, w_ref[:, D_SSM + 2 * D_ATTN:D_SSM + 3 * D_ATTN],
                  preferred_element_type=jnp.float32)
    v_ref[...] = p_v.astype(jnp.bfloat16)

    gate_off = D_SSM + 3 * D_ATTN
    for c in range(2 * D_MODEL // 512):
        pg = jnp.dot(u, w_ref[:, gate_off + c * 512:gate_off + (c + 1) * 512],
                     preferred_element_type=jnp.float32)
        pg = pg + bg_ref[:, c * 512:(c + 1) * 512]
        gates_ref[:, c * 512:(c + 1) * 512] = jax.nn.sigmoid(pg).astype(jnp.bfloat16)


def _inproj_call(x2, g, w_cat, wkt, bg, bsz, seq):
    tokens = bsz * seq
    rows = min(PROJ_ROWS, seq)
    tiles_per_seq = seq // rows
    n_kb = rows // ATT_BLOCK
    const = lambda i: (0, 0)
    return pl.pallas_call(
        _inproj_kernel,
        grid=(tokens // rows,),
        in_specs=[
            pl.BlockSpec((rows, D_MODEL), lambda i: (i, 0)),
            pl.BlockSpec((1, D_MODEL), const),
            pl.BlockSpec(w_cat.shape, const),
            pl.BlockSpec(wkt.shape, const),
            pl.BlockSpec((1, 2 * D_MODEL), const),
        ],
        out_specs=[
            pl.BlockSpec((N_LANE_BLOCKS, rows // SSM_CHUNK, CHUNK_COLS), lambda i: (0, i, 0)),
            pl.BlockSpec((rows, D_ATTN), lambda i: (i, 0)),
            pl.BlockSpec((1, n_kb, D_ATTN, ATT_BLOCK),
                         lambda i: (i // tiles_per_seq, i % tiles_per_seq, 0, 0)),
            pl.BlockSpec((rows, D_ATTN), lambda i: (i, 0)),
            pl.BlockSpec((rows, 2 * D_MODEL), lambda i: (i, 0)),
        ],
        out_shape=[
            jax.ShapeDtypeStruct((N_LANE_BLOCKS, tokens // SSM_CHUNK, CHUNK_COLS), jnp.bfloat16),
            jax.ShapeDtypeStruct((tokens, D_ATTN), jnp.bfloat16),
            jax.ShapeDtypeStruct((bsz, seq // ATT_BLOCK, D_ATTN, ATT_BLOCK), jnp.bfloat16),
            jax.ShapeDtypeStruct((tokens, D_ATTN), jnp.bfloat16),
            jax.ShapeDtypeStruct((tokens, 2 * D_MODEL), jnp.bfloat16),
        ],
        scratch_shapes=[pltpu.VMEM((N_LANE_BLOCKS, rows, LANES), jnp.float32)],
        compiler_params=pltpu.CompilerParams(
            dimension_semantics=("arbitrary",), vmem_limit_bytes=VMEM_LIMIT),
        name="inproj",
    )(x2, g, w_cat, wkt, bg)


def _ssm_tables(A_re, A_im, log_dt, B_re, B_im, C_re, C_im, D_skip):
    f32 = jnp.float32
    L = SSM_CHUNK
    nb, gb = N_LANE_BLOCKS, GROUPS_PER_BLOCK
    ar, ai = A_re.astype(f32), A_im.astype(f32)
    dt = jnp.exp(log_dt.astype(f32))[:, None]
    tau = jnp.arange(L + 1, dtype=f32)[:, None, None]
    mag = jnp.exp(ar[None] * dt[None] * tau)
    ang = ai[None] * dt[None] * tau
    pw_re, pw_im = mag * jnp.cos(ang), mag * jnp.sin(ang)
    num_re, num_im = pw_re[1] - 1.0, pw_im[1]
    den = ar * ar + ai * ai
    cf_re = (num_re * ar + num_im * ai) / den
    cf_im = (num_im * ar - num_re * ai) / den
    br, bi = B_re.astype(f32), B_im.astype(f32)
    bb_re = cf_re[..., None] * br - cf_im[..., None] * bi
    bb_im = cf_re[..., None] * bi + cf_im[..., None] * br
    ab_re = pw_re[:L, :, :, None] * bb_re[None] - pw_im[:L, :, :, None] * bb_im[None]
    ab_im = pw_re[:L, :, :, None] * bb_im[None] + pw_im[:L, :, :, None] * bb_re[None]
    cr, ci = C_re.astype(f32), C_im.astype(f32)

    kern = jnp.einsum('gdp,tgpc->tgdc', cr, ab_re) - jnp.einsum('gdp,tgpc->tgdc', ci, ab_im)
    kern = kern.reshape(L, nb, gb, SSM_GROUP, SSM_GROUP)
    toep = jnp.einsum('tqgdc,gh->qtgchd', kern, jnp.eye(gb, dtype=f32))
    toep = toep.reshape(nb, L, LANES, LANES)

    inj = jnp.stack([ab_re[::-1], ab_im[::-1]], axis=0)
    inj = inj.reshape(2, L, nb, gb, STATE, SSM_GROUP).transpose(2, 1, 3, 5, 0, 4)
    inj = inj.reshape(nb, CHUNK_COLS, 2 * STATE)

    ro_re = cr[None] * pw_re[1:L + 1, :, None, :] - ci[None] * pw_im[1:L + 1, :, None, :]
    ro_im = cr[None] * pw_im[1:L + 1, :, None, :] + ci[None] * pw_re[1:L + 1, :, None, :]
    ro = jnp.stack([ro_re, -ro_im], axis=0)
    ro = ro.reshape(2, L, nb, gb, SSM_GROUP, STATE).transpose(2, 0, 3, 5, 1, 4)
    ro = ro.reshape(nb, STATE_COLS, L * SSM_GROUP)

    a_tab = jnp.stack([pw_re[L].reshape(nb, gb * STATE), pw_im[L].reshape(nb, gb * STATE)], axis=1)
    d_tab = jnp.tile(D_skip.astype(f32).reshape(nb, 1, LANES), (1, 1, L))
    bf = jnp.bfloat16
    return toep.astype(bf), inj.astype(bf), ro.astype(bf), a_tab, d_tab


def _expansion_matrices():
    gb = GROUPS_PER_BLOCK
    src = jnp.arange(2 * STATE)[:, None]
    dst = jnp.arange(STATE_COLS)[None, :]
    e_inj = (src // STATE == dst // (gb * STATE)) & (src % STATE == dst % STATE)
    src = jnp.arange(SSM_CHUNK * SSM_GROUP)[:, None]
    dst = jnp.arange(CHUNK_COLS)[None, :]
    e_ro = (src // SSM_GROUP == dst // LANES) & (src % SSM_GROUP == dst % SSM_GROUP)
    return e_inj.astype(jnp.bfloat16), e_ro.astype(jnp.bfloat16)


def _ssm_kernel(x_ref, toep_ref, inj_ref, ro_ref, einj_ref, ero_ref, a_ref, d_ref, y_ref,
                m_scr, p_scr, q_scr, z_scr, hp_scr):
    L = SSM_CHUNK
    gb = GROUPS_PER_BLOCK
    half = STATE_COLS // 2
    n_rows = x_ref.shape[1]

    @pl.when(pl.program_id(1) == 0)
    def _():
        m_scr[...] = jnp.zeros_like(m_scr)
        for s in range(L):
            for t in range(s, L):
                m_scr[s * LANES:(s + 1) * LANES, t * LANES:(t + 1) * LANES] = toep_ref[0, t - s]
        def group_of(shape, axis, width):
            idx = lax.broadcasted_iota(jnp.int32, shape, axis)
            return lax.shift_right_logical(idx, width.bit_length() - 1) & (gb - 1)

        row_g = group_of((CHUNK_COLS, STATE_COLS), 0, SSM_GROUP)
        col_h = group_of((CHUNK_COLS, STATE_COLS), 1, STATE)
        p_full = jnp.dot(inj_ref[0], einj_ref[...], preferred_element_type=jnp.float32)
        p_scr[...] = jnp.where(row_g == col_h, p_full, 0.0).astype(jnp.bfloat16)
        row_g = group_of((STATE_COLS, CHUNK_COLS), 0, STATE)
        col_h = group_of((STATE_COLS, CHUNK_COLS), 1, SSM_GROUP)
        q_full = jnp.dot(ro_ref[0], ero_ref[...], preferred_element_type=jnp.float32)
        q_scr[...] = jnp.where(row_g == col_h, q_full, 0.0).astype(jnp.bfloat16)

    x = x_ref[0]
    z_scr[...] = jnp.dot(x, p_scr[...], preferred_element_type=jnp.float32)

    a_re = a_ref[0, 0:1, :]
    a_im = a_ref[0, 1:2, :]

    def step(k, h):
        h_re, h_im = h
        hp_scr[pl.ds(k, 1), 0:half] = h_re
        hp_scr[pl.ds(k, 1), half:STATE_COLS] = h_im
        z_re = z_scr[pl.ds(k, 1), 0:half]
        z_im = z_scr[pl.ds(k, 1), half:STATE_COLS]
        return (a_re * h_re - a_im * h_im + z_re, a_re * h_im + a_im * h_re + z_im)

    zero = jnp.zeros((1, half), jnp.float32)
    lax.fori_loop(0, n_rows, step, (zero, zero))

    hp = hp_scr[...].astype(jnp.bfloat16)
    for n in range(CHUNK_COLS // MXU_TILE):
        lo, hi = n * MXU_TILE, (n + 1) * MXU_TILE
        y = jnp.dot(x[:, 0:hi], m_scr[0:hi, lo:hi], preferred_element_type=jnp.float32)
        y = y + jnp.dot(hp, q_scr[:, lo:hi], preferred_element_type=jnp.float32)
        y = y + d_ref[0, :, lo:hi] * x[:, lo:hi].astype(jnp.float32)
        y_ref[0, :, lo:hi] = jax.nn.gelu(y).astype(jnp.bfloat16)


def _ssm_call(xc, toep, inj, ro, e_inj, e_ro, a_tab, d_tab, bsz):
    nb, total_rows, _ = xc.shape
    n_rows = total_rows // bsz
    per_q = lambda q, b: (q, 0, 0)
    const = lambda q, b: (0, 0)
    return pl.pallas_call(
        _ssm_kernel,
        grid=(nb, bsz),
        in_specs=[
            pl.BlockSpec((1, n_rows, CHUNK_COLS), lambda q, b: (q, b, 0)),
            pl.BlockSpec((1, SSM_CHUNK, LANES, LANES), lambda q, b: (q, 0, 0, 0)),
            pl.BlockSpec((1, CHUNK_COLS, 2 * STATE), per_q),
            pl.BlockSpec((1, STATE_COLS, SSM_CHUNK * SSM_GROUP), per_q),
            pl.BlockSpec(e_inj.shape, const),
            pl.BlockSpec(e_ro.shape, const),
            pl.BlockSpec((1, 2, STATE_COLS // 2), per_q),
            pl.BlockSpec((1, 1, CHUNK_COLS), per_q),
        ],
        out_specs=pl.BlockSpec((1, n_rows, CHUNK_COLS), lambda q, b: (q, b, 0)),
        out_shape=jax.ShapeDtypeStruct(xc.shape, jnp.bfloat16),
        scratch_shapes=[
            pltpu.VMEM((CHUNK_COLS, CHUNK_COLS), jnp.bfloat16),
            pltpu.VMEM((CHUNK_COLS, STATE_COLS), jnp.bfloat16),
            pltpu.VMEM((STATE_COLS, CHUNK_COLS), jnp.bfloat16),
            pltpu.VMEM((n_rows, STATE_COLS), jnp.float32),
            pltpu.VMEM((n_rows, STATE_COLS), jnp.float32),
        ],
        compiler_params=pltpu.CompilerParams(
            dimension_semantics=("arbitrary", "arbitrary"), vmem_limit_bytes=VMEM_LIMIT),
        name="ssm",
    )(xc, toep, inj, ro, e_inj, e_ro, a_tab, d_tab)


def _attn_kernel(q_ref, kt_ref, v_ref, tri_ref, o_ref, qm_scr, carry_scr, acc_scr):
    blk = ATT_BLOCK
    i = pl.program_id(1)
    n_pairs = N_HEADS // 2
    f32, bf = jnp.float32, jnp.bfloat16

    carry_scr[...] = jnp.zeros_like(carry_scr)
    acc_scr[...] = jnp.zeros_like(acc_scr)

    lane = lax.broadcasted_iota(jnp.int32, (blk, LANES), 1)
    first_head = lane < HEAD_DIM
    for pair in range(n_pairs):
        q_pair = q_ref[0, :, pair * LANES:(pair + 1) * LANES]
        zero = jnp.zeros_like(q_pair)
        qm_scr[2 * pair * blk:(2 * pair + 1) * blk] = jnp.where(first_head, q_pair, zero)
        qm_scr[(2 * pair + 1) * blk:(2 * pair + 2) * blk] = jnp.where(first_head, zero, q_pair)

    row = lax.broadcasted_iota(jnp.int32, (N_HEADS * blk, blk), 0) & (blk - 1)
    col = lax.broadcasted_iota(jnp.int32, (N_HEADS * blk, blk), 1)
    tri = tri_ref[...]

    def body(state):
        j, _ = state
        valid = col < row + (i - j) * blk
        ks = pl.multiple_of(j * blk, blk)
        z = jnp.concatenate(
            [jnp.dot(qm_scr[2 * pair * blk:(2 * pair + 2) * blk],
                     kt_ref[0, j, pair * LANES:(pair + 1) * LANES, :],
                     preferred_element_type=f32) for pair in range(n_pairs)], axis=0)
        lf = jnp.minimum(-z, 0.0) - jnp.log1p(jnp.exp(-jnp.abs(z)))
        lf = jnp.where(valid, lf, 0.0)
        lf_hi = lf.astype(bf)
        lf_lo = (lf - lf_hi.astype(f32)).astype(bf)
        cs = jnp.dot(jnp.concatenate([lf_hi, lf_lo], axis=1), tri, preferred_element_type=f32)
        carry = carry_scr[...]
        after = carry + cs[:, 0:blk]
        w = jnp.where(valid, jnp.exp(z + lf + after), 0.0).astype(bf)
        carry = carry + cs[:, blk:blk + LANES]
        carry_scr[...] = carry
        for pair in range(n_pairs):
            lanes = slice(pair * LANES, (pair + 1) * LANES)
            o2 = jnp.dot(w[2 * pair * blk:(2 * pair + 2) * blk], v_ref[0, pl.ds(ks, blk), lanes],
                         preferred_element_type=f32)
            acc_scr[:, lanes] += jnp.where(first_head, o2[0:blk], o2[blk:2 * blk])
        return j - 1, jnp.max(carry)

    def cond(state):
        j, worst = state
        return jnp.logical_and(j >= 0, worst > ATT_SKIP_LOG)

    lax.while_loop(cond, body, (i, jnp.float32(0.0)))
    o_ref[0] = acc_scr[...].astype(o_ref.dtype)


def _attn_call(q3, kt4, v3, tri):
    bsz, seq, _ = q3.shape
    blk = ATT_BLOCK
    return pl.pallas_call(
        _attn_kernel,
        grid=(bsz, seq // blk),
        in_specs=[
            pl.BlockSpec((1, blk, D_ATTN), lambda b, i: (b, i, 0)),
            pl.BlockSpec((1, seq // blk, D_ATTN, blk), lambda b, i: (b, 0, 0, 0)),
            pl.BlockSpec((1, seq, D_ATTN), lambda b, i: (b, 0, 0)),
            pl.BlockSpec(tri.shape, lambda b, i: (0, 0)),
        ],
        out_specs=pl.BlockSpec((1, blk, D_ATTN), lambda b, i: (b, i, 0)),
        out_shape=jax.ShapeDtypeStruct((bsz, seq, D_ATTN), jnp.bfloat16),
        scratch_shapes=[
            pltpu.VMEM((N_HEADS * blk, LANES), jnp.bfloat16),
            pltpu.VMEM((N_HEADS * blk, LANES), jnp.float32),
            pltpu.VMEM((blk, D_ATTN), jnp.float32),
        ],
        compiler_params=pltpu.CompilerParams(
            dimension_semantics=("arbitrary", "arbitrary"), vmem_limit_bytes=VMEM_LIMIT),
        name="attn",
    )(q3, kt4, v3, tri)


def _merge_kernel(x_ref, ys_ref, ya_ref, gates_ref, wglu_ref, bglu_ref, wus_ref, wua_ref,
                  wout_ref, gm_ref, w1_ref, w2_ref, gf_ref, o_ref, y_scr):
    f32, bf = jnp.float32, jnp.bfloat16
    chunk_rows = ys_ref.shape[1]
    for s in range(SSM_CHUNK):
        for blk in range(N_LANE_BLOCKS):
            piece = ys_ref[blk, :, s * LANES:(s + 1) * LANES].astype(f32)
            y_scr[blk, pl.ds(s, chunk_rows, stride=SSM_CHUNK), :] = piece
    y = jnp.concatenate([y_scr[blk] for blk in range(N_LANE_BLOCKS)], axis=-1)
    glu_arg = jnp.dot(y.astype(bf), wglu_ref[...], preferred_element_type=f32) + bglu_ref[...]
    y_ssm = (y * jax.nn.sigmoid(glu_arg)).astype(bf)
    up_s = jnp.dot(y_ssm, wus_ref[...], preferred_element_type=f32)
    up_a = jnp.dot(ya_ref[...], wua_ref[...], preferred_element_type=f32)
    g_s = gates_ref[:, 0:D_MODEL].astype(f32)
    g_a = gates_ref[:, D_MODEL:2 * D_MODEL].astype(f32)
    merged = (g_s * up_s + g_a * up_a).astype(bf)
    h = x_ref[...] + jnp.dot(merged, wout_ref[...], preferred_element_type=f32)
    n = _rmsnorm_f32(h, gm_ref[...]).astype(bf)
    for c in range(D_FF // FF_CHUNK):
        cols = slice(c * FF_CHUNK, (c + 1) * FF_CHUNK)
        hid = jnp.maximum(jnp.dot(n, w1_ref[:, cols], preferred_element_type=f32), 0.0)
        h = h + jnp.dot((hid * hid).astype(bf), w2_ref[cols, :], preferred_element_type=f32)
    o_ref[...] = _rmsnorm_f32(h, gf_ref[...])


def _merge_call(x2, ys, ya, gates, wglu, bglu, wus, wua, wout, gm, w1, w2, gf):
    tokens = x2.shape[0]
    rows = MERGE_ROWS
    const = lambda i: (0, 0)

    def resident(arr):
        return pl.BlockSpec(arr.shape, const, pipeline_mode=pl.Buffered(1))

    return pl.pallas_call(
        _merge_kernel,
        grid=(tokens // rows,),
        in_specs=[
            pl.BlockSpec((rows, D_MODEL), lambda i: (i, 0)),
            pl.BlockSpec((N_LANE_BLOCKS, rows // SSM_CHUNK, CHUNK_COLS), lambda i: (0, i, 0)),
            pl.BlockSpec((rows, D_ATTN), lambda i: (i, 0)),
            pl.BlockSpec((rows, 2 * D_MODEL), lambda i: (i, 0)),
            resident(wglu), resident(bglu), resident(wus), resident(wua), resident(wout),
            resident(gm), resident(w1), resident(w2), resident(gf),
        ],
        out_specs=pl.BlockSpec((rows, D_MODEL), lambda i: (i, 0)),
        out_shape=jax.ShapeDtypeStruct((tokens, D_MODEL), jnp.float32),
        scratch_shapes=[pltpu.VMEM((N_LANE_BLOCKS, rows, LANES), jnp.float32)],
        compiler_params=pltpu.CompilerParams(
            dimension_semantics=("arbitrary",), vmem_limit_bytes=VMEM_LIMIT),
        name="merge_mlp",
    )(x2, ys, ya, gates, wglu, bglu, wus, wua, wout, gm, w1, w2, gf)


def _suffix_sum_matrix():
    blk = ATT_BLOCK
    r = jnp.arange(blk)[:, None]
    c = jnp.arange(blk + LANES)[None, :]
    one = jnp.where((c >= blk) | (r > c), 1.0, 0.0).astype(jnp.bfloat16)
    return jnp.concatenate([one, one], axis=0)


def kernel(x, norm_mix, w_in, A_re, A_im, log_dt, B_re, B_im, C_re, C_im, D_skip, w_glu, b_glu,
           w_up_ssm, w_up_attn, w_gate, b_gate, w_out, norm_mlp, w_ff1, w_ff2, norm_final):
    bsz, seq, _ = x.shape
    tokens = bsz * seq
    bf = jnp.bfloat16
    assert norm_mix.shape[0] == 1, "single layer"
    assert seq % (SSM_CHUNK * 8) == 0 and seq % ATT_BLOCK == 0
    assert tokens % PROJ_ROWS == 0 and tokens % MERGE_ROWS == 0 and seq % min(PROJ_ROWS, seq) == 0

    x2 = x.reshape(tokens, D_MODEL)
    w_cat = jnp.concatenate([w_in[0], w_gate[0]], axis=1).astype(bf)
    wkt = w_in[0][:, D_SSM + D_ATTN:D_SSM + 2 * D_ATTN].T.astype(bf)
    ussm, q, kt, v, gates = _inproj_call(x2, norm_mix, w_cat, wkt, b_gate, bsz, seq)

    toep, inj, ro, a_tab, d_tab = _ssm_tables(
        A_re[0], A_im[0], log_dt[0], B_re[0], B_im[0], C_re[0], C_im[0], D_skip[0])
    e_inj, e_ro = _expansion_matrices()
    ys = _ssm_call(ussm, toep, inj, ro, e_inj, e_ro, a_tab, d_tab, bsz)

    ya = _attn_call(q.reshape(bsz, seq, D_ATTN), kt, v.reshape(bsz, seq, D_ATTN),
                    _suffix_sum_matrix()).reshape(tokens, D_ATTN)

    out = _merge_call(x2, ys, ya, gates, w_glu[0].astype(bf), b_glu, w_up_ssm[0].astype(bf),
                      w_up_attn[0].astype(bf), w_out[0].astype(bf), norm_mlp,
                      w_ff1[0].astype(bf), w_ff2[0].astype(bf), norm_final.reshape(1, D_MODEL))
    return out.reshape(bsz, seq, D_MODEL)
```

```python
import jax
import jax.numpy as jnp
from jax import lax
from jax.experimental import pallas as pl
from jax.experimental.pallas import tpu as pltpu

D_MODEL = 1024
D_SSM = 512
SSM_GROUP = 16
N_GROUPS = 32
STATE = 64
N_HEADS = 8
HEAD_DIM = 64
D_ATTN = 512
D_FF = 4096
EPS = 1e-6

LANES = 128
MXU_TILE = 256
VMEM_LIMIT = 52 * 1024 * 1024

SSM_CHUNK = 16
N_LANE_BLOCKS = D_SSM // LANES
GROUPS_PER_BLOCK = LANES // SSM_GROUP
STATE_COLS = 2 * GROUPS_PER_BLOCK * STATE
CHUNK_COLS = SSM_CHUNK * LANES
ATT_BLOCK = 128
ATT_SKIP_SUM = 64.0
ATT_MASKED = 1e30
PROJ_ROWS = 512
MERGE_ROWS = 512
FF_CHUNK = 1024


def _rmsnorm_f32(x, g):
    ms = jnp.mean(x * x, axis=-1, keepdims=True)
    return x * lax.rsqrt(ms + EPS) * g


def _inproj_kernel(x_ref, g_ref, w_ref, wkt_ref, wg_ref, bg_ref,
                   ussm_ref, q_ref, kt_ref, v_ref, gates_ref, pssm_scr):
    x = x_ref[...]
    u = _rmsnorm_f32(x, g_ref[...]).astype(jnp.bfloat16)
    rows = x.shape[0]
    chunk_rows = rows // SSM_CHUNK

    p_ssm = jnp.dot(u, w_ref[:, 0:D_SSM], preferred_element_type=jnp.float32)
    for blk in range(N_LANE_BLOCKS):
        pssm_scr[blk] = p_ssm[:, blk * LANES:(blk + 1) * LANES]
    for s in range(SSM_CHUNK):
        for blk in range(N_LANE_BLOCKS):
            piece = pssm_scr[blk, pl.ds(s, chunk_rows, stride=SSM_CHUNK), :]
            ussm_ref[blk, :, s * LANES:(s + 1) * LANES] = piece.astype(jnp.bfloat16)

    p_q = jnp.dot(u, w_ref[:, D_SSM:D_SSM + D_ATTN], preferred_element_type=jnp.float32)
    q_ref[...] = (p_q * (HEAD_DIM ** -0.5)).astype(jnp.bfloat16)

    kt = lax.dot_general(wkt_ref[...], u, (((1,), (1,)), ((), ())),
                         preferred_element_type=jnp.float32).astype(jnp.bfloat16)
    for c in range(rows // ATT_BLOCK):
        kt_ref[0, c] = kt[:, c * ATT_BLOCK:(c + 1) * ATT_BLOCK]

    p_v = jnp.dot(u, w_ref[:, D_SSM + 2 * D_ATTN:D_SSM + 3 * D_ATTN],
                  preferred_element_type=jnp.float32)
    v_ref[...] = p_v.astype(jnp.bfloat16)

    for c in range(2 * D_MODEL // 512):
        pg = jnp.dot(u, wg_ref[:, c * 512:(c + 1) * 512], preferred_element_type=jnp.float32)
        pg = pg + bg_ref[:, c * 512:(c + 1) * 512]
        gates_ref[:, c * 512:(c + 1) * 512] = jax.nn.sigmoid(pg).astype(jnp.bfloat16)


def _inproj_call(x2, g, w_in, wkt, w_gate, bg, bsz, seq):
    tokens = bsz * seq
    rows = min(PROJ_ROWS, seq)
    tiles_per_seq = seq // rows
    n_kb = rows // ATT_BLOCK
    const = lambda i: (0, 0)
    return pl.pallas_call(
        _inproj_kernel,
        grid=(tokens // rows,),
        in_specs=[
            pl.BlockSpec((rows, D_MODEL), lambda i: (i, 0)),
            pl.BlockSpec((1, D_MODEL), const),
            pl.BlockSpec(w_in.shape, const),
            pl.BlockSpec(wkt.shape, const),
            pl.BlockSpec(w_gate.shape, const),
            pl.BlockSpec((1, 2 * D_MODEL), const),
        ],
        out_specs=[
            pl.BlockSpec((N_LANE_BLOCKS, rows // SSM_CHUNK, CHUNK_COLS), lambda i: (0, i, 0)),
            pl.BlockSpec((rows, D_ATTN), lambda i: (i, 0)),
            pl.BlockSpec((1, n_kb, D_ATTN, ATT_BLOCK),
                         lambda i: (i // tiles_per_seq, i % tiles_per_seq, 0, 0)),
            pl.BlockSpec((rows, D_ATTN), lambda i: (i, 0)),
            pl.BlockSpec((rows, 2 * D_MODEL), lambda i: (i, 0)),
        ],
        out_shape=[
            jax.ShapeDtypeStruct((N_LANE_BLOCKS, tokens // SSM_CHUNK, CHUNK_COLS), jnp.bfloat16),
            jax.ShapeDtypeStruct((tokens, D_ATTN), jnp.bfloat16),
            jax.ShapeDtypeStruct((bsz, seq // ATT_BLOCK, D_ATTN, ATT_BLOCK), jnp.bfloat16),
            jax.ShapeDtypeStruct((tokens, D_ATTN), jnp.bfloat16),
            jax.ShapeDtypeStruct((tokens, 2 * D_MODEL), jnp.bfloat16),
        ],
        scratch_shapes=[pltpu.VMEM((N_LANE_BLOCKS, rows, LANES), jnp.float32)],
        compiler_params=pltpu.CompilerParams(
            dimension_semantics=("arbitrary",), vmem_limit_bytes=VMEM_LIMIT),
        name="inproj",
    )(x2, g, w_in, wkt, w_gate, bg)


def _ssm_tables(A_re, A_im, log_dt, B_re, B_im, C_re, C_im, D_skip):
    f32 = jnp.float32
    L = SSM_CHUNK
    nb, gb = N_LANE_BLOCKS, GROUPS_PER_BLOCK
    ar, ai = A_re.astype(f32), A_im.astype(f32)
    dt = jnp.exp(log_dt.astype(f32))[:, None]
    tau = jnp.arange(L + 1, dtype=f32)[:, None, None]
    mag = jnp.exp(ar[None] * dt[None] * tau)
    ang = ai[None] * dt[None] * tau
    pw_re, pw_im = mag * jnp.cos(ang), mag * jnp.sin(ang)
    num_re, num_im = pw_re[1] - 1.0, pw_im[1]
    den = ar * ar + ai * ai
    cf_re = (num_re * ar + num_im * ai) / den
    cf_im = (num_im * ar - num_re * ai) / den
    br, bi = B_re.astype(f32), B_im.astype(f32)
    bb_re = cf_re[..., None] * br - cf_im[..., None] * bi
    bb_im = cf_re[..., None] * bi + cf_im[..., None] * br
    ab_re = pw_re[:L, :, :, None] * bb_re[None] - pw_im[:L, :, :, None] * bb_im[None]
    ab_im = pw_re[:L, :, :, None] * bb_im[None] + pw_im[:L, :, :, None] * bb_re[None]
    cr, ci = C_re.astype(f32), C_im.astype(f32)

    kern = jnp.einsum('gdp,tgpc->tgdc', cr, ab_re) - jnp.einsum('gdp,tgpc->tgdc', ci, ab_im)
    kern = kern.reshape(L, nb, gb, SSM_GROUP, SSM_GROUP)
    toep = jnp.einsum('tqgdc,gh->qtgchd', kern, jnp.eye(gb, dtype=f32))
    toep = toep.reshape(nb, L, LANES, LANES)

    pw_cat = jnp.concatenate([pw_re[:L], pw_im[:L]], axis=-1)[::-1]
    pw_swp = jnp.concatenate([pw_im[:L], pw_re[:L]], axis=-1)[::-1]
    bbt_re, bbt_im = bb_re.transpose(0, 2, 1), bb_im.transpose(0, 2, 1)
    b_same = jnp.concatenate([bbt_re, bbt_re], axis=-1)
    b_cross = jnp.concatenate([-bbt_im, bbt_im], axis=-1)
    inj = pw_cat[:, :, None, :] * b_same[None] + pw_swp[:, :, None, :] * b_cross[None]
    inj = inj.reshape(L, nb, LANES, 2 * STATE).transpose(1, 0, 2, 3).reshape(nb, CHUNK_COLS, 2 * STATE)

    ro_re = cr[None] * pw_re[1:L + 1, :, None, :] - ci[None] * pw_im[1:L + 1, :, None, :]
    ro_im = cr[None] * pw_im[1:L + 1, :, None, :] + ci[None] * pw_re[1:L + 1, :, None, :]
    ro = jnp.stack([ro_re, -ro_im], axis=0)
    ro = ro.reshape(2, L, nb, gb, SSM_GROUP, STATE).transpose(2, 0, 3, 5, 1, 4)
    ro = ro.reshape(nb, STATE_COLS, L * SSM_GROUP)

    a_tab = jnp.stack([pw_re[L].reshape(nb, gb * STATE), pw_im[L].reshape(nb, gb * STATE)], axis=1)
    d_tab = jnp.tile(D_skip.astype(f32).reshape(nb, 1, LANES), (1, 1, L))
    bf = jnp.bfloat16
    return toep.astype(bf), inj.astype(bf), ro.astype(bf), a_tab, d_tab


def _expansion_matrices():
    gb = GROUPS_PER_BLOCK
    src = jnp.arange(2 * STATE)[:, None]
    dst = jnp.arange(STATE_COLS)[None, :]
    e_inj = (src // STATE == dst // (gb * STATE)) & (src % STATE == dst % STATE)
    src = jnp.arange(SSM_CHUNK * SSM_GROUP)[:, None]
    dst = jnp.arange(CHUNK_COLS)[None, :]
    e_ro = (src // SSM_GROUP == dst // LANES) & (src % SSM_GROUP == dst % SSM_GROUP)
    return e_inj.astype(jnp.bfloat16), e_ro.astype(jnp.bfloat16)


def _ssm_kernel(x_ref, toep_ref, inj_ref, ro_ref, einj_ref, ero_ref, a_ref, d_ref, y_ref,
                m_scr, p_scr, q_scr, z_scr, hp_scr):
    L = SSM_CHUNK
    gb = GROUPS_PER_BLOCK
    half = STATE_COLS // 2
    n_rows = x_ref.shape[1]

    @pl.when(pl.program_id(1) == 0)
    def _():
        m_scr[...] = jnp.zeros_like(m_scr)
        for s in range(L):
            for t in range(s, L):
                m_scr[s * LANES:(s + 1) * LANES, t * LANES:(t + 1) * LANES] = toep_ref[0, t - s]
        def group_of(shape, axis, width):
            idx = lax.broadcasted_iota(jnp.int32, shape, axis)
            return lax.shift_right_logical(idx, width.bit_length() - 1) & (gb - 1)

        row_g = group_of((CHUNK_COLS, STATE_COLS), 0, SSM_GROUP)
        col_h = group_of((CHUNK_COLS, STATE_COLS), 1, STATE)
        p_full = jnp.dot(inj_ref[0], einj_ref[...], preferred_element_type=jnp.float32)
        p_scr[...] = jnp.where(row_g == col_h, p_full, 0.0).astype(jnp.bfloat16)
        row_g = group_of((STATE_COLS, CHUNK_COLS), 0, STATE)
        col_h = group_of((STATE_COLS, CHUNK_COLS), 1, SSM_GROUP)
        q_full = jnp.dot(ro_ref[0], ero_ref[...], preferred_element_type=jnp.float32)
        q_scr[...] = jnp.where(row_g == col_h, q_full, 0.0).astype(jnp.bfloat16)

    x = x_ref[0]
    z_scr[...] = jnp.dot(x, p_scr[...], preferred_element_type=jnp.float32)

    a_re = a_ref[0, 0:1, :]
    a_im = a_ref[0, 1:2, :]

    def step(k, h):
        h_re, h_im = h
        hp_scr[pl.ds(k, 1), 0:half] = h_re
        hp_scr[pl.ds(k, 1), half:STATE_COLS] = h_im
        z_re = z_scr[pl.ds(k, 1), 0:half]
        z_im = z_scr[pl.ds(k, 1), half:STATE_COLS]
        return (a_re * h_re - a_im * h_im + z_re, a_re * h_im + a_im * h_re + z_im)

    zero = jnp.zeros((1, half), jnp.float32)
    lax.fori_loop(0, n_rows, step, (zero, zero))

    hp = hp_scr[...].astype(jnp.bfloat16)
    for n in range(CHUNK_COLS // MXU_TILE):
        lo, hi = n * MXU_TILE, (n + 1) * MXU_TILE
        y = jnp.dot(x[:, 0:hi], m_scr[0:hi, lo:hi], preferred_element_type=jnp.float32)
        y = y + jnp.dot(hp, q_scr[:, lo:hi], preferred_element_type=jnp.float32)
        y = y + d_ref[0, :, lo:hi] * x[:, lo:hi].astype(jnp.float32)
        y_ref[0, :, lo:hi] = jax.nn.gelu(y).astype(jnp.bfloat16)


def _ssm_call(xc, toep, inj, ro, e_inj, e_ro, a_tab, d_tab, bsz):
    nb, total_rows, _ = xc.shape
    n_rows = total_rows // bsz
    per_q = lambda q, b: (q, 0, 0)
    const = lambda q, b: (0, 0)
    return pl.pallas_call(
        _ssm_kernel,
        grid=(nb, bsz),
        in_specs=[
            pl.BlockSpec((1, n_rows, CHUNK_COLS), lambda q, b: (q, b, 0)),
            pl.BlockSpec((1, SSM_CHUNK, LANES, LANES), lambda q, b: (q, 0, 0, 0)),
            pl.BlockSpec((1, CHUNK_COLS, 2 * STATE), per_q),
            pl.BlockSpec((1, STATE_COLS, SSM_CHUNK * SSM_GROUP), per_q),
            pl.BlockSpec(e_inj.shape, const),
            pl.BlockSpec(e_ro.shape, const),
            pl.BlockSpec((1, 2, STATE_COLS // 2), per_q),
            pl.BlockSpec((1, 1, CHUNK_COLS), per_q),
        ],
        out_specs=pl.BlockSpec((1, n_rows, CHUNK_COLS), lambda q, b: (q, b, 0)),
        out_shape=jax.ShapeDtypeStruct(xc.shape, jnp.bfloat16),
        scratch_shapes=[
            pltpu.VMEM((CHUNK_COLS, CHUNK_COLS), jnp.bfloat16),
            pltpu.VMEM((CHUNK_COLS, STATE_COLS), jnp.bfloat16),
            pltpu.VMEM((STATE_COLS, CHUNK_COLS), jnp.bfloat16),
            pltpu.VMEM((n_rows, STATE_COLS), jnp.float32),
            pltpu.VMEM((n_rows, STATE_COLS), jnp.float32),
        ],
        compiler_params=pltpu.CompilerParams(
            dimension_semantics=("arbitrary", "arbitrary"), vmem_limit_bytes=VMEM_LIMIT),
        name="ssm",
    )(xc, toep, inj, ro, e_inj, e_ro, a_tab, d_tab)


def _attn_kernel(q_ref, kt_ref, v_ref, tri_ref, o_ref, qm_scr, carry_scr, acc_scr):
    blk = ATT_BLOCK
    i = pl.program_id(1)
    n_pairs = N_HEADS // 2
    f32, bf = jnp.float32, jnp.bfloat16

    carry_scr[...] = jnp.zeros_like(carry_scr)
    acc_scr[...] = jnp.zeros_like(acc_scr)

    lane = lax.broadcasted_iota(jnp.int32, (blk, LANES), 1)
    first_head = lane < HEAD_DIM
    for pair in range(n_pairs):
        q_pair = q_ref[0, :, pair * LANES:(pair + 1) * LANES]
        zero = jnp.zeros_like(q_pair)
        qm_scr[2 * pair * blk:(2 * pair + 1) * blk] = jnp.where(first_head, q_pair, zero)
        qm_scr[(2 * pair + 1) * blk:(2 * pair + 2) * blk] = jnp.where(first_head, zero, q_pair)

    row = lax.broadcasted_iota(jnp.int32, (blk, blk), 0)
    col = lax.broadcasted_iota(jnp.int32, (blk, blk), 1)
    tri = tri_ref[...]

    def body(state):
        j, _ = state
        pen = jnp.where(col < row + (i - j) * blk, 0.0, ATT_MASKED)
        ks = pl.multiple_of(j * blk, blk)
        z = jnp.concatenate(
            [jnp.dot(qm_scr[h * blk:(h + 2) * blk],
                     kt_ref[0, j, (h // 2) * LANES:(h // 2 + 1) * LANES, :],
                     preferred_element_type=f32).reshape(2, blk, blk) - pen[None]
             for h in range(0, N_HEADS, 2)], axis=0).reshape(N_HEADS * blk, blk)
        sp = jnp.maximum(z, 0.0) + jnp.log(1.0 + jnp.exp(-jnp.abs(z)))
        sp_hi = sp.astype(bf)
        sp_lo = (sp - sp_hi.astype(f32)).astype(bf)
        cs = jnp.dot(jnp.concatenate([sp_hi, sp_lo], axis=1), tri, preferred_element_type=f32)
        carry = carry_scr[...]
        w = jnp.exp(z - sp - cs[:, 0:blk] - carry).astype(bf)
        carry = carry + cs[:, blk:blk + LANES]
        carry_scr[...] = carry
        for pair in range(n_pairs):
            lanes = slice(pair * LANES, (pair + 1) * LANES)
            o2 = jnp.dot(w[2 * pair * blk:(2 * pair + 2) * blk], v_ref[0, pl.ds(ks, blk), lanes],
                         preferred_element_type=f32)
            acc_scr[:, lanes] += jnp.where(first_head, o2[0:blk], o2[blk:2 * blk])
        return j - 1, jnp.min(carry)

    def cond(state):
        j, least = state
        return jnp.logical_and(j >= 0, least < ATT_SKIP_SUM)

    lax.while_loop(cond, body, (i, jnp.float32(0.0)))
    o_ref[0] = acc_scr[...].astype(o_ref.dtype)


def _attn_call(q3, kt4, v3, tri):
    bsz, seq, _ = q3.shape
    blk = ATT_BLOCK
    return pl.pallas_call(
        _attn_kernel,
        grid=(bsz, seq // blk),
        in_specs=[
            pl.BlockSpec((1, blk, D_ATTN), lambda b, i: (b, i, 0)),
            pl.BlockSpec((1, seq // blk, D_ATTN, blk), lambda b, i: (b, 0, 0, 0)),
            pl.BlockSpec((1, seq, D_ATTN), lambda b, i: (b, 0, 0)),
            pl.BlockSpec(tri.shape, lambda b, i: (0, 0)),
        ],
        out_specs=pl.BlockSpec((1, blk, D_ATTN), lambda b, i: (b, i, 0)),
        out_shape=jax.ShapeDtypeStruct((bsz, seq, D_ATTN), jnp.bfloat16),
        scratch_shapes=[
            pltpu.VMEM((N_HEADS * blk, LANES), jnp.bfloat16),
            pltpu.VMEM((N_HEADS * blk, LANES), jnp.float32),
            pltpu.VMEM((blk, D_ATTN), jnp.float32),
        ],
        compiler_params=pltpu.CompilerParams(
            dimension_semantics=("arbitrary", "arbitrary"), vmem_limit_bytes=VMEM_LIMIT),
        name="attn",
    )(q3, kt4, v3, tri)


def _merge_kernel(x_ref, ys_ref, ya_ref, gates_ref, wglu_ref, bglu_ref, wus_ref, wua_ref,
                  wout_ref, gm_ref, w1_ref, w2_ref, gf_ref, o_ref, y_scr):
    f32, bf = jnp.float32, jnp.bfloat16
    chunk_rows = ys_ref.shape[1]
    for s in range(SSM_CHUNK):
        for blk in range(N_LANE_BLOCKS):
            piece = ys_ref[blk, :, s * LANES:(s + 1) * LANES].astype(f32)
            y_scr[blk, pl.ds(s, chunk_rows, stride=SSM_CHUNK), :] = piece
    y = jnp.concatenate([y_scr[blk] for blk in range(N_LANE_BLOCKS)], axis=-1)
    glu_arg = jnp.dot(y.astype(bf), wglu_ref[...], preferred_element_type=f32) + bglu_ref[...]
    y_ssm = (y * jax.nn.sigmoid(glu_arg)).astype(bf)
    up_s = jnp.dot(y_ssm, wus_ref[...], preferred_element_type=f32)
    up_a = jnp.dot(ya_ref[...], wua_ref[...], preferred_element_type=f32)
    g_s = gates_ref[:, 0:D_MODEL].astype(f32)
    g_a = gates_ref[:, D_MODEL:2 * D_MODEL].astype(f32)
    merged = (g_s * up_s + g_a * up_a).astype(bf)
    h = x_ref[...] + jnp.dot(merged, wout_ref[...], preferred_element_type=f32)
    n = _rmsnorm_f32(h, gm_ref[...]).astype(bf)
    for c in range(D_FF // FF_CHUNK):
        cols = slice(c * FF_CHUNK, (c + 1) * FF_CHUNK)
        hid = jnp.maximum(jnp.dot(n, w1_ref[:, cols], preferred_element_type=f32), 0.0)
        h = h + jnp.dot((hid * hid).astype(bf), w2_ref[cols, :], preferred_element_type=f32)
    o_ref[...] = _rmsnorm_f32(h, gf_ref[...])


def _merge_call(x2, ys, ya, gates, wglu, bglu, wus, wua, wout, gm, w1, w2, gf):
    tokens = x2.shape[0]
    rows = MERGE_ROWS
    const = lambda i: (0, 0)

    def resident(arr):
        return pl.BlockSpec(arr.shape, const, pipeline_mode=pl.Buffered(1))

    return pl.pallas_call(
        _merge_kernel,
        grid=(tokens // rows,),
        in_specs=[
            pl.BlockSpec((rows, D_MODEL), lambda i: (i, 0)),
            pl.BlockSpec((N_LANE_BLOCKS, rows // SSM_CHUNK, CHUNK_COLS), lambda i: (0, i, 0)),
            pl.BlockSpec((rows, D_ATTN), lambda i: (i, 0)),
            pl.BlockSpec((rows, 2 * D_MODEL), lambda i: (i, 0)),
            resident(wglu), resident(bglu), resident(wus), resident(wua), resident(wout),
            resident(gm), resident(w1), resident(w2), resident(gf),
        ],
        out_specs=pl.BlockSpec((rows, D_MODEL), lambda i: (i, 0)),
        out_shape=jax.ShapeDtypeStruct((tokens, D_MODEL), jnp.float32),
        scratch_shapes=[pltpu.VMEM((N_LANE_BLOCKS, rows, LANES), jnp.float32)],
        compiler_params=pltpu.CompilerParams(
            dimension_semantics=("arbitrary",), vmem_limit_bytes=VMEM_LIMIT),
        name="merge_mlp",
    )(x2, ys, ya, gates, wglu, bglu, wus, wua, wout, gm, w1, w2, gf)


def _suffix_sum_matrix():
    blk = ATT_BLOCK
    r = jnp.arange(blk)[:, None]
    c = jnp.arange(blk + LANES)[None, :]
    one = jnp.where((c >= blk) | (r > c), 1.0, 0.0).astype(jnp.bfloat16)
    return jnp.concatenate([one, one], axis=0)


def kernel(x, norm_mix, w_in, A_re, A_im, log_dt, B_re, B_im, C_re, C_im, D_skip, w_glu, b_glu,
           w_up_ssm, w_up_attn, w_gate, b_gate, w_out, norm_mlp, w_ff1, w_ff2, norm_final):
    bsz, seq, _ = x.shape
    tokens = bsz * seq
    bf = jnp.bfloat16
    assert norm_mix.shape[0] == 1, "single layer"
    assert seq % (SSM_CHUNK * 8) == 0 and seq % ATT_BLOCK == 0
    assert tokens % PROJ_ROWS == 0 and tokens % MERGE_ROWS == 0 and seq % min(PROJ_ROWS, seq) == 0

    x2 = x.reshape(tokens, D_MODEL)
    wkt = w_in[0][:, D_SSM + D_ATTN:D_SSM + 2 * D_ATTN].T.astype(bf)
    ussm, q, kt, v, gates = _inproj_call(x2, norm_mix, w_in[0].astype(bf), wkt,
                                         w_gate[0].astype(bf), b_gate, bsz, seq)

    toep, inj, ro, a_tab, d_tab = _ssm_tables(
        A_re[0], A_im[0], log_dt[0], B_re[0], B_im[0], C_re[0], C_im[0], D_skip[0])
    e_inj, e_ro = _expansion_matrices()
    ys = _ssm_call(ussm, toep, inj, ro, e_inj, e_ro, a_tab, d_tab, bsz)

    ya = _attn_call(q.reshape(bsz, seq, D_ATTN), kt, v.reshape(bsz, seq, D_ATTN),
                    _suffix_sum_matrix()).reshape(tokens, D_ATTN)

    out = _merge_call(x2, ys, ya, gates, w_glu[0].astype(bf), b_glu, w_up_ssm[0].astype(bf),
                      w_up_attn[0].astype(bf), w_out[0].astype(bf), norm_mlp,
                      w_ff1[0].astype(bf), w_ff2[0].astype(bf), norm_final.reshape(1, D_MODEL))
    return out.reshape(bsz, seq, D_MODEL)
```

```python
import jax
import jax.numpy as jnp
from jax import lax
from jax.experimental import pallas as pl
from jax.experimental.pallas import tpu as pltpu

D_MODEL = 1024
D_SSM = 512
SSM_GROUP = 16
N_GROUPS = 32
STATE = 64
N_HEADS = 8
HEAD_DIM = 64
D_ATTN = 512
D_FF = 4096
EPS = 1e-6

LANES = 128
MXU_TILE = 256
VMEM_LIMIT = 52 * 1024 * 1024

SSM_CHUNK = 16
N_LANE_BLOCKS = D_SSM // LANES
GROUPS_PER_BLOCK = LANES // SSM_GROUP
STATE_COLS = 2 * GROUPS_PER_BLOCK * STATE
CHUNK_COLS = SSM_CHUNK * LANES
ATT_BLOCK = 128
ATT_SKIP_SUM = 64.0
ATT_MASKED = 1e30
PROJ_ROWS = 512
MERGE_ROWS = 512
FF_CHUNK = 1024


def _rmsnorm_f32(x, g):
    ms = jnp.mean(x * x, axis=-1, keepdims=True)
    return x * lax.rsqrt(ms + EPS) * g


def _inproj_kernel(x_ref, g_ref, w_ref, wkt_ref, wg_ref, bg_ref,
                   ussm_ref, q_ref, kt_ref, v_ref, gates_ref, pssm_scr):
    x = x_ref[...]
    u = _rmsnorm_f32(x, g_ref[...]).astype(jnp.bfloat16)
    rows = x.shape[0]
    chunk_rows = rows // SSM_CHUNK

    p_ssm = jnp.dot(u, w_ref[:, 0:D_SSM], preferred_element_type=jnp.float32)
    for blk in range(N_LANE_BLOCKS):
        pssm_scr[blk] = p_ssm[:, blk * LANES:(blk + 1) * LANES]
    for s in range(SSM_CHUNK):
        for blk in range(N_LANE_BLOCKS):
            piece = pssm_scr[blk, pl.ds(s, chunk_rows, stride=SSM_CHUNK), :]
            ussm_ref[blk, :, s * LANES:(s + 1) * LANES] = piece.astype(jnp.bfloat16)

    p_q = jnp.dot(u, w_ref[:, D_SSM:D_SSM + D_ATTN], preferred_element_type=jnp.float32)
    q_ref[...] = (p_q * (HEAD_DIM ** -0.5)).astype(jnp.bfloat16)

    kt = lax.dot_general(wkt_ref[...], u, (((1,), (1,)), ((), ())),
                         preferred_element_type=jnp.float32).astype(jnp.bfloat16)
    for c in range(rows // ATT_BLOCK):
        kt_ref[0, c] = kt[:, c * ATT_BLOCK:(c + 1) * ATT_BLOCK]

    p_v = jnp.dot(u, w_ref[:, D_SSM + 2 * D_ATTN:D_SSM + 3 * D_ATTN],
                  preferred_element_type=jnp.float32)
    v_ref[...] = p_v.astype(jnp.bfloat16)

    for c in range(2 * D_MODEL // 512):
        pg = jnp.dot(u, wg_ref[:, c * 512:(c + 1) * 512], preferred_element_type=jnp.float32)
        pg = pg + bg_ref[:, c * 512:(c + 1) * 512]
        gates_ref[:, c * 512:(c + 1) * 512] = jax.nn.sigmoid(pg).astype(jnp.bfloat16)


def _inproj_call(x2, g, w_in, wkt, w_gate, bg, bsz, seq):
    tokens = bsz * seq
    rows = min(PROJ_ROWS, seq)
    tiles_per_seq = seq // rows
    n_kb = rows // ATT_BLOCK
    const = lambda i: (0, 0)
    return pl.pallas_call(
        _inproj_kernel,
        grid=(tokens // rows,),
        in_specs=[
            pl.BlockSpec((rows, D_MODEL), lambda i: (i, 0)),
            pl.BlockSpec((1, D_MODEL), const),
            pl.BlockSpec(w_in.shape, const),
            pl.BlockSpec(wkt.shape, const),
            pl.BlockSpec(w_gate.shape, const),
            pl.BlockSpec((1, 2 * D_MODEL), const),
        ],
        out_specs=[
            pl.BlockSpec((N_LANE_BLOCKS, rows // SSM_CHUNK, CHUNK_COLS), lambda i: (0, i, 0)),
            pl.BlockSpec((rows, D_ATTN), lambda i: (i, 0)),
            pl.BlockSpec((1, n_kb, D_ATTN, ATT_BLOCK),
                         lambda i: (i // tiles_per_seq, i % tiles_per_seq, 0, 0)),
            pl.BlockSpec((rows, D_ATTN), lambda i: (i, 0)),
            pl.BlockSpec((rows, 2 * D_MODEL), lambda i: (i, 0)),
        ],
        out_shape=[
            jax.ShapeDtypeStruct((N_LANE_BLOCKS, tokens // SSM_CHUNK, CHUNK_COLS), jnp.bfloat16),
            jax.ShapeDtypeStruct((tokens, D_ATTN), jnp.bfloat16),
            jax.ShapeDtypeStruct((bsz, seq // ATT_BLOCK, D_ATTN, ATT_BLOCK), jnp.bfloat16),
            jax.ShapeDtypeStruct((tokens, D_ATTN), jnp.bfloat16),
            jax.ShapeDtypeStruct((tokens, 2 * D_MODEL), jnp.bfloat16),
        ],
        scratch_shapes=[pltpu.VMEM((N_LANE_BLOCKS, rows, LANES), jnp.float32)],
        compiler_params=pltpu.CompilerParams(
            dimension_semantics=("arbitrary",), vmem_limit_bytes=VMEM_LIMIT),
        name="inproj",
    )(x2, g, w_in, wkt, w_gate, bg)


def _ssm_tables(A_re, A_im, log_dt, B_re, B_im, C_re, C_im, D_skip):
    f32 = jnp.float32
    L = SSM_CHUNK
    nb, gb = N_LANE_BLOCKS, GROUPS_PER_BLOCK
    ar, ai = A_re.astype(f32), A_im.astype(f32)
    dt = jnp.exp(log_dt.astype(f32))[:, None]
    tau = jnp.arange(L + 1, dtype=f32)[:, None, None]
    mag = jnp.exp(ar[None] * dt[None] * tau)
    ang = ai[None] * dt[None] * tau
    pw_re, pw_im = mag * jnp.cos(ang), mag * jnp.sin(ang)
    num_re, num_im = pw_re[1] - 1.0, pw_im[1]
    den = ar * ar + ai * ai
    cf_re = (num_re * ar + num_im * ai) / den
    cf_im = (num_im * ar - num_re * ai) / den
    br, bi = B_re.astype(f32), B_im.astype(f32)
    bb_re = cf_re[..., None] * br - cf_im[..., None] * bi
    bb_im = cf_re[..., None] * bi + cf_im[..., None] * br
    ab_re = pw_re[:L, :, :, None] * bb_re[None] - pw_im[:L, :, :, None] * bb_im[None]
    ab_im = pw_re[:L, :, :, None] * bb_im[None] + pw_im[:L, :, :, None] * bb_re[None]
    cr, ci = C_re.astype(f32), C_im.astype(f32)

    kern = jnp.einsum('gdp,tgpc->tgcd', cr, ab_re) - jnp.einsum('gdp,tgpc->tgcd', ci, ab_im)
    kern = kern.reshape(L, nb, LANES, SSM_GROUP).transpose(1, 0, 2, 3)
    lane_idx = jnp.arange(LANES)
    spread = (jnp.arange(SSM_GROUP)[:, None] == lane_idx[None, :] % SSM_GROUP).astype(f32)
    same_group = lane_idx[:, None] // SSM_GROUP == lane_idx[None, :] // SSM_GROUP
    toep = jnp.where(same_group, jnp.matmul(kern, spread), 0.0)

    pw_cat = jnp.concatenate([pw_re[:L], pw_im[:L]], axis=-1)[::-1]
    pw_swp = jnp.concatenate([pw_im[:L], pw_re[:L]], axis=-1)[::-1]
    bbt_re, bbt_im = bb_re.transpose(0, 2, 1), bb_im.transpose(0, 2, 1)
    b_same = jnp.concatenate([bbt_re, bbt_re], axis=-1)
    b_cross = jnp.concatenate([-bbt_im, bbt_im], axis=-1)
    inj = pw_cat[:, :, None, :] * b_same[None] + pw_swp[:, :, None, :] * b_cross[None]
    inj = inj.reshape(L, nb, LANES, 2 * STATE).transpose(1, 0, 2, 3).reshape(nb, CHUNK_COLS, 2 * STATE)

    ro_re = cr[None] * pw_re[1:L + 1, :, None, :] - ci[None] * pw_im[1:L + 1, :, None, :]
    ro_im = cr[None] * pw_im[1:L + 1, :, None, :] + ci[None] * pw_re[1:L + 1, :, None, :]
    ro = jnp.stack([ro_re, -ro_im], axis=0)
    ro = ro.reshape(2, L, nb, gb, SSM_GROUP, STATE).transpose(2, 0, 3, 5, 1, 4)
    ro = ro.reshape(nb, STATE_COLS, L * SSM_GROUP)

    a_tab = jnp.stack([pw_re[L].reshape(nb, gb * STATE), pw_im[L].reshape(nb, gb * STATE)], axis=1)
    d_tab = jnp.tile(D_skip.astype(f32).reshape(nb, 1, LANES), (1, 1, L))
    bf = jnp.bfloat16
    return toep.astype(bf), inj.astype(bf), ro.astype(bf), a_tab, d_tab


def _expansion_matrices():
    gb = GROUPS_PER_BLOCK
    src = jnp.arange(2 * STATE)[:, None]
    dst = jnp.arange(STATE_COLS)[None, :]
    e_inj = (src // STATE == dst // (gb * STATE)) & (src % STATE == dst % STATE)
    src = jnp.arange(SSM_CHUNK * SSM_GROUP)[:, None]
    dst = jnp.arange(CHUNK_COLS)[None, :]
    e_ro = (src // SSM_GROUP == dst // LANES) & (src % SSM_GROUP == dst % SSM_GROUP)
    return e_inj.astype(jnp.bfloat16), e_ro.astype(jnp.bfloat16)


def _ssm_kernel(x_ref, toep_ref, inj_ref, ro_ref, einj_ref, ero_ref, a_ref, d_ref, y_ref,
                m_scr, p_scr, q_scr, z_scr, hp_scr):
    L = SSM_CHUNK
    gb = GROUPS_PER_BLOCK
    half = STATE_COLS // 2
    n_rows = x_ref.shape[1]

    @pl.when(pl.program_id(1) == 0)
    def _():
        m_scr[...] = jnp.zeros_like(m_scr)
        for s in range(L):
            for t in range(s, L):
                m_scr[s * LANES:(s + 1) * LANES, t * LANES:(t + 1) * LANES] = toep_ref[0, t - s]
        def group_of(shape, axis, width):
            idx = lax.broadcasted_iota(jnp.int32, shape, axis)
            return lax.shift_right_logical(idx, width.bit_length() - 1) & (gb - 1)

        row_g = group_of((CHUNK_COLS, STATE_COLS), 0, SSM_GROUP)
        col_h = group_of((CHUNK_COLS, STATE_COLS), 1, STATE)
        p_full = jnp.dot(inj_ref[0], einj_ref[...], preferred_element_type=jnp.float32)
        p_scr[...] = jnp.where(row_g == col_h, p_full, 0.0).astype(jnp.bfloat16)
        row_g = group_of((STATE_COLS, CHUNK_COLS), 0, STATE)
        col_h = group_of((STATE_COLS, CHUNK_COLS), 1, SSM_GROUP)
        q_full = jnp.dot(ro_ref[0], ero_ref[...], preferred_element_type=jnp.float32)
        q_scr[...] = jnp.where(row_g == col_h, q_full, 0.0).astype(jnp.bfloat16)

    x = x_ref[0]
    z_scr[...] = jnp.dot(x, p_scr[...], preferred_element_type=jnp.float32)

    a_re = a_ref[0, 0:1, :]
    a_im = a_ref[0, 1:2, :]

    def step(k, h):
        h_re, h_im = h
        hp_scr[pl.ds(k, 1), 0:half] = h_re
        hp_scr[pl.ds(k, 1), half:STATE_COLS] = h_im
        z_re = z_scr[pl.ds(k, 1), 0:half]
        z_im = z_scr[pl.ds(k, 1), half:STATE_COLS]
        return (a_re * h_re - a_im * h_im + z_re, a_re * h_im + a_im * h_re + z_im)

    zero = jnp.zeros((1, half), jnp.float32)
    lax.fori_loop(0, n_rows, step, (zero, zero))

    hp = hp_scr[...].astype(jnp.bfloat16)
    for n in range(CHUNK_COLS // MXU_TILE):
        lo, hi = n * MXU_TILE, (n + 1) * MXU_TILE
        y = jnp.dot(x[:, 0:hi], m_scr[0:hi, lo:hi], preferred_element_type=jnp.float32)
        y = y + jnp.dot(hp, q_scr[:, lo:hi], preferred_element_type=jnp.float32)
        y = y + d_ref[0, :, lo:hi] * x[:, lo:hi].astype(jnp.float32)
        y_ref[0, :, lo:hi] = jax.nn.gelu(y).astype(jnp.bfloat16)


def _ssm_call(xc, toep, inj, ro, e_inj, e_ro, a_tab, d_tab, bsz):
    nb, total_rows, _ = xc.shape
    n_rows = total_rows // bsz
    per_q = lambda q, b: (q, 0, 0)
    const = lambda q, b: (0, 0)
    return pl.pallas_call(
        _ssm_kernel,
        grid=(nb, bsz),
        in_specs=[
            pl.BlockSpec((1, n_rows, CHUNK_COLS), lambda q, b: (q, b, 0)),
            pl.BlockSpec((1, SSM_CHUNK, LANES, LANES), lambda q, b: (q, 0, 0, 0)),
            pl.BlockSpec((1, CHUNK_COLS, 2 * STATE), per_q),
            pl.BlockSpec((1, STATE_COLS, SSM_CHUNK * SSM_GROUP), per_q),
            pl.BlockSpec(e_inj.shape, const),
            pl.BlockSpec(e_ro.shape, const),
            pl.BlockSpec((1, 2, STATE_COLS // 2), per_q),
            pl.BlockSpec((1, 1, CHUNK_COLS), per_q),
        ],
        out_specs=pl.BlockSpec((1, n_rows, CHUNK_COLS), lambda q, b: (q, b, 0)),
        out_shape=jax.ShapeDtypeStruct(xc.shape, jnp.bfloat16),
        scratch_shapes=[
            pltpu.VMEM((CHUNK_COLS, CHUNK_COLS), jnp.bfloat16),
            pltpu.VMEM((CHUNK_COLS, STATE_COLS), jnp.bfloat16),
            pltpu.VMEM((STATE_COLS, CHUNK_COLS), jnp.bfloat16),
            pltpu.VMEM((n_rows, STATE_COLS), jnp.float32),
            pltpu.VMEM((n_rows, STATE_COLS), jnp.float32),
        ],
        compiler_params=pltpu.CompilerParams(
            dimension_semantics=("arbitrary", "arbitrary"), vmem_limit_bytes=VMEM_LIMIT),
        name="ssm",
    )(xc, toep, inj, ro, e_inj, e_ro, a_tab, d_tab)


def _attn_kernel(q_ref, kt_ref, v_ref, tri_ref, o_ref, qm_scr, carry_scr, acc_scr):
    blk = ATT_BLOCK
    i = pl.program_id(1)
    n_pairs = N_HEADS // 2
    f32, bf = jnp.float32, jnp.bfloat16

    lane = lax.broadcasted_iota(jnp.int32, (blk, LANES), 1)
    first_head = lane < HEAD_DIM
    for pair in range(n_pairs):
        q_pair = q_ref[0, :, pair * LANES:(pair + 1) * LANES]
        zero = jnp.zeros_like(q_pair)
        qm_scr[2 * pair * blk:(2 * pair + 1) * blk] = jnp.where(first_head, q_pair, zero)
        qm_scr[(2 * pair + 1) * blk:(2 * pair + 2) * blk] = jnp.where(first_head, zero, q_pair)

    tri = tri_ref[...]

    def scores(j):
        return jnp.concatenate(
            [jnp.dot(qm_scr[h * blk:(h + 2) * blk],
                     kt_ref[0, j, (h // 2) * LANES:(h // 2 + 1) * LANES, :],
                     preferred_element_type=f32) for h in range(0, N_HEADS, 2)], axis=0)

    def stick(z, later):
        sp = jnp.maximum(z, 0.0) + jnp.log(1.0 + jnp.exp(-jnp.abs(z)))
        sp_hi = sp.astype(bf)
        sp_lo = (sp - sp_hi.astype(f32)).astype(bf)
        cs = jnp.dot(jnp.concatenate([sp_hi, sp_lo], axis=1), tri, preferred_element_type=f32)
        log_w = z - sp - cs[:, 0:blk]
        if later is not None:
            log_w = log_w - later
        return jnp.exp(log_w).astype(bf), cs[:, blk:blk + LANES]

    def weighted_values(w, j, pair):
        ks = pl.multiple_of(j * blk, blk)
        o2 = jnp.dot(w[2 * pair * blk:(2 * pair + 2) * blk],
                     v_ref[0, pl.ds(ks, blk), pair * LANES:(pair + 1) * LANES],
                     preferred_element_type=f32)
        return jnp.where(first_head, o2[0:blk], o2[blk:2 * blk])

    row = lax.broadcasted_iota(jnp.int32, (blk, blk), 0)
    col = lax.broadcasted_iota(jnp.int32, (blk, blk), 1)
    pen_diag = jnp.where(col < row, 0.0, ATT_MASKED)
    z_diag = (scores(i).reshape(N_HEADS, blk, blk) - pen_diag[None]).reshape(N_HEADS * blk, blk)
    w_diag, sum_diag = stick(z_diag, None)
    j_prev = jnp.maximum(i - 1, 0)
    z_prev = scores(j_prev) - jnp.where(i >= 1, 0.0, ATT_MASKED)
    w_prev, sum_prev = stick(z_prev, sum_diag)
    carry = sum_diag + sum_prev
    carry_scr[...] = carry
    for pair in range(n_pairs):
        acc_scr[:, pair * LANES:(pair + 1) * LANES] = (
            weighted_values(w_diag, i, pair) + weighted_values(w_prev, j_prev, pair))

    def body(state):
        j, _ = state
        carry = carry_scr[...]
        w, sum_j = stick(scores(j), carry)
        carry = carry + sum_j
        carry_scr[...] = carry
        for pair in range(n_pairs):
            acc_scr[:, pair * LANES:(pair + 1) * LANES] += weighted_values(w, j, pair)
        return j - 1, jnp.min(carry)

    def cond(state):
        j, least = state
        return jnp.logical_and(j >= 0, least < ATT_SKIP_SUM)

    lax.while_loop(cond, body, (i - 2, jnp.min(carry)))
    o_ref[0] = acc_scr[...].astype(o_ref.dtype)


def _attn_call(q3, kt4, v3, tri):
    bsz, seq, _ = q3.shape
    blk = ATT_BLOCK
    return pl.pallas_call(
        _attn_kernel,
        grid=(bsz, seq // blk),
        in_specs=[
            pl.BlockSpec((1, blk, D_ATTN), lambda b, i: (b, i, 0)),
            pl.BlockSpec((1, seq // blk, D_ATTN, blk), lambda b, i: (b, 0, 0, 0)),
            pl.BlockSpec((1, seq, D_ATTN), lambda b, i: (b, 0, 0)),
            pl.BlockSpec(tri.shape, lambda b, i: (0, 0)),
        ],
        out_specs=pl.BlockSpec((1, blk, D_ATTN), lambda b, i: (b, i, 0)),
        out_shape=jax.ShapeDtypeStruct((bsz, seq, D_ATTN), jnp.bfloat16),
        scratch_shapes=[
            pltpu.VMEM((N_HEADS * blk, LANES), jnp.bfloat16),
            pltpu.VMEM((N_HEADS * blk, LANES), jnp.float32),
            pltpu.VMEM((blk, D_ATTN), jnp.float32),
        ],
        compiler_params=pltpu.CompilerParams(
            dimension_semantics=("arbitrary", "arbitrary"), vmem_limit_bytes=VMEM_LIMIT),
        name="attn",
    )(q3, kt4, v3, tri)


def _merge_kernel(x_ref, ys_ref, ya_ref, gates_ref, wglu_ref, bglu_ref, wus_ref, wua_ref,
                  wout_ref, gm_ref, w1_ref, w2_ref, gf_ref, o_ref, y_scr):
    f32, bf = jnp.float32, jnp.bfloat16
    chunk_rows = ys_ref.shape[1]
    for s in range(SSM_CHUNK):
        for blk in range(N_LANE_BLOCKS):
            piece = ys_ref[blk, :, s * LANES:(s + 1) * LANES].astype(f32)
            y_scr[blk, pl.ds(s, chunk_rows, stride=SSM_CHUNK), :] = piece
    y = jnp.concatenate([y_scr[blk] for blk in range(N_LANE_BLOCKS)], axis=-1)
    glu_arg = jnp.dot(y.astype(bf), wglu_ref[...], preferred_element_type=f32) + bglu_ref[...]
    y_ssm = (y * jax.nn.sigmoid(glu_arg)).astype(bf)
    up_s = jnp.dot(y_ssm, wus_ref[...], preferred_element_type=f32)
    up_a = jnp.dot(ya_ref[...], wua_ref[...], preferred_element_type=f32)
    g_s = gates_ref[:, 0:D_MODEL].astype(f32)
    g_a = gates_ref[:, D_MODEL:2 * D_MODEL].astype(f32)
    merged = (g_s * up_s + g_a * up_a).astype(bf)
    h = x_ref[...] + jnp.dot(merged, wout_ref[...], preferred_element_type=f32)
    n = _rmsnorm_f32(h, gm_ref[...]).astype(bf)
    for c in range(D_FF // FF_CHUNK):
        cols = slice(c * FF_CHUNK, (c + 1) * FF_CHUNK)
        hid = jnp.maximum(jnp.dot(n, w1_ref[:, cols], preferred_element_type=f32), 0.0)
        h = h + jnp.dot((hid * hid).astype(bf), w2_ref[cols, :], preferred_element_type=f32)
    o_ref[...] = _rmsnorm_f32(h, gf_ref[...])


def _merge_call(x2, ys, ya, gates, wglu, bglu, wus, wua, wout, gm, w1, w2, gf):
    tokens = x2.shape[0]
    rows = MERGE_ROWS
    const = lambda i: (0, 0)

    def resident(arr):
        return pl.BlockSpec(arr.shape, const, pipeline_mode=pl.Buffered(1))

    return pl.pallas_call(
        _merge_kernel,
        grid=(tokens // rows,),
        in_specs=[
            pl.BlockSpec((rows, D_MODEL), lambda i: (i, 0)),
            pl.BlockSpec((N_LANE_BLOCKS, rows // SSM_CHUNK, CHUNK_COLS), lambda i: (0, i, 0)),
            pl.BlockSpec((rows, D_ATTN), lambda i: (i, 0)),
            pl.BlockSpec((rows, 2 * D_MODEL), lambda i: (i, 0)),
            resident(wglu), resident(bglu), resident(wus), resident(wua), resident(wout),
            resident(gm), resident(w1), resident(w2), resident(gf),
        ],
        out_specs=pl.BlockSpec((rows, D_MODEL), lambda i: (i, 0)),
        out_shape=jax.ShapeDtypeStruct((tokens, D_MODEL), jnp.float32),
        scratch_shapes=[pltpu.VMEM((N_LANE_BLOCKS, rows, LANES), jnp.float32)],
        compiler_params=pltpu.CompilerParams(
            dimension_semantics=("arbitrary",), vmem_limit_bytes=VMEM_LIMIT),
        name="merge_mlp",
    )(x2, ys, ya, gates, wglu, bglu, wus, wua, wout, gm, w1, w2, gf)


def _suffix_sum_matrix():
    blk = ATT_BLOCK
    r = jnp.arange(blk)[:, None]
    c = jnp.arange(blk + LANES)[None, :]
    one = jnp.where((c >= blk) | (r > c), 1.0, 0.0).astype(jnp.bfloat16)
    return jnp.concatenate([one, one], axis=0)


def kernel(x, norm_mix, w_in, A_re, A_im, log_dt, B_re, B_im, C_re, C_im, D_skip, w_glu, b_glu,
           w_up_ssm, w_up_attn, w_gate, b_gate, w_out, norm_mlp, w_ff1, w_ff2, norm_final):
    bsz, seq, _ = x.shape
    tokens = bsz * seq
    bf = jnp.bfloat16
    assert norm_mix.shape[0] == 1, "single layer"
    assert seq % (SSM_CHUNK * 8) == 0 and seq % ATT_BLOCK == 0
    assert tokens % PROJ_ROWS == 0 and tokens % MERGE_ROWS == 0 and seq % min(PROJ_ROWS, seq) == 0

    x2 = x.reshape(tokens, D_MODEL)
    wkt = w_in[0][:, D_SSM + D_ATTN:D_SSM + 2 * D_ATTN].T.astype(bf)
    ussm, q, kt, v, gates = _inproj_call(x2, norm_mix, w_in[0].astype(bf), wkt,
                                         w_gate[0].astype(bf), b_gate, bsz, seq)

    toep, inj, ro, a_tab, d_tab = _ssm_tables(
        A_re[0], A_im[0], log_dt[0], B_re[0], B_im[0], C_re[0], C_im[0], D_skip[0])
    e_inj, e_ro = _expansion_matrices()
    ys = _ssm_call(ussm, toep, inj, ro, e_inj, e_ro, a_tab, d_tab, bsz)

    ya = _attn_call(q.reshape(bsz, seq, D_ATTN), kt, v.reshape(bsz, seq, D_ATTN),
                    _suffix_sum_matrix()).reshape(tokens, D_ATTN)

    out = _merge_call(x2, ys, ya, gates, w_glu[0].astype(bf), b_glu, w_up_ssm[0].astype(bf),
                      w_up_attn[0].astype(bf), w_out[0].astype(bf), norm_mlp,
                      w_ff1[0].astype(bf), w_ff2[0].astype(bf), norm_final.reshape(1, D_MODEL))
    return out.reshape(bsz, seq, D_MODEL)
```

```python
import functools

import jax
import jax.numpy as jnp
from jax import lax
from jax.experimental import pallas as pl
from jax.experimental.pallas import tpu as pltpu

D_MODEL = 1024
D_SSM = 512
SSM_GROUP = 16
N_GROUPS = 32
STATE = 64
N_HEADS = 8
HEAD_DIM = 64
D_ATTN = 512
D_FF = 4096
EPS = 1e-6

LANES = 128
MXU_TILE = 256
VMEM_LIMIT = 52 * 1024 * 1024

SSM_CHUNK = 16
N_LANE_BLOCKS = D_SSM // LANES
GROUPS_PER_BLOCK = LANES // SSM_GROUP
STATE_COLS = 2 * GROUPS_PER_BLOCK * STATE
CHUNK_COLS = SSM_CHUNK * LANES
ATT_BLOCK = 128
ATT_SKIP_SUM = 64.0
ATT_MASKED = 1e30
LOG2_E = 1.4426950408889634
PROJ_ROWS = 512
MERGE_ROWS = 512
FF_CHUNK = 1024
ATT_ROWS = 512


def _rmsnorm_f32(x, g):
    ms = jnp.mean(x * x, axis=-1, keepdims=True)
    return x * lax.rsqrt(ms + EPS) * g


def _inproj_kernel(x_ref, g_ref, w_ref, wkt_ref, wg_ref, bg_ref,
                   ussm_ref, q_ref, kt_ref, v_ref, gates_ref, pssm_scr):
    x = x_ref[...]
    u = _rmsnorm_f32(x, g_ref[...]).astype(jnp.bfloat16)
    rows = x.shape[0]
    chunk_rows = rows // SSM_CHUNK

    p_ssm = jnp.dot(u, w_ref[:, 0:D_SSM], preferred_element_type=jnp.float32)
    for blk in range(N_LANE_BLOCKS):
        pssm_scr[blk] = p_ssm[:, blk * LANES:(blk + 1) * LANES]
    for s in range(SSM_CHUNK):
        for blk in range(N_LANE_BLOCKS):
            piece = pssm_scr[blk, pl.ds(s, chunk_rows, stride=SSM_CHUNK), :]
            ussm_ref[blk, :, s * LANES:(s + 1) * LANES] = piece.astype(jnp.bfloat16)

    p_q = jnp.dot(u, w_ref[:, D_SSM:D_SSM + D_ATTN], preferred_element_type=jnp.float32)
    q_ref[...] = (p_q * (HEAD_DIM ** -0.5)).astype(jnp.bfloat16)

    kt = lax.dot_general(wkt_ref[...], u, (((1,), (1,)), ((), ())),
                         preferred_element_type=jnp.float32).astype(jnp.bfloat16)
    for c in range(rows // ATT_BLOCK):
        kt_ref[0, c] = kt[:, c * ATT_BLOCK:(c + 1) * ATT_BLOCK]

    p_v = jnp.dot(u, w_ref[:, D_SSM + 2 * D_ATTN:D_SSM + 3 * D_ATTN],
                  preferred_element_type=jnp.float32)
    v_ref[...] = p_v.astype(jnp.bfloat16)

    for c in range(2 * D_MODEL // 512):
        pg = jnp.dot(u, wg_ref[:, c * 512:(c + 1) * 512], preferred_element_type=jnp.float32)
        pg = pg + bg_ref[:, c * 512:(c + 1) * 512]
        gates_ref[:, c * 512:(c + 1) * 512] = jax.nn.sigmoid(pg).astype(jnp.bfloat16)


def _inproj_call(x2, g, w_in, wkt, w_gate, bg, bsz, seq):
    tokens = bsz * seq
    rows = min(PROJ_ROWS, seq)
    tiles_per_seq = seq // rows
    n_kb = rows // ATT_BLOCK
    const = lambda i: (0, 0)
    return pl.pallas_call(
        _inproj_kernel,
        grid=(tokens // rows,),
        in_specs=[
            pl.BlockSpec((rows, D_MODEL), lambda i: (i, 0)),
            pl.BlockSpec((1, D_MODEL), const),
            pl.BlockSpec(w_in.shape, const),
            pl.BlockSpec(wkt.shape, const),
            pl.BlockSpec(w_gate.shape, const),
            pl.BlockSpec((1, 2 * D_MODEL), const),
        ],
        out_specs=[
            pl.BlockSpec((N_LANE_BLOCKS, rows // SSM_CHUNK, CHUNK_COLS), lambda i: (0, i, 0)),
            pl.BlockSpec((rows, D_ATTN), lambda i: (i, 0)),
            pl.BlockSpec((1, n_kb, D_ATTN, ATT_BLOCK),
                         lambda i: (i // tiles_per_seq, i % tiles_per_seq, 0, 0)),
            pl.BlockSpec((rows, D_ATTN), lambda i: (i, 0)),
            pl.BlockSpec((rows, 2 * D_MODEL), lambda i: (i, 0)),
        ],
        out_shape=[
            jax.ShapeDtypeStruct((N_LANE_BLOCKS, tokens // SSM_CHUNK, CHUNK_COLS), jnp.bfloat16),
            jax.ShapeDtypeStruct((tokens, D_ATTN), jnp.bfloat16),
            jax.ShapeDtypeStruct((bsz, seq // ATT_BLOCK, D_ATTN, ATT_BLOCK), jnp.bfloat16),
            jax.ShapeDtypeStruct((tokens, D_ATTN), jnp.bfloat16),
            jax.ShapeDtypeStruct((tokens, 2 * D_MODEL), jnp.bfloat16),
        ],
        scratch_shapes=[pltpu.VMEM((N_LANE_BLOCKS, rows, LANES), jnp.float32)],
        compiler_params=pltpu.CompilerParams(
            dimension_semantics=("arbitrary",), vmem_limit_bytes=VMEM_LIMIT),
        name="inproj",
    )(x2, g, w_in, wkt, w_gate, bg)


def _ssm_tables(A_re, A_im, log_dt, B_re, B_im, C_re, C_im, D_skip):
    f32 = jnp.float32
    L = SSM_CHUNK
    nb, gb = N_LANE_BLOCKS, GROUPS_PER_BLOCK
    ar, ai = A_re.astype(f32), A_im.astype(f32)
    dt = jnp.exp(log_dt.astype(f32))[:, None]
    tau = jnp.arange(L + 1, dtype=f32)[:, None, None]
    mag = jnp.exp(ar[None] * dt[None] * tau)
    ang = ai[None] * dt[None] * tau
    pw_re, pw_im = mag * jnp.cos(ang), mag * jnp.sin(ang)
    num_re, num_im = pw_re[1] - 1.0, pw_im[1]
    den = ar * ar + ai * ai
    cf_re = (num_re * ar + num_im * ai) / den
    cf_im = (num_im * ar - num_re * ai) / den
    br, bi = B_re.astype(f32), B_im.astype(f32)
    bb_re = cf_re[..., None] * br - cf_im[..., None] * bi
    bb_im = cf_re[..., None] * bi + cf_im[..., None] * br
    ab_re = pw_re[:L, :, :, None] * bb_re[None] - pw_im[:L, :, :, None] * bb_im[None]
    ab_im = pw_re[:L, :, :, None] * bb_im[None] + pw_im[:L, :, :, None] * bb_re[None]
    cr, ci = C_re.astype(f32), C_im.astype(f32)

    kern = jnp.einsum('gdp,tgpc->tgcd', cr, ab_re) - jnp.einsum('gdp,tgpc->tgcd', ci, ab_im)
    kern = kern.reshape(L, nb, LANES, SSM_GROUP).transpose(1, 0, 2, 3)
    lane_idx = jnp.arange(LANES)
    spread = (jnp.arange(SSM_GROUP)[:, None] == lane_idx[None, :] % SSM_GROUP).astype(f32)
    same_group = lane_idx[:, None] // SSM_GROUP == lane_idx[None, :] // SSM_GROUP
    toep = jnp.where(same_group, jnp.matmul(kern, spread), 0.0)

    pw_cat = jnp.concatenate([pw_re[:L], pw_im[:L]], axis=-1)[::-1]
    pw_swp = jnp.concatenate([pw_im[:L], pw_re[:L]], axis=-1)[::-1]
    bbt_re, bbt_im = bb_re.transpose(0, 2, 1), bb_im.transpose(0, 2, 1)
    b_same = jnp.concatenate([bbt_re, bbt_re], axis=-1)
    b_cross = jnp.concatenate([-bbt_im, bbt_im], axis=-1)
    inj = pw_cat[:, :, None, :] * b_same[None] + pw_swp[:, :, None, :] * b_cross[None]
    inj = inj.reshape(L, nb, LANES, 2 * STATE).transpose(1, 0, 2, 3).reshape(nb, CHUNK_COLS, 2 * STATE)

    ro_re = cr[None] * pw_re[1:L + 1, :, None, :] - ci[None] * pw_im[1:L + 1, :, None, :]
    ro_im = cr[None] * pw_im[1:L + 1, :, None, :] + ci[None] * pw_re[1:L + 1, :, None, :]
    ro = jnp.stack([ro_re, -ro_im], axis=0)
    ro = ro.reshape(2, L, nb, gb, SSM_GROUP, STATE).transpose(2, 0, 3, 5, 1, 4)
    ro = ro.reshape(nb, STATE_COLS, L * SSM_GROUP)

    a_tab = jnp.stack([pw_re[L].reshape(nb, gb * STATE), pw_im[L].reshape(nb, gb * STATE)], axis=1)
    d_tab = jnp.tile(D_skip.astype(f32).reshape(nb, 1, LANES), (1, 1, L))
    bf = jnp.bfloat16
    return toep.astype(bf), inj.astype(bf), ro.astype(bf), a_tab, d_tab


def _expansion_matrices():
    gb = GROUPS_PER_BLOCK
    src = jnp.arange(2 * STATE)[:, None]
    dst = jnp.arange(STATE_COLS)[None, :]
    e_inj = (src // STATE == dst // (gb * STATE)) & (src % STATE == dst % STATE)
    src = jnp.arange(SSM_CHUNK * SSM_GROUP)[:, None]
    dst = jnp.arange(CHUNK_COLS)[None, :]
    e_ro = (src // SSM_GROUP == dst // LANES) & (src % SSM_GROUP == dst % SSM_GROUP)
    return e_inj.astype(jnp.bfloat16), e_ro.astype(jnp.bfloat16)


def _ssm_kernel(x_ref, toep_ref, inj_ref, ro_ref, einj_ref, ero_ref, a_ref, d_ref, y_ref,
                m_scr, p_scr, q_scr, z_scr, hp_scr):
    L = SSM_CHUNK
    gb = GROUPS_PER_BLOCK
    half = STATE_COLS // 2
    n_rows = x_ref.shape[1]

    @pl.when(pl.program_id(1) == 0)
    def _():
        m_scr[...] = jnp.zeros_like(m_scr)
        for s in range(L):
            for t in range(s, L):
                m_scr[s * LANES:(s + 1) * LANES, t * LANES:(t + 1) * LANES] = toep_ref[0, t - s]
        def group_of(shape, axis, width):
            idx = lax.broadcasted_iota(jnp.int32, shape, axis)
            return lax.shift_right_logical(idx, width.bit_length() - 1) & (gb - 1)

        row_g = group_of((CHUNK_COLS, STATE_COLS), 0, SSM_GROUP)
        col_h = group_of((CHUNK_COLS, STATE_COLS), 1, STATE)
        p_full = jnp.dot(inj_ref[0], einj_ref[...], preferred_element_type=jnp.float32)
        p_scr[...] = jnp.where(row_g == col_h, p_full, 0.0).astype(jnp.bfloat16)
        row_g = group_of((STATE_COLS, CHUNK_COLS), 0, STATE)
        col_h = group_of((STATE_COLS, CHUNK_COLS), 1, SSM_GROUP)
        q_full = jnp.dot(ro_ref[0], ero_ref[...], preferred_element_type=jnp.float32)
        q_scr[...] = jnp.where(row_g == col_h, q_full, 0.0).astype(jnp.bfloat16)

    x = x_ref[0]
    z_scr[...] = jnp.dot(x, p_scr[...], preferred_element_type=jnp.float32)

    a_re = a_ref[0, 0:1, :]
    a_im = a_ref[0, 1:2, :]

    def step(k, h):
        h_re, h_im = h
        hp_scr[pl.ds(k, 1), 0:half] = h_re
        hp_scr[pl.ds(k, 1), half:STATE_COLS] = h_im
        z_re = z_scr[pl.ds(k, 1), 0:half]
        z_im = z_scr[pl.ds(k, 1), half:STATE_COLS]
        return (a_re * h_re - a_im * h_im + z_re, a_re * h_im + a_im * h_re + z_im)

    zero = jnp.zeros((1, half), jnp.float32)
    lax.fori_loop(0, n_rows, step, (zero, zero))

    hp = hp_scr[...].astype(jnp.bfloat16)
    for n in range(CHUNK_COLS // MXU_TILE):
        lo, hi = n * MXU_TILE, (n + 1) * MXU_TILE
        y = jnp.dot(x[:, 0:hi], m_scr[0:hi, lo:hi], preferred_element_type=jnp.float32)
        y = y + jnp.dot(hp, q_scr[:, lo:hi], preferred_element_type=jnp.float32)
        y = y + d_ref[0, :, lo:hi] * x[:, lo:hi].astype(jnp.float32)
        y_ref[0, :, lo:hi] = jax.nn.gelu(y).astype(jnp.bfloat16)


def _ssm_call(xc, toep, inj, ro, e_inj, e_ro, a_tab, d_tab, bsz):
    nb, total_rows, _ = xc.shape
    n_rows = total_rows // bsz
    per_q = lambda q, b: (q, 0, 0)
    const = lambda q, b: (0, 0)
    return pl.pallas_call(
        _ssm_kernel,
        grid=(nb, bsz),
        in_specs=[
            pl.BlockSpec((1, n_rows, CHUNK_COLS), lambda q, b: (q, b, 0)),
            pl.BlockSpec((1, SSM_CHUNK, LANES, LANES), lambda q, b: (q, 0, 0, 0)),
            pl.BlockSpec((1, CHUNK_COLS, 2 * STATE), per_q),
            pl.BlockSpec((1, STATE_COLS, SSM_CHUNK * SSM_GROUP), per_q),
            pl.BlockSpec(e_inj.shape, const),
            pl.BlockSpec(e_ro.shape, const),
            pl.BlockSpec((1, 2, STATE_COLS // 2), per_q),
            pl.BlockSpec((1, 1, CHUNK_COLS), per_q),
        ],
        out_specs=pl.BlockSpec((1, n_rows, CHUNK_COLS), lambda q, b: (q, b, 0)),
        out_shape=jax.ShapeDtypeStruct(xc.shape, jnp.bfloat16),
        scratch_shapes=[
            pltpu.VMEM((CHUNK_COLS, CHUNK_COLS), jnp.bfloat16),
            pltpu.VMEM((CHUNK_COLS, STATE_COLS), jnp.bfloat16),
            pltpu.VMEM((STATE_COLS, CHUNK_COLS), jnp.bfloat16),
            pltpu.VMEM((n_rows, STATE_COLS), jnp.float32),
            pltpu.VMEM((n_rows, STATE_COLS), jnp.float32),
        ],
        compiler_params=pltpu.CompilerParams(
            dimension_semantics=("arbitrary", "arbitrary"), vmem_limit_bytes=VMEM_LIMIT),
        name="ssm",
    )(xc, toep, inj, ro, e_inj, e_ro, a_tab, d_tab)


def _merge_mlp_stages(x_ref, ys_ref, ya, gates_ref, wglu_ref, bglu_ref, wus_ref, wua_ref,
                      wout_ref, gm_ref, w1_ref, w2_ref, gf_ref, o_ref, y_scr):
    f32, bf = jnp.float32, jnp.bfloat16
    chunk_rows = ys_ref.shape[1]
    st = {}

    def mix():
        for s in range(SSM_CHUNK):
            for blk in range(N_LANE_BLOCKS):
                piece = ys_ref[blk, :, s * LANES:(s + 1) * LANES].astype(f32)
                y_scr[blk, pl.ds(s, chunk_rows, stride=SSM_CHUNK), :] = piece
        y = jnp.concatenate([y_scr[blk] for blk in range(N_LANE_BLOCKS)], axis=-1)
        glu_arg = jnp.dot(y.astype(bf), wglu_ref[...], preferred_element_type=f32) + bglu_ref[...]
        y_ssm = (y * jax.nn.sigmoid(glu_arg)).astype(bf)
        up_s = jnp.dot(y_ssm, wus_ref[...], preferred_element_type=f32)
        up_a = jnp.dot(ya, wua_ref[...], preferred_element_type=f32)
        g_s = gates_ref[:, 0:D_MODEL].astype(f32)
        g_a = gates_ref[:, D_MODEL:2 * D_MODEL].astype(f32)
        st["merged"] = (g_s * up_s + g_a * up_a).astype(bf)

    def project_out():
        h = x_ref[...] + jnp.dot(st["merged"], wout_ref[...], preferred_element_type=f32)
        st["h"] = h
        st["n"] = _rmsnorm_f32(h, gm_ref[...]).astype(bf)

    def mlp_chunk(c):
        cols = slice(c * FF_CHUNK, (c + 1) * FF_CHUNK)
        hid = jnp.maximum(jnp.dot(st["n"], w1_ref[:, cols], preferred_element_type=f32), 0.0)
        st["h"] = st["h"] + jnp.dot((hid * hid).astype(bf), w2_ref[cols, :],
                                    preferred_element_type=f32)

    def finish():
        o_ref[...] = _rmsnorm_f32(st["h"], gf_ref[...])

    chunks = [functools.partial(mlp_chunk, c) for c in range(D_FF // FF_CHUNK)]
    return [mix, project_out] + chunks + [finish]


def _attn_kernel(q_ref, kt_ref, v_ref, tri_ref, o_ref, qm_scr, carry_scr, acc_scr):
    blk = ATT_BLOCK
    n_qb = q_ref.shape[1] // blk
    n_pairs = N_HEADS // 2
    head_rows = N_HEADS * blk
    f32, bf = jnp.float32, jnp.bfloat16
    first_qb = pl.program_id(1) * n_qb

    lane = lax.broadcasted_iota(jnp.int32, (blk, LANES), 1)
    first_head = lane < HEAD_DIM
    tri = tri_ref[...]

    for c in range(n_qb):
        for pair in range(n_pairs):
            q_pair = q_ref[0, c * blk:(c + 1) * blk, pair * LANES:(pair + 1) * LANES]
            zero = jnp.zeros_like(q_pair)
            lo = c * head_rows + 2 * pair * blk
            qm_scr[lo:lo + blk] = jnp.where(first_head, q_pair, zero)
            qm_scr[lo + blk:lo + 2 * blk] = jnp.where(first_head, zero, q_pair)

    def scores(c, j):
        base = c * head_rows
        return jnp.concatenate(
            [jnp.dot(qm_scr[base + 2 * p * blk:base + (2 * p + 2) * blk],
                     kt_ref[0, j, p * LANES:(p + 1) * LANES, :],
                     preferred_element_type=f32) for p in range(n_pairs)], axis=0)

    def stick(z, later):
        sp = jnp.maximum(z, 0.0) + jnp.log(1.0 + jnp.exp2(jnp.abs(z) * (-LOG2_E)))
        sp_hi = sp.astype(bf)
        sp_lo = (sp - sp_hi.astype(f32)).astype(bf)
        cs = jnp.dot(jnp.concatenate([sp_hi, sp_lo], axis=1), tri, preferred_element_type=f32)
        log_w = z - sp - cs[:, 0:blk]
        if later is not None:
            log_w = log_w - later
        return jnp.exp2(log_w * LOG2_E).astype(bf), cs[:, blk:blk + LANES]

    def weighted_values(w, j, pair):
        ks = pl.multiple_of(j * blk, blk)
        o2 = jnp.dot(w, v_ref[0, pl.ds(ks, blk), pair * LANES:(pair + 1) * LANES],
                     preferred_element_type=f32)
        return jnp.where(first_head, o2[0:blk], o2[blk:2 * blk])

    row = lax.broadcasted_iota(jnp.int32, (blk, blk), 0)
    col = lax.broadcasted_iota(jnp.int32, (blk, blk), 1)
    pen_diag = jnp.where(col < row, 0.0, ATT_MASKED)
    j_diag = [first_qb + c for c in range(n_qb)]
    j_prev = [jnp.maximum(first_qb - 1, 0)] + j_diag[:-1]
    z_diag = jnp.concatenate([scores(c, j_diag[c]) for c in range(n_qb)], axis=0)
    z_diag = (z_diag.reshape(n_qb * N_HEADS, blk, blk) - pen_diag[None]).reshape(n_qb * head_rows, blk)
    w_diag, sum_diag = stick(z_diag, None)
    z_prev = jnp.concatenate(
        [scores(0, j_prev[0]) - jnp.where(first_qb >= 1, 0.0, ATT_MASKED)]
        + [scores(c, j_prev[c]) for c in range(1, n_qb)], axis=0)
    w_prev, sum_prev = stick(z_prev, sum_diag)
    carry = sum_diag + sum_prev
    carry_scr[...] = carry
    least = [jnp.min(carry[c * head_rows:(c + 1) * head_rows]) for c in range(n_qb)]
    for c in range(n_qb):
        for pair in range(n_pairs):
            rows_w = slice(c * head_rows + 2 * pair * blk, c * head_rows + (2 * pair + 2) * blk)
            acc_scr[c * blk:(c + 1) * blk, pair * LANES:(pair + 1) * LANES] = (
                weighted_values(w_diag[rows_w], j_diag[c], pair)
                + weighted_values(w_prev[rows_w], j_prev[c], pair))

    for c in range(n_qb):
        rows_c = slice(c * blk, (c + 1) * blk)
        base = c * head_rows

        def body(state, c=c, rows_c=rows_c, base=base):
            j, _ = state
            carry = carry_scr[base:base + head_rows]
            w, sum_j = stick(scores(c, j), carry)
            carry = carry + sum_j
            carry_scr[base:base + head_rows] = carry
            for pair in range(n_pairs):
                acc_scr[rows_c, pair * LANES:(pair + 1) * LANES] += weighted_values(
                    w[2 * pair * blk:(2 * pair + 2) * blk], j, pair)
            return j - 1, jnp.min(carry)

        def cond(state):
            j, least_c = state
            return jnp.logical_and(j >= 0, least_c < ATT_SKIP_SUM)

        lax.while_loop(cond, body, (first_qb + (c - 2), least[c]))

    o_ref[0] = acc_scr[...].astype(o_ref.dtype)


def _attn_call(q3, kt4, v3, tri):
    bsz, seq, _ = q3.shape
    blk = ATT_BLOCK
    rows = ATT_ROWS
    n_qb = rows // blk
    return pl.pallas_call(
        _attn_kernel,
        grid=(bsz, seq // rows),
        in_specs=[
            pl.BlockSpec((1, rows, D_ATTN), lambda b, i: (b, i, 0)),
            pl.BlockSpec((1, seq // blk, D_ATTN, blk), lambda b, i: (b, 0, 0, 0)),
            pl.BlockSpec((1, seq, D_ATTN), lambda b, i: (b, 0, 0)),
            pl.BlockSpec(tri.shape, lambda b, i: (0, 0)),
        ],
        out_specs=pl.BlockSpec((1, rows, D_ATTN), lambda b, i: (b, i, 0)),
        out_shape=jax.ShapeDtypeStruct((bsz, seq, D_ATTN), jnp.bfloat16),
        scratch_shapes=[
            pltpu.VMEM((n_qb * N_HEADS * blk, LANES), jnp.bfloat16),
            pltpu.VMEM((n_qb * N_HEADS * blk, LANES), jnp.float32),
            pltpu.VMEM((rows, D_ATTN), jnp.float32),
        ],
        compiler_params=pltpu.CompilerParams(
            dimension_semantics=("arbitrary", "arbitrary"), vmem_limit_bytes=VMEM_LIMIT),
        name="attn",
    )(q3, kt4, v3, tri)


def _merge_kernel(x_ref, ys_ref, ya_ref, gates_ref, wglu_ref, bglu_ref, wus_ref, wua_ref,
                  wout_ref, gm_ref, w1_ref, w2_ref, gf_ref, o_ref, y_scr):
    for stage in _merge_mlp_stages(x_ref, ys_ref, ya_ref[...], gates_ref, wglu_ref, bglu_ref,
                                   wus_ref, wua_ref, wout_ref, gm_ref, w1_ref, w2_ref, gf_ref,
                                   o_ref, y_scr):
        stage()


def _merge_call(x2, ys, ya, gates, wglu, bglu, wus, wua, wout, gm, w1, w2, gf):
    tokens = x2.shape[0]
    rows = MERGE_ROWS
    const = lambda i: (0, 0)

    def resident(arr):
        return pl.BlockSpec(arr.shape, const, pipeline_mode=pl.Buffered(1))

    return pl.pallas_call(
        _merge_kernel,
        grid=(tokens // rows,),
        in_specs=[
            pl.BlockSpec((rows, D_MODEL), lambda i: (i, 0)),
            pl.BlockSpec((N_LANE_BLOCKS, rows // SSM_CHUNK, CHUNK_COLS), lambda i: (0, i, 0)),
            pl.BlockSpec((rows, D_ATTN), lambda i: (i, 0)),
            pl.BlockSpec((rows, 2 * D_MODEL), lambda i: (i, 0)),
            resident(wglu), resident(bglu), resident(wus), resident(wua), resident(wout),
            resident(gm), resident(w1), resident(w2), resident(gf),
        ],
        out_specs=pl.BlockSpec((rows, D_MODEL), lambda i: (i, 0)),
        out_shape=jax.ShapeDtypeStruct((tokens, D_MODEL), jnp.float32),
        scratch_shapes=[pltpu.VMEM((N_LANE_BLOCKS, rows, LANES), jnp.float32)],
        compiler_params=pltpu.CompilerParams(
            dimension_semantics=("arbitrary",), vmem_limit_bytes=VMEM_LIMIT),
        name="merge_mlp",
    )(x2, ys, ya, gates, wglu, bglu, wus, wua, wout, gm, w1, w2, gf)


def _suffix_sum_matrix():
    blk = ATT_BLOCK
    r = jnp.arange(blk)[:, None]
    c = jnp.arange(blk + LANES)[None, :]
    one = jnp.where((c >= blk) | (r > c), 1.0, 0.0).astype(jnp.bfloat16)
    return jnp.concatenate([one, one], axis=0)


def kernel(x, norm_mix, w_in, A_re, A_im, log_dt, B_re, B_im, C_re, C_im, D_skip, w_glu, b_glu,
           w_up_ssm, w_up_attn, w_gate, b_gate, w_out, norm_mlp, w_ff1, w_ff2, norm_final):
    bsz, seq, _ = x.shape
    tokens = bsz * seq
    bf = jnp.bfloat16
    assert norm_mix.shape[0] == 1, "single layer"
    assert seq % (SSM_CHUNK * 8) == 0 and seq % ATT_BLOCK == 0
    assert tokens % PROJ_ROWS == 0 and tokens % MERGE_ROWS == 0 and seq % min(PROJ_ROWS, seq) == 0

    x2 = x.reshape(tokens, D_MODEL)
    wkt = w_in[0][:, D_SSM + D_ATTN:D_SSM + 2 * D_ATTN].T.astype(bf)
    ussm, q, kt, v, gates = _inproj_call(x2, norm_mix, w_in[0].astype(bf), wkt,
                                         w_gate[0].astype(bf), b_gate, bsz, seq)

    toep, inj, ro, a_tab, d_tab = _ssm_tables(
        A_re[0], A_im[0], log_dt[0], B_re[0], B_im[0], C_re[0], C_im[0], D_skip[0])
    e_inj, e_ro = _expansion_matrices()
    ys = _ssm_call(ussm, toep, inj, ro, e_inj, e_ro, a_tab, d_tab, bsz)

    ya = _attn_call(q.reshape(bsz, seq, D_ATTN), kt, v.reshape(bsz, seq, D_ATTN),
                    _suffix_sum_matrix()).reshape(tokens, D_ATTN)

    out = _merge_call(x2, ys, ya, gates, w_glu[0].astype(bf), b_glu, w_up_ssm[0].astype(bf),
                      w_up_attn[0].astype(bf), w_out[0].astype(bf), norm_mlp,
                      w_ff1[0].astype(bf), w_ff2[0].astype(bf), norm_final.reshape(1, D_MODEL))
    return out.reshape(bsz, seq, D_MODEL)
```

```python
import functools

import jax
import jax.numpy as jnp
from jax import lax
from jax.experimental import pallas as pl
from jax.experimental.pallas import tpu as pltpu

D_MODEL = 1024
D_SSM = 512
SSM_GROUP = 16
N_GROUPS = 32
STATE = 64
N_HEADS = 8
HEAD_DIM = 64
D_ATTN = 512
D_FF = 4096
EPS = 1e-6

LANES = 128
MXU_TILE = 256
VMEM_LIMIT = 52 * 1024 * 1024

SSM_CHUNK = 16
N_LANE_BLOCKS = D_SSM // LANES
GROUPS_PER_BLOCK = LANES // SSM_GROUP
STATE_COLS = 2 * GROUPS_PER_BLOCK * STATE
CHUNK_COLS = SSM_CHUNK * LANES
ATT_BLOCK = 128
ATT_SKIP_SUM = 64.0
ATT_MASKED = 1e30
LOG2_E = 1.4426950408889634
PROJ_ROWS = 512
MERGE_ROWS = 512
FF_CHUNK = 1024
ATT_ROWS = 512


def _rmsnorm_f32(x, g):
    ms = jnp.mean(x * x, axis=-1, keepdims=True)
    return x * lax.rsqrt(ms + EPS) * g


def _inproj_kernel(x_ref, g_ref, w_ref, wkt_ref, wg_ref, bg_ref,
                   ussm_ref, q_ref, kt_ref, v_ref, gates_ref, pssm_scr):
    x = x_ref[...]
    u = _rmsnorm_f32(x, g_ref[...]).astype(jnp.bfloat16)
    rows = x.shape[0]
    chunk_rows = rows // SSM_CHUNK

    p_ssm = jnp.dot(u, w_ref[:, 0:D_SSM], preferred_element_type=jnp.float32)
    for blk in range(N_LANE_BLOCKS):
        pssm_scr[blk] = p_ssm[:, blk * LANES:(blk + 1) * LANES]
    for s in range(SSM_CHUNK):
        for blk in range(N_LANE_BLOCKS):
            piece = pssm_scr[blk, pl.ds(s, chunk_rows, stride=SSM_CHUNK), :]
            ussm_ref[blk, :, s * LANES:(s + 1) * LANES] = piece.astype(jnp.bfloat16)

    p_q = jnp.dot(u, w_ref[:, D_SSM:D_SSM + D_ATTN], preferred_element_type=jnp.float32)
    q_ref[...] = (p_q * (HEAD_DIM ** -0.5)).astype(jnp.bfloat16)

    kt = lax.dot_general(wkt_ref[...], u, (((1,), (1,)), ((), ())),
                         preferred_element_type=jnp.float32).astype(jnp.bfloat16)
    for c in range(rows // ATT_BLOCK):
        kt_ref[0, c] = kt[:, c * ATT_BLOCK:(c + 1) * ATT_BLOCK]

    p_v = jnp.dot(u, w_ref[:, D_SSM + 2 * D_ATTN:D_SSM + 3 * D_ATTN],
                  preferred_element_type=jnp.float32)
    v_ref[...] = p_v.astype(jnp.bfloat16)

    for c in range(2 * D_MODEL // 512):
        pg = jnp.dot(u, wg_ref[:, c * 512:(c + 1) * 512], preferred_element_type=jnp.float32)
        pg = pg + bg_ref[:, c * 512:(c + 1) * 512]
        gates_ref[:, c * 512:(c + 1) * 512] = jax.nn.sigmoid(pg).astype(jnp.bfloat16)


def _inproj_call(x2, g, w_in, wkt, w_gate, bg, bsz, seq):
    tokens = bsz * seq
    rows = min(PROJ_ROWS, seq)
    tiles_per_seq = seq // rows
    n_kb = rows // ATT_BLOCK
    const = lambda i: (0, 0)
    return pl.pallas_call(
        _inproj_kernel,
        grid=(tokens // rows,),
        in_specs=[
            pl.BlockSpec((rows, D_MODEL), lambda i: (i, 0)),
            pl.BlockSpec((1, D_MODEL), const),
            pl.BlockSpec(w_in.shape, const),
            pl.BlockSpec(wkt.shape, const),
            pl.BlockSpec(w_gate.shape, const),
            pl.BlockSpec((1, 2 * D_MODEL), const),
        ],
        out_specs=[
            pl.BlockSpec((N_LANE_BLOCKS, rows // SSM_CHUNK, CHUNK_COLS), lambda i: (0, i, 0)),
            pl.BlockSpec((rows, D_ATTN), lambda i: (i, 0)),
            pl.BlockSpec((1, n_kb, D_ATTN, ATT_BLOCK),
                         lambda i: (i // tiles_per_seq, i % tiles_per_seq, 0, 0)),
            pl.BlockSpec((rows, D_ATTN), lambda i: (i, 0)),
            pl.BlockSpec((rows, 2 * D_MODEL), lambda i: (i, 0)),
        ],
        out_shape=[
            jax.ShapeDtypeStruct((N_LANE_BLOCKS, tokens // SSM_CHUNK, CHUNK_COLS), jnp.bfloat16),
            jax.ShapeDtypeStruct((tokens, D_ATTN), jnp.bfloat16),
            jax.ShapeDtypeStruct((bsz, seq // ATT_BLOCK, D_ATTN, ATT_BLOCK), jnp.bfloat16),
            jax.ShapeDtypeStruct((tokens, D_ATTN), jnp.bfloat16),
            jax.ShapeDtypeStruct((tokens, 2 * D_MODEL), jnp.bfloat16),
        ],
        scratch_shapes=[pltpu.VMEM((N_LANE_BLOCKS, rows, LANES), jnp.float32)],
        compiler_params=pltpu.CompilerParams(
            dimension_semantics=("arbitrary",), vmem_limit_bytes=VMEM_LIMIT),
        name="inproj",
    )(x2, g, w_in, wkt, w_gate, bg)


def _ssm_tables(A_re, A_im, log_dt, B_re, B_im, C_re, C_im, D_skip):
    f32 = jnp.float32
    L = SSM_CHUNK
    nb, gb = N_LANE_BLOCKS, GROUPS_PER_BLOCK
    ar, ai = A_re.astype(f32), A_im.astype(f32)
    dt = jnp.exp(log_dt.astype(f32))[:, None]
    tau = jnp.arange(L + 1, dtype=f32)[:, None, None]
    mag = jnp.exp(ar[None] * dt[None] * tau)
    ang = ai[None] * dt[None] * tau
    pw_re, pw_im = mag * jnp.cos(ang), mag * jnp.sin(ang)
    num_re, num_im = pw_re[1] - 1.0, pw_im[1]
    den = ar * ar + ai * ai
    cf_re = (num_re * ar + num_im * ai) / den
    cf_im = (num_im * ar - num_re * ai) / den
    br, bi = B_re.astype(f32), B_im.astype(f32)
    bb_re = cf_re[..., None] * br - cf_im[..., None] * bi
    bb_im = cf_re[..., None] * bi + cf_im[..., None] * br
    ab_re = pw_re[:L, :, :, None] * bb_re[None] - pw_im[:L, :, :, None] * bb_im[None]
    ab_im = pw_re[:L, :, :, None] * bb_im[None] + pw_im[:L, :, :, None] * bb_re[None]
    cr, ci = C_re.astype(f32), C_im.astype(f32)

    kern = jnp.einsum('gdp,tgpc->tgcd', cr, ab_re) - jnp.einsum('gdp,tgpc->tgcd', ci, ab_im)
    kern = kern.reshape(L, nb, LANES, SSM_GROUP).transpose(1, 0, 2, 3)
    lane_idx = jnp.arange(LANES)
    spread = (jnp.arange(SSM_GROUP)[:, None] == lane_idx[None, :] % SSM_GROUP).astype(f32)
    same_group = lane_idx[:, None] // SSM_GROUP == lane_idx[None, :] // SSM_GROUP
    toep = jnp.where(same_group, jnp.matmul(kern, spread), 0.0)

    pw_cat = jnp.concatenate([pw_re[:L], pw_im[:L]], axis=-1)[::-1]
    pw_swp = jnp.concatenate([pw_im[:L], pw_re[:L]], axis=-1)[::-1]
    bbt_re, bbt_im = bb_re.transpose(0, 2, 1), bb_im.transpose(0, 2, 1)
    b_same = jnp.concatenate([bbt_re, bbt_re], axis=-1)
    b_cross = jnp.concatenate([-bbt_im, bbt_im], axis=-1)
    inj = pw_cat[:, :, None, :] * b_same[None] + pw_swp[:, :, None, :] * b_cross[None]
    inj = inj.reshape(L, nb, LANES, 2 * STATE).transpose(1, 0, 2, 3).reshape(nb, CHUNK_COLS, 2 * STATE)

    ro_re = cr[None] * pw_re[1:L + 1, :, None, :] - ci[None] * pw_im[1:L + 1, :, None, :]
    ro_im = cr[None] * pw_im[1:L + 1, :, None, :] + ci[None] * pw_re[1:L + 1, :, None, :]
    ro = jnp.stack([ro_re, -ro_im], axis=0)
    ro = ro.reshape(2, L, nb, gb, SSM_GROUP, STATE).transpose(2, 0, 3, 5, 1, 4)
    ro = ro.reshape(nb, STATE_COLS, L * SSM_GROUP)

    a_tab = jnp.stack([pw_re[L].reshape(nb, gb * STATE), pw_im[L].reshape(nb, gb * STATE)], axis=1)
    d_tab = jnp.tile(D_skip.astype(f32).reshape(nb, 1, LANES), (1, 1, L))
    bf = jnp.bfloat16
    return toep.astype(bf), inj.astype(bf), ro.astype(bf), a_tab, d_tab


def _expansion_matrices():
    gb = GROUPS_PER_BLOCK
    src = jnp.arange(2 * STATE)[:, None]
    dst = jnp.arange(STATE_COLS)[None, :]
    e_inj = (src // STATE == dst // (gb * STATE)) & (src % STATE == dst % STATE)
    src = jnp.arange(SSM_CHUNK * SSM_GROUP)[:, None]
    dst = jnp.arange(CHUNK_COLS)[None, :]
    e_ro = (src // SSM_GROUP == dst // LANES) & (src % SSM_GROUP == dst % SSM_GROUP)
    return e_inj.astype(jnp.bfloat16), e_ro.astype(jnp.bfloat16)


def _ssm_kernel(x_ref, toep_ref, inj_ref, ro_ref, einj_ref, ero_ref, a_ref, d_ref, y_ref,
                m_scr, p_scr, q_scr, z_scr, hp_scr):
    L = SSM_CHUNK
    gb = GROUPS_PER_BLOCK
    half = STATE_COLS // 2
    n_rows = x_ref.shape[1]

    @pl.when(pl.program_id(1) == 0)
    def _():
        m_scr[...] = jnp.zeros_like(m_scr)
        for s in range(L):
            for t in range(s, L):
                m_scr[s * LANES:(s + 1) * LANES, t * LANES:(t + 1) * LANES] = toep_ref[0, t - s]
        def group_of(shape, axis, width):
            idx = lax.broadcasted_iota(jnp.int32, shape, axis)
            return lax.shift_right_logical(idx, width.bit_length() - 1) & (gb - 1)

        row_g = group_of((CHUNK_COLS, STATE_COLS), 0, SSM_GROUP)
        col_h = group_of((CHUNK_COLS, STATE_COLS), 1, STATE)
        p_full = jnp.dot(inj_ref[0], einj_ref[...], preferred_element_type=jnp.float32)
        p_scr[...] = jnp.where(row_g == col_h, p_full, 0.0).astype(jnp.bfloat16)
        row_g = group_of((STATE_COLS, CHUNK_COLS), 0, STATE)
        col_h = group_of((STATE_COLS, CHUNK_COLS), 1, SSM_GROUP)
        q_full = jnp.dot(ro_ref[0], ero_ref[...], preferred_element_type=jnp.float32)
        q_scr[...] = jnp.where(row_g == col_h, q_full, 0.0).astype(jnp.bfloat16)

    x = x_ref[0]
    z_scr[...] = jnp.dot(x, p_scr[...], preferred_element_type=jnp.float32)

    a_re = a_ref[0, 0:1, :]
    a_im = a_ref[0, 1:2, :]

    def step(k, h):
        h_re, h_im = h
        hp_scr[pl.ds(k, 1), 0:half] = h_re
        hp_scr[pl.ds(k, 1), half:STATE_COLS] = h_im
        z_re = z_scr[pl.ds(k, 1), 0:half]
        z_im = z_scr[pl.ds(k, 1), half:STATE_COLS]
        return (a_re * h_re - a_im * h_im + z_re, a_re * h_im + a_im * h_re + z_im)

    zero = jnp.zeros((1, half), jnp.float32)
    lax.fori_loop(0, n_rows, step, (zero, zero))

    hp = hp_scr[...].astype(jnp.bfloat16)
    for n in range(CHUNK_COLS // MXU_TILE):
        lo, hi = n * MXU_TILE, (n + 1) * MXU_TILE
        y = jnp.dot(x[:, 0:hi], m_scr[0:hi, lo:hi], preferred_element_type=jnp.float32)
        y = y + jnp.dot(hp, q_scr[:, lo:hi], preferred_element_type=jnp.float32)
        y = y + d_ref[0, :, lo:hi] * x[:, lo:hi].astype(jnp.float32)
        y_ref[0, :, lo:hi] = jax.nn.gelu(y).astype(jnp.bfloat16)


def _ssm_call(xc, toep, inj, ro, e_inj, e_ro, a_tab, d_tab, bsz):
    nb, total_rows, _ = xc.shape
    n_rows = total_rows // bsz
    per_q = lambda q, b: (q, 0, 0)
    const = lambda q, b: (0, 0)
    return pl.pallas_call(
        _ssm_kernel,
        grid=(nb, bsz),
        in_specs=[
            pl.BlockSpec((1, n_rows, CHUNK_COLS), lambda q, b: (q, b, 0)),
            pl.BlockSpec((1, SSM_CHUNK, LANES, LANES), lambda q, b: (q, 0, 0, 0)),
            pl.BlockSpec((1, CHUNK_COLS, 2 * STATE), per_q),
            pl.BlockSpec((1, STATE_COLS, SSM_CHUNK * SSM_GROUP), per_q),
            pl.BlockSpec(e_inj.shape, const),
            pl.BlockSpec(e_ro.shape, const),
            pl.BlockSpec((1, 2, STATE_COLS // 2), per_q),
            pl.BlockSpec((1, 1, CHUNK_COLS), per_q),
        ],
        out_specs=pl.BlockSpec((1, n_rows, CHUNK_COLS), lambda q, b: (q, b, 0)),
        out_shape=jax.ShapeDtypeStruct(xc.shape, jnp.bfloat16),
        scratch_shapes=[
            pltpu.VMEM((CHUNK_COLS, CHUNK_COLS), jnp.bfloat16),
            pltpu.VMEM((CHUNK_COLS, STATE_COLS), jnp.bfloat16),
            pltpu.VMEM((STATE_COLS, CHUNK_COLS), jnp.bfloat16),
            pltpu.VMEM((n_rows, STATE_COLS), jnp.float32),
            pltpu.VMEM((n_rows, STATE_COLS), jnp.float32),
        ],
        compiler_params=pltpu.CompilerParams(
            dimension_semantics=("arbitrary", "arbitrary"), vmem_limit_bytes=VMEM_LIMIT),
        name="ssm",
    )(xc, toep, inj, ro, e_inj, e_ro, a_tab, d_tab)


def _merge_mlp_stages(x_ref, ys_ref, ya, gates_ref, wglu_ref, bglu_ref, wus_ref, wua_ref,
                      wout_ref, gm_ref, w1_ref, w2_ref, gf_ref, o_ref, y_scr):
    f32, bf = jnp.float32, jnp.bfloat16
    chunk_rows = ys_ref.shape[1]
    st = {}

    def mix():
        for s in range(SSM_CHUNK):
            for blk in range(N_LANE_BLOCKS):
                piece = ys_ref[blk, :, s * LANES:(s + 1) * LANES].astype(f32)
                y_scr[blk, pl.ds(s, chunk_rows, stride=SSM_CHUNK), :] = piece
        y = jnp.concatenate([y_scr[blk] for blk in range(N_LANE_BLOCKS)], axis=-1)
        glu_arg = jnp.dot(y.astype(bf), wglu_ref[...], preferred_element_type=f32) + bglu_ref[...]
        y_ssm = (y * jax.nn.sigmoid(glu_arg)).astype(bf)
        up_s = jnp.dot(y_ssm, wus_ref[...], preferred_element_type=f32)
        up_a = jnp.dot(ya, wua_ref[...], preferred_element_type=f32)
        g_s = gates_ref[:, 0:D_MODEL].astype(f32)
        g_a = gates_ref[:, D_MODEL:2 * D_MODEL].astype(f32)
        st["merged"] = (g_s * up_s + g_a * up_a).astype(bf)

    def project_out():
        h = x_ref[...] + jnp.dot(st["merged"], wout_ref[...], preferred_element_type=f32)
        st["h"] = h
        st["n"] = _rmsnorm_f32(h, gm_ref[...]).astype(bf)

    def mlp_chunk(c):
        cols = slice(c * FF_CHUNK, (c + 1) * FF_CHUNK)
        hid = jnp.maximum(jnp.dot(st["n"], w1_ref[:, cols], preferred_element_type=f32), 0.0)
        st["h"] = st["h"] + jnp.dot((hid * hid).astype(bf), w2_ref[cols, :],
                                    preferred_element_type=f32)

    def finish():
        o_ref[...] = _rmsnorm_f32(st["h"], gf_ref[...])

    chunks = [functools.partial(mlp_chunk, c) for c in range(D_FF // FF_CHUNK)]
    return [mix, project_out] + chunks + [finish]


def _attn_kernel(q_ref, kt_ref, v_ref, tri_ref, o_ref, qm_scr, carry_scr, acc_scr):
    blk = ATT_BLOCK
    n_qb = q_ref.shape[1] // blk
    n_pairs = N_HEADS // 2
    head_rows = N_HEADS * blk
    f32, bf = jnp.float32, jnp.bfloat16
    first_qb = pl.program_id(1) * n_qb

    lane = lax.broadcasted_iota(jnp.int32, (blk, LANES), 1)
    first_head = lane < HEAD_DIM
    tri = tri_ref[...]

    for c in range(n_qb):
        for pair in range(n_pairs):
            q_pair = q_ref[0, c * blk:(c + 1) * blk, pair * LANES:(pair + 1) * LANES]
            zero = jnp.zeros_like(q_pair)
            lo = c * head_rows + 2 * pair * blk
            qm_scr[lo:lo + blk] = jnp.where(first_head, q_pair, zero)
            qm_scr[lo + blk:lo + 2 * blk] = jnp.where(first_head, zero, q_pair)

    def scores(c, j):
        base = c * head_rows
        return jnp.concatenate(
            [jnp.dot(qm_scr[base + 2 * p * blk:base + (2 * p + 2) * blk],
                     kt_ref[0, j, p * LANES:(p + 1) * LANES, :],
                     preferred_element_type=f32) for p in range(n_pairs)], axis=0)

    def stick(z, later):
        sp = jnp.maximum(z, 0.0) + jnp.log(1.0 + jnp.exp2(jnp.abs(z) * (-LOG2_E)))
        cs = jnp.dot(sp.astype(bf), tri, preferred_element_type=f32)
        log_w = z - sp - cs[:, 0:blk]
        if later is not None:
            log_w = log_w - later
        return jnp.exp2(log_w * LOG2_E).astype(bf), cs[:, blk:blk + LANES]

    def weighted_values(w, j, pair):
        ks = pl.multiple_of(j * blk, blk)
        o2 = jnp.dot(w, v_ref[0, pl.ds(ks, blk), pair * LANES:(pair + 1) * LANES],
                     preferred_element_type=f32)
        return jnp.where(first_head, o2[0:blk], o2[blk:2 * blk])

    row = lax.broadcasted_iota(jnp.int32, (blk, blk), 0)
    col = lax.broadcasted_iota(jnp.int32, (blk, blk), 1)
    pen_diag = jnp.where(col < row, 0.0, ATT_MASKED)
    j_diag = [first_qb + c for c in range(n_qb)]
    j_prev = [jnp.maximum(first_qb - 1, 0)] + j_diag[:-1]
    z_diag = jnp.concatenate([scores(c, j_diag[c]) for c in range(n_qb)], axis=0)
    z_diag = (z_diag.reshape(n_qb * N_HEADS, blk, blk) - pen_diag[None]).reshape(n_qb * head_rows, blk)
    w_diag, sum_diag = stick(z_diag, None)
    z_prev = jnp.concatenate(
        [scores(0, j_prev[0]) - jnp.where(first_qb >= 1, 0.0, ATT_MASKED)]
        + [scores(c, j_prev[c]) for c in range(1, n_qb)], axis=0)
    w_prev, sum_prev = stick(z_prev, sum_diag)
    carry = sum_diag + sum_prev
    carry_scr[...] = carry
    least = [jnp.min(carry[c * head_rows:(c + 1) * head_rows]) for c in range(n_qb)]
    for c in range(n_qb):
        for pair in range(n_pairs):
            rows_w = slice(c * head_rows + 2 * pair * blk, c * head_rows + (2 * pair + 2) * blk)
            acc_scr[c * blk:(c + 1) * blk, pair * LANES:(pair + 1) * LANES] = (
                weighted_values(w_diag[rows_w], j_diag[c], pair)
                + weighted_values(w_prev[rows_w], j_prev[c], pair))

    for c in range(n_qb):
        rows_c = slice(c * blk, (c + 1) * blk)
        base = c * head_rows

        def body(state, c=c, rows_c=rows_c, base=base):
            j, _ = state
            carry = carry_scr[base:base + head_rows]
            w, sum_j = stick(scores(c, j), carry)
            carry = carry + sum_j
            carry_scr[base:base + head_rows] = carry
            for pair in range(n_pairs):
                acc_scr[rows_c, pair * LANES:(pair + 1) * LANES] += weighted_values(
                    w[2 * pair * blk:(2 * pair + 2) * blk], j, pair)
            return j - 1, jnp.min(carry)

        def cond(state):
            j, least_c = state
            return jnp.logical_and(j >= 0, least_c < ATT_SKIP_SUM)

        lax.while_loop(cond, body, (first_qb + (c - 2), least[c]))

    o_ref[0] = acc_scr[...].astype(o_ref.dtype)


def _attn_call(q3, kt4, v3, tri):
    bsz, seq, _ = q3.shape
    blk = ATT_BLOCK
    rows = ATT_ROWS
    n_qb = rows // blk
    return pl.pallas_call(
        _attn_kernel,
        grid=(bsz, seq // rows),
        in_specs=[
            pl.BlockSpec((1, rows, D_ATTN), lambda b, i: (b, i, 0)),
            pl.BlockSpec((1, seq // blk, D_ATTN, blk), lambda b, i: (b, 0, 0, 0)),
            pl.BlockSpec((1, seq, D_ATTN), lambda b, i: (b, 0, 0)),
            pl.BlockSpec(tri.shape, lambda b, i: (0, 0)),
        ],
        out_specs=pl.BlockSpec((1, rows, D_ATTN), lambda b, i: (b, i, 0)),
        out_shape=jax.ShapeDtypeStruct((bsz, seq, D_ATTN), jnp.bfloat16),
        scratch_shapes=[
            pltpu.VMEM((n_qb * N_HEADS * blk, LANES), jnp.bfloat16),
            pltpu.VMEM((n_qb * N_HEADS * blk, LANES), jnp.float32),
            pltpu.VMEM((rows, D_ATTN), jnp.float32),
        ],
        compiler_params=pltpu.CompilerParams(
            dimension_semantics=("arbitrary", "arbitrary"), vmem_limit_bytes=VMEM_LIMIT),
        name="attn",
    )(q3, kt4, v3, tri)


def _merge_kernel(x_ref, ys_ref, ya_ref, gates_ref, wglu_ref, bglu_ref, wus_ref, wua_ref,
                  wout_ref, gm_ref, w1_ref, w2_ref, gf_ref, o_ref, y_scr):
    for stage in _merge_mlp_stages(x_ref, ys_ref, ya_ref[...], gates_ref, wglu_ref, bglu_ref,
                                   wus_ref, wua_ref, wout_ref, gm_ref, w1_ref, w2_ref, gf_ref,
                                   o_ref, y_scr):
        stage()


def _merge_call(x2, ys, ya, gates, wglu, bglu, wus, wua, wout, gm, w1, w2, gf):
    tokens = x2.shape[0]
    rows = MERGE_ROWS
    const = lambda i: (0, 0)

    def resident(arr):
        return pl.BlockSpec(arr.shape, const, pipeline_mode=pl.Buffered(1))

    return pl.pallas_call(
        _merge_kernel,
        grid=(tokens // rows,),
        in_specs=[
            pl.BlockSpec((rows, D_MODEL), lambda i: (i, 0)),
            pl.BlockSpec((N_LANE_BLOCKS, rows // SSM_CHUNK, CHUNK_COLS), lambda i: (0, i, 0)),
            pl.BlockSpec((rows, D_ATTN), lambda i: (i, 0)),
            pl.BlockSpec((rows, 2 * D_MODEL), lambda i: (i, 0)),
            resident(wglu), resident(bglu), resident(wus), resident(wua), resident(wout),
            resident(gm), resident(w1), resident(w2), resident(gf),
        ],
        out_specs=pl.BlockSpec((rows, D_MODEL), lambda i: (i, 0)),
        out_shape=jax.ShapeDtypeStruct((tokens, D_MODEL), jnp.float32),
        scratch_shapes=[pltpu.VMEM((N_LANE_BLOCKS, rows, LANES), jnp.float32)],
        compiler_params=pltpu.CompilerParams(
            dimension_semantics=("arbitrary",), vmem_limit_bytes=VMEM_LIMIT),
        name="merge_mlp",
    )(x2, ys, ya, gates, wglu, bglu, wus, wua, wout, gm, w1, w2, gf)


def _suffix_sum_matrix():
    blk = ATT_BLOCK
    r = jnp.arange(blk)[:, None]
    c = jnp.arange(blk + LANES)[None, :]
    return jnp.where((c >= blk) | (r > c), 1.0, 0.0).astype(jnp.bfloat16)


def kernel(x, norm_mix, w_in, A_re, A_im, log_dt, B_re, B_im, C_re, C_im, D_skip, w_glu, b_glu,
           w_up_ssm, w_up_attn, w_gate, b_gate, w_out, norm_mlp, w_ff1, w_ff2, norm_final):
    bsz, seq, _ = x.shape
    tokens = bsz * seq
    bf = jnp.bfloat16
    assert norm_mix.shape[0] == 1, "single layer"
    assert seq % (SSM_CHUNK * 8) == 0 and seq % ATT_BLOCK == 0
    assert tokens % PROJ_ROWS == 0 and tokens % MERGE_ROWS == 0 and seq % min(PROJ_ROWS, seq) == 0

    x2 = x.reshape(tokens, D_MODEL)
    wkt = w_in[0][:, D_SSM + D_ATTN:D_SSM + 2 * D_ATTN].T.astype(bf)
    ussm, q, kt, v, gates = _inproj_call(x2, norm_mix, w_in[0].astype(bf), wkt,
                                         w_gate[0].astype(bf), b_gate, bsz, seq)

    toep, inj, ro, a_tab, d_tab = _ssm_tables(
        A_re[0], A_im[0], log_dt[0], B_re[0], B_im[0], C_re[0], C_im[0], D_skip[0])
    e_inj, e_ro = _expansion_matrices()
    ys = _ssm_call(ussm, toep, inj, ro, e_inj, e_ro, a_tab, d_tab, bsz)

    ya = _attn_call(q.reshape(bsz, seq, D_ATTN), kt, v.reshape(bsz, seq, D_ATTN),
                    _suffix_sum_matrix()).reshape(tokens, D_ATTN)

    out = _merge_call(x2, ys, ya, gates, w_glu[0].astype(bf), b_glu, w_up_ssm[0].astype(bf),
                      w_up_attn[0].astype(bf), w_out[0].astype(bf), norm_mlp,
                      w_ff1[0].astype(bf), w_ff2[0].astype(bf), norm_final.reshape(1, D_MODEL))
    return out.reshape(bsz, seq, D_MODEL)
```

```python
import functools

import jax
import jax.numpy as jnp
from jax import lax
from jax.experimental import pallas as pl
from jax.experimental.pallas import tpu as pltpu

D_MODEL = 1024
D_SSM = 512
SSM_GROUP = 16
N_GROUPS = 32
STATE = 64
N_HEADS = 8
HEAD_DIM = 64
D_ATTN = 512
D_FF = 4096
EPS = 1e-6

LANES = 128
MXU_TILE = 256
VMEM_LIMIT = 52 * 1024 * 1024

SSM_CHUNK = 16
N_LANE_BLOCKS = D_SSM // LANES
GROUPS_PER_BLOCK = LANES // SSM_GROUP
STATE_COLS = 2 * GROUPS_PER_BLOCK * STATE
CHUNK_COLS = SSM_CHUNK * LANES
SCAN_POWERS = (1, 2, 4, 8)
SCAN_PAD = 8
ATT_BLOCK = 128
ATT_SKIP_SUM = 64.0
ATT_MASKED = 1e30
LOG2_E = 1.4426950408889634
PROJ_ROWS = 512
MERGE_ROWS = 512
FF_CHUNK = 1024
ATT_ROWS = 512


def _rmsnorm_f32(x, g):
    ms = jnp.mean(x * x, axis=-1, keepdims=True)
    return x * lax.rsqrt(ms + EPS) * g


def _inproj_kernel(x_ref, g_ref, w_ref, wkt_ref, wg_ref, bg_ref,
                   ussm_ref, q_ref, kt_ref, v_ref, gates_ref, pssm_scr):
    x = x_ref[...]
    u = _rmsnorm_f32(x, g_ref[...]).astype(jnp.bfloat16)
    rows = x.shape[0]
    chunk_rows = rows // SSM_CHUNK

    p_ssm = jnp.dot(u, w_ref[:, 0:D_SSM], preferred_element_type=jnp.float32)
    for blk in range(N_LANE_BLOCKS):
        pssm_scr[blk] = p_ssm[:, blk * LANES:(blk + 1) * LANES]
    for s in range(SSM_CHUNK):
        for blk in range(N_LANE_BLOCKS):
            piece = pssm_scr[blk, pl.ds(s, chunk_rows, stride=SSM_CHUNK), :]
            ussm_ref[blk, :, s * LANES:(s + 1) * LANES] = piece.astype(jnp.bfloat16)

    p_q = jnp.dot(u, w_ref[:, D_SSM:D_SSM + D_ATTN], preferred_element_type=jnp.float32)
    q_ref[...] = (p_q * (HEAD_DIM ** -0.5)).astype(jnp.bfloat16)

    kt = lax.dot_general(wkt_ref[...], u, (((1,), (1,)), ((), ())),
                         preferred_element_type=jnp.float32).astype(jnp.bfloat16)
    for c in range(rows // ATT_BLOCK):
        kt_ref[0, c] = kt[:, c * ATT_BLOCK:(c + 1) * ATT_BLOCK]

    p_v = jnp.dot(u, w_ref[:, D_SSM + 2 * D_ATTN:D_SSM + 3 * D_ATTN],
                  preferred_element_type=jnp.float32)
    v_ref[...] = p_v.astype(jnp.bfloat16)

    for c in range(2 * D_MODEL // 512):
        pg = jnp.dot(u, wg_ref[:, c * 512:(c + 1) * 512], preferred_element_type=jnp.float32)
        pg = pg + bg_ref[:, c * 512:(c + 1) * 512]
        gates_ref[:, c * 512:(c + 1) * 512] = jax.nn.sigmoid(pg).astype(jnp.bfloat16)


def _inproj_call(x2, g, w_in, wkt, w_gate, bg, bsz, seq):
    tokens = bsz * seq
    rows = min(PROJ_ROWS, seq)
    tiles_per_seq = seq // rows
    n_kb = rows // ATT_BLOCK
    const = lambda i: (0, 0)
    return pl.pallas_call(
        _inproj_kernel,
        grid=(tokens // rows,),
        in_specs=[
            pl.BlockSpec((rows, D_MODEL), lambda i: (i, 0)),
            pl.BlockSpec((1, D_MODEL), const),
            pl.BlockSpec(w_in.shape, const),
            pl.BlockSpec(wkt.shape, const),
            pl.BlockSpec(w_gate.shape, const),
            pl.BlockSpec((1, 2 * D_MODEL), const),
        ],
        out_specs=[
            pl.BlockSpec((N_LANE_BLOCKS, rows // SSM_CHUNK, CHUNK_COLS), lambda i: (0, i, 0)),
            pl.BlockSpec((rows, D_ATTN), lambda i: (i, 0)),
            pl.BlockSpec((1, n_kb, D_ATTN, ATT_BLOCK),
                         lambda i: (i // tiles_per_seq, i % tiles_per_seq, 0, 0)),
            pl.BlockSpec((rows, D_ATTN), lambda i: (i, 0)),
            pl.BlockSpec((rows, 2 * D_MODEL), lambda i: (i, 0)),
        ],
        out_shape=[
            jax.ShapeDtypeStruct((N_LANE_BLOCKS, tokens // SSM_CHUNK, CHUNK_COLS), jnp.bfloat16),
            jax.ShapeDtypeStruct((tokens, D_ATTN), jnp.bfloat16),
            jax.ShapeDtypeStruct((bsz, seq // ATT_BLOCK, D_ATTN, ATT_BLOCK), jnp.bfloat16),
            jax.ShapeDtypeStruct((tokens, D_ATTN), jnp.bfloat16),
            jax.ShapeDtypeStruct((tokens, 2 * D_MODEL), jnp.bfloat16),
        ],
        scratch_shapes=[pltpu.VMEM((N_LANE_BLOCKS, rows, LANES), jnp.float32)],
        compiler_params=pltpu.CompilerParams(
            dimension_semantics=("arbitrary",), vmem_limit_bytes=VMEM_LIMIT),
        name="inproj",
    )(x2, g, w_in, wkt, w_gate, bg)


def _ssm_tables(A_re, A_im, log_dt, B_re, B_im, C_re, C_im, D_skip):
    f32 = jnp.float32
    L = SSM_CHUNK
    nb, gb = N_LANE_BLOCKS, GROUPS_PER_BLOCK
    ar, ai = A_re.astype(f32), A_im.astype(f32)
    dt = jnp.exp(log_dt.astype(f32))[:, None]
    tau = jnp.arange(L + 1, dtype=f32)[:, None, None]
    mag = jnp.exp(ar[None] * dt[None] * tau)
    ang = ai[None] * dt[None] * tau
    pw_re, pw_im = mag * jnp.cos(ang), mag * jnp.sin(ang)
    num_re, num_im = pw_re[1] - 1.0, pw_im[1]
    den = ar * ar + ai * ai
    cf_re = (num_re * ar + num_im * ai) / den
    cf_im = (num_im * ar - num_re * ai) / den
    br, bi = B_re.astype(f32), B_im.astype(f32)
    bb_re = cf_re[..., None] * br - cf_im[..., None] * bi
    bb_im = cf_re[..., None] * bi + cf_im[..., None] * br
    ab_re = pw_re[:L, :, :, None] * bb_re[None] - pw_im[:L, :, :, None] * bb_im[None]
    ab_im = pw_re[:L, :, :, None] * bb_im[None] + pw_im[:L, :, :, None] * bb_re[None]
    cr, ci = C_re.astype(f32), C_im.astype(f32)

    kern = jnp.einsum('gdp,tgpc->tgcd', cr, ab_re) - jnp.einsum('gdp,tgpc->tgcd', ci, ab_im)
    kern = kern.reshape(L, nb, LANES, SSM_GROUP).transpose(1, 0, 2, 3)
    lane_idx = jnp.arange(LANES)
    spread = (jnp.arange(SSM_GROUP)[:, None] == lane_idx[None, :] % SSM_GROUP).astype(f32)
    same_group = lane_idx[:, None] // SSM_GROUP == lane_idx[None, :] // SSM_GROUP
    toep = jnp.where(same_group, jnp.matmul(kern, spread), 0.0)

    pw_cat = jnp.concatenate([pw_re[:L], pw_im[:L]], axis=-1)[::-1]
    pw_swp = jnp.concatenate([pw_im[:L], pw_re[:L]], axis=-1)[::-1]
    bbt_re, bbt_im = bb_re.transpose(0, 2, 1), bb_im.transpose(0, 2, 1)
    b_same = jnp.concatenate([bbt_re, bbt_re], axis=-1)
    b_cross = jnp.concatenate([-bbt_im, bbt_im], axis=-1)
    inj = pw_cat[:, :, None, :] * b_same[None] + pw_swp[:, :, None, :] * b_cross[None]
    inj = inj.reshape(L, nb, LANES, 2 * STATE).transpose(1, 0, 2, 3).reshape(nb, CHUNK_COLS, 2 * STATE)

    ro_re = cr[None] * pw_re[1:L + 1, :, None, :] - ci[None] * pw_im[1:L + 1, :, None, :]
    ro_im = cr[None] * pw_im[1:L + 1, :, None, :] + ci[None] * pw_re[1:L + 1, :, None, :]
    ro = jnp.stack([ro_re, -ro_im], axis=0)
    ro = ro.reshape(2, L, nb, gb, SSM_GROUP, STATE).transpose(2, 0, 3, 5, 1, 4)
    ro = ro.reshape(nb, STATE_COLS, L * SSM_GROUP)

    steps = (L * jnp.array(SCAN_POWERS, f32))[:, None, None]
    sc_mag = jnp.exp(ar[None] * dt[None] * steps)
    sc_ang = ai[None] * dt[None] * steps
    a_tab = jnp.stack([sc_mag * jnp.cos(sc_ang), sc_mag * jnp.sin(sc_ang)], axis=1)
    a_tab = a_tab.reshape(2 * len(SCAN_POWERS), nb, gb * STATE).transpose(1, 0, 2)
    d_tab = jnp.tile(D_skip.astype(f32).reshape(nb, 1, LANES), (1, 1, L))
    bf = jnp.bfloat16
    return toep.astype(bf), inj.astype(bf), ro.astype(bf), a_tab, d_tab


def _expansion_matrices():
    gb = GROUPS_PER_BLOCK
    src = jnp.arange(2 * STATE)[:, None]
    dst = jnp.arange(STATE_COLS)[None, :]
    e_inj = (src // STATE == dst // (gb * STATE)) & (src % STATE == dst % STATE)
    src = jnp.arange(SSM_CHUNK * SSM_GROUP)[:, None]
    dst = jnp.arange(CHUNK_COLS)[None, :]
    e_ro = (src // SSM_GROUP == dst // LANES) & (src % SSM_GROUP == dst % SSM_GROUP)
    return e_inj.astype(jnp.bfloat16), e_ro.astype(jnp.bfloat16)


def _ssm_kernel(x_ref, toep_ref, inj_ref, ro_ref, einj_ref, ero_ref, a_ref, d_ref, y_ref,
                m_scr, p_scr, q_scr, z_scr, w_scr, hp_scr):
    L = SSM_CHUNK
    gb = GROUPS_PER_BLOCK
    half = STATE_COLS // 2
    n_rows = x_ref.shape[1]

    @pl.when(pl.program_id(1) == 0)
    def _():
        m_scr[...] = jnp.zeros_like(m_scr)
        for s in range(L):
            for t in range(s, L):
                m_scr[s * LANES:(s + 1) * LANES, t * LANES:(t + 1) * LANES] = toep_ref[0, t - s]
        def group_of(shape, axis, width):
            idx = lax.broadcasted_iota(jnp.int32, shape, axis)
            return lax.shift_right_logical(idx, width.bit_length() - 1) & (gb - 1)

        row_g = group_of((CHUNK_COLS, STATE_COLS), 0, SSM_GROUP)
        col_h = group_of((CHUNK_COLS, STATE_COLS), 1, STATE)
        p_full = jnp.dot(inj_ref[0], einj_ref[...], preferred_element_type=jnp.float32)
        p_scr[...] = jnp.where(row_g == col_h, p_full, 0.0).astype(jnp.bfloat16)
        row_g = group_of((STATE_COLS, CHUNK_COLS), 0, STATE)
        col_h = group_of((STATE_COLS, CHUNK_COLS), 1, SSM_GROUP)
        q_full = jnp.dot(ro_ref[0], ero_ref[...], preferred_element_type=jnp.float32)
        q_scr[...] = jnp.where(row_g == col_h, q_full, 0.0).astype(jnp.bfloat16)

    x = x_ref[0]
    pad = SCAN_PAD
    re, im = slice(0, half), slice(half, STATE_COLS)
    for buf in (z_scr, w_scr, hp_scr):
        buf[0:pad, :] = jnp.zeros((pad, STATE_COLS), jnp.float32)
    z_scr[pad:pad + n_rows, :] = jnp.dot(x, p_scr[...], preferred_element_type=jnp.float32)

    def coef(i):
        return a_ref[0, 2 * i:2 * i + 1, :], a_ref[0, 2 * i + 1:2 * i + 2, :]

    def doubling_pass(src, dst, shift, c_re, c_im):
        s_re = src[pad - shift:pad - shift + n_rows, re]
        s_im = src[pad - shift:pad - shift + n_rows, im]
        dst[pad:pad + n_rows, re] = src[pad:pad + n_rows, re] + c_re * s_re - c_im * s_im
        dst[pad:pad + n_rows, im] = src[pad:pad + n_rows, im] + c_re * s_im + c_im * s_re

    doubling_pass(z_scr, w_scr, 1, *coef(0))
    doubling_pass(w_scr, z_scr, 2, *coef(1))
    doubling_pass(z_scr, w_scr, 4, *coef(2))
    c8_re, c8_im = coef(3)

    def tile_step(m, h):
        h_re, h_im = h
        rows8 = pl.ds(pl.multiple_of(pad + 8 * m, 8), 8)
        n_re = w_scr[rows8, re] + c8_re * h_re - c8_im * h_im
        n_im = w_scr[rows8, im] + c8_re * h_im + c8_im * h_re
        hp_scr[rows8, re] = n_re
        hp_scr[rows8, im] = n_im
        return n_re, n_im

    zero = jnp.zeros((8, half), jnp.float32)
    lax.fori_loop(0, n_rows // 8, tile_step, (zero, zero), unroll=8)

    hp = hp_scr[pad - 1:pad - 1 + n_rows, :].astype(jnp.bfloat16)
    for n in range(CHUNK_COLS // MXU_TILE):
        lo, hi = n * MXU_TILE, (n + 1) * MXU_TILE
        y = jnp.dot(x[:, 0:hi], m_scr[0:hi, lo:hi], preferred_element_type=jnp.float32)
        y = y + jnp.dot(hp, q_scr[:, lo:hi], preferred_element_type=jnp.float32)
        y = y + d_ref[0, :, lo:hi] * x[:, lo:hi].astype(jnp.float32)
        y_ref[0, :, lo:hi] = jax.nn.gelu(y).astype(jnp.bfloat16)


def _ssm_call(xc, toep, inj, ro, e_inj, e_ro, a_tab, d_tab, bsz):
    nb, total_rows, _ = xc.shape
    n_rows = total_rows // bsz
    per_q = lambda q, b: (q, 0, 0)
    const = lambda q, b: (0, 0)
    return pl.pallas_call(
        _ssm_kernel,
        grid=(nb, bsz),
        in_specs=[
            pl.BlockSpec((1, n_rows, CHUNK_COLS), lambda q, b: (q, b, 0)),
            pl.BlockSpec((1, SSM_CHUNK, LANES, LANES), lambda q, b: (q, 0, 0, 0)),
            pl.BlockSpec((1, CHUNK_COLS, 2 * STATE), per_q),
            pl.BlockSpec((1, STATE_COLS, SSM_CHUNK * SSM_GROUP), per_q),
            pl.BlockSpec(e_inj.shape, const),
            pl.BlockSpec(e_ro.shape, const),
            pl.BlockSpec((1, 2 * len(SCAN_POWERS), STATE_COLS // 2), per_q),
            pl.BlockSpec((1, 1, CHUNK_COLS), per_q),
        ],
        out_specs=pl.BlockSpec((1, n_rows, CHUNK_COLS), lambda q, b: (q, b, 0)),
        out_shape=jax.ShapeDtypeStruct(xc.shape, jnp.bfloat16),
        scratch_shapes=[
            pltpu.VMEM((CHUNK_COLS, CHUNK_COLS), jnp.bfloat16),
            pltpu.VMEM((CHUNK_COLS, STATE_COLS), jnp.bfloat16),
            pltpu.VMEM((STATE_COLS, CHUNK_COLS), jnp.bfloat16),
            pltpu.VMEM((SCAN_PAD + n_rows, STATE_COLS), jnp.float32),
            pltpu.VMEM((SCAN_PAD + n_rows, STATE_COLS), jnp.float32),
            pltpu.VMEM((SCAN_PAD + n_rows, STATE_COLS), jnp.float32),
        ],
        compiler_params=pltpu.CompilerParams(
            dimension_semantics=("arbitrary", "arbitrary"), vmem_limit_bytes=VMEM_LIMIT),
        name="ssm",
    )(xc, toep, inj, ro, e_inj, e_ro, a_tab, d_tab)


def _merge_mlp_stages(x_ref, ys_ref, ya, gates_ref, wglu_ref, bglu_ref, wus_ref, wua_ref,
                      wout_ref, gm_ref, w1_ref, w2_ref, gf_ref, o_ref, y_scr):
    f32, bf = jnp.float32, jnp.bfloat16
    chunk_rows = ys_ref.shape[1]
    st = {}

    def mix():
        for s in range(SSM_CHUNK):
            for blk in range(N_LANE_BLOCKS):
                piece = ys_ref[blk, :, s * LANES:(s + 1) * LANES].astype(f32)
                y_scr[blk, pl.ds(s, chunk_rows, stride=SSM_CHUNK), :] = piece
        y = jnp.concatenate([y_scr[blk] for blk in range(N_LANE_BLOCKS)], axis=-1)
        glu_arg = jnp.dot(y.astype(bf), wglu_ref[...], preferred_element_type=f32) + bglu_ref[...]
        y_ssm = (y * jax.nn.sigmoid(glu_arg)).astype(bf)
        up_s = jnp.dot(y_ssm, wus_ref[...], preferred_element_type=f32)
        up_a = jnp.dot(ya, wua_ref[...], preferred_element_type=f32)
        g_s = gates_ref[:, 0:D_MODEL].astype(f32)
        g_a = gates_ref[:, D_MODEL:2 * D_MODEL].astype(f32)
        st["merged"] = (g_s * up_s + g_a * up_a).astype(bf)

    def project_out():
        h = x_ref[...] + jnp.dot(st["merged"], wout_ref[...], preferred_element_type=f32)
        st["h"] = h
        st["n"] = _rmsnorm_f32(h, gm_ref[...]).astype(bf)

    def mlp_chunk(c):
        cols = slice(c * FF_CHUNK, (c + 1) * FF_CHUNK)
        hid = jnp.maximum(jnp.dot(st["n"], w1_ref[:, cols], preferred_element_type=f32), 0.0)
        st["h"] = st["h"] + jnp.dot((hid * hid).astype(bf), w2_ref[cols, :],
                                    preferred_element_type=f32)

    def finish():
        o_ref[...] = _rmsnorm_f32(st["h"], gf_ref[...])

    chunks = [functools.partial(mlp_chunk, c) for c in range(D_FF // FF_CHUNK)]
    return [mix, project_out] + chunks + [finish]


def _attn_kernel(q_ref, kt_ref, v_ref, tri_ref, o_ref, qm_scr, carry_scr, acc_scr):
    blk = ATT_BLOCK
    n_qb = q_ref.shape[1] // blk
    n_pairs = N_HEADS // 2
    head_rows = N_HEADS * blk
    f32, bf = jnp.float32, jnp.bfloat16
    first_qb = pl.program_id(1) * n_qb

    lane = lax.broadcasted_iota(jnp.int32, (blk, LANES), 1)
    first_head = lane < HEAD_DIM
    tri = tri_ref[...]

    for c in range(n_qb):
        for pair in range(n_pairs):
            q_pair = q_ref[0, c * blk:(c + 1) * blk, pair * LANES:(pair + 1) * LANES]
            zero = jnp.zeros_like(q_pair)
            lo = c * head_rows + 2 * pair * blk
            qm_scr[lo:lo + blk] = jnp.where(first_head, q_pair, zero)
            qm_scr[lo + blk:lo + 2 * blk] = jnp.where(first_head, zero, q_pair)

    def scores(c, j):
        base = c * head_rows
        return jnp.concatenate(
            [jnp.dot(qm_scr[base + 2 * p * blk:base + (2 * p + 2) * blk],
                     kt_ref[0, j, p * LANES:(p + 1) * LANES, :],
                     preferred_element_type=f32) for p in range(n_pairs)], axis=0)

    def stick(z, later):
        sp = jnp.maximum(z, 0.0) + jnp.log(1.0 + jnp.exp2(jnp.abs(z) * (-LOG2_E)))
        cs = jnp.dot(sp.astype(bf), tri, preferred_element_type=f32)
        log_w = z - sp - cs[:, 0:blk]
        if later is not None:
            log_w = log_w - later
        return jnp.exp2(log_w * LOG2_E).astype(bf), cs[:, blk:blk + LANES]

    def weighted_values(w, j, pair):
        ks = pl.multiple_of(j * blk, blk)
        o2 = jnp.dot(w, v_ref[0, pl.ds(ks, blk), pair * LANES:(pair + 1) * LANES],
                     preferred_element_type=f32)
        return jnp.where(first_head, o2[0:blk], o2[blk:2 * blk])

    row = lax.broadcasted_iota(jnp.int32, (blk, blk), 0)
    col = lax.broadcasted_iota(jnp.int32, (blk, blk), 1)
    pen_diag = jnp.where(col < row, 0.0, ATT_MASKED)
    j_diag = [first_qb + c for c in range(n_qb)]
    j_prev = [jnp.maximum(first_qb - 1, 0)] + j_diag[:-1]
    z_diag = jnp.concatenate([scores(c, j_diag[c]) for c in range(n_qb)], axis=0)
    z_diag = (z_diag.reshape(n_qb * N_HEADS, blk, blk) - pen_diag[None]).reshape(n_qb * head_rows, blk)
    w_diag, sum_diag = stick(z_diag, None)
    z_prev = jnp.concatenate(
        [scores(0, j_prev[0]) - jnp.where(first_qb >= 1, 0.0, ATT_MASKED)]
        + [scores(c, j_prev[c]) for c in range(1, n_qb)], axis=0)
    w_prev, sum_prev = stick(z_prev, sum_diag)
    carry = sum_diag + sum_prev
    carry_scr[...] = carry
    least = [jnp.min(carry[c * head_rows:(c + 1) * head_rows]) for c in range(n_qb)]
    for c in range(n_qb):
        for pair in range(n_pairs):
            rows_w = slice(c * head_rows + 2 * pair * blk, c * head_rows + (2 * pair + 2) * blk)
            acc_scr[c * blk:(c + 1) * blk, pair * LANES:(pair + 1) * LANES] = (
                weighted_values(w_diag[rows_w], j_diag[c], pair)
                + weighted_values(w_prev[rows_w], j_prev[c], pair))

    for c in range(n_qb):
        rows_c = slice(c * blk, (c + 1) * blk)
        base = c * head_rows

        def body(state, c=c, rows_c=rows_c, base=base):
            j, _ = state
            carry = carry_scr[base:base + head_rows]
            w, sum_j = stick(scores(c, j), carry)
            carry = carry + sum_j
            carry_scr[base:base + head_rows] = carry
            for pair in range(n_pairs):
                acc_scr[rows_c, pair * LANES:(pair + 1) * LANES] += weighted_values(
                    w[2 * pair * blk:(2 * pair + 2) * blk], j, pair)
            return j - 1, jnp.min(carry)

        def cond(state):
            j, least_c = state
            return jnp.logical_and(j >= 0, least_c < ATT_SKIP_SUM)

        lax.while_loop(cond, body, (first_qb + (c - 2), least[c]))

    o_ref[0] = acc_scr[...].astype(o_ref.dtype)


def _attn_call(q3, kt4, v3, tri):
    bsz, seq, _ = q3.shape
    blk = ATT_BLOCK
    rows = ATT_ROWS
    n_qb = rows // blk
    return pl.pallas_call(
        _attn_kernel,
        grid=(bsz, seq // rows),
        in_specs=[
            pl.BlockSpec((1, rows, D_ATTN), lambda b, i: (b, i, 0)),
            pl.BlockSpec((1, seq // blk, D_ATTN, blk), lambda b, i: (b, 0, 0, 0)),
            pl.BlockSpec((1, seq, D_ATTN), lambda b, i: (b, 0, 0)),
            pl.BlockSpec(tri.shape, lambda b, i: (0, 0)),
        ],
        out_specs=pl.BlockSpec((1, rows, D_ATTN), lambda b, i: (b, i, 0)),
        out_shape=jax.ShapeDtypeStruct((bsz, seq, D_ATTN), jnp.bfloat16),
        scratch_shapes=[
            pltpu.VMEM((n_qb * N_HEADS * blk, LANES), jnp.bfloat16),
            pltpu.VMEM((n_qb * N_HEADS * blk, LANES), jnp.float32),
            pltpu.VMEM((rows, D_ATTN), jnp.float32),
        ],
        compiler_params=pltpu.CompilerParams(
            dimension_semantics=("arbitrary", "arbitrary"), vmem_limit_bytes=VMEM_LIMIT),
        name="attn",
    )(q3, kt4, v3, tri)


def _merge_kernel(x_ref, ys_ref, ya_ref, gates_ref, wglu_ref, bglu_ref, wus_ref, wua_ref,
                  wout_ref, gm_ref, w1_ref, w2_ref, gf_ref, o_ref, y_scr):
    for stage in _merge_mlp_stages(x_ref, ys_ref, ya_ref[...], gates_ref, wglu_ref, bglu_ref,
                                   wus_ref, wua_ref, wout_ref, gm_ref, w1_ref, w2_ref, gf_ref,
                                   o_ref, y_scr):
        stage()


def _merge_call(x2, ys, ya, gates, wglu, bglu, wus, wua, wout, gm, w1, w2, gf):
    tokens = x2.shape[0]
    rows = MERGE_ROWS
    const = lambda i: (0, 0)

    def resident(arr):
        return pl.BlockSpec(arr.shape, const, pipeline_mode=pl.Buffered(1))

    return pl.pallas_call(
        _merge_kernel,
        grid=(tokens // rows,),
        in_specs=[
            pl.BlockSpec((rows, D_MODEL), lambda i: (i, 0)),
            pl.BlockSpec((N_LANE_BLOCKS, rows // SSM_CHUNK, CHUNK_COLS), lambda i: (0, i, 0)),
            pl.BlockSpec((rows, D_ATTN), lambda i: (i, 0)),
            pl.BlockSpec((rows, 2 * D_MODEL), lambda i: (i, 0)),
            resident(wglu), resident(bglu), resident(wus), resident(wua), resident(wout),
            resident(gm), resident(w1), resident(w2), resident(gf),
        ],
        out_specs=pl.BlockSpec((rows, D_MODEL), lambda i: (i, 0)),
        out_shape=jax.ShapeDtypeStruct((tokens, D_MODEL), jnp.float32),
        scratch_shapes=[pltpu.VMEM((N_LANE_BLOCKS, rows, LANES), jnp.float32)],
        compiler_params=pltpu.CompilerParams(
            dimension_semantics=("arbitrary",), vmem_limit_bytes=VMEM_LIMIT),
        name="merge_mlp",
    )(x2, ys, ya, gates, wglu, bglu, wus, wua, wout, gm, w1, w2, gf)


def _suffix_sum_matrix():
    blk = ATT_BLOCK
    r = jnp.arange(blk)[:, None]
    c = jnp.arange(blk + LANES)[None, :]
    return jnp.where((c >= blk) | (r > c), 1.0, 0.0).astype(jnp.bfloat16)


def kernel(x, norm_mix, w_in, A_re, A_im, log_dt, B_re, B_im, C_re, C_im, D_skip, w_glu, b_glu,
           w_up_ssm, w_up_attn, w_gate, b_gate, w_out, norm_mlp, w_ff1, w_ff2, norm_final):
    bsz, seq, _ = x.shape
    tokens = bsz * seq
    bf = jnp.bfloat16
    assert norm_mix.shape[0] == 1, "single layer"
    assert seq % (SSM_CHUNK * 8) == 0 and seq % ATT_BLOCK == 0
    assert tokens % PROJ_ROWS == 0 and tokens % MERGE_ROWS == 0 and seq % min(PROJ_ROWS, seq) == 0

    x2 = x.reshape(tokens, D_MODEL)
    wkt = w_in[0][:, D_SSM + D_ATTN:D_SSM + 2 * D_ATTN].T.astype(bf)
    ussm, q, kt, v, gates = _inproj_call(x2, norm_mix, w_in[0].astype(bf), wkt,
                                         w_gate[0].astype(bf), b_gate, bsz, seq)

    toep, inj, ro, a_tab, d_tab = _ssm_tables(
        A_re[0], A_im[0], log_dt[0], B_re[0], B_im[0], C_re[0], C_im[0], D_skip[0])
    e_inj, e_ro = _expansion_matrices()
    ys = _ssm_call(ussm, toep, inj, ro, e_inj, e_ro, a_tab, d_tab, bsz)

    ya = _attn_call(q.reshape(bsz, seq, D_ATTN), kt, v.reshape(bsz, seq, D_ATTN),
                    _suffix_sum_matrix()).reshape(tokens, D_ATTN)

    out = _merge_call(x2, ys, ya, gates, w_glu[0].astype(bf), b_glu, w_up_ssm[0].astype(bf),
                      w_up_attn[0].astype(bf), w_out[0].astype(bf), norm_mlp,
                      w_ff1[0].astype(bf), w_ff2[0].astype(bf), norm_final.reshape(1, D_MODEL))
    return out.reshape(bsz, seq, D_MODEL)
```

```python
import functools

import jax
import jax.numpy as jnp
from jax import lax
from jax.experimental import pallas as pl
from jax.experimental.pallas import tpu as pltpu

D_MODEL = 1024
D_SSM = 512
SSM_GROUP = 16
N_GROUPS = 32
STATE = 64
N_HEADS = 8
HEAD_DIM = 64
D_ATTN = 512
D_FF = 4096
EPS = 1e-6

LANES = 128
MXU_TILE = 256
VMEM_LIMIT = 52 * 1024 * 1024

SSM_CHUNK = 16
N_LANE_BLOCKS = D_SSM // LANES
GROUPS_PER_BLOCK = LANES // SSM_GROUP
STATE_COLS = 2 * GROUPS_PER_BLOCK * STATE
CHUNK_COLS = SSM_CHUNK * LANES
HALF_LANES = LANES // 2
HALF_COLS = SSM_CHUNK * HALF_LANES
HALF_STATE = STATE_COLS // 2
SCAN_POWERS = (1, 2, 4, 8)
SCAN_PAD = 8
ATT_BLOCK = 128
ATT_SKIP_SUM = 64.0
ATT_MASKED = 1e30
LOG2_E = 1.4426950408889634
PROJ_ROWS = 512
MERGE_ROWS = 512
FF_CHUNK = 1024
ATT_ROWS = 512


def _rmsnorm_f32(x, g):
    ms = jnp.mean(x * x, axis=-1, keepdims=True)
    return x * lax.rsqrt(ms + EPS) * g


def _inproj_kernel(x_ref, g_ref, w_ref, wkt_ref, wg_ref, bg_ref,
                   ussm_ref, q_ref, kt_ref, v_ref, gates_ref, pssm_scr):
    x = x_ref[...]
    u = _rmsnorm_f32(x, g_ref[...]).astype(jnp.bfloat16)
    rows = x.shape[0]
    chunk_rows = rows // SSM_CHUNK

    p_ssm = jnp.dot(u, w_ref[:, 0:D_SSM], preferred_element_type=jnp.float32)
    for blk in range(N_LANE_BLOCKS):
        pssm_scr[blk] = p_ssm[:, blk * LANES:(blk + 1) * LANES]
    low_lanes = lax.broadcasted_iota(jnp.int32, (chunk_rows, LANES), 1) < HALF_LANES
    for m in range(SSM_CHUNK // 2):
        for blk in range(N_LANE_BLOCKS):
            even = pssm_scr[blk, pl.ds(2 * m, chunk_rows, stride=SSM_CHUNK), :]
            odd = pssm_scr[blk, pl.ds(2 * m + 1, chunk_rows, stride=SSM_CHUNK), :]
            first = jnp.where(low_lanes, even, pltpu.roll(odd, HALF_LANES, 1))
            second = jnp.where(low_lanes, pltpu.roll(even, HALF_LANES, 1), odd)
            ussm_ref[blk, :, m * LANES:(m + 1) * LANES] = first.astype(jnp.bfloat16)
            ussm_ref[blk, :, HALF_COLS + m * LANES:HALF_COLS + (m + 1) * LANES] = (
                second.astype(jnp.bfloat16))

    p_q = jnp.dot(u, w_ref[:, D_SSM:D_SSM + D_ATTN], preferred_element_type=jnp.float32)
    q_ref[...] = (p_q * (HEAD_DIM ** -0.5)).astype(jnp.bfloat16)

    kt = lax.dot_general(wkt_ref[...], u, (((1,), (1,)), ((), ())),
                         preferred_element_type=jnp.float32).astype(jnp.bfloat16)
    for c in range(rows // ATT_BLOCK):
        kt_ref[0, c] = kt[:, c * ATT_BLOCK:(c + 1) * ATT_BLOCK]

    p_v = jnp.dot(u, w_ref[:, D_SSM + 2 * D_ATTN:D_SSM + 3 * D_ATTN],
                  preferred_element_type=jnp.float32)
    v_ref[...] = p_v.astype(jnp.bfloat16)

    for c in range(2 * D_MODEL // 512):
        pg = jnp.dot(u, wg_ref[:, c * 512:(c + 1) * 512], preferred_element_type=jnp.float32)
        pg = pg + bg_ref[:, c * 512:(c + 1) * 512]
        gates_ref[:, c * 512:(c + 1) * 512] = jax.nn.sigmoid(pg).astype(jnp.bfloat16)


def _inproj_call(x2, g, w_in, wkt, w_gate, bg, bsz, seq):
    tokens = bsz * seq
    rows = min(PROJ_ROWS, seq)
    tiles_per_seq = seq // rows
    n_kb = rows // ATT_BLOCK
    const = lambda i: (0, 0)
    return pl.pallas_call(
        _inproj_kernel,
        grid=(tokens // rows,),
        in_specs=[
            pl.BlockSpec((rows, D_MODEL), lambda i: (i, 0)),
            pl.BlockSpec((1, D_MODEL), const),
            pl.BlockSpec(w_in.shape, const),
            pl.BlockSpec(wkt.shape, const),
            pl.BlockSpec(w_gate.shape, const),
            pl.BlockSpec((1, 2 * D_MODEL), const),
        ],
        out_specs=[
            pl.BlockSpec((N_LANE_BLOCKS, rows // SSM_CHUNK, CHUNK_COLS), lambda i: (0, i, 0)),
            pl.BlockSpec((rows, D_ATTN), lambda i: (i, 0)),
            pl.BlockSpec((1, n_kb, D_ATTN, ATT_BLOCK),
                         lambda i: (i // tiles_per_seq, i % tiles_per_seq, 0, 0)),
            pl.BlockSpec((rows, D_ATTN), lambda i: (i, 0)),
            pl.BlockSpec((rows, 2 * D_MODEL), lambda i: (i, 0)),
        ],
        out_shape=[
            jax.ShapeDtypeStruct((N_LANE_BLOCKS, tokens // SSM_CHUNK, CHUNK_COLS), jnp.bfloat16),
            jax.ShapeDtypeStruct((tokens, D_ATTN), jnp.bfloat16),
            jax.ShapeDtypeStruct((bsz, seq // ATT_BLOCK, D_ATTN, ATT_BLOCK), jnp.bfloat16),
            jax.ShapeDtypeStruct((tokens, D_ATTN), jnp.bfloat16),
            jax.ShapeDtypeStruct((tokens, 2 * D_MODEL), jnp.bfloat16),
        ],
        scratch_shapes=[pltpu.VMEM((N_LANE_BLOCKS, rows, LANES), jnp.float32)],
        compiler_params=pltpu.CompilerParams(
            dimension_semantics=("arbitrary",), vmem_limit_bytes=VMEM_LIMIT),
        name="inproj",
    )(x2, g, w_in, wkt, w_gate, bg)


def _ssm_tables(A_re, A_im, log_dt, B_re, B_im, C_re, C_im, D_skip):
    f32 = jnp.float32
    L = SSM_CHUNK
    nb, gb = N_LANE_BLOCKS, GROUPS_PER_BLOCK
    ar, ai = A_re.astype(f32), A_im.astype(f32)
    dt = jnp.exp(log_dt.astype(f32))[:, None]
    tau = jnp.arange(L + 1, dtype=f32)[:, None, None]
    mag = jnp.exp(ar[None] * dt[None] * tau)
    ang = ai[None] * dt[None] * tau
    pw_re, pw_im = mag * jnp.cos(ang), mag * jnp.sin(ang)
    num_re, num_im = pw_re[1] - 1.0, pw_im[1]
    den = ar * ar + ai * ai
    cf_re = (num_re * ar + num_im * ai) / den
    cf_im = (num_im * ar - num_re * ai) / den
    br, bi = B_re.astype(f32), B_im.astype(f32)
    bb_re = cf_re[..., None] * br - cf_im[..., None] * bi
    bb_im = cf_re[..., None] * bi + cf_im[..., None] * br
    ab_re = pw_re[:L, :, :, None] * bb_re[None] - pw_im[:L, :, :, None] * bb_im[None]
    ab_im = pw_re[:L, :, :, None] * bb_im[None] + pw_im[:L, :, :, None] * bb_re[None]
    cr, ci = C_re.astype(f32), C_im.astype(f32)

    kern = jnp.einsum('gdp,tgpc->tgcd', cr, ab_re) - jnp.einsum('gdp,tgpc->tgcd', ci, ab_im)
    kern = kern.reshape(L, nb, LANES, SSM_GROUP).transpose(1, 0, 2, 3)
    lane_idx = jnp.arange(LANES)
    spread = (jnp.arange(SSM_GROUP)[:, None] == lane_idx[None, :] % SSM_GROUP).astype(f32)
    same_group = lane_idx[:, None] // SSM_GROUP == lane_idx[None, :] // SSM_GROUP
    toep = jnp.where(same_group, jnp.matmul(kern, spread), 0.0)

    pw_cat = jnp.concatenate([pw_re[:L], pw_im[:L]], axis=-1)[::-1]
    pw_swp = jnp.concatenate([pw_im[:L], pw_re[:L]], axis=-1)[::-1]
    bbt_re, bbt_im = bb_re.transpose(0, 2, 1), bb_im.transpose(0, 2, 1)
    b_same = jnp.concatenate([bbt_re, bbt_re], axis=-1)
    b_cross = jnp.concatenate([-bbt_im, bbt_im], axis=-1)
    inj = pw_cat[:, :, None, :] * b_same[None] + pw_swp[:, :, None, :] * b_cross[None]
    inj = inj.reshape(L, nb, 2, HALF_LANES, 2 * STATE).transpose(1, 2, 0, 3, 4)
    inj = inj.reshape(nb, 2, HALF_COLS, 2 * STATE)

    ro_re = cr[None] * pw_re[1:L + 1, :, None, :] - ci[None] * pw_im[1:L + 1, :, None, :]
    ro_im = cr[None] * pw_im[1:L + 1, :, None, :] + ci[None] * pw_re[1:L + 1, :, None, :]
    ro = jnp.stack([ro_re, -ro_im], axis=0)
    ro = ro.reshape(2, L, nb, 2, gb // 2, SSM_GROUP, STATE).transpose(2, 3, 0, 4, 6, 1, 5)
    ro = ro.reshape(nb, 2, HALF_STATE, L * SSM_GROUP)

    t_half = jnp.stack([toep[:, :, h * HALF_LANES:(h + 1) * HALF_LANES,
                             h * HALF_LANES:(h + 1) * HALF_LANES] for h in range(2)], axis=1)
    t_pad = jnp.pad(t_half, ((0, 0), (0, 0), (1, 0), (0, 0), (0, 0)))
    n_d2 = L // 2
    top = jnp.concatenate([t_pad[:, :, 1::2][:, :, :n_d2], t_pad[:, :, 2::2][:, :, :n_d2]], axis=-1)
    bot = jnp.concatenate([t_pad[:, :, 0::2][:, :, :n_d2], t_pad[:, :, 1::2][:, :, :n_d2]], axis=-1)
    toep = jnp.concatenate([top, bot], axis=-2)

    steps = (L * jnp.array(SCAN_POWERS, f32))[:, None, None]
    sc_mag = jnp.exp(ar[None] * dt[None] * steps)
    sc_ang = ai[None] * dt[None] * steps
    a_tab = jnp.stack([sc_mag * jnp.cos(sc_ang), sc_mag * jnp.sin(sc_ang)], axis=1)
    a_tab = a_tab.reshape(2 * len(SCAN_POWERS), nb, gb * STATE).transpose(1, 0, 2)
    d_tab = jnp.tile(D_skip.astype(f32).reshape(nb, 2, 1, HALF_LANES), (1, 1, L, 1))
    d_tab = d_tab.reshape(nb, 1, CHUNK_COLS)
    bf = jnp.bfloat16
    return toep.astype(bf), inj.astype(bf), ro.astype(bf), a_tab, d_tab


def _expansion_matrices():
    gh = GROUPS_PER_BLOCK // 2
    src = jnp.arange(2 * STATE)[:, None]
    dst = jnp.arange(HALF_STATE)[None, :]
    e_inj = (src // STATE == dst // (gh * STATE)) & (src % STATE == dst % STATE)
    src = jnp.arange(SSM_CHUNK * SSM_GROUP)[:, None]
    dst = jnp.arange(HALF_COLS)[None, :]
    e_ro = (src // SSM_GROUP == dst // HALF_LANES) & (src % SSM_GROUP == dst % SSM_GROUP)
    return e_inj.astype(jnp.bfloat16), e_ro.astype(jnp.bfloat16)


def _ssm_kernel(x_ref, toep_ref, inj_ref, ro_ref, einj_ref, ero_ref, a_ref, d_ref, y_ref,
                m_scr, p_scr, q_scr, z_scr, w_scr, hp_scr):
    gh = GROUPS_PER_BLOCK // 2
    half = STATE_COLS // 2
    quarter = HALF_STATE // 2
    n_pairs = SSM_CHUNK // 2
    n_rows = x_ref.shape[1]

    @pl.when(pl.program_id(1) == 0)
    def _():
        def group_of(shape, axis, width):
            idx = lax.broadcasted_iota(jnp.int32, shape, axis)
            return lax.shift_right_logical(idx, width.bit_length() - 1) & (gh - 1)

        m_scr[...] = jnp.zeros_like(m_scr)
        for hf in range(2):
            for sp in range(n_pairs):
                for tp in range(sp, n_pairs):
                    m_scr[hf, sp * LANES:(sp + 1) * LANES, tp * LANES:(tp + 1) * LANES] = (
                        toep_ref[0, hf, tp - sp])
            row_g = group_of((HALF_COLS, HALF_STATE), 0, SSM_GROUP)
            col_h = group_of((HALF_COLS, HALF_STATE), 1, STATE)
            p_full = jnp.dot(inj_ref[0, hf], einj_ref[...], preferred_element_type=jnp.float32)
            p_scr[hf] = jnp.where(row_g == col_h, p_full, 0.0).astype(jnp.bfloat16)
            row_g = group_of((HALF_STATE, HALF_COLS), 0, STATE)
            col_h = group_of((HALF_STATE, HALF_COLS), 1, SSM_GROUP)
            q_full = jnp.dot(ro_ref[0, hf], ero_ref[...], preferred_element_type=jnp.float32)
            q_scr[hf] = jnp.where(row_g == col_h, q_full, 0.0).astype(jnp.bfloat16)

    pad = SCAN_PAD
    re, im = slice(0, half), slice(half, STATE_COLS)
    for buf in (z_scr, w_scr, hp_scr):
        buf[0:pad, :] = jnp.zeros((pad, STATE_COLS), jnp.float32)
    for hf in range(2):
        z_h = jnp.dot(x_ref[0, :, hf * HALF_COLS:(hf + 1) * HALF_COLS], p_scr[hf],
                      preferred_element_type=jnp.float32)
        z_scr[pad:pad + n_rows, hf * quarter:(hf + 1) * quarter] = z_h[:, 0:quarter]
        z_scr[pad:pad + n_rows, half + hf * quarter:half + (hf + 1) * quarter] = z_h[:, quarter:]

    def coef(i):
        return a_ref[0, 2 * i:2 * i + 1, :], a_ref[0, 2 * i + 1:2 * i + 2, :]

    def doubling_pass(src, dst, shift, c_re, c_im):
        s_re = src[pad - shift:pad - shift + n_rows, re]
        s_im = src[pad - shift:pad - shift + n_rows, im]
        dst[pad:pad + n_rows, re] = src[pad:pad + n_rows, re] + c_re * s_re - c_im * s_im
        dst[pad:pad + n_rows, im] = src[pad:pad + n_rows, im] + c_re * s_im + c_im * s_re

    doubling_pass(z_scr, w_scr, 1, *coef(0))
    doubling_pass(w_scr, z_scr, 2, *coef(1))
    doubling_pass(z_scr, w_scr, 4, *coef(2))
    c8_re, c8_im = coef(3)

    def tile_step(m, h):
        h_re, h_im = h
        rows8 = pl.ds(pl.multiple_of(pad + 8 * m, 8), 8)
        n_re = w_scr[rows8, re] + c8_re * h_re - c8_im * h_im
        n_im = w_scr[rows8, im] + c8_re * h_im + c8_im * h_re
        hp_scr[rows8, re] = n_re
        hp_scr[rows8, im] = n_im
        return n_re, n_im

    zero = jnp.zeros((8, half), jnp.float32)
    lax.fori_loop(0, n_rows // 8, tile_step, (zero, zero), unroll=8)

    for hf in range(2):
        base = hf * HALF_COLS
        x_h = x_ref[0, :, base:base + HALF_COLS]
        hp_h = jnp.concatenate(
            [hp_scr[pad - 1:pad - 1 + n_rows, hf * quarter:(hf + 1) * quarter],
             hp_scr[pad - 1:pad - 1 + n_rows, half + hf * quarter:half + (hf + 1) * quarter]],
            axis=1).astype(jnp.bfloat16)
        for n in range(HALF_COLS // MXU_TILE):
            lo, hi = n * MXU_TILE, (n + 1) * MXU_TILE
            y = jnp.dot(x_h[:, 0:hi], m_scr[hf, 0:hi, lo:hi], preferred_element_type=jnp.float32)
            y = y + jnp.dot(hp_h, q_scr[hf, :, lo:hi], preferred_element_type=jnp.float32)
            y = y + d_ref[0, :, base + lo:base + hi] * x_h[:, lo:hi].astype(jnp.float32)
            y_ref[0, :, base + lo:base + hi] = jax.nn.gelu(y).astype(jnp.bfloat16)


def _ssm_call(xc, toep, inj, ro, e_inj, e_ro, a_tab, d_tab, bsz):
    nb, total_rows, _ = xc.shape
    n_rows = total_rows // bsz
    per_q = lambda q, b: (q, 0, 0)
    const = lambda q, b: (0, 0)
    return pl.pallas_call(
        _ssm_kernel,
        grid=(nb, bsz),
        in_specs=[
            pl.BlockSpec((1, n_rows, CHUNK_COLS), lambda q, b: (q, b, 0)),
            pl.BlockSpec((1, 2, SSM_CHUNK // 2, LANES, LANES), lambda q, b: (q, 0, 0, 0, 0)),
            pl.BlockSpec((1, 2, HALF_COLS, 2 * STATE), lambda q, b: (q, 0, 0, 0)),
            pl.BlockSpec((1, 2, HALF_STATE, SSM_CHUNK * SSM_GROUP), lambda q, b: (q, 0, 0, 0)),
            pl.BlockSpec(e_inj.shape, const),
            pl.BlockSpec(e_ro.shape, const),
            pl.BlockSpec((1, 2 * len(SCAN_POWERS), STATE_COLS // 2), per_q),
            pl.BlockSpec((1, 1, CHUNK_COLS), per_q),
        ],
        out_specs=pl.BlockSpec((1, n_rows, CHUNK_COLS), lambda q, b: (q, b, 0)),
        out_shape=jax.ShapeDtypeStruct(xc.shape, jnp.bfloat16),
        scratch_shapes=[
            pltpu.VMEM((2, HALF_COLS, HALF_COLS), jnp.bfloat16),
            pltpu.VMEM((2, HALF_COLS, HALF_STATE), jnp.bfloat16),
            pltpu.VMEM((2, HALF_STATE, HALF_COLS), jnp.bfloat16),
            pltpu.VMEM((SCAN_PAD + n_rows, STATE_COLS), jnp.float32),
            pltpu.VMEM((SCAN_PAD + n_rows, STATE_COLS), jnp.float32),
            pltpu.VMEM((SCAN_PAD + n_rows, STATE_COLS), jnp.float32),
        ],
        compiler_params=pltpu.CompilerParams(
            dimension_semantics=("arbitrary", "arbitrary"), vmem_limit_bytes=VMEM_LIMIT),
        name="ssm",
    )(xc, toep, inj, ro, e_inj, e_ro, a_tab, d_tab)


def _merge_mlp_stages(x_ref, ys_ref, ya, gates_ref, wglu_ref, bglu_ref, wus_ref, wua_ref,
                      wout_ref, gm_ref, w1_ref, w2_ref, gf_ref, o_ref, y_scr):
    f32, bf = jnp.float32, jnp.bfloat16
    chunk_rows = ys_ref.shape[1]
    st = {}

    def mix():
        low_lanes = lax.broadcasted_iota(jnp.int32, (chunk_rows, LANES), 1) < HALF_LANES
        for m in range(SSM_CHUNK // 2):
            for blk in range(N_LANE_BLOCKS):
                first = ys_ref[blk, :, m * LANES:(m + 1) * LANES].astype(f32)
                second = ys_ref[blk, :, HALF_COLS + m * LANES:HALF_COLS + (m + 1) * LANES].astype(f32)
                even = jnp.where(low_lanes, first, pltpu.roll(second, HALF_LANES, 1))
                odd = jnp.where(low_lanes, pltpu.roll(first, HALF_LANES, 1), second)
                y_scr[blk, pl.ds(2 * m, chunk_rows, stride=SSM_CHUNK), :] = even
                y_scr[blk, pl.ds(2 * m + 1, chunk_rows, stride=SSM_CHUNK), :] = odd
        y = jnp.concatenate([y_scr[blk] for blk in range(N_LANE_BLOCKS)], axis=-1)
        glu_arg = jnp.dot(y.astype(bf), wglu_ref[...], preferred_element_type=f32) + bglu_ref[...]
        y_ssm = (y * jax.nn.sigmoid(glu_arg)).astype(bf)
        up_s = jnp.dot(y_ssm, wus_ref[...], preferred_element_type=f32)
        up_a = jnp.dot(ya, wua_ref[...], preferred_element_type=f32)
        g_s = gates_ref[:, 0:D_MODEL].astype(f32)
        g_a = gates_ref[:, D_MODEL:2 * D_MODEL].astype(f32)
        st["merged"] = (g_s * up_s + g_a * up_a).astype(bf)

    def project_out():
        h = x_ref[...] + jnp.dot(st["merged"], wout_ref[...], preferred_element_type=f32)
        st["h"] = h
        st["n"] = _rmsnorm_f32(h, gm_ref[...]).astype(bf)

    def mlp_chunk(c):
        cols = slice(c * FF_CHUNK, (c + 1) * FF_CHUNK)
        hid = jnp.maximum(jnp.dot(st["n"], w1_ref[:, cols], preferred_element_type=f32), 0.0)
        st["h"] = st["h"] + jnp.dot((hid * hid).astype(bf), w2_ref[cols, :],
                                    preferred_element_type=f32)

    def finish():
        o_ref[...] = _rmsnorm_f32(st["h"], gf_ref[...])

    chunks = [functools.partial(mlp_chunk, c) for c in range(D_FF // FF_CHUNK)]
    return [mix, project_out] + chunks + [finish]


def _attn_kernel(q_ref, kt_ref, v_ref, tri_ref, o_ref, qm_scr, carry_scr, acc_scr):
    blk = ATT_BLOCK
    n_qb = q_ref.shape[1] // blk
    n_pairs = N_HEADS // 2
    head_rows = N_HEADS * blk
    f32, bf = jnp.float32, jnp.bfloat16
    first_qb = pl.program_id(1) * n_qb

    lane = lax.broadcasted_iota(jnp.int32, (blk, LANES), 1)
    first_head = lane < HEAD_DIM
    tri = tri_ref[...]

    for c in range(n_qb):
        for pair in range(n_pairs):
            q_pair = q_ref[0, c * blk:(c + 1) * blk, pair * LANES:(pair + 1) * LANES]
            zero = jnp.zeros_like(q_pair)
            lo = c * head_rows + 2 * pair * blk
            qm_scr[lo:lo + blk] = jnp.where(first_head, q_pair, zero)
            qm_scr[lo + blk:lo + 2 * blk] = jnp.where(first_head, zero, q_pair)

    def scores(c, j):
        base = c * head_rows
        return jnp.concatenate(
            [jnp.dot(qm_scr[base + 2 * p * blk:base + (2 * p + 2) * blk],
                     kt_ref[0, j, p * LANES:(p + 1) * LANES, :],
                     preferred_element_type=f32) for p in range(n_pairs)], axis=0)

    def stick(z, later):
        sp = jnp.maximum(z, 0.0) + jnp.log(1.0 + jnp.exp2(jnp.abs(z) * (-LOG2_E)))
        cs = jnp.dot(sp.astype(bf), tri, preferred_element_type=f32)
        log_w = z - sp - cs[:, 0:blk]
        if later is not None:
            log_w = log_w - later
        return jnp.exp2(log_w * LOG2_E).astype(bf), cs[:, blk:blk + LANES]

    def weighted_values(w, j, pair):
        ks = pl.multiple_of(j * blk, blk)
        o2 = jnp.dot(w, v_ref[0, pl.ds(ks, blk), pair * LANES:(pair + 1) * LANES],
                     preferred_element_type=f32)
        return jnp.where(first_head, o2[0:blk], o2[blk:2 * blk])

    row = lax.broadcasted_iota(jnp.int32, (blk, blk), 0)
    col = lax.broadcasted_iota(jnp.int32, (blk, blk), 1)
    pen_diag = jnp.where(col < row, 0.0, ATT_MASKED)
    j_diag = [first_qb + c for c in range(n_qb)]
    j_prev = [jnp.maximum(first_qb - 1, 0)] + j_diag[:-1]

    def near_scores(c):
        base = c * head_rows
        return jnp.concatenate(
            [jnp.dot(qm_scr[base + 2 * p * blk:base + (2 * p + 2) * blk],
                     jnp.concatenate([kt_ref[0, j_prev[c], p * LANES:(p + 1) * LANES, :],
                                      kt_ref[0, j_diag[c], p * LANES:(p + 1) * LANES, :]], axis=1),
                     preferred_element_type=f32) for p in range(n_pairs)], axis=0)

    z_near = jnp.concatenate([near_scores(c) for c in range(n_qb)], axis=0)
    z_diag = z_near[:, blk:2 * blk]
    z_diag = (z_diag.reshape(n_qb * N_HEADS, blk, blk) - pen_diag[None]).reshape(n_qb * head_rows, blk)
    w_diag, sum_diag = stick(z_diag, None)
    z_prev = z_near[:, 0:blk]
    z_prev = jnp.concatenate(
        [z_prev[0:head_rows] - jnp.where(first_qb >= 1, 0.0, ATT_MASKED), z_prev[head_rows:]], axis=0)
    w_prev, sum_prev = stick(z_prev, sum_diag)
    carry = sum_diag + sum_prev
    carry_scr[...] = carry
    least = [jnp.min(carry[c * head_rows:(c + 1) * head_rows]) for c in range(n_qb)]
    for c in range(n_qb):
        ks_prev = pl.multiple_of(j_prev[c] * blk, blk)
        ks_diag = pl.multiple_of(j_diag[c] * blk, blk)
        for pair in range(n_pairs):
            rows_w = slice(c * head_rows + 2 * pair * blk, c * head_rows + (2 * pair + 2) * blk)
            lanes = slice(pair * LANES, (pair + 1) * LANES)
            w_near = jnp.concatenate([w_prev[rows_w], w_diag[rows_w]], axis=1)
            v_near = jnp.concatenate([v_ref[0, pl.ds(ks_prev, blk), lanes],
                                      v_ref[0, pl.ds(ks_diag, blk), lanes]], axis=0)
            o2 = jnp.dot(w_near, v_near, preferred_element_type=f32)
            acc_scr[c * blk:(c + 1) * blk, lanes] = jnp.where(first_head, o2[0:blk], o2[blk:2 * blk])

    for c in range(n_qb):
        rows_c = slice(c * blk, (c + 1) * blk)
        base = c * head_rows

        def body(state, c=c, rows_c=rows_c, base=base):
            j, _ = state
            carry = carry_scr[base:base + head_rows]
            w, sum_j = stick(scores(c, j), carry)
            carry = carry + sum_j
            carry_scr[base:base + head_rows] = carry
            for pair in range(n_pairs):
                acc_scr[rows_c, pair * LANES:(pair + 1) * LANES] += weighted_values(
                    w[2 * pair * blk:(2 * pair + 2) * blk], j, pair)
            return j - 1, jnp.min(carry)

        def cond(state):
            j, least_c = state
            return jnp.logical_and(j >= 0, least_c < ATT_SKIP_SUM)

        lax.while_loop(cond, body, (first_qb + (c - 2), least[c]))

    o_ref[0] = acc_scr[...].astype(o_ref.dtype)


def _attn_call(q3, kt4, v3, tri):
    bsz, seq, _ = q3.shape
    blk = ATT_BLOCK
    rows = ATT_ROWS
    n_qb = rows // blk
    return pl.pallas_call(
        _attn_kernel,
        grid=(bsz, seq // rows),
        in_specs=[
            pl.BlockSpec((1, rows, D_ATTN), lambda b, i: (b, i, 0)),
            pl.BlockSpec((1, seq // blk, D_ATTN, blk), lambda b, i: (b, 0, 0, 0)),
            pl.BlockSpec((1, seq, D_ATTN), lambda b, i: (b, 0, 0)),
            pl.BlockSpec(tri.shape, lambda b, i: (0, 0)),
        ],
        out_specs=pl.BlockSpec((1, rows, D_ATTN), lambda b, i: (b, i, 0)),
        out_shape=jax.ShapeDtypeStruct((bsz, seq, D_ATTN), jnp.bfloat16),
        scratch_shapes=[
            pltpu.VMEM((n_qb * N_HEADS * blk, LANES), jnp.bfloat16),
            pltpu.VMEM((n_qb * N_HEADS * blk, LANES), jnp.float32),
            pltpu.VMEM((rows, D_ATTN), jnp.float32),
        ],
        compiler_params=pltpu.CompilerParams(
            dimension_semantics=("arbitrary", "arbitrary"), vmem_limit_bytes=VMEM_LIMIT),
        name="attn",
    )(q3, kt4, v3, tri)


def _merge_kernel(x_ref, ys_ref, ya_ref, gates_ref, wglu_ref, bglu_ref, wus_ref, wua_ref,
                  wout_ref, gm_ref, w1_ref, w2_ref, gf_ref, o_ref, y_scr):
    for stage in _merge_mlp_stages(x_ref, ys_ref, ya_ref[...], gates_ref, wglu_ref, bglu_ref,
                                   wus_ref, wua_ref, wout_ref, gm_ref, w1_ref, w2_ref, gf_ref,
                                   o_ref, y_scr):
        stage()


def _merge_call(x2, ys, ya, gates, wglu, bglu, wus, wua, wout, gm, w1, w2, gf):
    tokens = x2.shape[0]
    rows = MERGE_ROWS
    const = lambda i: (0, 0)

    def resident(arr):
        return pl.BlockSpec(arr.shape, const, pipeline_mode=pl.Buffered(1))

    return pl.pallas_call(
        _merge_kernel,
        grid=(tokens // rows,),
        in_specs=[
            pl.BlockSpec((rows, D_MODEL), lambda i: (i, 0)),
            pl.BlockSpec((N_LANE_BLOCKS, rows // SSM_CHUNK, CHUNK_COLS), lambda i: (0, i, 0)),
            pl.BlockSpec((rows, D_ATTN), lambda i: (i, 0)),
            pl.BlockSpec((rows, 2 * D_MODEL), lambda i: (i, 0)),
            resident(wglu), resident(bglu), resident(wus), resident(wua), resident(wout),
            resident(gm), resident(w1), resident(w2), resident(gf),
        ],
        out_specs=pl.BlockSpec((rows, D_MODEL), lambda i: (i, 0)),
        out_shape=jax.ShapeDtypeStruct((tokens, D_MODEL), jnp.float32),
        scratch_shapes=[pltpu.VMEM((N_LANE_BLOCKS, rows, LANES), jnp.float32)],
        compiler_params=pltpu.CompilerParams(
            dimension_semantics=("arbitrary",), vmem_limit_bytes=VMEM_LIMIT),
        name="merge_mlp",
    )(x2, ys, ya, gates, wglu, bglu, wus, wua, wout, gm, w1, w2, gf)


def _suffix_sum_matrix():
    blk = ATT_BLOCK
    r = jnp.arange(blk)[:, None]
    c = jnp.arange(blk + LANES)[None, :]
    return jnp.where((c >= blk) | (r > c), 1.0, 0.0).astype(jnp.bfloat16)


def kernel(x, norm_mix, w_in, A_re, A_im, log_dt, B_re, B_im, C_re, C_im, D_skip, w_glu, b_glu,
           w_up_ssm, w_up_attn, w_gate, b_gate, w_out, norm_mlp, w_ff1, w_ff2, norm_final):
    bsz, seq, _ = x.shape
    tokens = bsz * seq
    bf = jnp.bfloat16
    assert norm_mix.shape[0] == 1, "single layer"
    assert seq % (SSM_CHUNK * 8) == 0 and seq % ATT_BLOCK == 0
    assert tokens % PROJ_ROWS == 0 and tokens % MERGE_ROWS == 0 and seq % min(PROJ_ROWS, seq) == 0

    x2 = x.reshape(tokens, D_MODEL)
    wkt = w_in[0][:, D_SSM + D_ATTN:D_SSM + 2 * D_ATTN].T.astype(bf)
    ussm, q, kt, v, gates = _inproj_call(x2, norm_mix, w_in[0].astype(bf), wkt,
                                         w_gate[0].astype(bf), b_gate, bsz, seq)

    toep, inj, ro, a_tab, d_tab = _ssm_tables(
        A_re[0], A_im[0], log_dt[0], B_re[0], B_im[0], C_re[0], C_im[0], D_skip[0])
    e_inj, e_ro = _expansion_matrices()
    ys = _ssm_call(ussm, toep, inj, ro, e_inj, e_ro, a_tab, d_tab, bsz)

    ya = _attn_call(q.reshape(bsz, seq, D_ATTN), kt, v.reshape(bsz, seq, D_ATTN),
                    _suffix_sum_matrix()).reshape(tokens, D_ATTN)

    out = _merge_call(x2, ys, ya, gates, w_glu[0].astype(bf), b_glu, w_up_ssm[0].astype(bf),
                      w_up_attn[0].astype(bf), w_out[0].astype(bf), norm_mlp,
                      w_ff1[0].astype(bf), w_ff2[0].astype(bf), norm_final.reshape(1, D_MODEL))
    return out.reshape(bsz, seq, D_MODEL)
```

```python
import functools

import jax
import jax.numpy as jnp
from jax import lax
from jax.experimental import pallas as pl
from jax.experimental.pallas import tpu as pltpu

D_MODEL = 1024
D_SSM = 512
SSM_GROUP = 16
N_GROUPS = 32
STATE = 64
N_HEADS = 8
HEAD_DIM = 64
D_ATTN = 512
D_FF = 4096
EPS = 1e-6

LANES = 128
MXU_TILE = 256
VMEM_LIMIT = 52 * 1024 * 1024

SSM_CHUNK = 16
N_LANE_BLOCKS = D_SSM // LANES
GROUPS_PER_BLOCK = LANES // SSM_GROUP
STATE_COLS = 2 * GROUPS_PER_BLOCK * STATE
CHUNK_COLS = SSM_CHUNK * LANES
HALF_LANES = LANES // 2
HALF_COLS = SSM_CHUNK * HALF_LANES
HALF_STATE = STATE_COLS // 2
SCAN_POWERS = (1, 2, 4, 8)
SCAN_PAD = 8
ATT_BLOCK = 128
ATT_SKIP_SUM = 64.0
ATT_MASKED = 1e30
LOG2_E = 1.4426950408889634
NT_DIMS = (((1,), (1,)), ((), ()))
PROJ_ROWS = 512
MERGE_ROWS = 512
FF_CHUNK = 1024
ATT_ROWS = 512


def _rmsnorm_f32(x, g):
    ms = jnp.mean(x * x, axis=-1, keepdims=True)
    return x * lax.rsqrt(ms + EPS) * g


def _inproj_kernel(x_ref, g_ref, w_ref, wg_ref, bg_ref,
                   ussm_ref, q_ref, k_ref, v_ref, gates_ref, pssm_scr):
    x = x_ref[...]
    u = _rmsnorm_f32(x, g_ref[...]).astype(jnp.bfloat16)
    rows = x.shape[0]
    chunk_rows = rows // SSM_CHUNK

    p_ssm = jnp.dot(u, w_ref[:, 0:D_SSM], preferred_element_type=jnp.float32)
    for blk in range(N_LANE_BLOCKS):
        pssm_scr[blk] = p_ssm[:, blk * LANES:(blk + 1) * LANES]
    low_lanes = lax.broadcasted_iota(jnp.int32, (chunk_rows, LANES), 1) < HALF_LANES
    for m in range(SSM_CHUNK // 2):
        for blk in range(N_LANE_BLOCKS):
            even = pssm_scr[blk, pl.ds(2 * m, chunk_rows, stride=SSM_CHUNK), :]
            odd = pssm_scr[blk, pl.ds(2 * m + 1, chunk_rows, stride=SSM_CHUNK), :]
            first = jnp.where(low_lanes, even, pltpu.roll(odd, HALF_LANES, 1))
            second = jnp.where(low_lanes, pltpu.roll(even, HALF_LANES, 1), odd)
            ussm_ref[blk, :, m * LANES:(m + 1) * LANES] = first.astype(jnp.bfloat16)
            ussm_ref[blk, :, HALF_COLS + m * LANES:HALF_COLS + (m + 1) * LANES] = (
                second.astype(jnp.bfloat16))

    p_q = jnp.dot(u, w_ref[:, D_SSM:D_SSM + D_ATTN], preferred_element_type=jnp.float32)
    q_ref[...] = (p_q * (HEAD_DIM ** -0.5)).astype(jnp.bfloat16)

    p_k = jnp.dot(u, w_ref[:, D_SSM + D_ATTN:D_SSM + 2 * D_ATTN],
                  preferred_element_type=jnp.float32)
    k_ref[...] = p_k.astype(jnp.bfloat16)

    p_v = jnp.dot(u, w_ref[:, D_SSM + 2 * D_ATTN:D_SSM + 3 * D_ATTN],
                  preferred_element_type=jnp.float32)
    v_ref[...] = p_v.astype(jnp.bfloat16)

    for c in range(2 * D_MODEL // 512):
        pg = jnp.dot(u, wg_ref[:, c * 512:(c + 1) * 512], preferred_element_type=jnp.float32)
        pg = pg + bg_ref[:, c * 512:(c + 1) * 512]
        gates_ref[:, c * 512:(c + 1) * 512] = jax.nn.sigmoid(pg).astype(jnp.bfloat16)


def _inproj_call(x2, g, w_in, w_gate, bg, bsz, seq):
    tokens = bsz * seq
    rows = min(PROJ_ROWS, seq)
    const = lambda i: (0, 0)
    return pl.pallas_call(
        _inproj_kernel,
        grid=(tokens // rows,),
        in_specs=[
            pl.BlockSpec((rows, D_MODEL), lambda i: (i, 0)),
            pl.BlockSpec((1, D_MODEL), const),
            pl.BlockSpec(w_in.shape, const),
            pl.BlockSpec(w_gate.shape, const),
            pl.BlockSpec((1, 2 * D_MODEL), const),
        ],
        out_specs=[
            pl.BlockSpec((N_LANE_BLOCKS, rows // SSM_CHUNK, CHUNK_COLS), lambda i: (0, i, 0)),
            pl.BlockSpec((rows, D_ATTN), lambda i: (i, 0)),
            pl.BlockSpec((rows, D_ATTN), lambda i: (i, 0)),
            pl.BlockSpec((rows, D_ATTN), lambda i: (i, 0)),
            pl.BlockSpec((rows, 2 * D_MODEL), lambda i: (i, 0)),
        ],
        out_shape=[
            jax.ShapeDtypeStruct((N_LANE_BLOCKS, tokens // SSM_CHUNK, CHUNK_COLS), jnp.bfloat16),
            jax.ShapeDtypeStruct((tokens, D_ATTN), jnp.bfloat16),
            jax.ShapeDtypeStruct((tokens, D_ATTN), jnp.bfloat16),
            jax.ShapeDtypeStruct((tokens, D_ATTN), jnp.bfloat16),
            jax.ShapeDtypeStruct((tokens, 2 * D_MODEL), jnp.bfloat16),
        ],
        scratch_shapes=[pltpu.VMEM((N_LANE_BLOCKS, rows, LANES), jnp.float32)],
        compiler_params=pltpu.CompilerParams(
            dimension_semantics=("arbitrary",), vmem_limit_bytes=VMEM_LIMIT),
        name="inproj",
    )(x2, g, w_in, w_gate, bg)


def _ssm_tables(A_re, A_im, log_dt, B_re, B_im, C_re, C_im, D_skip):
    f32 = jnp.float32
    L = SSM_CHUNK
    nb, gb = N_LANE_BLOCKS, GROUPS_PER_BLOCK
    ar, ai = A_re.astype(f32), A_im.astype(f32)
    dt = jnp.exp(log_dt.astype(f32))[:, None]
    tau = jnp.arange(L + 1, dtype=f32)[:, None, None]
    mag = jnp.exp(ar[None] * dt[None] * tau)
    ang = ai[None] * dt[None] * tau
    pw_re, pw_im = mag * jnp.cos(ang), mag * jnp.sin(ang)
    num_re, num_im = pw_re[1] - 1.0, pw_im[1]
    den = ar * ar + ai * ai
    cf_re = (num_re * ar + num_im * ai) / den
    cf_im = (num_im * ar - num_re * ai) / den
    br, bi = B_re.astype(f32), B_im.astype(f32)
    bb_re = cf_re[..., None] * br - cf_im[..., None] * bi
    bb_im = cf_re[..., None] * bi + cf_im[..., None] * br
    ab_re = pw_re[:L, :, :, None] * bb_re[None] - pw_im[:L, :, :, None] * bb_im[None]
    ab_im = pw_re[:L, :, :, None] * bb_im[None] + pw_im[:L, :, :, None] * bb_re[None]
    cr, ci = C_re.astype(f32), C_im.astype(f32)

    kern = jnp.einsum('gdp,tgpc->tgcd', cr, ab_re) - jnp.einsum('gdp,tgpc->tgcd', ci, ab_im)
    kern = kern.reshape(L, nb, LANES, SSM_GROUP).transpose(1, 0, 2, 3)
    lane_idx = jnp.arange(LANES)
    spread = (jnp.arange(SSM_GROUP)[:, None] == lane_idx[None, :] % SSM_GROUP).astype(f32)
    same_group = lane_idx[:, None] // SSM_GROUP == lane_idx[None, :] // SSM_GROUP
    toep = jnp.where(same_group, jnp.matmul(kern, spread), 0.0)

    pw_cat = jnp.concatenate([pw_re[:L], pw_im[:L]], axis=-1)[::-1]
    pw_swp = jnp.concatenate([pw_im[:L], pw_re[:L]], axis=-1)[::-1]
    bbt_re, bbt_im = bb_re.transpose(0, 2, 1), bb_im.transpose(0, 2, 1)
    b_same = jnp.concatenate([bbt_re, bbt_re], axis=-1)
    b_cross = jnp.concatenate([-bbt_im, bbt_im], axis=-1)
    inj = pw_cat[:, :, None, :] * b_same[None] + pw_swp[:, :, None, :] * b_cross[None]
    inj = inj.reshape(L, nb, 2, HALF_LANES, 2 * STATE).transpose(1, 2, 0, 3, 4)
    inj = inj.reshape(nb, 2, HALF_COLS, 2 * STATE)

    n_gp = HALF_STATE // 2
    crt = cr.transpose(1, 0, 2).reshape(SSM_GROUP, nb, 2, n_gp)
    cit = ci.transpose(1, 0, 2).reshape(SSM_GROUP, nb, 2, n_gp)
    pwr = pw_re[1:L + 1].reshape(L, 1, nb, 2, n_gp)
    pwi = pw_im[1:L + 1].reshape(L, 1, nb, 2, n_gp)
    ro = jnp.concatenate([crt[None] * pwr - cit[None] * pwi,
                          -(crt[None] * pwi + cit[None] * pwr)], axis=-1)
    ro = ro.transpose(2, 3, 0, 1, 4).reshape(nb, 2, L * SSM_GROUP, HALF_STATE)

    t_half = jnp.stack([toep[:, :, h * HALF_LANES:(h + 1) * HALF_LANES,
                             h * HALF_LANES:(h + 1) * HALF_LANES] for h in range(2)], axis=1)
    t_pad = jnp.pad(t_half, ((0, 0), (0, 0), (1, 0), (0, 0), (0, 0)))
    n_d2 = L // 2
    top = jnp.concatenate([t_pad[:, :, 1::2][:, :, :n_d2], t_pad[:, :, 2::2][:, :, :n_d2]], axis=-1)
    bot = jnp.concatenate([t_pad[:, :, 0::2][:, :, :n_d2], t_pad[:, :, 1::2][:, :, :n_d2]], axis=-1)
    toep = jnp.concatenate([top, bot], axis=-2)

    steps = (L * jnp.array(SCAN_POWERS, f32))[:, None, None]
    sc_mag = jnp.exp(ar[None] * dt[None] * steps)
    sc_ang = ai[None] * dt[None] * steps
    a_tab = jnp.stack([sc_mag * jnp.cos(sc_ang), sc_mag * jnp.sin(sc_ang)], axis=1)
    a_tab = a_tab.reshape(2 * len(SCAN_POWERS), nb, gb * STATE).transpose(1, 0, 2)
    d_tab = jnp.tile(D_skip.astype(f32).reshape(nb, 2, 1, HALF_LANES), (1, 1, L, 1))
    d_tab = d_tab.reshape(nb, 1, CHUNK_COLS)
    bf = jnp.bfloat16
    return toep.astype(bf), inj.astype(bf), ro.astype(bf), a_tab, d_tab


def _expansion_matrices():
    gh = GROUPS_PER_BLOCK // 2
    src = jnp.arange(2 * STATE)[:, None]
    dst = jnp.arange(HALF_STATE)[None, :]
    e_inj = (src // STATE == dst // (gh * STATE)) & (src % STATE == dst % STATE)
    src = jnp.arange(SSM_CHUNK * SSM_GROUP)[None, :]
    dst = jnp.arange(HALF_COLS)[:, None]
    e_ro = (src // SSM_GROUP == dst // HALF_LANES) & (src % SSM_GROUP == dst % SSM_GROUP)
    return e_inj.astype(jnp.bfloat16), e_ro.astype(jnp.bfloat16)


def _ssm_kernel(x_ref, toep_ref, inj_ref, ro_ref, einj_ref, ero_ref, a_ref, d_ref, y_ref,
                m_scr, p_scr, q_scr, z_scr, w_scr, hp_scr):
    gh = GROUPS_PER_BLOCK // 2
    half = STATE_COLS // 2
    quarter = HALF_STATE // 2
    n_pairs = SSM_CHUNK // 2
    n_rows = x_ref.shape[1]

    @pl.when(pl.program_id(1) == 0)
    def _():
        def group_of(shape, axis, width):
            idx = lax.broadcasted_iota(jnp.int32, shape, axis)
            return lax.shift_right_logical(idx, width.bit_length() - 1) & (gh - 1)

        m_scr[...] = jnp.zeros_like(m_scr)
        for hf in range(2):
            for sp in range(n_pairs):
                for tp in range(sp, n_pairs):
                    m_scr[hf, sp * LANES:(sp + 1) * LANES, tp * LANES:(tp + 1) * LANES] = (
                        toep_ref[0, hf, tp - sp])
            row_g = group_of((HALF_COLS, HALF_STATE), 0, SSM_GROUP)
            col_h = group_of((HALF_COLS, HALF_STATE), 1, STATE)
            p_full = jnp.dot(inj_ref[0, hf], einj_ref[...], preferred_element_type=jnp.float32)
            p_scr[hf] = jnp.where(row_g == col_h, p_full, 0.0).astype(jnp.bfloat16)
            row_h = group_of((HALF_COLS, HALF_STATE), 0, SSM_GROUP)
            col_g = group_of((HALF_COLS, HALF_STATE), 1, STATE)
            q_t = jnp.dot(ero_ref[...], ro_ref[0, hf], preferred_element_type=jnp.float32)
            q_scr[hf] = jnp.where(row_h == col_g, q_t, 0.0).T.astype(jnp.bfloat16)

    pad = SCAN_PAD
    re, im = slice(0, half), slice(half, STATE_COLS)
    for buf in (z_scr, w_scr, hp_scr):
        buf[0:pad, :] = jnp.zeros((pad, STATE_COLS), jnp.float32)
    for hf in range(2):
        z_h = jnp.dot(x_ref[0, :, hf * HALF_COLS:(hf + 1) * HALF_COLS], p_scr[hf],
                      preferred_element_type=jnp.float32)
        z_scr[pad:pad + n_rows, hf * quarter:(hf + 1) * quarter] = z_h[:, 0:quarter]
        z_scr[pad:pad + n_rows, half + hf * quarter:half + (hf + 1) * quarter] = z_h[:, quarter:]

    def coef(i):
        return a_ref[0, 2 * i:2 * i + 1, :], a_ref[0, 2 * i + 1:2 * i + 2, :]

    def doubling_pass(src, dst, shift, c_re, c_im):
        s_re = src[pad - shift:pad - shift + n_rows, re]
        s_im = src[pad - shift:pad - shift + n_rows, im]
        dst[pad:pad + n_rows, re] = src[pad:pad + n_rows, re] + c_re * s_re - c_im * s_im
        dst[pad:pad + n_rows, im] = src[pad:pad + n_rows, im] + c_re * s_im + c_im * s_re

    doubling_pass(z_scr, w_scr, 1, *coef(0))
    doubling_pass(w_scr, z_scr, 2, *coef(1))
    doubling_pass(z_scr, w_scr, 4, *coef(2))
    c8_re, c8_im = coef(3)

    def tile_step(m, h):
        h_re, h_im = h
        rows8 = pl.ds(pl.multiple_of(pad + 8 * m, 8), 8)
        n_re = w_scr[rows8, re] + c8_re * h_re - c8_im * h_im
        n_im = w_scr[rows8, im] + c8_re * h_im + c8_im * h_re
        hp_scr[rows8, re] = n_re
        hp_scr[rows8, im] = n_im
        return n_re, n_im

    zero = jnp.zeros((8, half), jnp.float32)
    lax.fori_loop(0, n_rows // 8, tile_step, (zero, zero), unroll=8)

    for hf in range(2):
        base = hf * HALF_COLS
        x_h = x_ref[0, :, base:base + HALF_COLS]
        hp_h = jnp.concatenate(
            [hp_scr[pad - 1:pad - 1 + n_rows, hf * quarter:(hf + 1) * quarter],
             hp_scr[pad - 1:pad - 1 + n_rows, half + hf * quarter:half + (hf + 1) * quarter]],
            axis=1).astype(jnp.bfloat16)
        for n in range(HALF_COLS // MXU_TILE):
            lo, hi = n * MXU_TILE, (n + 1) * MXU_TILE
            y = jnp.dot(x_h[:, 0:hi], m_scr[hf, 0:hi, lo:hi], preferred_element_type=jnp.float32)
            y = y + jnp.dot(hp_h, q_scr[hf, :, lo:hi], preferred_element_type=jnp.float32)
            y = y + d_ref[0, :, base + lo:base + hi] * x_h[:, lo:hi].astype(jnp.float32)
            y_ref[0, :, base + lo:base + hi] = jax.nn.gelu(y).astype(jnp.bfloat16)


def _ssm_call(xc, toep, inj, ro, e_inj, e_ro, a_tab, d_tab, bsz):
    nb, total_rows, _ = xc.shape
    n_rows = total_rows // bsz
    per_q = lambda q, b: (q, 0, 0)
    const = lambda q, b: (0, 0)
    return pl.pallas_call(
        _ssm_kernel,
        grid=(nb, bsz),
        in_specs=[
            pl.BlockSpec((1, n_rows, CHUNK_COLS), lambda q, b: (q, b, 0)),
            pl.BlockSpec((1, 2, SSM_CHUNK // 2, LANES, LANES), lambda q, b: (q, 0, 0, 0, 0)),
            pl.BlockSpec((1, 2, HALF_COLS, 2 * STATE), lambda q, b: (q, 0, 0, 0)),
            pl.BlockSpec((1, 2, SSM_CHUNK * SSM_GROUP, HALF_STATE), lambda q, b: (q, 0, 0, 0)),
            pl.BlockSpec(e_inj.shape, const),
            pl.BlockSpec(e_ro.shape, const),
            pl.BlockSpec((1, 2 * len(SCAN_POWERS), STATE_COLS // 2), per_q),
            pl.BlockSpec((1, 1, CHUNK_COLS), per_q),
        ],
        out_specs=pl.BlockSpec((1, n_rows, CHUNK_COLS), lambda q, b: (q, b, 0)),
        out_shape=jax.ShapeDtypeStruct(xc.shape, jnp.bfloat16),
        scratch_shapes=[
            pltpu.VMEM((2, HALF_COLS, HALF_COLS), jnp.bfloat16),
            pltpu.VMEM((2, HALF_COLS, HALF_STATE), jnp.bfloat16),
            pltpu.VMEM((2, HALF_STATE, HALF_COLS), jnp.bfloat16),
            pltpu.VMEM((SCAN_PAD + n_rows, STATE_COLS), jnp.float32),
            pltpu.VMEM((SCAN_PAD + n_rows, STATE_COLS), jnp.float32),
            pltpu.VMEM((SCAN_PAD + n_rows, STATE_COLS), jnp.float32),
        ],
        compiler_params=pltpu.CompilerParams(
            dimension_semantics=("arbitrary", "arbitrary"), vmem_limit_bytes=VMEM_LIMIT),
        name="ssm",
    )(xc, toep, inj, ro, e_inj, e_ro, a_tab, d_tab)


def _merge_mlp_stages(x_ref, ys_ref, ya, gates_ref, wglu_ref, bglu_ref, wus_ref, wua_ref,
                      wout_ref, gm_ref, w1_ref, w2_ref, gf_ref, o_ref, y_scr):
    f32, bf = jnp.float32, jnp.bfloat16
    chunk_rows = ys_ref.shape[1]
    st = {}

    def mix():
        low_lanes = lax.broadcasted_iota(jnp.int32, (chunk_rows, LANES), 1) < HALF_LANES
        for m in range(SSM_CHUNK // 2):
            for blk in range(N_LANE_BLOCKS):
                first = ys_ref[blk, :, m * LANES:(m + 1) * LANES].astype(f32)
                second = ys_ref[blk, :, HALF_COLS + m * LANES:HALF_COLS + (m + 1) * LANES].astype(f32)
                even = jnp.where(low_lanes, first, pltpu.roll(second, HALF_LANES, 1))
                odd = jnp.where(low_lanes, pltpu.roll(first, HALF_LANES, 1), second)
                y_scr[blk, pl.ds(2 * m, chunk_rows, stride=SSM_CHUNK), :] = even
                y_scr[blk, pl.ds(2 * m + 1, chunk_rows, stride=SSM_CHUNK), :] = odd
        y = jnp.concatenate([y_scr[blk] for blk in range(N_LANE_BLOCKS)], axis=-1)
        glu_arg = jnp.dot(y.astype(bf), wglu_ref[...], preferred_element_type=f32) + bglu_ref[...]
        y_ssm = (y * jax.nn.sigmoid(glu_arg)).astype(bf)
        up_s = jnp.dot(y_ssm, wus_ref[...], preferred_element_type=f32)
        up_a = jnp.dot(ya, wua_ref[...], preferred_element_type=f32)
        g_s = gates_ref[:, 0:D_MODEL].astype(f32)
        g_a = gates_ref[:, D_MODEL:2 * D_MODEL].astype(f32)
        st["merged"] = (g_s * up_s + g_a * up_a).astype(bf)

    def project_out():
        h = x_ref[...] + jnp.dot(st["merged"], wout_ref[...], preferred_element_type=f32)
        st["h"] = h
        st["n"] = _rmsnorm_f32(h, gm_ref[...]).astype(bf)

    def mlp_chunk(c):
        cols = slice(c * FF_CHUNK, (c + 1) * FF_CHUNK)
        hid = jnp.maximum(jnp.dot(st["n"], w1_ref[:, cols], preferred_element_type=f32), 0.0)
        st["h"] = st["h"] + jnp.dot((hid * hid).astype(bf), w2_ref[cols, :],
                                    preferred_element_type=f32)

    def finish():
        o_ref[...] = _rmsnorm_f32(st["h"], gf_ref[...])

    chunks = [functools.partial(mlp_chunk, c) for c in range(D_FF // FF_CHUNK)]
    return [mix, project_out] + chunks + [finish]


def _attn_kernel(q_ref, k_ref, v_ref, tri_ref, o_ref, qm_scr, carry_scr, acc_scr):
    blk = ATT_BLOCK
    n_qb = q_ref.shape[1] // blk
    n_pairs = N_HEADS // 2
    head_rows = N_HEADS * blk
    f32, bf = jnp.float32, jnp.bfloat16
    first_qb = pl.program_id(1) * n_qb

    lane = lax.broadcasted_iota(jnp.int32, (blk, LANES), 1)
    first_head = lane < HEAD_DIM
    tri = tri_ref[...]

    for c in range(n_qb):
        for pair in range(n_pairs):
            q_pair = q_ref[0, c * blk:(c + 1) * blk, pair * LANES:(pair + 1) * LANES]
            zero = jnp.zeros_like(q_pair)
            lo = c * head_rows + 2 * pair * blk
            qm_scr[lo:lo + blk] = jnp.where(first_head, q_pair, zero)
            qm_scr[lo + blk:lo + 2 * blk] = jnp.where(first_head, zero, q_pair)

    def scores(c, j):
        base = c * head_rows
        return jnp.concatenate(
            [lax.dot_general(qm_scr[base + 2 * p * blk:base + (2 * p + 2) * blk],
                             k_ref[0, pl.ds(pl.multiple_of(j * blk, blk), blk), p * LANES:(p + 1) * LANES],
                             NT_DIMS, preferred_element_type=f32) for p in range(n_pairs)], axis=0)

    def stick(z, later):
        sp = jnp.maximum(z, 0.0) + jnp.log(1.0 + jnp.exp2(jnp.abs(z) * (-LOG2_E)))
        cs = jnp.dot(sp.astype(bf), tri, preferred_element_type=f32)
        log_w = z - sp - cs[:, 0:blk]
        if later is not None:
            log_w = log_w - later
        return jnp.exp2(log_w * LOG2_E).astype(bf), cs[:, blk:blk + LANES]

    def weighted_values(w, j, pair):
        ks = pl.multiple_of(j * blk, blk)
        o2 = jnp.dot(w, v_ref[0, pl.ds(ks, blk), pair * LANES:(pair + 1) * LANES],
                     preferred_element_type=f32)
        return jnp.where(first_head, o2[0:blk], o2[blk:2 * blk])

    row = lax.broadcasted_iota(jnp.int32, (blk, blk), 0)
    col = lax.broadcasted_iota(jnp.int32, (blk, blk), 1)
    pen_diag = jnp.where(col < row, 0.0, ATT_MASKED)
    j_diag = [first_qb + c for c in range(n_qb)]
    j_prev = [jnp.maximum(first_qb - 1, 0)] + j_diag[:-1]

    def near_scores(c):
        base = c * head_rows
        ks_prev = pl.multiple_of(j_prev[c] * blk, blk)
        ks_diag = pl.multiple_of(j_diag[c] * blk, blk)
        return jnp.concatenate(
            [lax.dot_general(
                qm_scr[base + 2 * p * blk:base + (2 * p + 2) * blk],
                jnp.concatenate([k_ref[0, pl.ds(ks_prev, blk), p * LANES:(p + 1) * LANES],
                                 k_ref[0, pl.ds(ks_diag, blk), p * LANES:(p + 1) * LANES]], axis=0),
                NT_DIMS, preferred_element_type=f32) for p in range(n_pairs)], axis=0)

    z_near = jnp.concatenate([near_scores(c) for c in range(n_qb)], axis=0)
    z_diag = z_near[:, blk:2 * blk]
    z_diag = (z_diag.reshape(n_qb * N_HEADS, blk, blk) - pen_diag[None]).reshape(n_qb * head_rows, blk)
    w_diag, sum_diag = stick(z_diag, None)
    z_prev = z_near[:, 0:blk]
    z_prev = jnp.concatenate(
        [z_prev[0:head_rows] - jnp.where(first_qb >= 1, 0.0, ATT_MASKED), z_prev[head_rows:]], axis=0)
    w_prev, sum_prev = stick(z_prev, sum_diag)
    carry = sum_diag + sum_prev
    carry_scr[...] = carry
    least = [jnp.min(carry[c * head_rows:(c + 1) * head_rows]) for c in range(n_qb)]
    for c in range(n_qb):
        ks_prev = pl.multiple_of(j_prev[c] * blk, blk)
        ks_diag = pl.multiple_of(j_diag[c] * blk, blk)
        for pair in range(n_pairs):
            rows_w = slice(c * head_rows + 2 * pair * blk, c * head_rows + (2 * pair + 2) * blk)
            lanes = slice(pair * LANES, (pair + 1) * LANES)
            w_near = jnp.concatenate([w_prev[rows_w], w_diag[rows_w]], axis=1)
            v_near = jnp.concatenate([v_ref[0, pl.ds(ks_prev, blk), lanes],
                                      v_ref[0, pl.ds(ks_diag, blk), lanes]], axis=0)
            o2 = jnp.dot(w_near, v_near, preferred_element_type=f32)
            acc_scr[c * blk:(c + 1) * blk, lanes] = jnp.where(first_head, o2[0:blk], o2[blk:2 * blk])

    for c in range(n_qb):
        rows_c = slice(c * blk, (c + 1) * blk)
        base = c * head_rows

        def body(state, c=c, rows_c=rows_c, base=base):
            j, _ = state
            carry = carry_scr[base:base + head_rows]
            w, sum_j = stick(scores(c, j), carry)
            carry = carry + sum_j
            carry_scr[base:base + head_rows] = carry
            for pair in range(n_pairs):
                acc_scr[rows_c, pair * LANES:(pair + 1) * LANES] += weighted_values(
                    w[2 * pair * blk:(2 * pair + 2) * blk], j, pair)
            return j - 1, jnp.min(carry)

        def cond(state):
            j, least_c = state
            return jnp.logical_and(j >= 0, least_c < ATT_SKIP_SUM)

        lax.while_loop(cond, body, (first_qb + (c - 2), least[c]))

    o_ref[0] = acc_scr[...].astype(o_ref.dtype)


def _attn_call(q3, k3, v3, tri):
    bsz, seq, _ = q3.shape
    blk = ATT_BLOCK
    rows = ATT_ROWS
    n_qb = rows // blk
    return pl.pallas_call(
        _attn_kernel,
        grid=(bsz, seq // rows),
        in_specs=[
            pl.BlockSpec((1, rows, D_ATTN), lambda b, i: (b, i, 0)),
            pl.BlockSpec((1, seq, D_ATTN), lambda b, i: (b, 0, 0)),
            pl.BlockSpec((1, seq, D_ATTN), lambda b, i: (b, 0, 0)),
            pl.BlockSpec(tri.shape, lambda b, i: (0, 0)),
        ],
        out_specs=pl.BlockSpec((1, rows, D_ATTN), lambda b, i: (b, i, 0)),
        out_shape=jax.ShapeDtypeStruct((bsz, seq, D_ATTN), jnp.bfloat16),
        scratch_shapes=[
            pltpu.VMEM((n_qb * N_HEADS * blk, LANES), jnp.bfloat16),
            pltpu.VMEM((n_qb * N_HEADS * blk, LANES), jnp.float32),
            pltpu.VMEM((rows, D_ATTN), jnp.float32),
        ],
        compiler_params=pltpu.CompilerParams(
            dimension_semantics=("arbitrary", "arbitrary"), vmem_limit_bytes=VMEM_LIMIT),
        name="attn",
    )(q3, k3, v3, tri)


def _merge_kernel(x_ref, ys_ref, ya_ref, gates_ref, wglu_ref, bglu_ref, wus_ref, wua_ref,
                  wout_ref, gm_ref, w1_ref, w2_ref, gf_ref, o_ref, y_scr):
    for stage in _merge_mlp_stages(x_ref, ys_ref, ya_ref[...], gates_ref, wglu_ref, bglu_ref,
                                   wus_ref, wua_ref, wout_ref, gm_ref, w1_ref, w2_ref, gf_ref,
                                   o_ref, y_scr):
        stage()


def _merge_call(x2, ys, ya, gates, wglu, bglu, wus, wua, wout, gm, w1, w2, gf):
    tokens = x2.shape[0]
    rows = MERGE_ROWS
    const = lambda i: (0, 0)

    def resident(arr):
        return pl.BlockSpec(arr.shape, const, pipeline_mode=pl.Buffered(1))

    return pl.pallas_call(
        _merge_kernel,
        grid=(tokens // rows,),
        in_specs=[
            pl.BlockSpec((rows, D_MODEL), lambda i: (i, 0)),
            pl.BlockSpec((N_LANE_BLOCKS, rows // SSM_CHUNK, CHUNK_COLS), lambda i: (0, i, 0)),
            pl.BlockSpec((rows, D_ATTN), lambda i: (i, 0)),
            pl.BlockSpec((rows, 2 * D_MODEL), lambda i: (i, 0)),
            resident(wglu), resident(bglu), resident(wus), resident(wua), resident(wout),
            resident(gm), resident(w1), resident(w2), resident(gf),
        ],
        out_specs=pl.BlockSpec((rows, D_MODEL), lambda i: (i, 0)),
        out_shape=jax.ShapeDtypeStruct((tokens, D_MODEL), jnp.float32),
        scratch_shapes=[pltpu.VMEM((N_LANE_BLOCKS, rows, LANES), jnp.float32)],
        compiler_params=pltpu.CompilerParams(
            dimension_semantics=("arbitrary",), vmem_limit_bytes=VMEM_LIMIT),
        name="merge_mlp",
    )(x2, ys, ya, gates, wglu, bglu, wus, wua, wout, gm, w1, w2, gf)


def _suffix_sum_matrix():
    blk = ATT_BLOCK
    r = jnp.arange(blk)[:, None]
    c = jnp.arange(blk + LANES)[None, :]
    return jnp.where((c >= blk) | (r > c), 1.0, 0.0).astype(jnp.bfloat16)


def kernel(x, norm_mix, w_in, A_re, A_im, log_dt, B_re, B_im, C_re, C_im, D_skip, w_glu, b_glu,
           w_up_ssm, w_up_attn, w_gate, b_gate, w_out, norm_mlp, w_ff1, w_ff2, norm_final):
    bsz, seq, _ = x.shape
    tokens = bsz * seq
    bf = jnp.bfloat16
    assert norm_mix.shape[0] == 1, "single layer"
    assert seq % (SSM_CHUNK * 8) == 0 and seq % ATT_BLOCK == 0
    assert tokens % PROJ_ROWS == 0 and tokens % MERGE_ROWS == 0 and seq % min(PROJ_ROWS, seq) == 0

    x2 = x.reshape(tokens, D_MODEL)
    ussm, q, k, v, gates = _inproj_call(x2, norm_mix, w_in[0].astype(bf), w_gate[0].astype(bf),
                                        b_gate, bsz, seq)

    toep, inj, ro, a_tab, d_tab = _ssm_tables(
        A_re[0], A_im[0], log_dt[0], B_re[0], B_im[0], C_re[0], C_im[0], D_skip[0])
    e_inj, e_ro = _expansion_matrices()
    ys = _ssm_call(ussm, toep, inj, ro, e_inj, e_ro, a_tab, d_tab, bsz)

    ya = _attn_call(q.reshape(bsz, seq, D_ATTN), k.reshape(bsz, seq, D_ATTN),
                    v.reshape(bsz, seq, D_ATTN), _suffix_sum_matrix()).reshape(tokens, D_ATTN)

    out = _merge_call(x2, ys, ya, gates, w_glu[0].astype(bf), b_glu, w_up_ssm[0].astype(bf),
                      w_up_attn[0].astype(bf), w_out[0].astype(bf), norm_mlp,
                      w_ff1[0].astype(bf), w_ff2[0].astype(bf), norm_final.reshape(1, D_MODEL))
    return out.reshape(bsz, seq, D_MODEL)
```

```python
import functools

import jax
import jax.numpy as jnp
from jax import lax
from jax.experimental import pallas as pl
from jax.experimental.pallas import tpu as pltpu

D_MODEL = 1024
D_SSM = 512
SSM_GROUP = 16
N_GROUPS = 32
STATE = 64
N_HEADS = 8
HEAD_DIM = 64
D_ATTN = 512
D_FF = 4096
EPS = 1e-6

LANES = 128
MXU_TILE = 256
VMEM_LIMIT = 52 * 1024 * 1024

SSM_CHUNK = 16
N_LANE_BLOCKS = D_SSM // LANES
GROUPS_PER_BLOCK = LANES // SSM_GROUP
STATE_COLS = 2 * GROUPS_PER_BLOCK * STATE
CHUNK_COLS = SSM_CHUNK * LANES
HALF_LANES = LANES // 2
HALF_COLS = SSM_CHUNK * HALF_LANES
HALF_STATE = STATE_COLS // 2
SCAN_POWERS = (1, 2, 4, 8)
SCAN_PAD = 8
ATT_BLOCK = 128
ATT_SKIP_SUM = 64.0
ATT_MASKED = 1e30
LOG2_E = 1.4426950408889634
NT_DIMS = (((1,), (1,)), ((), ()))
PROJ_ROWS = 512
MERGE_ROWS = 512
FF_CHUNK = 1024
ATT_ROWS = 512


def _rmsnorm_f32(x, g):
    ms = jnp.mean(x * x, axis=-1, keepdims=True)
    return x * lax.rsqrt(ms + EPS) * g


def _inproj_kernel(x_ref, g_ref, w_ref, wg_ref, bg_ref,
                   ussm_ref, q_ref, k_ref, v_ref, gates_ref, pssm_scr):
    x = x_ref[...]
    u = _rmsnorm_f32(x, g_ref[...]).astype(jnp.bfloat16)
    rows = x.shape[0]
    chunk_rows = rows // SSM_CHUNK

    p_ssm = jnp.dot(u, w_ref[:, 0:D_SSM], preferred_element_type=jnp.float32)
    for blk in range(N_LANE_BLOCKS):
        pssm_scr[blk] = p_ssm[:, blk * LANES:(blk + 1) * LANES]
    low_lanes = lax.broadcasted_iota(jnp.int32, (chunk_rows, LANES), 1) < HALF_LANES
    for m in range(SSM_CHUNK // 2):
        for blk in range(N_LANE_BLOCKS):
            even = pssm_scr[blk, pl.ds(2 * m, chunk_rows, stride=SSM_CHUNK), :]
            odd = pssm_scr[blk, pl.ds(2 * m + 1, chunk_rows, stride=SSM_CHUNK), :]
            first = jnp.where(low_lanes, even, pltpu.roll(odd, HALF_LANES, 1))
            second = jnp.where(low_lanes, pltpu.roll(even, HALF_LANES, 1), odd)
            ussm_ref[blk, :, m * LANES:(m + 1) * LANES] = first.astype(jnp.bfloat16)
            ussm_ref[blk, :, HALF_COLS + m * LANES:HALF_COLS + (m + 1) * LANES] = (
                second.astype(jnp.bfloat16))

    p_q = jnp.dot(u, w_ref[:, D_SSM:D_SSM + D_ATTN], preferred_element_type=jnp.float32)
    q_ref[...] = (p_q * (HEAD_DIM ** -0.5)).astype(jnp.bfloat16)

    p_k = jnp.dot(u, w_ref[:, D_SSM + D_ATTN:D_SSM + 2 * D_ATTN],
                  preferred_element_type=jnp.float32)
    k_ref[...] = p_k.astype(jnp.bfloat16)

    p_v = jnp.dot(u, w_ref[:, D_SSM + 2 * D_ATTN:D_SSM + 3 * D_ATTN],
                  preferred_element_type=jnp.float32)
    v_ref[...] = p_v.astype(jnp.bfloat16)

    for c in range(2 * D_MODEL // 512):
        pg = jnp.dot(u, wg_ref[:, c * 512:(c + 1) * 512], preferred_element_type=jnp.float32)
        pg = pg + bg_ref[:, c * 512:(c + 1) * 512]
        gates_ref[:, c * 512:(c + 1) * 512] = jax.nn.sigmoid(pg).astype(jnp.bfloat16)


def _inproj_call(x2, g, w_in, w_gate, bg, bsz, seq):
    tokens = bsz * seq
    rows = min(PROJ_ROWS, seq)
    const = lambda i: (0, 0)
    return pl.pallas_call(
        _inproj_kernel,
        grid=(tokens // rows,),
        in_specs=[
            pl.BlockSpec((rows, D_MODEL), lambda i: (i, 0)),
            pl.BlockSpec((1, D_MODEL), const),
            pl.BlockSpec(w_in.shape, const),
            pl.BlockSpec(w_gate.shape, const),
            pl.BlockSpec((1, 2 * D_MODEL), const),
        ],
        out_specs=[
            pl.BlockSpec((N_LANE_BLOCKS, rows // SSM_CHUNK, CHUNK_COLS), lambda i: (0, i, 0)),
            pl.BlockSpec((rows, D_ATTN), lambda i: (i, 0)),
            pl.BlockSpec((rows, D_ATTN), lambda i: (i, 0)),
            pl.BlockSpec((rows, D_ATTN), lambda i: (i, 0)),
            pl.BlockSpec((rows, 2 * D_MODEL), lambda i: (i, 0)),
        ],
        out_shape=[
            jax.ShapeDtypeStruct((N_LANE_BLOCKS, tokens // SSM_CHUNK, CHUNK_COLS), jnp.bfloat16),
            jax.ShapeDtypeStruct((tokens, D_ATTN), jnp.bfloat16),
            jax.ShapeDtypeStruct((tokens, D_ATTN), jnp.bfloat16),
            jax.ShapeDtypeStruct((tokens, D_ATTN), jnp.bfloat16),
            jax.ShapeDtypeStruct((tokens, 2 * D_MODEL), jnp.bfloat16),
        ],
        scratch_shapes=[pltpu.VMEM((N_LANE_BLOCKS, rows, LANES), jnp.float32)],
        compiler_params=pltpu.CompilerParams(
            dimension_semantics=("arbitrary",), vmem_limit_bytes=VMEM_LIMIT),
        name="inproj",
    )(x2, g, w_in, w_gate, bg)


def _ssm_tables(A_re, A_im, log_dt, B_re, B_im, C_re, C_im, D_skip):
    f32 = jnp.float32
    L = SSM_CHUNK
    nb, gb = N_LANE_BLOCKS, GROUPS_PER_BLOCK
    ar, ai = A_re.astype(f32), A_im.astype(f32)
    dt = jnp.exp(log_dt.astype(f32))[:, None]
    tau = jnp.arange(L + 1, dtype=f32)[:, None, None]
    mag = jnp.exp(ar[None] * dt[None] * tau)
    ang = ai[None] * dt[None] * tau
    pw_re, pw_im = mag * jnp.cos(ang), mag * jnp.sin(ang)
    num_re, num_im = pw_re[1] - 1.0, pw_im[1]
    den = ar * ar + ai * ai
    cf_re = (num_re * ar + num_im * ai) / den
    cf_im = (num_im * ar - num_re * ai) / den
    br, bi = B_re.astype(f32), B_im.astype(f32)
    bb_re = cf_re[..., None] * br - cf_im[..., None] * bi
    bb_im = cf_re[..., None] * bi + cf_im[..., None] * br
    ab_re = pw_re[:L, :, :, None] * bb_re[None] - pw_im[:L, :, :, None] * bb_im[None]
    ab_im = pw_re[:L, :, :, None] * bb_im[None] + pw_im[:L, :, :, None] * bb_re[None]
    cr, ci = C_re.astype(f32), C_im.astype(f32)

    kern = jnp.einsum('gdp,tgpc->tgcd', cr, ab_re) - jnp.einsum('gdp,tgpc->tgcd', ci, ab_im)
    kern = kern.reshape(L, nb, LANES, SSM_GROUP).transpose(1, 0, 2, 3)
    lane_idx = jnp.arange(LANES)
    spread = (jnp.arange(SSM_GROUP)[:, None] == lane_idx[None, :] % SSM_GROUP).astype(f32)
    same_group = lane_idx[:, None] // SSM_GROUP == lane_idx[None, :] // SSM_GROUP
    toep = jnp.where(same_group, jnp.matmul(kern, spread), 0.0)

    pw_cat = jnp.concatenate([pw_re[:L], pw_im[:L]], axis=-1)[::-1]
    pw_swp = jnp.concatenate([pw_im[:L], pw_re[:L]], axis=-1)[::-1]
    bbt_re, bbt_im = bb_re.transpose(0, 2, 1), bb_im.transpose(0, 2, 1)
    b_same = jnp.concatenate([bbt_re, bbt_re], axis=-1)
    b_cross = jnp.concatenate([-bbt_im, bbt_im], axis=-1)
    inj = pw_cat[:, :, None, :] * b_same[None] + pw_swp[:, :, None, :] * b_cross[None]
    inj = inj.reshape(L, nb, 2, HALF_LANES, 2 * STATE).transpose(1, 2, 0, 3, 4)
    inj = inj.reshape(nb, 2, HALF_COLS, 2 * STATE)

    n_gp = HALF_STATE // 2
    crt = cr.transpose(1, 0, 2).reshape(SSM_GROUP, nb, 2, n_gp)
    cit = ci.transpose(1, 0, 2).reshape(SSM_GROUP, nb, 2, n_gp)
    pwr = pw_re[1:L + 1].reshape(L, 1, nb, 2, n_gp)
    pwi = pw_im[1:L + 1].reshape(L, 1, nb, 2, n_gp)
    ro = jnp.concatenate([crt[None] * pwr - cit[None] * pwi,
                          -(crt[None] * pwi + cit[None] * pwr)], axis=-1)
    ro = ro.transpose(2, 3, 0, 1, 4).reshape(nb, 2, L * SSM_GROUP, HALF_STATE)

    t_half = jnp.stack([toep[:, :, h * HALF_LANES:(h + 1) * HALF_LANES,
                             h * HALF_LANES:(h + 1) * HALF_LANES] for h in range(2)], axis=1)
    t_pad = jnp.pad(t_half, ((0, 0), (0, 0), (1, 0), (0, 0), (0, 0)))
    n_d2 = L // 2
    top = jnp.concatenate([t_pad[:, :, 1::2][:, :, :n_d2], t_pad[:, :, 2::2][:, :, :n_d2]], axis=-1)
    bot = jnp.concatenate([t_pad[:, :, 0::2][:, :, :n_d2], t_pad[:, :, 1::2][:, :, :n_d2]], axis=-1)
    toep = jnp.concatenate([top, bot], axis=-2)

    steps = (L * jnp.array(SCAN_POWERS, f32))[:, None, None]
    sc_mag = jnp.exp(ar[None] * dt[None] * steps)
    sc_ang = ai[None] * dt[None] * steps
    a_tab = jnp.stack([sc_mag * jnp.cos(sc_ang), sc_mag * jnp.sin(sc_ang)], axis=1)
    a_tab = a_tab.reshape(2 * len(SCAN_POWERS), nb, gb * STATE).transpose(1, 0, 2)
    d_tab = jnp.tile(D_skip.astype(f32).reshape(nb, 2, 1, HALF_LANES), (1, 1, L, 1))
    d_tab = d_tab.reshape(nb, 1, CHUNK_COLS)
    bf = jnp.bfloat16
    return toep.astype(bf), inj.astype(bf), ro.astype(bf), a_tab, d_tab


def _expansion_matrices():
    gh = GROUPS_PER_BLOCK // 2
    src = jnp.arange(2 * STATE)[:, None]
    dst = jnp.arange(HALF_STATE)[None, :]
    e_inj = (src // STATE == dst // (gh * STATE)) & (src % STATE == dst % STATE)
    src = jnp.arange(SSM_CHUNK * SSM_GROUP)[None, :]
    dst = jnp.arange(HALF_COLS)[:, None]
    e_ro = (src // SSM_GROUP == dst // HALF_LANES) & (src % SSM_GROUP == dst % SSM_GROUP)
    return e_inj.astype(jnp.bfloat16), e_ro.astype(jnp.bfloat16)


def _ssm_kernel(x_ref, toep_ref, inj_ref, ro_ref, einj_ref, ero_ref, a_ref, d_ref, y_ref,
                m_scr, p_scr, q_scr, z_scr, w_scr, hp_scr):
    gh = GROUPS_PER_BLOCK // 2
    half = STATE_COLS // 2
    quarter = HALF_STATE // 2
    n_pairs = SSM_CHUNK // 2
    n_rows = x_ref.shape[1]

    @pl.when(pl.program_id(1) == 0)
    def _():
        def group_of(shape, axis, width):
            idx = lax.broadcasted_iota(jnp.int32, shape, axis)
            return lax.shift_right_logical(idx, width.bit_length() - 1) & (gh - 1)

        m_scr[...] = jnp.zeros_like(m_scr)
        for hf in range(2):
            for sp in range(n_pairs):
                for tp in range(sp, n_pairs):
                    m_scr[hf, sp * LANES:(sp + 1) * LANES, tp * LANES:(tp + 1) * LANES] = (
                        toep_ref[0, hf, tp - sp])
            row_g = group_of((HALF_COLS, HALF_STATE), 0, SSM_GROUP)
            col_h = group_of((HALF_COLS, HALF_STATE), 1, STATE)
            p_full = jnp.dot(inj_ref[0, hf], einj_ref[...], preferred_element_type=jnp.float32)
            p_scr[hf] = jnp.where(row_g == col_h, p_full, 0.0).astype(jnp.bfloat16)
            row_h = group_of((HALF_COLS, HALF_STATE), 0, SSM_GROUP)
            col_g = group_of((HALF_COLS, HALF_STATE), 1, STATE)
            q_t = jnp.dot(ero_ref[...], ro_ref[0, hf], preferred_element_type=jnp.float32)
            q_scr[hf] = jnp.where(row_h == col_g, q_t, 0.0).T.astype(jnp.bfloat16)

    pad = SCAN_PAD
    re, im = slice(0, half), slice(half, STATE_COLS)
    for buf in (z_scr, w_scr, hp_scr):
        buf[0:pad, :] = jnp.zeros((pad, STATE_COLS), jnp.float32)
    for hf in range(2):
        z_h = jnp.dot(x_ref[0, :, hf * HALF_COLS:(hf + 1) * HALF_COLS], p_scr[hf],
                      preferred_element_type=jnp.float32)
        z_scr[pad:pad + n_rows, hf * quarter:(hf + 1) * quarter] = z_h[:, 0:quarter]
        z_scr[pad:pad + n_rows, half + hf * quarter:half + (hf + 1) * quarter] = z_h[:, quarter:]

    def coef(i):
        return a_ref[0, 2 * i:2 * i + 1, :], a_ref[0, 2 * i + 1:2 * i + 2, :]

    def doubling_pass(src, dst, shift, c_re, c_im):
        s_re = src[pad - shift:pad - shift + n_rows, re]
        s_im = src[pad - shift:pad - shift + n_rows, im]
        dst[pad:pad + n_rows, re] = src[pad:pad + n_rows, re] + c_re * s_re - c_im * s_im
        dst[pad:pad + n_rows, im] = src[pad:pad + n_rows, im] + c_re * s_im + c_im * s_re

    doubling_pass(z_scr, w_scr, 1, *coef(0))
    doubling_pass(w_scr, z_scr, 2, *coef(1))
    doubling_pass(z_scr, w_scr, 4, *coef(2))
    c8_re, c8_im = coef(3)

    def tile_step(m, h):
        h_re, h_im = h
        rows8 = pl.ds(pl.multiple_of(pad + 8 * m, 8), 8)
        n_re = w_scr[rows8, re] + c8_re * h_re - c8_im * h_im
        n_im = w_scr[rows8, im] + c8_re * h_im + c8_im * h_re
        hp_scr[rows8, re] = n_re
        hp_scr[rows8, im] = n_im
        return n_re, n_im

    zero = jnp.zeros((8, half), jnp.float32)
    lax.fori_loop(0, n_rows // 8, tile_step, (zero, zero), unroll=8)

    for hf in range(2):
        base = hf * HALF_COLS
        x_h = x_ref[0, :, base:base + HALF_COLS]
        hp_h = jnp.concatenate(
            [hp_scr[pad - 1:pad - 1 + n_rows, hf * quarter:(hf + 1) * quarter],
             hp_scr[pad - 1:pad - 1 + n_rows, half + hf * quarter:half + (hf + 1) * quarter]],
            axis=1).astype(jnp.bfloat16)
        for n in range(HALF_COLS // MXU_TILE):
            lo, hi = n * MXU_TILE, (n + 1) * MXU_TILE
            y = jnp.dot(x_h[:, 0:hi], m_scr[hf, 0:hi, lo:hi], preferred_element_type=jnp.float32)
            y = y + jnp.dot(hp_h, q_scr[hf, :, lo:hi], preferred_element_type=jnp.float32)
            y = y + d_ref[0, :, base + lo:base + hi] * x_h[:, lo:hi].astype(jnp.float32)
            y_ref[0, :, base + lo:base + hi] = jax.nn.gelu(y).astype(jnp.bfloat16)


def _ssm_call(xc, toep, inj, ro, e_inj, e_ro, a_tab, d_tab, bsz):
    nb, total_rows, _ = xc.shape
    n_rows = total_rows // bsz
    per_q = lambda q, b: (q, 0, 0)
    const = lambda q, b: (0, 0)
    return pl.pallas_call(
        _ssm_kernel,
        grid=(nb, bsz),
        in_specs=[
            pl.BlockSpec((1, n_rows, CHUNK_COLS), lambda q, b: (q, b, 0)),
            pl.BlockSpec((1, 2, SSM_CHUNK // 2, LANES, LANES), lambda q, b: (q, 0, 0, 0, 0)),
            pl.BlockSpec((1, 2, HALF_COLS, 2 * STATE), lambda q, b: (q, 0, 0, 0)),
            pl.BlockSpec((1, 2, SSM_CHUNK * SSM_GROUP, HALF_STATE), lambda q, b: (q, 0, 0, 0)),
            pl.BlockSpec(e_inj.shape, const),
            pl.BlockSpec(e_ro.shape, const),
            pl.BlockSpec((1, 2 * len(SCAN_POWERS), STATE_COLS // 2), per_q),
            pl.BlockSpec((1, 1, CHUNK_COLS), per_q),
        ],
        out_specs=pl.BlockSpec((1, n_rows, CHUNK_COLS), lambda q, b: (q, b, 0)),
        out_shape=jax.ShapeDtypeStruct(xc.shape, jnp.bfloat16),
        scratch_shapes=[
            pltpu.VMEM((2, HALF_COLS, HALF_COLS), jnp.bfloat16),
            pltpu.VMEM((2, HALF_COLS, HALF_STATE), jnp.bfloat16),
            pltpu.VMEM((2, HALF_STATE, HALF_COLS), jnp.bfloat16),
            pltpu.VMEM((SCAN_PAD + n_rows, STATE_COLS), jnp.float32),
            pltpu.VMEM((SCAN_PAD + n_rows, STATE_COLS), jnp.float32),
            pltpu.VMEM((SCAN_PAD + n_rows, STATE_COLS), jnp.float32),
        ],
        compiler_params=pltpu.CompilerParams(
            dimension_semantics=("arbitrary", "arbitrary"), vmem_limit_bytes=VMEM_LIMIT),
        name="ssm",
    )(xc, toep, inj, ro, e_inj, e_ro, a_tab, d_tab)


def _merge_mlp_stages(x_ref, ys_ref, ya, gates_ref, wglu_ref, bglu_ref, wus_ref, wua_ref,
                      wout_ref, gm_ref, w1_ref, w2_ref, gf_ref, o_ref, y_scr):
    f32, bf = jnp.float32, jnp.bfloat16
    chunk_rows = ys_ref.shape[1]
    st = {}

    def mix():
        low_lanes = lax.broadcasted_iota(jnp.int32, (chunk_rows, LANES), 1) < HALF_LANES
        for m in range(SSM_CHUNK // 2):
            for blk in range(N_LANE_BLOCKS):
                first = ys_ref[blk, :, m * LANES:(m + 1) * LANES].astype(f32)
                second = ys_ref[blk, :, HALF_COLS + m * LANES:HALF_COLS + (m + 1) * LANES].astype(f32)
                even = jnp.where(low_lanes, first, pltpu.roll(second, HALF_LANES, 1))
                odd = jnp.where(low_lanes, pltpu.roll(first, HALF_LANES, 1), second)
                y_scr[blk, pl.ds(2 * m, chunk_rows, stride=SSM_CHUNK), :] = even
                y_scr[blk, pl.ds(2 * m + 1, chunk_rows, stride=SSM_CHUNK), :] = odd
        y = jnp.concatenate([y_scr[blk] for blk in range(N_LANE_BLOCKS)], axis=-1)
        glu_arg = jnp.dot(y.astype(bf), wglu_ref[...], preferred_element_type=f32) + bglu_ref[...]
        y_ssm = (y * jax.nn.sigmoid(glu_arg)).astype(bf)
        up_s = jnp.dot(y_ssm, wus_ref[...], preferred_element_type=f32)
        up_a = jnp.dot(ya, wua_ref[...], preferred_element_type=f32)
        g_s = gates_ref[:, 0:D_MODEL].astype(f32)
        g_a = gates_ref[:, D_MODEL:2 * D_MODEL].astype(f32)
        st["merged"] = (g_s * up_s + g_a * up_a).astype(bf)

    def project_out():
        h = x_ref[...] + jnp.dot(st["merged"], wout_ref[...], preferred_element_type=f32)
        st["h"] = h
        st["n"] = _rmsnorm_f32(h, gm_ref[...]).astype(bf)

    def mlp_chunk(c):
        cols = slice(c * FF_CHUNK, (c + 1) * FF_CHUNK)
        hid = jnp.maximum(jnp.dot(st["n"], w1_ref[:, cols], preferred_element_type=f32), 0.0)
        st["h"] = st["h"] + jnp.dot((hid * hid).astype(bf), w2_ref[cols, :],
                                    preferred_element_type=f32)

    def finish():
        o_ref[...] = _rmsnorm_f32(st["h"], gf_ref[...])

    chunks = [functools.partial(mlp_chunk, c) for c in range(D_FF // FF_CHUNK)]
    return [mix, project_out] + chunks + [finish]


def _attn_kernel(q_ref, k_ref, v_ref, tri_ref, o_ref, qm_scr, carry_scr, acc_scr):
    blk = ATT_BLOCK
    n_qb = q_ref.shape[1] // blk
    n_pairs = N_HEADS // 2
    head_rows = N_HEADS * blk
    f32, bf = jnp.float32, jnp.bfloat16
    first_qb = pl.program_id(1) * n_qb

    lane = lax.broadcasted_iota(jnp.int32, (blk, LANES), 1)
    first_head = lane < HEAD_DIM
    tri_near = tri_ref[...]
    tri = tri_ref[blk:2 * blk, blk:2 * blk + LANES]

    for c in range(n_qb):
        for pair in range(n_pairs):
            q_pair = q_ref[0, c * blk:(c + 1) * blk, pair * LANES:(pair + 1) * LANES]
            zero = jnp.zeros_like(q_pair)
            lo = c * head_rows + 2 * pair * blk
            qm_scr[lo:lo + blk] = jnp.where(first_head, q_pair, zero)
            qm_scr[lo + blk:lo + 2 * blk] = jnp.where(first_head, zero, q_pair)

    def scores(c, j):
        base = c * head_rows
        return jnp.concatenate(
            [lax.dot_general(qm_scr[base + 2 * p * blk:base + (2 * p + 2) * blk],
                             k_ref[0, pl.ds(pl.multiple_of(j * blk, blk), blk), p * LANES:(p + 1) * LANES],
                             NT_DIMS, preferred_element_type=f32) for p in range(n_pairs)], axis=0)

    def stick(z, later):
        sp = jnp.maximum(z, 0.0) + jnp.log(1.0 + jnp.exp2(jnp.abs(z) * (-LOG2_E)))
        cs = jnp.dot(sp.astype(bf), tri, preferred_element_type=f32)
        log_w = z - sp - cs[:, 0:blk]
        if later is not None:
            log_w = log_w - later
        return jnp.exp2(log_w * LOG2_E).astype(bf), cs[:, blk:blk + LANES]

    def weighted_values(w, j, pair):
        ks = pl.multiple_of(j * blk, blk)
        o2 = jnp.dot(w, v_ref[0, pl.ds(ks, blk), pair * LANES:(pair + 1) * LANES],
                     preferred_element_type=f32)
        return jnp.where(first_head, o2[0:blk], o2[blk:2 * blk])

    row = lax.broadcasted_iota(jnp.int32, (blk, 2 * blk), 0)
    col = lax.broadcasted_iota(jnp.int32, (blk, 2 * blk), 1)
    pen_near = jnp.where(col < row + blk, 0.0, ATT_MASKED)
    pen_first = jnp.where(jnp.logical_and(col >= blk, col < row + blk), 0.0, ATT_MASKED)
    j_diag = [first_qb + c for c in range(n_qb)]
    j_prev = [jnp.maximum(first_qb - 1, 0)] + j_diag[:-1]

    def near_scores(c):
        base = c * head_rows
        ks_prev = pl.multiple_of(j_prev[c] * blk, blk)
        ks_diag = pl.multiple_of(j_diag[c] * blk, blk)
        return jnp.concatenate(
            [lax.dot_general(
                qm_scr[base + 2 * p * blk:base + (2 * p + 2) * blk],
                jnp.concatenate([k_ref[0, pl.ds(ks_prev, blk), p * LANES:(p + 1) * LANES],
                                 k_ref[0, pl.ds(ks_diag, blk), p * LANES:(p + 1) * LANES]], axis=0),
                NT_DIMS, preferred_element_type=f32) for p in range(n_pairs)], axis=0)

    pen_0 = jnp.where(first_qb >= 1, pen_near, pen_first)
    z_near = jnp.concatenate(
        [near_scores(c).reshape(N_HEADS, blk, 2 * blk) - (pen_0 if c == 0 else pen_near)[None]
         for c in range(n_qb)], axis=0).reshape(n_qb * head_rows, 2 * blk)
    sp = jnp.maximum(z_near, 0.0) + jnp.log(1.0 + jnp.exp2(jnp.abs(z_near) * (-LOG2_E)))
    sp_b = sp.astype(bf)
    cs = jnp.dot(sp_b, tri_near[:, 0:2 * blk], preferred_element_type=f32)
    w_near = jnp.exp2((z_near - sp - cs) * LOG2_E).astype(bf)
    total = cs[:, 0:1] + sp_b[:, 0:1].astype(f32)
    carry_scr[:, 0:1] = total
    least = [jnp.min(total[c * head_rows:(c + 1) * head_rows]) for c in range(n_qb)]
    for c in range(n_qb):
        ks_prev = pl.multiple_of(j_prev[c] * blk, blk)
        ks_diag = pl.multiple_of(j_diag[c] * blk, blk)
        for pair in range(n_pairs):
            rows_w = slice(c * head_rows + 2 * pair * blk, c * head_rows + (2 * pair + 2) * blk)
            lanes = slice(pair * LANES, (pair + 1) * LANES)
            v_near = jnp.concatenate([v_ref[0, pl.ds(ks_prev, blk), lanes],
                                      v_ref[0, pl.ds(ks_diag, blk), lanes]], axis=0)
            o2 = jnp.dot(w_near[rows_w], v_near, preferred_element_type=f32)
            acc_scr[c * blk:(c + 1) * blk, lanes] = jnp.where(first_head, o2[0:blk], o2[blk:2 * blk])

    for c in range(n_qb):
        rows_c = slice(c * blk, (c + 1) * blk)
        base = c * head_rows

        def body(state, c=c, rows_c=rows_c, base=base):
            j, _ = state
            carry = jnp.broadcast_to(carry_scr[base:base + head_rows, 0:1], (head_rows, LANES))
            w, sum_j = stick(scores(c, j), carry)
            carry = carry + sum_j
            carry_scr[base:base + head_rows] = carry
            for pair in range(n_pairs):
                acc_scr[rows_c, pair * LANES:(pair + 1) * LANES] += weighted_values(
                    w[2 * pair * blk:(2 * pair + 2) * blk], j, pair)
            return j - 1, jnp.min(carry)

        def cond(state):
            j, least_c = state
            return jnp.logical_and(j >= 0, least_c < ATT_SKIP_SUM)

        lax.while_loop(cond, body, (first_qb + (c - 2), least[c]))

    o_ref[0] = acc_scr[...].astype(o_ref.dtype)


def _attn_call(q3, k3, v3, tri):
    bsz, seq, _ = q3.shape
    blk = ATT_BLOCK
    rows = ATT_ROWS
    n_qb = rows // blk
    return pl.pallas_call(
        _attn_kernel,
        grid=(bsz, seq // rows),
        in_specs=[
            pl.BlockSpec((1, rows, D_ATTN), lambda b, i: (b, i, 0)),
            pl.BlockSpec((1, seq, D_ATTN), lambda b, i: (b, 0, 0)),
            pl.BlockSpec((1, seq, D_ATTN), lambda b, i: (b, 0, 0)),
            pl.BlockSpec(tri.shape, lambda b, i: (0, 0)),
        ],
        out_specs=pl.BlockSpec((1, rows, D_ATTN), lambda b, i: (b, i, 0)),
        out_shape=jax.ShapeDtypeStruct((bsz, seq, D_ATTN), jnp.bfloat16),
        scratch_shapes=[
            pltpu.VMEM((n_qb * N_HEADS * blk, LANES), jnp.bfloat16),
            pltpu.VMEM((n_qb * N_HEADS * blk, LANES), jnp.float32),
            pltpu.VMEM((rows, D_ATTN), jnp.float32),
        ],
        compiler_params=pltpu.CompilerParams(
            dimension_semantics=("arbitrary", "arbitrary"), vmem_limit_bytes=VMEM_LIMIT),
        name="attn",
    )(q3, k3, v3, tri)


def _merge_kernel(x_ref, ys_ref, ya_ref, gates_ref, wglu_ref, bglu_ref, wus_ref, wua_ref,
                  wout_ref, gm_ref, w1_ref, w2_ref, gf_ref, o_ref, y_scr):
    for stage in _merge_mlp_stages(x_ref, ys_ref, ya_ref[...], gates_ref, wglu_ref, bglu_ref,
                                   wus_ref, wua_ref, wout_ref, gm_ref, w1_ref, w2_ref, gf_ref,
                                   o_ref, y_scr):
        stage()


def _merge_call(x2, ys, ya, gates, wglu, bglu, wus, wua, wout, gm, w1, w2, gf):
    tokens = x2.shape[0]
    rows = MERGE_ROWS
    const = lambda i: (0, 0)

    def resident(arr):
        return pl.BlockSpec(arr.shape, const, pipeline_mode=pl.Buffered(1))

    return pl.pallas_call(
        _merge_kernel,
        grid=(tokens // rows,),
        in_specs=[
            pl.BlockSpec((rows, D_MODEL), lambda i: (i, 0)),
            pl.BlockSpec((N_LANE_BLOCKS, rows // SSM_CHUNK, CHUNK_COLS), lambda i: (0, i, 0)),
            pl.BlockSpec((rows, D_ATTN), lambda i: (i, 0)),
            pl.BlockSpec((rows, 2 * D_MODEL), lambda i: (i, 0)),
            resident(wglu), resident(bglu), resident(wus), resident(wua), resident(wout),
            resident(gm), resident(w1), resident(w2), resident(gf),
        ],
        out_specs=pl.BlockSpec((rows, D_MODEL), lambda i: (i, 0)),
        out_shape=jax.ShapeDtypeStruct((tokens, D_MODEL), jnp.float32),
        scratch_shapes=[pltpu.VMEM((N_LANE_BLOCKS, rows, LANES), jnp.float32)],
        compiler_params=pltpu.CompilerParams(
            dimension_semantics=("arbitrary",), vmem_limit_bytes=VMEM_LIMIT),
        name="merge_mlp",
    )(x2, ys, ya, gates, wglu, bglu, wus, wua, wout, gm, w1, w2, gf)


def _suffix_sum_matrix():
    win = 2 * ATT_BLOCK
    r = jnp.arange(win)[:, None]
    c = jnp.arange(win + LANES)[None, :]
    return jnp.where((c >= win) | (r > c), 1.0, 0.0).astype(jnp.bfloat16)


def kernel(x, norm_mix, w_in, A_re, A_im, log_dt, B_re, B_im, C_re, C_im, D_skip, w_glu, b_glu,
           w_up_ssm, w_up_attn, w_gate, b_gate, w_out, norm_mlp, w_ff1, w_ff2, norm_final):
    bsz, seq, _ = x.shape
    tokens = bsz * seq
    bf = jnp.bfloat16
    assert norm_mix.shape[0] == 1, "single layer"
    assert seq % (SSM_CHUNK * 8) == 0 and seq % ATT_BLOCK == 0
    assert tokens % PROJ_ROWS == 0 and tokens % MERGE_ROWS == 0 and seq % min(PROJ_ROWS, seq) == 0

    x2 = x.reshape(tokens, D_MODEL)
    ussm, q, k, v, gates = _inproj_call(x2, norm_mix, w_in[0].astype(bf), w_gate[0].astype(bf),
                                        b_gate, bsz, seq)

    toep, inj, ro, a_tab, d_tab = _ssm_tables(
        A_re[0], A_im[0], log_dt[0], B_re[0], B_im[0], C_re[0], C_im[0], D_skip[0])
    e_inj, e_ro = _expansion_matrices()
    ys = _ssm_call(ussm, toep, inj, ro, e_inj, e_ro, a_tab, d_tab, bsz)

    ya = _attn_call(q.reshape(bsz, seq, D_ATTN), k.reshape(bsz, seq, D_ATTN),
                    v.reshape(bsz, seq, D_ATTN), _suffix_sum_matrix()).reshape(tokens, D_ATTN)

    out = _merge_call(x2, ys, ya, gates, w_glu[0].astype(bf), b_glu, w_up_ssm[0].astype(bf),
                      w_up_attn[0].astype(bf), w_out[0].astype(bf), norm_mlp,
                      w_ff1[0].astype(bf), w_ff2[0].astype(bf), norm_final.reshape(1, D_MODEL))
    return out.reshape(bsz, seq, D_MODEL)
```

```python
import functools

import jax
import jax.numpy as jnp
from jax import lax
from jax.experimental import pallas as pl
from jax.experimental.pallas import tpu as pltpu

D_MODEL = 1024
D_SSM = 512
SSM_GROUP = 16
N_GROUPS = 32
STATE = 64
N_HEADS = 8
HEAD_DIM = 64
D_ATTN = 512
D_FF = 4096
EPS = 1e-6

LANES = 128
MXU_TILE = 256
VMEM_LIMIT = 52 * 1024 * 1024

SSM_CHUNK = 16
N_LANE_BLOCKS = D_SSM // LANES
GROUPS_PER_BLOCK = LANES // SSM_GROUP
STATE_COLS = 2 * GROUPS_PER_BLOCK * STATE
CHUNK_COLS = SSM_CHUNK * LANES
HALF_LANES = LANES // 2
HALF_COLS = SSM_CHUNK * HALF_LANES
HALF_STATE = STATE_COLS // 2
SCAN_POWERS = (1, 2, 4, 8)
SCAN_PAD = 8
ATT_BLOCK = 128
ATT_SKIP_SUM = 64.0
ATT_MASKED = 1e30
LOG2_E = 1.4426950408889634
NT_DIMS = (((1,), (1,)), ((), ()))
PROJ_ROWS = 512
MERGE_ROWS = 512
FF_CHUNK = 1024
ATT_ROWS = 512


def _rmsnorm_f32(x, g):
    ms = jnp.mean(x * x, axis=-1, keepdims=True)
    return x * lax.rsqrt(ms + EPS) * g


def _inproj_kernel(x_ref, g_ref, w_ref, wg_ref, bg_ref,
                   ussm_ref, q_ref, k_ref, v_ref, gates_ref, pssm_scr):
    x = x_ref[...]
    u = _rmsnorm_f32(x, g_ref[...]).astype(jnp.bfloat16)
    rows = x.shape[0]
    chunk_rows = rows // SSM_CHUNK

    p_ssm = jnp.dot(u, w_ref[:, 0:D_SSM], preferred_element_type=jnp.float32)
    for blk in range(N_LANE_BLOCKS):
        pssm_scr[blk] = p_ssm[:, blk * LANES:(blk + 1) * LANES]
    low_lanes = lax.broadcasted_iota(jnp.int32, (chunk_rows, LANES), 1) < HALF_LANES
    for m in range(SSM_CHUNK // 2):
        for blk in range(N_LANE_BLOCKS):
            even = pssm_scr[blk, pl.ds(2 * m, chunk_rows, stride=SSM_CHUNK), :]
            odd = pssm_scr[blk, pl.ds(2 * m + 1, chunk_rows, stride=SSM_CHUNK), :]
            first = jnp.where(low_lanes, even, pltpu.roll(odd, HALF_LANES, 1))
            second = jnp.where(low_lanes, pltpu.roll(even, HALF_LANES, 1), odd)
            ussm_ref[blk, :, m * LANES:(m + 1) * LANES] = first.astype(jnp.bfloat16)
            ussm_ref[blk, :, HALF_COLS + m * LANES:HALF_COLS + (m + 1) * LANES] = (
                second.astype(jnp.bfloat16))

    p_q = jnp.dot(u, w_ref[:, D_SSM:D_SSM + D_ATTN], preferred_element_type=jnp.float32)
    q_ref[...] = (p_q * (HEAD_DIM ** -0.5)).astype(jnp.bfloat16)

    p_k = jnp.dot(u, w_ref[:, D_SSM + D_ATTN:D_SSM + 2 * D_ATTN],
                  preferred_element_type=jnp.float32)
    k_ref[...] = p_k.astype(jnp.bfloat16)

    p_v = jnp.dot(u, w_ref[:, D_SSM + 2 * D_ATTN:D_SSM + 3 * D_ATTN],
                  preferred_element_type=jnp.float32)
    v_ref[...] = p_v.astype(jnp.bfloat16)

    for c in range(2 * D_MODEL // 512):
        pg = jnp.dot(u, wg_ref[:, c * 512:(c + 1) * 512], preferred_element_type=jnp.float32)
        pg = pg + bg_ref[:, c * 512:(c + 1) * 512]
        gates_ref[:, c * 512:(c + 1) * 512] = jax.nn.sigmoid(pg).astype(jnp.bfloat16)


def _inproj_call(x2, g, w_in, w_gate, bg, bsz, seq):
    tokens = bsz * seq
    rows = min(PROJ_ROWS, seq)
    const = lambda i: (0, 0)
    return pl.pallas_call(
        _inproj_kernel,
        grid=(tokens // rows,),
        in_specs=[
            pl.BlockSpec((rows, D_MODEL), lambda i: (i, 0)),
            pl.BlockSpec((1, D_MODEL), const),
            pl.BlockSpec(w_in.shape, const),
            pl.BlockSpec(w_gate.shape, const),
            pl.BlockSpec((1, 2 * D_MODEL), const),
        ],
        out_specs=[
            pl.BlockSpec((N_LANE_BLOCKS, rows // SSM_CHUNK, CHUNK_COLS), lambda i: (0, i, 0)),
            pl.BlockSpec((rows, D_ATTN), lambda i: (i, 0)),
            pl.BlockSpec((rows, D_ATTN), lambda i: (i, 0)),
            pl.BlockSpec((rows, D_ATTN), lambda i: (i, 0)),
            pl.BlockSpec((rows, 2 * D_MODEL), lambda i: (i, 0)),
        ],
        out_shape=[
            jax.ShapeDtypeStruct((N_LANE_BLOCKS, tokens // SSM_CHUNK, CHUNK_COLS), jnp.bfloat16),
            jax.ShapeDtypeStruct((tokens, D_ATTN), jnp.bfloat16),
            jax.ShapeDtypeStruct((tokens, D_ATTN), jnp.bfloat16),
            jax.ShapeDtypeStruct((tokens, D_ATTN), jnp.bfloat16),
            jax.ShapeDtypeStruct((tokens, 2 * D_MODEL), jnp.bfloat16),
        ],
        scratch_shapes=[pltpu.VMEM((N_LANE_BLOCKS, rows, LANES), jnp.float32)],
        compiler_params=pltpu.CompilerParams(
            dimension_semantics=("arbitrary",), vmem_limit_bytes=VMEM_LIMIT),
        name="inproj",
    )(x2, g, w_in, w_gate, bg)


def _ssm_tables(A_re, A_im, log_dt, B_re, B_im, C_re, C_im, D_skip):
    f32 = jnp.float32
    L = SSM_CHUNK
    nb, gb = N_LANE_BLOCKS, GROUPS_PER_BLOCK
    ar, ai = A_re.astype(f32), A_im.astype(f32)
    dt = jnp.exp(log_dt.astype(f32))[:, None]
    tau = jnp.arange(L + 1, dtype=f32)[:, None, None]
    mag = jnp.exp(ar[None] * dt[None] * tau)
    ang = ai[None] * dt[None] * tau
    pw_re, pw_im = mag * jnp.cos(ang), mag * jnp.sin(ang)
    num_re, num_im = pw_re[1] - 1.0, pw_im[1]
    den = ar * ar + ai * ai
    cf_re = (num_re * ar + num_im * ai) / den
    cf_im = (num_im * ar - num_re * ai) / den
    br, bi = B_re.astype(f32), B_im.astype(f32)
    bb_re = cf_re[..., None] * br - cf_im[..., None] * bi
    bb_im = cf_re[..., None] * bi + cf_im[..., None] * br
    ab_re = pw_re[:L, :, :, None] * bb_re[None] - pw_im[:L, :, :, None] * bb_im[None]
    ab_im = pw_re[:L, :, :, None] * bb_im[None] + pw_im[:L, :, :, None] * bb_re[None]
    cr, ci = C_re.astype(f32), C_im.astype(f32)

    kern = jnp.einsum('gdp,tgpc->tgcd', cr, ab_re) - jnp.einsum('gdp,tgpc->tgcd', ci, ab_im)
    kern = kern.reshape(L, nb, LANES, SSM_GROUP).transpose(1, 0, 2, 3)
    lane_idx = jnp.arange(LANES)
    spread = (jnp.arange(SSM_GROUP)[:, None] == lane_idx[None, :] % SSM_GROUP).astype(f32)
    same_group = lane_idx[:, None] // SSM_GROUP == lane_idx[None, :] // SSM_GROUP
    toep = jnp.where(same_group, jnp.matmul(kern, spread), 0.0)

    pw_cat = jnp.concatenate([pw_re[:L], pw_im[:L]], axis=-1)[::-1]
    pw_swp = jnp.concatenate([pw_im[:L], pw_re[:L]], axis=-1)[::-1]
    bbt_re, bbt_im = bb_re.transpose(0, 2, 1), bb_im.transpose(0, 2, 1)
    b_same = jnp.concatenate([bbt_re, bbt_re], axis=-1)
    b_cross = jnp.concatenate([-bbt_im, bbt_im], axis=-1)
    inj = pw_cat[:, :, None, :] * b_same[None] + pw_swp[:, :, None, :] * b_cross[None]
    inj = inj.reshape(L, nb, 2, HALF_LANES, 2 * STATE).transpose(1, 2, 0, 3, 4)
    inj = inj.reshape(nb, 2, HALF_COLS, 2 * STATE)

    n_gp = HALF_STATE // 2
    crt = cr.transpose(1, 0, 2).reshape(SSM_GROUP, nb, 2, n_gp)
    cit = ci.transpose(1, 0, 2).reshape(SSM_GROUP, nb, 2, n_gp)
    pwr = pw_re[1:L + 1].reshape(L, 1, nb, 2, n_gp)
    pwi = pw_im[1:L + 1].reshape(L, 1, nb, 2, n_gp)
    ro = jnp.concatenate([crt[None] * pwr - cit[None] * pwi,
                          -(crt[None] * pwi + cit[None] * pwr)], axis=-1)
    ro = ro.transpose(2, 3, 0, 1, 4).reshape(nb, 2, L * SSM_GROUP, HALF_STATE)

    t_half = jnp.stack([toep[:, :, h * HALF_LANES:(h + 1) * HALF_LANES,
                             h * HALF_LANES:(h + 1) * HALF_LANES] for h in range(2)], axis=1)
    t_pad = jnp.pad(t_half, ((0, 0), (0, 0), (1, 0), (0, 0), (0, 0)))
    n_d2 = L // 2
    top = jnp.concatenate([t_pad[:, :, 1::2][:, :, :n_d2], t_pad[:, :, 2::2][:, :, :n_d2]], axis=-1)
    bot = jnp.concatenate([t_pad[:, :, 0::2][:, :, :n_d2], t_pad[:, :, 1::2][:, :, :n_d2]], axis=-1)
    toep = jnp.concatenate([top, bot], axis=-2)

    steps = (L * jnp.array(SCAN_POWERS, f32))[:, None, None]
    sc_mag = jnp.exp(ar[None] * dt[None] * steps)
    sc_ang = ai[None] * dt[None] * steps
    a_tab = jnp.stack([sc_mag * jnp.cos(sc_ang), sc_mag * jnp.sin(sc_ang)], axis=1)
    a_tab = a_tab.reshape(2 * len(SCAN_POWERS), nb, gb * STATE).transpose(1, 0, 2)
    d_tab = jnp.tile(D_skip.astype(f32).reshape(nb, 2, 1, HALF_LANES), (1, 1, L, 1))
    d_tab = d_tab.reshape(nb, 1, CHUNK_COLS)
    bf = jnp.bfloat16
    return toep.astype(bf), inj.astype(bf), ro.astype(bf), a_tab, d_tab


def _expansion_matrices():
    gh = GROUPS_PER_BLOCK // 2
    src = jnp.arange(2 * STATE)[:, None]
    dst = jnp.arange(HALF_STATE)[None, :]
    e_inj = (src // STATE == dst // (gh * STATE)) & (src % STATE == dst % STATE)
    src = jnp.arange(SSM_CHUNK * SSM_GROUP)[None, :]
    dst = jnp.arange(HALF_COLS)[:, None]
    e_ro = (src // SSM_GROUP == dst // HALF_LANES) & (src % SSM_GROUP == dst % SSM_GROUP)
    return e_inj.astype(jnp.bfloat16), e_ro.astype(jnp.bfloat16)


def _ssm_kernel(x_ref, toep_ref, inj_ref, ro_ref, einj_ref, ero_ref, a_ref, d_ref, y_ref,
                m_scr, p_scr, q_scr, z_scr, w_scr, hp_scr):
    gh = GROUPS_PER_BLOCK // 2
    half = STATE_COLS // 2
    quarter = HALF_STATE // 2
    n_pairs = SSM_CHUNK // 2
    n_rows = x_ref.shape[1]

    @pl.when(pl.program_id(1) == 0)
    def _():
        def group_of(shape, axis, width):
            idx = lax.broadcasted_iota(jnp.int32, shape, axis)
            return lax.shift_right_logical(idx, width.bit_length() - 1) & (gh - 1)

        m_scr[...] = jnp.zeros_like(m_scr)
        for hf in range(2):
            for sp in range(n_pairs):
                for tp in range(sp, n_pairs):
                    m_scr[hf, sp * LANES:(sp + 1) * LANES, tp * LANES:(tp + 1) * LANES] = (
                        toep_ref[0, hf, tp - sp])
            row_g = group_of((HALF_COLS, HALF_STATE), 0, SSM_GROUP)
            col_h = group_of((HALF_COLS, HALF_STATE), 1, STATE)
            p_full = jnp.dot(inj_ref[0, hf], einj_ref[...], preferred_element_type=jnp.float32)
            p_scr[hf] = jnp.where(row_g == col_h, p_full, 0.0).astype(jnp.bfloat16)
            row_h = group_of((HALF_COLS, HALF_STATE), 0, SSM_GROUP)
            col_g = group_of((HALF_COLS, HALF_STATE), 1, STATE)
            q_t = jnp.dot(ero_ref[...], ro_ref[0, hf], preferred_element_type=jnp.float32)
            q_scr[hf] = jnp.where(row_h == col_g, q_t, 0.0).T.astype(jnp.bfloat16)

    pad = SCAN_PAD
    re, im = slice(0, half), slice(half, STATE_COLS)
    for buf in (z_scr, w_scr, hp_scr):
        buf[0:pad, :] = jnp.zeros((pad, STATE_COLS), jnp.float32)
    for hf in range(2):
        z_h = jnp.dot(x_ref[0, :, hf * HALF_COLS:(hf + 1) * HALF_COLS], p_scr[hf],
                      preferred_element_type=jnp.float32)
        z_scr[pad:pad + n_rows, hf * quarter:(hf + 1) * quarter] = z_h[:, 0:quarter]
        z_scr[pad:pad + n_rows, half + hf * quarter:half + (hf + 1) * quarter] = z_h[:, quarter:]

    def coef(i):
        return a_ref[0, 2 * i:2 * i + 1, :], a_ref[0, 2 * i + 1:2 * i + 2, :]

    def doubling_pass(src, dst, shift, c_re, c_im):
        s_re = src[pad - shift:pad - shift + n_rows, re]
        s_im = src[pad - shift:pad - shift + n_rows, im]
        dst[pad:pad + n_rows, re] = src[pad:pad + n_rows, re] + c_re * s_re - c_im * s_im
        dst[pad:pad + n_rows, im] = src[pad:pad + n_rows, im] + c_re * s_im + c_im * s_re

    doubling_pass(z_scr, w_scr, 1, *coef(0))
    doubling_pass(w_scr, z_scr, 2, *coef(1))
    doubling_pass(z_scr, w_scr, 4, *coef(2))
    c8_re, c8_im = coef(3)

    def tile_step(m, h):
        h_re, h_im = h
        rows8 = pl.ds(pl.multiple_of(pad + 8 * m, 8), 8)
        n_re = w_scr[rows8, re] + c8_re * h_re - c8_im * h_im
        n_im = w_scr[rows8, im] + c8_re * h_im + c8_im * h_re
        hp_scr[rows8, re] = n_re
        hp_scr[rows8, im] = n_im
        return n_re, n_im

    zero = jnp.zeros((8, half), jnp.float32)
    lax.fori_loop(0, n_rows // 8, tile_step, (zero, zero), unroll=8)

    for hf in range(2):
        base = hf * HALF_COLS
        x_h = x_ref[0, :, base:base + HALF_COLS]
        hp_h = jnp.concatenate(
            [hp_scr[pad - 1:pad - 1 + n_rows, hf * quarter:(hf + 1) * quarter],
             hp_scr[pad - 1:pad - 1 + n_rows, half + hf * quarter:half + (hf + 1) * quarter]],
            axis=1).astype(jnp.bfloat16)
        for n in range(HALF_COLS // MXU_TILE):
            lo, hi = n * MXU_TILE, (n + 1) * MXU_TILE
            y = jnp.dot(x_h[:, 0:hi], m_scr[hf, 0:hi, lo:hi], preferred_element_type=jnp.float32)
            y = y + jnp.dot(hp_h, q_scr[hf, :, lo:hi], preferred_element_type=jnp.float32)
            y = y + d_ref[0, :, base + lo:base + hi] * x_h[:, lo:hi].astype(jnp.float32)
            y_ref[0, :, base + lo:base + hi] = jax.nn.gelu(y).astype(jnp.bfloat16)


def _ssm_call(xc, toep, inj, ro, e_inj, e_ro, a_tab, d_tab, bsz):
    nb, total_rows, _ = xc.shape
    n_rows = total_rows // bsz
    per_q = lambda q, b: (q, 0, 0)
    const = lambda q, b: (0, 0)
    return pl.pallas_call(
        _ssm_kernel,
        grid=(nb, bsz),
        in_specs=[
            pl.BlockSpec((1, n_rows, CHUNK_COLS), lambda q, b: (q, b, 0)),
            pl.BlockSpec((1, 2, SSM_CHUNK // 2, LANES, LANES), lambda q, b: (q, 0, 0, 0, 0)),
            pl.BlockSpec((1, 2, HALF_COLS, 2 * STATE), lambda q, b: (q, 0, 0, 0)),
            pl.BlockSpec((1, 2, SSM_CHUNK * SSM_GROUP, HALF_STATE), lambda q, b: (q, 0, 0, 0)),
            pl.BlockSpec(e_inj.shape, const),
            pl.BlockSpec(e_ro.shape, const),
            pl.BlockSpec((1, 2 * len(SCAN_POWERS), STATE_COLS // 2), per_q),
            pl.BlockSpec((1, 1, CHUNK_COLS), per_q),
        ],
        out_specs=pl.BlockSpec((1, n_rows, CHUNK_COLS), lambda q, b: (q, b, 0)),
        out_shape=jax.ShapeDtypeStruct(xc.shape, jnp.bfloat16),
        scratch_shapes=[
            pltpu.VMEM((2, HALF_COLS, HALF_COLS), jnp.bfloat16),
            pltpu.VMEM((2, HALF_COLS, HALF_STATE), jnp.bfloat16),
            pltpu.VMEM((2, HALF_STATE, HALF_COLS), jnp.bfloat16),
            pltpu.VMEM((SCAN_PAD + n_rows, STATE_COLS), jnp.float32),
            pltpu.VMEM((SCAN_PAD + n_rows, STATE_COLS), jnp.float32),
            pltpu.VMEM((SCAN_PAD + n_rows, STATE_COLS), jnp.float32),
        ],
        compiler_params=pltpu.CompilerParams(
            dimension_semantics=("arbitrary", "arbitrary"), vmem_limit_bytes=VMEM_LIMIT),
        name="ssm",
    )(xc, toep, inj, ro, e_inj, e_ro, a_tab, d_tab)


def _merge_mlp_stages(x_ref, ys_ref, ya, gates_ref, wglu_ref, bglu_ref, wus_ref, wua_ref,
                      wout_ref, gm_ref, w1_ref, w2_ref, gf_ref, o_ref, y_scr):
    f32, bf = jnp.float32, jnp.bfloat16
    chunk_rows = ys_ref.shape[1]
    st = {}

    def mix():
        low_lanes = lax.broadcasted_iota(jnp.int32, (chunk_rows, LANES), 1) < HALF_LANES
        for m in range(SSM_CHUNK // 2):
            for blk in range(N_LANE_BLOCKS):
                first = ys_ref[blk, :, m * LANES:(m + 1) * LANES].astype(f32)
                second = ys_ref[blk, :, HALF_COLS + m * LANES:HALF_COLS + (m + 1) * LANES].astype(f32)
                even = jnp.where(low_lanes, first, pltpu.roll(second, HALF_LANES, 1))
                odd = jnp.where(low_lanes, pltpu.roll(first, HALF_LANES, 1), second)
                y_scr[blk, pl.ds(2 * m, chunk_rows, stride=SSM_CHUNK), :] = even
                y_scr[blk, pl.ds(2 * m + 1, chunk_rows, stride=SSM_CHUNK), :] = odd
        y = jnp.concatenate([y_scr[blk] for blk in range(N_LANE_BLOCKS)], axis=-1)
        glu_arg = jnp.dot(y.astype(bf), wglu_ref[...], preferred_element_type=f32) + bglu_ref[...]
        y_ssm = (y * jax.nn.sigmoid(glu_arg)).astype(bf)
        up_s = jnp.dot(y_ssm, wus_ref[...], preferred_element_type=f32)
        up_a = jnp.dot(ya, wua_ref[...], preferred_element_type=f32)
        g_s = gates_ref[:, 0:D_MODEL].astype(f32)
        g_a = gates_ref[:, D_MODEL:2 * D_MODEL].astype(f32)
        st["merged"] = (g_s * up_s + g_a * up_a).astype(bf)

    def project_out():
        h = x_ref[...] + jnp.dot(st["merged"], wout_ref[...], preferred_element_type=f32)
        st["h"] = h
        st["n"] = _rmsnorm_f32(h, gm_ref[...]).astype(bf)

    def mlp_chunk(c):
        cols = slice(c * FF_CHUNK, (c + 1) * FF_CHUNK)
        hid = jnp.maximum(jnp.dot(st["n"], w1_ref[:, cols], preferred_element_type=f32), 0.0)
        st["h"] = st["h"] + jnp.dot((hid * hid).astype(bf), w2_ref[cols, :],
                                    preferred_element_type=f32)

    def finish():
        o_ref[...] = _rmsnorm_f32(st["h"], gf_ref[...])

    chunks = [functools.partial(mlp_chunk, c) for c in range(D_FF // FF_CHUNK)]
    return [mix, project_out] + chunks + [finish]


def _attn_kernel(q_ref, k_ref, v_ref, tri_ref, o_ref, qm_scr, carry_scr, acc_scr):
    blk = ATT_BLOCK
    n_qb = q_ref.shape[1] // blk
    n_pairs = N_HEADS // 2
    head_rows = N_HEADS * blk
    f32, bf = jnp.float32, jnp.bfloat16
    first_qb = pl.program_id(1) * n_qb

    lane = lax.broadcasted_iota(jnp.int32, (blk, LANES), 1)
    first_head = lane < HEAD_DIM
    tri_near = tri_ref[...]
    tri = tri_ref[blk:2 * blk, blk:2 * blk + LANES]

    for c in range(n_qb):
        for pair in range(n_pairs):
            q_pair = q_ref[0, c * blk:(c + 1) * blk, pair * LANES:(pair + 1) * LANES]
            zero = jnp.zeros_like(q_pair)
            lo = c * head_rows + 2 * pair * blk
            qm_scr[lo:lo + blk] = jnp.where(first_head, q_pair, zero)
            qm_scr[lo + blk:lo + 2 * blk] = jnp.where(first_head, zero, q_pair)

    def scores(c, j):
        base = c * head_rows
        return jnp.concatenate(
            [lax.dot_general(qm_scr[base + 2 * p * blk:base + (2 * p + 2) * blk],
                             k_ref[0, pl.ds(pl.multiple_of(j * blk, blk), blk), p * LANES:(p + 1) * LANES],
                             NT_DIMS, preferred_element_type=f32) for p in range(n_pairs)], axis=0)

    def stick(z, later):
        sp = jnp.maximum(z, 0.0) + jnp.log(1.0 + jnp.exp2(jnp.abs(z) * (-LOG2_E)))
        cs = jnp.dot(sp.astype(bf), tri, preferred_element_type=f32)
        log_w = z - sp - cs[:, 0:blk]
        if later is not None:
            log_w = log_w - later
        return jnp.exp2(log_w * LOG2_E).astype(bf), cs[:, blk:blk + LANES]

    def weighted_values(w, j, pair):
        ks = pl.multiple_of(j * blk, blk)
        o2 = jnp.dot(w, v_ref[0, pl.ds(ks, blk), pair * LANES:(pair + 1) * LANES],
                     preferred_element_type=f32)
        return jnp.where(first_head, o2[0:blk], o2[blk:2 * blk])

    row = lax.broadcasted_iota(jnp.int32, (blk, blk), 0)
    col = lax.broadcasted_iota(jnp.int32, (blk, blk), 1)
    pen_diag = jnp.where(col < row, 0.0, ATT_MASKED)
    j_diag = [first_qb + c for c in range(n_qb)]
    j_prev = [jnp.maximum(first_qb - 1, 0)] + j_diag[:-1]

    def near_scores(c):
        base = c * head_rows
        ks_prev = pl.multiple_of(j_prev[c] * blk, blk)
        ks_diag = pl.multiple_of(j_diag[c] * blk, blk)
        return jnp.concatenate(
            [lax.dot_general(
                qm_scr[base + 2 * p * blk:base + (2 * p + 2) * blk],
                jnp.concatenate([k_ref[0, pl.ds(ks_prev, blk), p * LANES:(p + 1) * LANES],
                                 k_ref[0, pl.ds(ks_diag, blk), p * LANES:(p + 1) * LANES]], axis=0),
                NT_DIMS, preferred_element_type=f32) for p in range(n_pairs)], axis=0)

    def masked_near_scores(c):
        z = near_scores(c)
        z_prev = z[:, 0:blk]
        if c == 0:
            z_prev = z_prev - jnp.where(first_qb >= 1, 0.0, ATT_MASKED)
        z_diag = (z[:, blk:2 * blk].reshape(N_HEADS, blk, blk) - pen_diag[None]).reshape(head_rows, blk)
        return jnp.concatenate([z_prev, z_diag], axis=1)

    all_qbs = range(n_qb)
    least = [None] * n_qb

    def phase_scores(qbs):
        return jnp.concatenate([masked_near_scores(c) for c in qbs], axis=0)

    def phase_softplus(z):
        return jnp.maximum(z, 0.0) + jnp.log(1.0 + jnp.exp2(jnp.abs(z) * (-LOG2_E)))

    def phase_suffix(sp):
        return jnp.dot(sp.astype(bf), tri_near[:, 0:2 * blk], preferred_element_type=f32)

    def phase_weights(qbs, z, sp, cs):
        total = cs[:, 0:LANES] + sp[:, 0:LANES]
        carry_scr[qbs[0] * head_rows:(qbs[-1] + 1) * head_rows] = total
        for k, c in enumerate(qbs):
            least[c] = jnp.min(total[k * head_rows:(k + 1) * head_rows], axis=0, keepdims=True)[0, 0]
        return jnp.exp2((z - sp - cs) * LOG2_E).astype(bf)

    def phase_values(qbs, w):
        for k, c in enumerate(qbs):
            ks_prev = pl.multiple_of(j_prev[c] * blk, blk)
            ks_diag = pl.multiple_of(j_diag[c] * blk, blk)
            for pair in range(n_pairs):
                rows_w = slice(k * head_rows + 2 * pair * blk, k * head_rows + (2 * pair + 2) * blk)
                lanes = slice(pair * LANES, (pair + 1) * LANES)
                v_near = jnp.concatenate([v_ref[0, pl.ds(ks_prev, blk), lanes],
                                          v_ref[0, pl.ds(ks_diag, blk), lanes]], axis=0)
                o2 = jnp.dot(w[rows_w], v_near, preferred_element_type=f32)
                acc_scr[c * blk:(c + 1) * blk, lanes] = jnp.where(first_head, o2[0:blk], o2[blk:2 * blk])

    z_near = phase_scores(all_qbs)
    sp = phase_softplus(z_near)
    cs = phase_suffix(sp)
    phase_values(all_qbs, phase_weights(all_qbs, z_near, sp, cs))

    for c in range(n_qb):
        rows_c = slice(c * blk, (c + 1) * blk)
        base = c * head_rows

        def body(state, c=c, rows_c=rows_c, base=base):
            j, _ = state
            carry = jnp.broadcast_to(carry_scr[base:base + head_rows, 0:1], (head_rows, LANES))
            w, sum_j = stick(scores(c, j), carry)
            carry = carry + sum_j
            carry_scr[base:base + head_rows] = carry
            for pair in range(n_pairs):
                acc_scr[rows_c, pair * LANES:(pair + 1) * LANES] += weighted_values(
                    w[2 * pair * blk:(2 * pair + 2) * blk], j, pair)
            return j - 1, jnp.min(carry)

        def cond(state):
            j, least_c = state
            return jnp.logical_and(j >= 0, least_c < ATT_SKIP_SUM)

        lax.while_loop(cond, body, (first_qb + (c - 2), least[c]))

    o_ref[0] = acc_scr[...].astype(o_ref.dtype)


def _attn_call(q3, k3, v3, tri):
    bsz, seq, _ = q3.shape
    blk = ATT_BLOCK
    rows = ATT_ROWS
    n_qb = rows // blk
    return pl.pallas_call(
        _attn_kernel,
        grid=(bsz, seq // rows),
        in_specs=[
            pl.BlockSpec((1, rows, D_ATTN), lambda b, i: (b, i, 0)),
            pl.BlockSpec((1, seq, D_ATTN), lambda b, i: (b, 0, 0)),
            pl.BlockSpec((1, seq, D_ATTN), lambda b, i: (b, 0, 0)),
            pl.BlockSpec(tri.shape, lambda b, i: (0, 0)),
        ],
        out_specs=pl.BlockSpec((1, rows, D_ATTN), lambda b, i: (b, i, 0)),
        out_shape=jax.ShapeDtypeStruct((bsz, seq, D_ATTN), jnp.bfloat16),
        scratch_shapes=[
            pltpu.VMEM((n_qb * N_HEADS * blk, LANES), jnp.bfloat16),
            pltpu.VMEM((n_qb * N_HEADS * blk, LANES), jnp.float32),
            pltpu.VMEM((rows, D_ATTN), jnp.float32),
        ],
        compiler_params=pltpu.CompilerParams(
            dimension_semantics=("arbitrary", "arbitrary"), vmem_limit_bytes=VMEM_LIMIT),
        name="attn",
    )(q3, k3, v3, tri)


def _merge_kernel(x_ref, ys_ref, ya_ref, gates_ref, wglu_ref, bglu_ref, wus_ref, wua_ref,
                  wout_ref, gm_ref, w1_ref, w2_ref, gf_ref, o_ref, y_scr):
    for stage in _merge_mlp_stages(x_ref, ys_ref, ya_ref[...], gates_ref, wglu_ref, bglu_ref,
                                   wus_ref, wua_ref, wout_ref, gm_ref, w1_ref, w2_ref, gf_ref,
                                   o_ref, y_scr):
        stage()


def _merge_call(x2, ys, ya, gates, wglu, bglu, wus, wua, wout, gm, w1, w2, gf):
    tokens = x2.shape[0]
    rows = MERGE_ROWS
    const = lambda i: (0, 0)

    def resident(arr):
        return pl.BlockSpec(arr.shape, const, pipeline_mode=pl.Buffered(1))

    return pl.pallas_call(
        _merge_kernel,
        grid=(tokens // rows,),
        in_specs=[
            pl.BlockSpec((rows, D_MODEL), lambda i: (i, 0)),
            pl.BlockSpec((N_LANE_BLOCKS, rows // SSM_CHUNK, CHUNK_COLS), lambda i: (0, i, 0)),
            pl.BlockSpec((rows, D_ATTN), lambda i: (i, 0)),
            pl.BlockSpec((rows, 2 * D_MODEL), lambda i: (i, 0)),
            resident(wglu), resident(bglu), resident(wus), resident(wua), resident(wout),
            resident(gm), resident(w1), resident(w2), resident(gf),
        ],
        out_specs=pl.BlockSpec((rows, D_MODEL), lambda i: (i, 0)),
        out_shape=jax.ShapeDtypeStruct((tokens, D_MODEL), jnp.float32),
        scratch_shapes=[pltpu.VMEM((N_LANE_BLOCKS, rows, LANES), jnp.float32)],
        compiler_params=pltpu.CompilerParams(
            dimension_semantics=("arbitrary",), vmem_limit_bytes=VMEM_LIMIT),
        name="merge_mlp",
    )(x2, ys, ya, gates, wglu, bglu, wus, wua, wout, gm, w1, w2, gf)


def _suffix_sum_matrix():
    win = 2 * ATT_BLOCK
    r = jnp.arange(win)[:, None]
    c = jnp.arange(win + LANES)[None, :]
    return jnp.where((c >= win) | (r > c), 1.0, 0.0).astype(jnp.bfloat16)


def kernel(x, norm_mix, w_in, A_re, A_im, log_dt, B_re, B_im, C_re, C_im, D_skip, w_glu, b_glu,
           w_up_ssm, w_up_attn, w_gate, b_gate, w_out, norm_mlp, w_ff1, w_ff2, norm_final):
    bsz, seq, _ = x.shape
    tokens = bsz * seq
    bf = jnp.bfloat16
    assert norm_mix.shape[0] == 1, "single layer"
    assert seq % (SSM_CHUNK * 8) == 0 and seq % ATT_BLOCK == 0
    assert tokens % PROJ_ROWS == 0 and tokens % MERGE_ROWS == 0 and seq % min(PROJ_ROWS, seq) == 0

    x2 = x.reshape(tokens, D_MODEL)
    ussm, q, k, v, gates = _inproj_call(x2, norm_mix, w_in[0].astype(bf), w_gate[0].astype(bf),
                                        b_gate, bsz, seq)

    toep, inj, ro, a_tab, d_tab = _ssm_tables(
        A_re[0], A_im[0], log_dt[0], B_re[0], B_im[0], C_re[0], C_im[0], D_skip[0])
    e_inj, e_ro = _expansion_matrices()
    ys = _ssm_call(ussm, toep, inj, ro, e_inj, e_ro, a_tab, d_tab, bsz)

    ya = _attn_call(q.reshape(bsz, seq, D_ATTN), k.reshape(bsz, seq, D_ATTN),
                    v.reshape(bsz, seq, D_ATTN), _suffix_sum_matrix()).reshape(tokens, D_ATTN)

    out = _merge_call(x2, ys, ya, gates, w_glu[0].astype(bf), b_glu, w_up_ssm[0].astype(bf),
                      w_up_attn[0].astype(bf), w_out[0].astype(bf), norm_mlp,
                      w_ff1[0].astype(bf), w_ff2[0].astype(bf), norm_final.reshape(1, D_MODEL))
    return out.reshape(bsz, seq, D_MODEL)
```

```python
import functools

import jax
import jax.numpy as jnp
from jax import lax
from jax.experimental import pallas as pl
from jax.experimental.pallas import tpu as pltpu

D_MODEL = 1024
D_SSM = 512
SSM_GROUP = 16
N_GROUPS = 32
STATE = 64
N_HEADS = 8
HEAD_DIM = 64
D_ATTN = 512
D_FF = 4096
EPS = 1e-6

LANES = 128
MXU_TILE = 256
VMEM_LIMIT = 52 * 1024 * 1024

SSM_CHUNK = 16
N_LANE_BLOCKS = D_SSM // LANES
GROUPS_PER_BLOCK = LANES // SSM_GROUP
STATE_COLS = 2 * GROUPS_PER_BLOCK * STATE
CHUNK_COLS = SSM_CHUNK * LANES
HALF_LANES = LANES // 2
HALF_COLS = SSM_CHUNK * HALF_LANES
HALF_STATE = STATE_COLS // 2
SCAN_POWERS = (1, 2, 4, 8)
SCAN_PAD = 8
ATT_BLOCK = 128
ATT_SKIP_SUM = 64.0
ATT_MASKED = 1e30
LOG2_E = 1.4426950408889634
NT_DIMS = (((1,), (1,)), ((), ()))
GELU_C1 = 0.7978845608028654
GELU_C3 = GELU_C1 * 0.044715
PROJ_ROWS = 512
MERGE_ROWS = 512
FF_CHUNK = 1024
ATT_ROWS = 512


def _rmsnorm_f32(x, g):
    ms = jnp.mean(x * x, axis=-1, keepdims=True)
    return x * lax.rsqrt(ms + EPS) * g


def _inproj_kernel(x_ref, g_ref, w_ref, wg_ref, bg_ref,
                   ussm_ref, q_ref, k_ref, v_ref, gates_ref, pssm_scr):
    x = x_ref[...]
    u = _rmsnorm_f32(x, g_ref[...]).astype(jnp.bfloat16)
    rows = x.shape[0]
    chunk_rows = rows // SSM_CHUNK

    p_ssm = jnp.dot(u, w_ref[:, 0:D_SSM], preferred_element_type=jnp.float32)
    for blk in range(N_LANE_BLOCKS):
        pssm_scr[blk] = p_ssm[:, blk * LANES:(blk + 1) * LANES]
    low_lanes = lax.broadcasted_iota(jnp.int32, (chunk_rows, LANES), 1) < HALF_LANES
    for m in range(SSM_CHUNK // 2):
        for blk in range(N_LANE_BLOCKS):
            even = pssm_scr[blk, pl.ds(2 * m, chunk_rows, stride=SSM_CHUNK), :]
            odd = pssm_scr[blk, pl.ds(2 * m + 1, chunk_rows, stride=SSM_CHUNK), :]
            first = jnp.where(low_lanes, even, pltpu.roll(odd, HALF_LANES, 1))
            second = jnp.where(low_lanes, pltpu.roll(even, HALF_LANES, 1), odd)
            ussm_ref[blk, :, m * LANES:(m + 1) * LANES] = first.astype(jnp.bfloat16)
            ussm_ref[blk, :, HALF_COLS + m * LANES:HALF_COLS + (m + 1) * LANES] = (
                second.astype(jnp.bfloat16))

    p_q = jnp.dot(u, w_ref[:, D_SSM:D_SSM + D_ATTN], preferred_element_type=jnp.float32)
    q_ref[...] = (p_q * (HEAD_DIM ** -0.5)).astype(jnp.bfloat16)

    p_k = jnp.dot(u, w_ref[:, D_SSM + D_ATTN:D_SSM + 2 * D_ATTN],
                  preferred_element_type=jnp.float32)
    k_ref[...] = p_k.astype(jnp.bfloat16)

    p_v = jnp.dot(u, w_ref[:, D_SSM + 2 * D_ATTN:D_SSM + 3 * D_ATTN],
                  preferred_element_type=jnp.float32)
    v_ref[...] = p_v.astype(jnp.bfloat16)

    for c in range(2 * D_MODEL // 512):
        pg = jnp.dot(u, wg_ref[:, c * 512:(c + 1) * 512], preferred_element_type=jnp.float32)
        pg = pg + bg_ref[:, c * 512:(c + 1) * 512]
        gates_ref[:, c * 512:(c + 1) * 512] = jax.nn.sigmoid(pg).astype(jnp.bfloat16)


def _inproj_call(x2, g, w_in, w_gate, bg, bsz, seq):
    tokens = bsz * seq
    rows = min(PROJ_ROWS, seq)
    const = lambda i: (0, 0)
    return pl.pallas_call(
        _inproj_kernel,
        grid=(tokens // rows,),
        in_specs=[
            pl.BlockSpec((rows, D_MODEL), lambda i: (i, 0)),
            pl.BlockSpec((1, D_MODEL), const),
            pl.BlockSpec(w_in.shape, const),
            pl.BlockSpec(w_gate.shape, const),
            pl.BlockSpec((1, 2 * D_MODEL), const),
        ],
        out_specs=[
            pl.BlockSpec((N_LANE_BLOCKS, rows // SSM_CHUNK, CHUNK_COLS), lambda i: (0, i, 0)),
            pl.BlockSpec((rows, D_ATTN), lambda i: (i, 0)),
            pl.BlockSpec((rows, D_ATTN), lambda i: (i, 0)),
            pl.BlockSpec((rows, D_ATTN), lambda i: (i, 0)),
            pl.BlockSpec((rows, 2 * D_MODEL), lambda i: (i, 0)),
        ],
        out_shape=[
            jax.ShapeDtypeStruct((N_LANE_BLOCKS, tokens // SSM_CHUNK, CHUNK_COLS), jnp.bfloat16),
            jax.ShapeDtypeStruct((tokens, D_ATTN), jnp.bfloat16),
            jax.ShapeDtypeStruct((tokens, D_ATTN), jnp.bfloat16),
            jax.ShapeDtypeStruct((tokens, D_ATTN), jnp.bfloat16),
            jax.ShapeDtypeStruct((tokens, 2 * D_MODEL), jnp.bfloat16),
        ],
        scratch_shapes=[pltpu.VMEM((N_LANE_BLOCKS, rows, LANES), jnp.float32)],
        compiler_params=pltpu.CompilerParams(
            dimension_semantics=("arbitrary",), vmem_limit_bytes=VMEM_LIMIT),
        name="inproj",
    )(x2, g, w_in, w_gate, bg)


def _ssm_tables(A_re, A_im, log_dt, B_re, B_im, C_re, C_im, D_skip):
    f32 = jnp.float32
    L = SSM_CHUNK
    nb, gb = N_LANE_BLOCKS, GROUPS_PER_BLOCK
    ar, ai = A_re.astype(f32), A_im.astype(f32)
    dt = jnp.exp(log_dt.astype(f32))[:, None]
    tau = jnp.arange(L + 1, dtype=f32)[:, None, None]
    mag = jnp.exp(ar[None] * dt[None] * tau)
    ang = ai[None] * dt[None] * tau
    pw_re, pw_im = mag * jnp.cos(ang), mag * jnp.sin(ang)
    num_re, num_im = pw_re[1] - 1.0, pw_im[1]
    den = ar * ar + ai * ai
    cf_re = (num_re * ar + num_im * ai) / den
    cf_im = (num_im * ar - num_re * ai) / den
    br, bi = B_re.astype(f32), B_im.astype(f32)
    bb_re = cf_re[..., None] * br - cf_im[..., None] * bi
    bb_im = cf_re[..., None] * bi + cf_im[..., None] * br
    ab_re = pw_re[:L, :, :, None] * bb_re[None] - pw_im[:L, :, :, None] * bb_im[None]
    ab_im = pw_re[:L, :, :, None] * bb_im[None] + pw_im[:L, :, :, None] * bb_re[None]
    cr, ci = C_re.astype(f32), C_im.astype(f32)

    kern = jnp.einsum('gdp,tgpc->tgcd', cr, ab_re) - jnp.einsum('gdp,tgpc->tgcd', ci, ab_im)
    kern = kern.reshape(L, nb, LANES, SSM_GROUP).transpose(1, 0, 2, 3)
    lane_idx = jnp.arange(LANES)
    spread = (jnp.arange(SSM_GROUP)[:, None] == lane_idx[None, :] % SSM_GROUP).astype(f32)
    same_group = lane_idx[:, None] // SSM_GROUP == lane_idx[None, :] // SSM_GROUP
    toep = jnp.where(same_group, jnp.matmul(kern, spread), 0.0)
    d_diag = D_skip.astype(f32).reshape(nb, LANES)[:, :, None] * jnp.eye(LANES, dtype=f32)
    toep = toep.at[:, 0].add(d_diag)

    pw_cat = jnp.concatenate([pw_re[:L], pw_im[:L]], axis=-1)[::-1]
    pw_swp = jnp.concatenate([pw_im[:L], pw_re[:L]], axis=-1)[::-1]
    bbt_re, bbt_im = bb_re.transpose(0, 2, 1), bb_im.transpose(0, 2, 1)
    b_same = jnp.concatenate([bbt_re, bbt_re], axis=-1)
    b_cross = jnp.concatenate([-bbt_im, bbt_im], axis=-1)
    inj = pw_cat[:, :, None, :] * b_same[None] + pw_swp[:, :, None, :] * b_cross[None]
    inj = inj.reshape(L, nb, 2, HALF_LANES, 2 * STATE).transpose(1, 2, 0, 3, 4)
    inj = inj.reshape(nb, 2, HALF_COLS, 2 * STATE)

    n_gp = HALF_STATE // 2
    crt = cr.transpose(1, 0, 2).reshape(SSM_GROUP, nb, 2, n_gp)
    cit = ci.transpose(1, 0, 2).reshape(SSM_GROUP, nb, 2, n_gp)
    pwr = pw_re[1:L + 1].reshape(L, 1, nb, 2, n_gp)
    pwi = pw_im[1:L + 1].reshape(L, 1, nb, 2, n_gp)
    ro = jnp.concatenate([crt[None] * pwr - cit[None] * pwi,
                          -(crt[None] * pwi + cit[None] * pwr)], axis=-1)
    ro = ro.transpose(2, 3, 0, 1, 4).reshape(nb, 2, L * SSM_GROUP, HALF_STATE)

    t_half = jnp.stack([toep[:, :, h * HALF_LANES:(h + 1) * HALF_LANES,
                             h * HALF_LANES:(h + 1) * HALF_LANES] for h in range(2)], axis=1)
    t_pad = jnp.pad(t_half, ((0, 0), (0, 0), (1, 0), (0, 0), (0, 0)))
    n_d2 = L // 2
    top = jnp.concatenate([t_pad[:, :, 1::2][:, :, :n_d2], t_pad[:, :, 2::2][:, :, :n_d2]], axis=-1)
    bot = jnp.concatenate([t_pad[:, :, 0::2][:, :, :n_d2], t_pad[:, :, 1::2][:, :, :n_d2]], axis=-1)
    toep = jnp.concatenate([top, bot], axis=-2)

    steps = (L * jnp.array(SCAN_POWERS, f32))[:, None, None]
    sc_mag = jnp.exp(ar[None] * dt[None] * steps)
    sc_ang = ai[None] * dt[None] * steps
    a_tab = jnp.stack([sc_mag * jnp.cos(sc_ang), sc_mag * jnp.sin(sc_ang)], axis=1)
    a_tab = a_tab.reshape(2 * len(SCAN_POWERS), nb, gb * STATE).transpose(1, 0, 2)
    bf = jnp.bfloat16
    return toep.astype(bf), inj.astype(bf), ro.astype(bf), a_tab


def _expansion_matrices():
    gh = GROUPS_PER_BLOCK // 2
    src = jnp.arange(2 * STATE)[:, None]
    dst = jnp.arange(HALF_STATE)[None, :]
    e_inj = (src // STATE == dst // (gh * STATE)) & (src % STATE == dst % STATE)
    src = jnp.arange(SSM_CHUNK * SSM_GROUP)[None, :]
    dst = jnp.arange(HALF_COLS)[:, None]
    e_ro = (src // SSM_GROUP == dst // HALF_LANES) & (src % SSM_GROUP == dst % SSM_GROUP)
    return e_inj.astype(jnp.bfloat16), e_ro.astype(jnp.bfloat16)


def _gelu_tanh(y):
    half_y = 0.5 * y
    return half_y + half_y * jnp.tanh(y * (GELU_C1 + GELU_C3 * (y * y)))


def _ssm_kernel(x_ref, toep_ref, inj_ref, ro_ref, einj_ref, ero_ref, a_ref, y_ref,
                m_scr, p_scr, q_scr, z_scr, w_scr, hp_scr):
    gh = GROUPS_PER_BLOCK // 2
    half = STATE_COLS // 2
    quarter = HALF_STATE // 2
    n_pairs = SSM_CHUNK // 2
    n_rows = x_ref.shape[1]

    @pl.when(pl.program_id(1) == 0)
    def _():
        def group_of(shape, axis, width):
            idx = lax.broadcasted_iota(jnp.int32, shape, axis)
            return lax.shift_right_logical(idx, width.bit_length() - 1) & (gh - 1)

        m_scr[...] = jnp.zeros_like(m_scr)
        for hf in range(2):
            for sp in range(n_pairs):
                for tp in range(sp, n_pairs):
                    m_scr[hf, sp * LANES:(sp + 1) * LANES, tp * LANES:(tp + 1) * LANES] = (
                        toep_ref[0, hf, tp - sp])
            row_g = group_of((HALF_COLS, HALF_STATE), 0, SSM_GROUP)
            col_h = group_of((HALF_COLS, HALF_STATE), 1, STATE)
            p_full = jnp.dot(inj_ref[0, hf], einj_ref[...], preferred_element_type=jnp.float32)
            p_scr[hf] = jnp.where(row_g == col_h, p_full, 0.0).astype(jnp.bfloat16)
            row_h = group_of((HALF_COLS, HALF_STATE), 0, SSM_GROUP)
            col_g = group_of((HALF_COLS, HALF_STATE), 1, STATE)
            q_t = jnp.dot(ero_ref[...], ro_ref[0, hf], preferred_element_type=jnp.float32)
            q_scr[hf] = jnp.where(row_h == col_g, q_t, 0.0).T.astype(jnp.bfloat16)

    pad = SCAN_PAD
    re, im = slice(0, half), slice(half, STATE_COLS)
    for buf in (z_scr, w_scr, hp_scr):
        buf[0:pad, :] = jnp.zeros((pad, STATE_COLS), jnp.float32)
    for hf in range(2):
        z_h = jnp.dot(x_ref[0, :, hf * HALF_COLS:(hf + 1) * HALF_COLS], p_scr[hf],
                      preferred_element_type=jnp.float32)
        z_scr[pad:pad + n_rows, hf * quarter:(hf + 1) * quarter] = z_h[:, 0:quarter]
        z_scr[pad:pad + n_rows, half + hf * quarter:half + (hf + 1) * quarter] = z_h[:, quarter:]

    def coef(i):
        return a_ref[0, 2 * i:2 * i + 1, :], a_ref[0, 2 * i + 1:2 * i + 2, :]

    def doubling_pass(src, dst, shift, c_re, c_im):
        s_re = src[pad - shift:pad - shift + n_rows, re]
        s_im = src[pad - shift:pad - shift + n_rows, im]
        dst[pad:pad + n_rows, re] = src[pad:pad + n_rows, re] + c_re * s_re - c_im * s_im
        dst[pad:pad + n_rows, im] = src[pad:pad + n_rows, im] + c_re * s_im + c_im * s_re

    doubling_pass(z_scr, w_scr, 1, *coef(0))
    doubling_pass(w_scr, z_scr, 2, *coef(1))
    doubling_pass(z_scr, w_scr, 4, *coef(2))
    c8_re, c8_im = coef(3)

    def tile_step(m, h):
        h_re, h_im = h
        rows8 = pl.ds(pl.multiple_of(pad + 8 * m, 8), 8)
        n_re = w_scr[rows8, re] + c8_re * h_re - c8_im * h_im
        n_im = w_scr[rows8, im] + c8_re * h_im + c8_im * h_re
        hp_scr[rows8, re] = n_re
        hp_scr[rows8, im] = n_im
        return n_re, n_im

    zero = jnp.zeros((8, half), jnp.float32)
    lax.fori_loop(0, n_rows // 8, tile_step, (zero, zero), unroll=8)

    for hf in range(2):
        base = hf * HALF_COLS
        x_h = x_ref[0, :, base:base + HALF_COLS]
        hp_h = jnp.concatenate(
            [hp_scr[pad - 1:pad - 1 + n_rows, hf * quarter:(hf + 1) * quarter],
             hp_scr[pad - 1:pad - 1 + n_rows, half + hf * quarter:half + (hf + 1) * quarter]],
            axis=1).astype(jnp.bfloat16)
        for n in range(HALF_COLS // MXU_TILE):
            lo, hi = n * MXU_TILE, (n + 1) * MXU_TILE
            y = jnp.dot(x_h[:, 0:hi], m_scr[hf, 0:hi, lo:hi], preferred_element_type=jnp.float32)
            y = y + jnp.dot(hp_h, q_scr[hf, :, lo:hi], preferred_element_type=jnp.float32)
            y_ref[0, :, base + lo:base + hi] = _gelu_tanh(y).astype(jnp.bfloat16)


def _ssm_call(xc, toep, inj, ro, e_inj, e_ro, a_tab, bsz):
    nb, total_rows, _ = xc.shape
    n_rows = total_rows // bsz
    per_q = lambda q, b: (q, 0, 0)
    const = lambda q, b: (0, 0)
    return pl.pallas_call(
        _ssm_kernel,
        grid=(nb, bsz),
        in_specs=[
            pl.BlockSpec((1, n_rows, CHUNK_COLS), lambda q, b: (q, b, 0)),
            pl.BlockSpec((1, 2, SSM_CHUNK // 2, LANES, LANES), lambda q, b: (q, 0, 0, 0, 0)),
            pl.BlockSpec((1, 2, HALF_COLS, 2 * STATE), lambda q, b: (q, 0, 0, 0)),
            pl.BlockSpec((1, 2, SSM_CHUNK * SSM_GROUP, HALF_STATE), lambda q, b: (q, 0, 0, 0)),
            pl.BlockSpec(e_inj.shape, const),
            pl.BlockSpec(e_ro.shape, const),
            pl.BlockSpec((1, 2 * len(SCAN_POWERS), STATE_COLS // 2), per_q),
        ],
        out_specs=pl.BlockSpec((1, n_rows, CHUNK_COLS), lambda q, b: (q, b, 0)),
        out_shape=jax.ShapeDtypeStruct(xc.shape, jnp.bfloat16),
        scratch_shapes=[
            pltpu.VMEM((2, HALF_COLS, HALF_COLS), jnp.bfloat16),
            pltpu.VMEM((2, HALF_COLS, HALF_STATE), jnp.bfloat16),
            pltpu.VMEM((2, HALF_STATE, HALF_COLS), jnp.bfloat16),
            pltpu.VMEM((SCAN_PAD + n_rows, STATE_COLS), jnp.float32),
            pltpu.VMEM((SCAN_PAD + n_rows, STATE_COLS), jnp.float32),
            pltpu.VMEM((SCAN_PAD + n_rows, STATE_COLS), jnp.float32),
        ],
        compiler_params=pltpu.CompilerParams(
            dimension_semantics=("arbitrary", "arbitrary"), vmem_limit_bytes=VMEM_LIMIT),
        name="ssm",
    )(xc, toep, inj, ro, e_inj, e_ro, a_tab)


def _merge_mlp_stages(x_ref, ys_ref, ya, gates_ref, wglu_ref, bglu_ref, wus_ref, wua_ref,
                      wout_ref, gm_ref, w1_ref, w2_ref, gf_ref, o_ref, y_scr):
    f32, bf = jnp.float32, jnp.bfloat16
    chunk_rows = ys_ref.shape[1]
    st = {}

    def mix():
        gated_a = gates_ref[:, D_MODEL:2 * D_MODEL].astype(f32) * jnp.dot(
            ya, wua_ref[...], preferred_element_type=f32)
        low_lanes = lax.broadcasted_iota(jnp.int32, (chunk_rows, LANES), 1) < HALF_LANES
        for m in range(SSM_CHUNK // 2):
            for blk in range(N_LANE_BLOCKS):
                first = ys_ref[blk, :, m * LANES:(m + 1) * LANES].astype(f32)
                second = ys_ref[blk, :, HALF_COLS + m * LANES:HALF_COLS + (m + 1) * LANES].astype(f32)
                even = jnp.where(low_lanes, first, pltpu.roll(second, HALF_LANES, 1))
                odd = jnp.where(low_lanes, pltpu.roll(first, HALF_LANES, 1), second)
                y_scr[blk, pl.ds(2 * m, chunk_rows, stride=SSM_CHUNK), :] = even
                y_scr[blk, pl.ds(2 * m + 1, chunk_rows, stride=SSM_CHUNK), :] = odd
        y = jnp.concatenate([y_scr[blk] for blk in range(N_LANE_BLOCKS)], axis=-1)
        glu_arg = jnp.dot(y.astype(bf), wglu_ref[...], preferred_element_type=f32) + bglu_ref[...]
        y_ssm = (y * jax.nn.sigmoid(glu_arg)).astype(bf)
        up_s = jnp.dot(y_ssm, wus_ref[...], preferred_element_type=f32)
        g_s = gates_ref[:, 0:D_MODEL].astype(f32)
        st["merged"] = (g_s * up_s + gated_a).astype(bf)

    def project_out():
        h = x_ref[...] + jnp.dot(st["merged"], wout_ref[...], preferred_element_type=f32)
        st["h"] = h
        st["n"] = _rmsnorm_f32(h, gm_ref[...]).astype(bf)

    def mlp_chunk(c):
        cols = slice(c * FF_CHUNK, (c + 1) * FF_CHUNK)
        hid = jnp.maximum(jnp.dot(st["n"], w1_ref[:, cols], preferred_element_type=f32), 0.0)
        st["h"] = st["h"] + jnp.dot((hid * hid).astype(bf), w2_ref[cols, :],
                                    preferred_element_type=f32)

    def finish():
        o_ref[...] = _rmsnorm_f32(st["h"], gf_ref[...])

    chunks = [functools.partial(mlp_chunk, c) for c in range(D_FF // FF_CHUNK)]
    return [mix, project_out] + chunks + [finish]


def _attn_kernel(q_ref, k_ref, v_ref, tri_ref, o_ref, qm_scr, carry_scr, acc_scr):
    blk = ATT_BLOCK
    n_qb = q_ref.shape[1] // blk
    n_pairs = N_HEADS // 2
    head_rows = N_HEADS * blk
    f32, bf = jnp.float32, jnp.bfloat16
    first_qb = pl.program_id(1) * n_qb

    lane = lax.broadcasted_iota(jnp.int32, (blk, LANES), 1)
    first_head = lane < HEAD_DIM
    tri_near = tri_ref[...]
    tri = tri_ref[blk:2 * blk, blk:2 * blk + LANES]

    for c in range(n_qb):
        for pair in range(n_pairs):
            q_pair = q_ref[0, c * blk:(c + 1) * blk, pair * LANES:(pair + 1) * LANES]
            zero = jnp.zeros_like(q_pair)
            lo = c * head_rows + 2 * pair * blk
            qm_scr[lo:lo + blk] = jnp.where(first_head, q_pair, zero)
            qm_scr[lo + blk:lo + 2 * blk] = jnp.where(first_head, zero, q_pair)

    def scores(c, j):
        base = c * head_rows
        return jnp.concatenate(
            [lax.dot_general(qm_scr[base + 2 * p * blk:base + (2 * p + 2) * blk],
                             k_ref[0, pl.ds(pl.multiple_of(j * blk, blk), blk), p * LANES:(p + 1) * LANES],
                             NT_DIMS, preferred_element_type=f32) for p in range(n_pairs)], axis=0)

    def stick(z, later):
        sp = jnp.maximum(z, 0.0) + jnp.log(1.0 + jnp.exp2(jnp.abs(z) * (-LOG2_E)))
        cs = jnp.dot(sp.astype(bf), tri, preferred_element_type=f32)
        log_w = z - sp - cs[:, 0:blk]
        if later is not None:
            log_w = log_w - later
        return jnp.exp2(log_w * LOG2_E).astype(bf), cs[:, blk:blk + LANES]

    def weighted_values(w, j, pair):
        ks = pl.multiple_of(j * blk, blk)
        o2 = jnp.dot(w, v_ref[0, pl.ds(ks, blk), pair * LANES:(pair + 1) * LANES],
                     preferred_element_type=f32)
        return jnp.where(first_head, o2[0:blk], o2[blk:2 * blk])

    row = lax.broadcasted_iota(jnp.int32, (blk, blk), 0)
    col = lax.broadcasted_iota(jnp.int32, (blk, blk), 1)
    pen_diag = jnp.where(col < row, 0.0, ATT_MASKED)
    j_diag = [first_qb + c for c in range(n_qb)]
    j_prev = [jnp.maximum(first_qb - 1, 0)] + j_diag[:-1]

    def near_scores(c):
        base = c * head_rows
        ks_prev = pl.multiple_of(j_prev[c] * blk, blk)
        ks_diag = pl.multiple_of(j_diag[c] * blk, blk)
        return jnp.concatenate(
            [lax.dot_general(
                qm_scr[base + 2 * p * blk:base + (2 * p + 2) * blk],
                jnp.concatenate([k_ref[0, pl.ds(ks_prev, blk), p * LANES:(p + 1) * LANES],
                                 k_ref[0, pl.ds(ks_diag, blk), p * LANES:(p + 1) * LANES]], axis=0),
                NT_DIMS, preferred_element_type=f32) for p in range(n_pairs)], axis=0)

    def masked_near_scores(c):
        z = near_scores(c)
        z_prev = z[:, 0:blk]
        if c == 0:
            z_prev = z_prev - jnp.where(first_qb >= 1, 0.0, ATT_MASKED)
        z_diag = (z[:, blk:2 * blk].reshape(N_HEADS, blk, blk) - pen_diag[None]).reshape(head_rows, blk)
        return jnp.concatenate([z_prev, z_diag], axis=1)

    all_qbs = range(n_qb)
    least = [None] * n_qb

    def phase_scores(qbs):
        return jnp.concatenate([masked_near_scores(c) for c in qbs], axis=0)

    def phase_softplus(z):
        return jnp.maximum(z, 0.0) + jnp.log(1.0 + jnp.exp2(jnp.abs(z) * (-LOG2_E)))

    def phase_suffix(sp):
        return jnp.dot(sp.astype(bf), tri_near[:, 0:2 * blk], preferred_element_type=f32)

    def phase_weights(qbs, z, sp, cs):
        total = cs[:, 0:LANES] + sp[:, 0:LANES]
        carry_scr[qbs[0] * head_rows:(qbs[-1] + 1) * head_rows] = total
        for k, c in enumerate(qbs):
            least[c] = jnp.min(total[k * head_rows:(k + 1) * head_rows], axis=0, keepdims=True)[0, 0]
        return jnp.exp2((z - sp - cs) * LOG2_E).astype(bf)

    def phase_values(qbs, w):
        for k, c in enumerate(qbs):
            ks_prev = pl.multiple_of(j_prev[c] * blk, blk)
            ks_diag = pl.multiple_of(j_diag[c] * blk, blk)
            for pair in range(n_pairs):
                rows_w = slice(k * head_rows + 2 * pair * blk, k * head_rows + (2 * pair + 2) * blk)
                lanes = slice(pair * LANES, (pair + 1) * LANES)
                v_near = jnp.concatenate([v_ref[0, pl.ds(ks_prev, blk), lanes],
                                          v_ref[0, pl.ds(ks_diag, blk), lanes]], axis=0)
                o2 = jnp.dot(w[rows_w], v_near, preferred_element_type=f32)
                acc_scr[c * blk:(c + 1) * blk, lanes] = jnp.where(first_head, o2[0:blk], o2[blk:2 * blk])

    z_near = phase_scores(all_qbs)
    sp = phase_softplus(z_near)
    cs = phase_suffix(sp)
    phase_values(all_qbs, phase_weights(all_qbs, z_near, sp, cs))

    for c in range(n_qb):
        rows_c = slice(c * blk, (c + 1) * blk)
        base = c * head_rows

        def body(state, c=c, rows_c=rows_c, base=base):
            j, _ = state
            carry = jnp.broadcast_to(carry_scr[base:base + head_rows, 0:1], (head_rows, LANES))
            w, sum_j = stick(scores(c, j), carry)
            carry = carry + sum_j
            carry_scr[base:base + head_rows] = carry
            for pair in range(n_pairs):
                acc_scr[rows_c, pair * LANES:(pair + 1) * LANES] += weighted_values(
                    w[2 * pair * blk:(2 * pair + 2) * blk], j, pair)
            return j - 1, jnp.min(carry)

        def cond(state):
            j, least_c = state
            return jnp.logical_and(j >= 0, least_c < ATT_SKIP_SUM)

        lax.while_loop(cond, body, (first_qb + (c - 2), least[c]))

    o_ref[0] = acc_scr[...].astype(o_ref.dtype)


def _attn_call(q3, k3, v3, tri):
    bsz, seq, _ = q3.shape
    blk = ATT_BLOCK
    rows = ATT_ROWS
    n_qb = rows // blk
    return pl.pallas_call(
        _attn_kernel,
        grid=(bsz, seq // rows),
        in_specs=[
            pl.BlockSpec((1, rows, D_ATTN), lambda b, i: (b, i, 0)),
            pl.BlockSpec((1, seq, D_ATTN), lambda b, i: (b, 0, 0)),
            pl.BlockSpec((1, seq, D_ATTN), lambda b, i: (b, 0, 0)),
            pl.BlockSpec(tri.shape, lambda b, i: (0, 0)),
        ],
        out_specs=pl.BlockSpec((1, rows, D_ATTN), lambda b, i: (b, i, 0)),
        out_shape=jax.ShapeDtypeStruct((bsz, seq, D_ATTN), jnp.bfloat16),
        scratch_shapes=[
            pltpu.VMEM((n_qb * N_HEADS * blk, LANES), jnp.bfloat16),
            pltpu.VMEM((n_qb * N_HEADS * blk, LANES), jnp.float32),
            pltpu.VMEM((rows, D_ATTN), jnp.float32),
        ],
        compiler_params=pltpu.CompilerParams(
            dimension_semantics=("arbitrary", "arbitrary"), vmem_limit_bytes=VMEM_LIMIT),
        name="attn",
    )(q3, k3, v3, tri)


def _merge_kernel(x_ref, ys_ref, ya_ref, gates_ref, wglu_ref, bglu_ref, wus_ref, wua_ref,
                  wout_ref, gm_ref, w1_ref, w2_ref, gf_ref, o_ref, y_scr):
    for stage in _merge_mlp_stages(x_ref, ys_ref, ya_ref[...], gates_ref, wglu_ref, bglu_ref,
                                   wus_ref, wua_ref, wout_ref, gm_ref, w1_ref, w2_ref, gf_ref,
                                   o_ref, y_scr):
        stage()


def _merge_call(x2, ys, ya, gates, wglu, bglu, wus, wua, wout, gm, w1, w2, gf):
    tokens = x2.shape[0]
    rows = MERGE_ROWS
    const = lambda i: (0, 0)

    def resident(arr):
        return pl.BlockSpec(arr.shape, const, pipeline_mode=pl.Buffered(1))

    return pl.pallas_call(
        _merge_kernel,
        grid=(tokens // rows,),
        in_specs=[
            pl.BlockSpec((rows, D_MODEL), lambda i: (i, 0)),
            pl.BlockSpec((N_LANE_BLOCKS, rows // SSM_CHUNK, CHUNK_COLS), lambda i: (0, i, 0)),
            pl.BlockSpec((rows, D_ATTN), lambda i: (i, 0)),
            pl.BlockSpec((rows, 2 * D_MODEL), lambda i: (i, 0)),
            resident(wglu), resident(bglu), resident(wus), resident(wua), resident(wout),
            resident(gm), resident(w1), resident(w2), resident(gf),
        ],
        out_specs=pl.BlockSpec((rows, D_MODEL), lambda i: (i, 0)),
        out_shape=jax.ShapeDtypeStruct((tokens, D_MODEL), jnp.float32),
        scratch_shapes=[pltpu.VMEM((N_LANE_BLOCKS, rows, LANES), jnp.float32)],
        compiler_params=pltpu.CompilerParams(
            dimension_semantics=("arbitrary",), vmem_limit_bytes=VMEM_LIMIT),
        name="merge_mlp",
    )(x2, ys, ya, gates, wglu, bglu, wus, wua, wout, gm, w1, w2, gf)


def _suffix_sum_matrix():
    win = 2 * ATT_BLOCK
    r = jnp.arange(win)[:, None]
    c = jnp.arange(win + LANES)[None, :]
    return jnp.where((c >= win) | (r > c), 1.0, 0.0).astype(jnp.bfloat16)


def kernel(x, norm_mix, w_in, A_re, A_im, log_dt, B_re, B_im, C_re, C_im, D_skip, w_glu, b_glu,
           w_up_ssm, w_up_attn, w_gate, b_gate, w_out, norm_mlp, w_ff1, w_ff2, norm_final):
    bsz, seq, _ = x.shape
    tokens = bsz * seq
    bf = jnp.bfloat16
    assert norm_mix.shape[0] == 1, "single layer"
    assert seq % (SSM_CHUNK * 8) == 0 and seq % ATT_BLOCK == 0
    assert tokens % PROJ_ROWS == 0 and tokens % MERGE_ROWS == 0 and seq % min(PROJ_ROWS, seq) == 0

    x2 = x.reshape(tokens, D_MODEL)
    ussm, q, k, v, gates = _inproj_call(x2, norm_mix, w_in[0].astype(bf), w_gate[0].astype(bf),
                                        b_gate, bsz, seq)

    toep, inj, ro, a_tab = _ssm_tables(
        A_re[0], A_im[0], log_dt[0], B_re[0], B_im[0], C_re[0], C_im[0], D_skip[0])
    e_inj, e_ro = _expansion_matrices()
    ys = _ssm_call(ussm, toep, inj, ro, e_inj, e_ro, a_tab, bsz)

    ya = _attn_call(q.reshape(bsz, seq, D_ATTN), k.reshape(bsz, seq, D_ATTN),
                    v.reshape(bsz, seq, D_ATTN), _suffix_sum_matrix()).reshape(tokens, D_ATTN)

    out = _merge_call(x2, ys, ya, gates, w_glu[0].astype(bf), b_glu, w_up_ssm[0].astype(bf),
                      w_up_attn[0].astype(bf), w_out[0].astype(bf), norm_mlp,
                      w_ff1[0].astype(bf), w_ff2[0].astype(bf), norm_final.reshape(1, D_MODEL))
    return out.reshape(bsz, seq, D_MODEL)
```

```python
import functools

import jax
import jax.numpy as jnp
from jax import lax
from jax.experimental import pallas as pl
from jax.experimental.pallas import tpu as pltpu

D_MODEL = 1024
D_SSM = 512
SSM_GROUP = 16
N_GROUPS = 32
STATE = 64
N_HEADS = 8
HEAD_DIM = 64
D_ATTN = 512
D_FF = 4096
EPS = 1e-6

LANES = 128
MXU_TILE = 256
VMEM_LIMIT = 52 * 1024 * 1024

SSM_CHUNK = 16
N_LANE_BLOCKS = D_SSM // LANES
GROUPS_PER_BLOCK = LANES // SSM_GROUP
STATE_COLS = 2 * GROUPS_PER_BLOCK * STATE
CHUNK_COLS = SSM_CHUNK * LANES
HALF_LANES = LANES // 2
HALF_COLS = SSM_CHUNK * HALF_LANES
HALF_STATE = STATE_COLS // 2
SCAN_POWERS = (1, 2, 4, 8)
SCAN_PAD = 8
ATT_BLOCK = 128
ATT_SKIP_SUM = 64.0
ATT_MASKED = 1e30
LOG2_E = 1.4426950408889634
NT_DIMS = (((1,), (1,)), ((), ()))
GELU_C1 = 0.7978845608028654
GELU_C3 = GELU_C1 * 0.044715
PROJ_ROWS = 1024
MERGE_ROWS = 512
FF_CHUNK = 1024
ATT_ROWS = 512


def _rmsnorm_f32(x, g):
    ms = jnp.mean(x * x, axis=-1, keepdims=True)
    return x * lax.rsqrt(ms + EPS) * g


def _inproj_kernel(x_ref, g_ref, w_ref, wg_ref, bg_ref,
                   ussm_ref, q_ref, k_ref, v_ref, gates_ref, pssm_scr):
    x = x_ref[...]
    u = _rmsnorm_f32(x, g_ref[...]).astype(jnp.bfloat16)
    rows = x.shape[0]
    chunk_rows = rows // SSM_CHUNK

    p_ssm = jnp.dot(u, w_ref[:, 0:D_SSM], preferred_element_type=jnp.float32)
    for blk in range(N_LANE_BLOCKS):
        pssm_scr[blk] = p_ssm[:, blk * LANES:(blk + 1) * LANES]
    low_lanes = lax.broadcasted_iota(jnp.int32, (chunk_rows, LANES), 1) < HALF_LANES
    for m in range(SSM_CHUNK // 2):
        for blk in range(N_LANE_BLOCKS):
            even = pssm_scr[blk, pl.ds(2 * m, chunk_rows, stride=SSM_CHUNK), :]
            odd = pssm_scr[blk, pl.ds(2 * m + 1, chunk_rows, stride=SSM_CHUNK), :]
            first = jnp.where(low_lanes, even, pltpu.roll(odd, HALF_LANES, 1))
            second = jnp.where(low_lanes, pltpu.roll(even, HALF_LANES, 1), odd)
            ussm_ref[blk, :, m * LANES:(m + 1) * LANES] = first.astype(jnp.bfloat16)
            ussm_ref[blk, :, HALF_COLS + m * LANES:HALF_COLS + (m + 1) * LANES] = (
                second.astype(jnp.bfloat16))

    p_q = jnp.dot(u, w_ref[:, D_SSM:D_SSM + D_ATTN], preferred_element_type=jnp.float32)
    q_ref[...] = (p_q * (HEAD_DIM ** -0.5)).astype(jnp.bfloat16)

    p_k = jnp.dot(u, w_ref[:, D_SSM + D_ATTN:D_SSM + 2 * D_ATTN],
                  preferred_element_type=jnp.float32)
    k_ref[...] = p_k.astype(jnp.bfloat16)

    p_v = jnp.dot(u, w_ref[:, D_SSM + 2 * D_ATTN:D_SSM + 3 * D_ATTN],
                  preferred_element_type=jnp.float32)
    v_ref[...] = p_v.astype(jnp.bfloat16)

    for c in range(2 * D_MODEL // 512):
        pg = jnp.dot(u, wg_ref[:, c * 512:(c + 1) * 512], preferred_element_type=jnp.float32)
        pg = pg + bg_ref[:, c * 512:(c + 1) * 512]
        gates_ref[:, c * 512:(c + 1) * 512] = jax.nn.sigmoid(pg).astype(jnp.bfloat16)


def _inproj_call(x2, g, w_in, w_gate, bg, bsz, seq):
    tokens = bsz * seq
    rows = min(PROJ_ROWS, seq)
    const = lambda i: (0, 0)
    return pl.pallas_call(
        _inproj_kernel,
        grid=(tokens // rows,),
        in_specs=[
            pl.BlockSpec((rows, D_MODEL), lambda i: (i, 0)),
            pl.BlockSpec((1, D_MODEL), const),
            pl.BlockSpec(w_in.shape, const),
            pl.BlockSpec(w_gate.shape, const),
            pl.BlockSpec((1, 2 * D_MODEL), const),
        ],
        out_specs=[
            pl.BlockSpec((N_LANE_BLOCKS, rows // SSM_CHUNK, CHUNK_COLS), lambda i: (0, i, 0)),
            pl.BlockSpec((rows, D_ATTN), lambda i: (i, 0)),
            pl.BlockSpec((rows, D_ATTN), lambda i: (i, 0)),
            pl.BlockSpec((rows, D_ATTN), lambda i: (i, 0)),
            pl.BlockSpec((rows, 2 * D_MODEL), lambda i: (i, 0)),
        ],
        out_shape=[
            jax.ShapeDtypeStruct((N_LANE_BLOCKS, tokens // SSM_CHUNK, CHUNK_COLS), jnp.bfloat16),
            jax.ShapeDtypeStruct((tokens, D_ATTN), jnp.bfloat16),
            jax.ShapeDtypeStruct((tokens, D_ATTN), jnp.bfloat16),
            jax.ShapeDtypeStruct((tokens, D_ATTN), jnp.bfloat16),
            jax.ShapeDtypeStruct((tokens, 2 * D_MODEL), jnp.bfloat16),
        ],
        scratch_shapes=[pltpu.VMEM((N_LANE_BLOCKS, rows, LANES), jnp.float32)],
        compiler_params=pltpu.CompilerParams(
            dimension_semantics=("arbitrary",), vmem_limit_bytes=VMEM_LIMIT),
        name="inproj",
    )(x2, g, w_in, w_gate, bg)


def _ssm_tables(A_re, A_im, log_dt, B_re, B_im, C_re, C_im, D_skip):
    f32 = jnp.float32
    L = SSM_CHUNK
    nb, gb = N_LANE_BLOCKS, GROUPS_PER_BLOCK
    ar, ai = A_re.astype(f32), A_im.astype(f32)
    dt = jnp.exp(log_dt.astype(f32))[:, None]
    tau = jnp.arange(L + 1, dtype=f32)[:, None, None]
    mag = jnp.exp(ar[None] * dt[None] * tau)
    ang = ai[None] * dt[None] * tau
    pw_re, pw_im = mag * jnp.cos(ang), mag * jnp.sin(ang)
    num_re, num_im = pw_re[1] - 1.0, pw_im[1]
    den = ar * ar + ai * ai
    cf_re = (num_re * ar + num_im * ai) / den
    cf_im = (num_im * ar - num_re * ai) / den
    br, bi = B_re.astype(f32), B_im.astype(f32)
    bb_re = cf_re[..., None] * br - cf_im[..., None] * bi
    bb_im = cf_re[..., None] * bi + cf_im[..., None] * br
    ab_re = pw_re[:L, :, :, None] * bb_re[None] - pw_im[:L, :, :, None] * bb_im[None]
    ab_im = pw_re[:L, :, :, None] * bb_im[None] + pw_im[:L, :, :, None] * bb_re[None]
    cr, ci = C_re.astype(f32), C_im.astype(f32)

    kern = jnp.einsum('gdp,tgpc->tgcd', cr, ab_re) - jnp.einsum('gdp,tgpc->tgcd', ci, ab_im)
    kern = kern.reshape(L, nb, LANES, SSM_GROUP).transpose(1, 0, 2, 3)
    lane_idx = jnp.arange(LANES)
    spread = (jnp.arange(SSM_GROUP)[:, None] == lane_idx[None, :] % SSM_GROUP).astype(f32)
    same_group = lane_idx[:, None] // SSM_GROUP == lane_idx[None, :] // SSM_GROUP
    toep = jnp.where(same_group, jnp.matmul(kern, spread), 0.0)
    d_diag = D_skip.astype(f32).reshape(nb, LANES)[:, :, None] * jnp.eye(LANES, dtype=f32)
    toep = toep.at[:, 0].add(d_diag)

    pw_cat = jnp.concatenate([pw_re[:L], pw_im[:L]], axis=-1)[::-1]
    pw_swp = jnp.concatenate([pw_im[:L], pw_re[:L]], axis=-1)[::-1]
    bbt_re, bbt_im = bb_re.transpose(0, 2, 1), bb_im.transpose(0, 2, 1)
    b_same = jnp.concatenate([bbt_re, bbt_re], axis=-1)
    b_cross = jnp.concatenate([-bbt_im, bbt_im], axis=-1)
    inj = pw_cat[:, :, None, :] * b_same[None] + pw_swp[:, :, None, :] * b_cross[None]
    inj = inj.reshape(L, nb, 2, HALF_LANES, 2 * STATE).transpose(1, 2, 0, 3, 4)
    inj = inj.reshape(nb, 2, HALF_COLS, 2 * STATE)

    n_gp = HALF_STATE // 2
    crt = cr.transpose(1, 0, 2).reshape(SSM_GROUP, nb, 2, n_gp)
    cit = ci.transpose(1, 0, 2).reshape(SSM_GROUP, nb, 2, n_gp)
    pwr = pw_re[1:L + 1].reshape(L, 1, nb, 2, n_gp)
    pwi = pw_im[1:L + 1].reshape(L, 1, nb, 2, n_gp)
    ro = jnp.concatenate([crt[None] * pwr - cit[None] * pwi,
                          -(crt[None] * pwi + cit[None] * pwr)], axis=-1)
    ro = ro.transpose(2, 3, 0, 1, 4).reshape(nb, 2, L * SSM_GROUP, HALF_STATE)

    t_half = jnp.stack([toep[:, :, h * HALF_LANES:(h + 1) * HALF_LANES,
                             h * HALF_LANES:(h + 1) * HALF_LANES] for h in range(2)], axis=1)
    t_pad = jnp.pad(t_half, ((0, 0), (0, 0), (1, 0), (0, 0), (0, 0)))
    n_d2 = L // 2
    top = jnp.concatenate([t_pad[:, :, 1::2][:, :, :n_d2], t_pad[:, :, 2::2][:, :, :n_d2]], axis=-1)
    bot = jnp.concatenate([t_pad[:, :, 0::2][:, :, :n_d2], t_pad[:, :, 1::2][:, :, :n_d2]], axis=-1)
    toep = jnp.concatenate([top, bot], axis=-2)

    steps = (L * jnp.array(SCAN_POWERS, f32))[:, None, None]
    sc_mag = jnp.exp(ar[None] * dt[None] * steps)
    sc_ang = ai[None] * dt[None] * steps
    a_tab = jnp.stack([sc_mag * jnp.cos(sc_ang), sc_mag * jnp.sin(sc_ang)], axis=1)
    a_tab = a_tab.reshape(2 * len(SCAN_POWERS), nb, gb * STATE).transpose(1, 0, 2)
    bf = jnp.bfloat16
    return toep.astype(bf), inj.astype(bf), ro.astype(bf), a_tab


def _expansion_matrices():
    gh = GROUPS_PER_BLOCK // 2
    src = jnp.arange(2 * STATE)[:, None]
    dst = jnp.arange(HALF_STATE)[None, :]
    e_inj = (src // STATE == dst // (gh * STATE)) & (src % STATE == dst % STATE)
    src = jnp.arange(SSM_CHUNK * SSM_GROUP)[None, :]
    dst = jnp.arange(HALF_COLS)[:, None]
    e_ro = (src // SSM_GROUP == dst // HALF_LANES) & (src % SSM_GROUP == dst % SSM_GROUP)
    return e_inj.astype(jnp.bfloat16), e_ro.astype(jnp.bfloat16)


def _gelu_tanh(y):
    half_y = 0.5 * y
    return half_y + half_y * jnp.tanh(y * (GELU_C1 + GELU_C3 * (y * y)))


def _ssm_kernel(x_ref, toep_ref, inj_ref, ro_ref, einj_ref, ero_ref, a_ref, y_ref,
                m_scr, p_scr, q_scr, z_scr, w_scr, hp_scr):
    gh = GROUPS_PER_BLOCK // 2
    half = STATE_COLS // 2
    quarter = HALF_STATE // 2
    n_pairs = SSM_CHUNK // 2
    n_rows = x_ref.shape[1]

    @pl.when(pl.program_id(1) == 0)
    def _():
        def group_of(shape, axis, width):
            idx = lax.broadcasted_iota(jnp.int32, shape, axis)
            return lax.shift_right_logical(idx, width.bit_length() - 1) & (gh - 1)

        m_scr[...] = jnp.zeros_like(m_scr)
        for hf in range(2):
            for sp in range(n_pairs):
                for tp in range(sp, n_pairs):
                    m_scr[hf, sp * LANES:(sp + 1) * LANES, tp * LANES:(tp + 1) * LANES] = (
                        toep_ref[0, hf, tp - sp])
            row_g = group_of((HALF_COLS, HALF_STATE), 0, SSM_GROUP)
            col_h = group_of((HALF_COLS, HALF_STATE), 1, STATE)
            p_full = jnp.dot(inj_ref[0, hf], einj_ref[...], preferred_element_type=jnp.float32)
            p_scr[hf] = jnp.where(row_g == col_h, p_full, 0.0).astype(jnp.bfloat16)
            row_h = group_of((HALF_COLS, HALF_STATE), 0, SSM_GROUP)
            col_g = group_of((HALF_COLS, HALF_STATE), 1, STATE)
            q_t = jnp.dot(ero_ref[...], ro_ref[0, hf], preferred_element_type=jnp.float32)
            q_scr[hf] = jnp.where(row_h == col_g, q_t, 0.0).T.astype(jnp.bfloat16)

    pad = SCAN_PAD
    re, im = slice(0, half), slice(half, STATE_COLS)
    for buf in (z_scr, w_scr, hp_scr):
        buf[0:pad, :] = jnp.zeros((pad, STATE_COLS), jnp.float32)
    for hf in range(2):
        z_h = jnp.dot(x_ref[0, :, hf * HALF_COLS:(hf + 1) * HALF_COLS], p_scr[hf],
                      preferred_element_type=jnp.float32)
        z_scr[pad:pad + n_rows, hf * quarter:(hf + 1) * quarter] = z_h[:, 0:quarter]
        z_scr[pad:pad + n_rows, half + hf * quarter:half + (hf + 1) * quarter] = z_h[:, quarter:]

    def coef(i):
        return a_ref[0, 2 * i:2 * i + 1, :], a_ref[0, 2 * i + 1:2 * i + 2, :]

    def doubling_pass(src, dst, shift, c_re, c_im):
        s_re = src[pad - shift:pad - shift + n_rows, re]
        s_im = src[pad - shift:pad - shift + n_rows, im]
        dst[pad:pad + n_rows, re] = src[pad:pad + n_rows, re] + c_re * s_re - c_im * s_im
        dst[pad:pad + n_rows, im] = src[pad:pad + n_rows, im] + c_re * s_im + c_im * s_re

    doubling_pass(z_scr, w_scr, 1, *coef(0))
    doubling_pass(w_scr, z_scr, 2, *coef(1))
    doubling_pass(z_scr, w_scr, 4, *coef(2))
    c8_re, c8_im = coef(3)

    def tile_step(m, h):
        h_re, h_im = h
        rows8 = pl.ds(pl.multiple_of(pad + 8 * m, 8), 8)
        n_re = w_scr[rows8, re] + c8_re * h_re - c8_im * h_im
        n_im = w_scr[rows8, im] + c8_re * h_im + c8_im * h_re
        hp_scr[rows8, re] = n_re
        hp_scr[rows8, im] = n_im
        return n_re, n_im

    zero = jnp.zeros((8, half), jnp.float32)
    lax.fori_loop(0, n_rows // 8, tile_step, (zero, zero), unroll=8)

    for hf in range(2):
        base = hf * HALF_COLS
        x_h = x_ref[0, :, base:base + HALF_COLS]
        hp_h = jnp.concatenate(
            [hp_scr[pad - 1:pad - 1 + n_rows, hf * quarter:(hf + 1) * quarter],
             hp_scr[pad - 1:pad - 1 + n_rows, half + hf * quarter:half + (hf + 1) * quarter]],
            axis=1).astype(jnp.bfloat16)
        for n in range(HALF_COLS // MXU_TILE):
            lo, hi = n * MXU_TILE, (n + 1) * MXU_TILE
            y = jnp.dot(x_h[:, 0:hi], m_scr[hf, 0:hi, lo:hi], preferred_element_type=jnp.float32)
            y = y + jnp.dot(hp_h, q_scr[hf, :, lo:hi], preferred_element_type=jnp.float32)
            y_ref[0, :, base + lo:base + hi] = _gelu_tanh(y).astype(jnp.bfloat16)


def _ssm_call(xc, toep, inj, ro, e_inj, e_ro, a_tab, bsz):
    nb, total_rows, _ = xc.shape
    n_rows = total_rows // bsz
    per_q = lambda q, b: (q, 0, 0)
    const = lambda q, b: (0, 0)
    return pl.pallas_call(
        _ssm_kernel,
        grid=(nb, bsz),
        in_specs=[
            pl.BlockSpec((1, n_rows, CHUNK_COLS), lambda q, b: (q, b, 0)),
            pl.BlockSpec((1, 2, SSM_CHUNK // 2, LANES, LANES), lambda q, b: (q, 0, 0, 0, 0)),
            pl.BlockSpec((1, 2, HALF_COLS, 2 * STATE), lambda q, b: (q, 0, 0, 0)),
            pl.BlockSpec((1, 2, SSM_CHUNK * SSM_GROUP, HALF_STATE), lambda q, b: (q, 0, 0, 0)),
            pl.BlockSpec(e_inj.shape, const),
            pl.BlockSpec(e_ro.shape, const),
            pl.BlockSpec((1, 2 * len(SCAN_POWERS), STATE_COLS // 2), per_q),
        ],
        out_specs=pl.BlockSpec((1, n_rows, CHUNK_COLS), lambda q, b: (q, b, 0)),
        out_shape=jax.ShapeDtypeStruct(xc.shape, jnp.bfloat16),
        scratch_shapes=[
            pltpu.VMEM((2, HALF_COLS, HALF_COLS), jnp.bfloat16),
            pltpu.VMEM((2, HALF_COLS, HALF_STATE), jnp.bfloat16),
            pltpu.VMEM((2, HALF_STATE, HALF_COLS), jnp.bfloat16),
            pltpu.VMEM((SCAN_PAD + n_rows, STATE_COLS), jnp.float32),
            pltpu.VMEM((SCAN_PAD + n_rows, STATE_COLS), jnp.float32),
            pltpu.VMEM((SCAN_PAD + n_rows, STATE_COLS), jnp.float32),
        ],
        compiler_params=pltpu.CompilerParams(
            dimension_semantics=("arbitrary", "arbitrary"), vmem_limit_bytes=VMEM_LIMIT),
        name="ssm",
    )(xc, toep, inj, ro, e_inj, e_ro, a_tab)


def _merge_mlp_stages(x_ref, ys_ref, ya, gates_ref, wglu_ref, bglu_ref, wus_ref, wua_ref,
                      wout_ref, gm_ref, w1_ref, w2_ref, gf_ref, o_ref, y_scr):
    f32, bf = jnp.float32, jnp.bfloat16
    chunk_rows = ys_ref.shape[1]
    st = {}

    def mix():
        gated_a = gates_ref[:, D_MODEL:2 * D_MODEL].astype(f32) * jnp.dot(
            ya, wua_ref[...], preferred_element_type=f32)
        low_lanes = lax.broadcasted_iota(jnp.int32, (chunk_rows, LANES), 1) < HALF_LANES
        for m in range(SSM_CHUNK // 2):
            for blk in range(N_LANE_BLOCKS):
                first = ys_ref[blk, :, m * LANES:(m + 1) * LANES].astype(f32)
                second = ys_ref[blk, :, HALF_COLS + m * LANES:HALF_COLS + (m + 1) * LANES].astype(f32)
                even = jnp.where(low_lanes, first, pltpu.roll(second, HALF_LANES, 1))
                odd = jnp.where(low_lanes, pltpu.roll(first, HALF_LANES, 1), second)
                y_scr[blk, pl.ds(2 * m, chunk_rows, stride=SSM_CHUNK), :] = even
                y_scr[blk, pl.ds(2 * m + 1, chunk_rows, stride=SSM_CHUNK), :] = odd
        y = jnp.concatenate([y_scr[blk] for blk in range(N_LANE_BLOCKS)], axis=-1)
        glu_arg = jnp.dot(y.astype(bf), wglu_ref[...], preferred_element_type=f32) + bglu_ref[...]
        y_ssm = (y * jax.nn.sigmoid(glu_arg)).astype(bf)
        up_s = jnp.dot(y_ssm, wus_ref[...], preferred_element_type=f32)
        g_s = gates_ref[:, 0:D_MODEL].astype(f32)
        st["merged"] = (g_s * up_s + gated_a).astype(bf)

    def project_out():
        h = x_ref[...] + jnp.dot(st["merged"], wout_ref[...], preferred_element_type=f32)
        st["h"] = h
        st["n"] = _rmsnorm_f32(h, gm_ref[...]).astype(bf)

    def mlp_chunk(c):
        cols = slice(c * FF_CHUNK, (c + 1) * FF_CHUNK)
        hid = jnp.maximum(jnp.dot(st["n"], w1_ref[:, cols], preferred_element_type=f32), 0.0)
        st["h"] = st["h"] + jnp.dot((hid * hid).astype(bf), w2_ref[cols, :],
                                    preferred_element_type=f32)

    def finish():
        o_ref[...] = _rmsnorm_f32(st["h"], gf_ref[...])

    chunks = [functools.partial(mlp_chunk, c) for c in range(D_FF // FF_CHUNK)]
    return [mix, project_out] + chunks + [finish]


def _attn_kernel(q_ref, k_ref, v_ref, tri_ref, o_ref, qm_scr, carry_scr, acc_scr):
    blk = ATT_BLOCK
    n_qb = q_ref.shape[1] // blk
    n_pairs = N_HEADS // 2
    head_rows = N_HEADS * blk
    f32, bf = jnp.float32, jnp.bfloat16
    first_qb = pl.program_id(1) * n_qb

    lane = lax.broadcasted_iota(jnp.int32, (blk, LANES), 1)
    first_head = lane < HEAD_DIM
    tri_near = tri_ref[...]
    tri = tri_ref[blk:2 * blk, blk:2 * blk + LANES]

    for c in range(n_qb):
        for pair in range(n_pairs):
            q_pair = q_ref[0, c * blk:(c + 1) * blk, pair * LANES:(pair + 1) * LANES]
            zero = jnp.zeros_like(q_pair)
            lo = c * head_rows + 2 * pair * blk
            qm_scr[lo:lo + blk] = jnp.where(first_head, q_pair, zero)
            qm_scr[lo + blk:lo + 2 * blk] = jnp.where(first_head, zero, q_pair)

    def scores(c, j):
        base = c * head_rows
        return jnp.concatenate(
            [lax.dot_general(qm_scr[base + 2 * p * blk:base + (2 * p + 2) * blk],
                             k_ref[0, pl.ds(pl.multiple_of(j * blk, blk), blk), p * LANES:(p + 1) * LANES],
                             NT_DIMS, preferred_element_type=f32) for p in range(n_pairs)], axis=0)

    def stick(z, later):
        sp = jnp.maximum(z, 0.0) + jnp.log(1.0 + jnp.exp2(jnp.abs(z) * (-LOG2_E)))
        cs = jnp.dot(sp.astype(bf), tri, preferred_element_type=f32)
        log_w = z - sp - cs[:, 0:blk]
        if later is not None:
            log_w = log_w - later
        return jnp.exp2(log_w * LOG2_E).astype(bf), cs[:, blk:blk + LANES]

    def weighted_values(w, j, pair):
        ks = pl.multiple_of(j * blk, blk)
        o2 = jnp.dot(w, v_ref[0, pl.ds(ks, blk), pair * LANES:(pair + 1) * LANES],
                     preferred_element_type=f32)
        return jnp.where(first_head, o2[0:blk], o2[blk:2 * blk])

    row = lax.broadcasted_iota(jnp.int32, (blk, blk), 0)
    col = lax.broadcasted_iota(jnp.int32, (blk, blk), 1)
    pen_diag = jnp.where(col < row, 0.0, ATT_MASKED)
    j_diag = [first_qb + c for c in range(n_qb)]
    j_prev = [jnp.maximum(first_qb - 1, 0)] + j_diag[:-1]

    def near_scores(c):
        base = c * head_rows
        ks_prev = pl.multiple_of(j_prev[c] * blk, blk)
        ks_diag = pl.multiple_of(j_diag[c] * blk, blk)
        return jnp.concatenate(
            [lax.dot_general(
                qm_scr[base + 2 * p * blk:base + (2 * p + 2) * blk],
                jnp.concatenate([k_ref[0, pl.ds(ks_prev, blk), p * LANES:(p + 1) * LANES],
                                 k_ref[0, pl.ds(ks_diag, blk), p * LANES:(p + 1) * LANES]], axis=0),
                NT_DIMS, preferred_element_type=f32) for p in range(n_pairs)], axis=0)

    def masked_near_scores(c):
        z = near_scores(c)
        z_prev = z[:, 0:blk]
        if c == 0:
            z_prev = z_prev - jnp.where(first_qb >= 1, 0.0, ATT_MASKED)
        z_diag = (z[:, blk:2 * blk].reshape(N_HEADS, blk, blk) - pen_diag[None]).reshape(head_rows, blk)
        return jnp.concatenate([z_prev, z_diag], axis=1)

    all_qbs = range(n_qb)
    least = [None] * n_qb

    def phase_scores(qbs):
        return jnp.concatenate([masked_near_scores(c) for c in qbs], axis=0)

    def phase_softplus(z):
        return jnp.maximum(z, 0.0) + jnp.log(1.0 + jnp.exp2(jnp.abs(z) * (-LOG2_E)))

    def phase_suffix(sp):
        return jnp.dot(sp.astype(bf), tri_near[:, 0:2 * blk], preferred_element_type=f32)

    def phase_weights(qbs, z, sp, cs):
        total = cs[:, 0:LANES] + sp[:, 0:LANES]
        carry_scr[qbs[0] * head_rows:(qbs[-1] + 1) * head_rows] = total
        for k, c in enumerate(qbs):
            least[c] = jnp.min(total[k * head_rows:(k + 1) * head_rows], axis=0, keepdims=True)[0, 0]
        return jnp.exp2((z - sp - cs) * LOG2_E).astype(bf)

    def phase_values(qbs, w):
        for k, c in enumerate(qbs):
            ks_prev = pl.multiple_of(j_prev[c] * blk, blk)
            ks_diag = pl.multiple_of(j_diag[c] * blk, blk)
            for pair in range(n_pairs):
                rows_w = slice(k * head_rows + 2 * pair * blk, k * head_rows + (2 * pair + 2) * blk)
                lanes = slice(pair * LANES, (pair + 1) * LANES)
                v_near = jnp.concatenate([v_ref[0, pl.ds(ks_prev, blk), lanes],
                                          v_ref[0, pl.ds(ks_diag, blk), lanes]], axis=0)
                o2 = jnp.dot(w[rows_w], v_near, preferred_element_type=f32)
                acc_scr[c * blk:(c + 1) * blk, lanes] = jnp.where(first_head, o2[0:blk], o2[blk:2 * blk])

    z_near = phase_scores(all_qbs)
    sp = phase_softplus(z_near)
    cs = phase_suffix(sp)
    phase_values(all_qbs, phase_weights(all_qbs, z_near, sp, cs))

    for c in range(n_qb):
        rows_c = slice(c * blk, (c + 1) * blk)
        base = c * head_rows

        def body(state, c=c, rows_c=rows_c, base=base):
            j, _ = state
            carry = jnp.broadcast_to(carry_scr[base:base + head_rows, 0:1], (head_rows, LANES))
            w, sum_j = stick(scores(c, j), carry)
            carry = carry + sum_j
            carry_scr[base:base + head_rows] = carry
            for pair in range(n_pairs):
                acc_scr[rows_c, pair * LANES:(pair + 1) * LANES] += weighted_values(
                    w[2 * pair * blk:(2 * pair + 2) * blk], j, pair)
            return j - 1, jnp.min(carry)

        def cond(state):
            j, least_c = state
            return jnp.logical_and(j >= 0, least_c < ATT_SKIP_SUM)

        lax.while_loop(cond, body, (first_qb + (c - 2), least[c]))

    o_ref[0] = acc_scr[...].astype(o_ref.dtype)


def _attn_call(q3, k3, v3, tri):
    bsz, seq, _ = q3.shape
    blk = ATT_BLOCK
    rows = ATT_ROWS
    n_qb = rows // blk
    return pl.pallas_call(
        _attn_kernel,
        grid=(bsz, seq // rows),
        in_specs=[
            pl.BlockSpec((1, rows, D_ATTN), lambda b, i: (b, i, 0)),
            pl.BlockSpec((1, seq, D_ATTN), lambda b, i: (b, 0, 0)),
            pl.BlockSpec((1, seq, D_ATTN), lambda b, i: (b, 0, 0)),
            pl.BlockSpec(tri.shape, lambda b, i: (0, 0)),
        ],
        out_specs=pl.BlockSpec((1, rows, D_ATTN), lambda b, i: (b, i, 0)),
        out_shape=jax.ShapeDtypeStruct((bsz, seq, D_ATTN), jnp.bfloat16),
        scratch_shapes=[
            pltpu.VMEM((n_qb * N_HEADS * blk, LANES), jnp.bfloat16),
            pltpu.VMEM((n_qb * N_HEADS * blk, LANES), jnp.float32),
            pltpu.VMEM((rows, D_ATTN), jnp.float32),
        ],
        compiler_params=pltpu.CompilerParams(
            dimension_semantics=("arbitrary", "arbitrary"), vmem_limit_bytes=VMEM_LIMIT),
        name="attn",
    )(q3, k3, v3, tri)


def _merge_kernel(x_ref, ys_ref, ya_ref, gates_ref, wglu_ref, bglu_ref, wus_ref, wua_ref,
                  wout_ref, gm_ref, w1_ref, w2_ref, gf_ref, o_ref, y_scr):
    for stage in _merge_mlp_stages(x_ref, ys_ref, ya_ref[...], gates_ref, wglu_ref, bglu_ref,
                                   wus_ref, wua_ref, wout_ref, gm_ref, w1_ref, w2_ref, gf_ref,
                                   o_ref, y_scr):
        stage()


def _merge_call(x2, ys, ya, gates, wglu, bglu, wus, wua, wout, gm, w1, w2, gf):
    tokens = x2.shape[0]
    rows = MERGE_ROWS
    const = lambda i: (0, 0)

    def resident(arr):
        return pl.BlockSpec(arr.shape, const, pipeline_mode=pl.Buffered(1))

    return pl.pallas_call(
        _merge_kernel,
        grid=(tokens // rows,),
        in_specs=[
            pl.BlockSpec((rows, D_MODEL), lambda i: (i, 0)),
            pl.BlockSpec((N_LANE_BLOCKS, rows // SSM_CHUNK, CHUNK_COLS), lambda i: (0, i, 0)),
            pl.BlockSpec((rows, D_ATTN), lambda i: (i, 0)),
            pl.BlockSpec((rows, 2 * D_MODEL), lambda i: (i, 0)),
            resident(wglu), resident(bglu), resident(wus), resident(wua), resident(wout),
            resident(gm), resident(w1), resident(w2), resident(gf),
        ],
        out_specs=pl.BlockSpec((rows, D_MODEL), lambda i: (i, 0)),
        out_shape=jax.ShapeDtypeStruct((tokens, D_MODEL), jnp.float32),
        scratch_shapes=[pltpu.VMEM((N_LANE_BLOCKS, rows, LANES), jnp.float32)],
        compiler_params=pltpu.CompilerParams(
            dimension_semantics=("arbitrary",), vmem_limit_bytes=VMEM_LIMIT),
        name="merge_mlp",
    )(x2, ys, ya, gates, wglu, bglu, wus, wua, wout, gm, w1, w2, gf)


def _suffix_sum_matrix():
    win = 2 * ATT_BLOCK
    r = jnp.arange(win)[:, None]
    c = jnp.arange(win + LANES)[None, :]
    return jnp.where((c >= win) | (r > c), 1.0, 0.0).astype(jnp.bfloat16)


def kernel(x, norm_mix, w_in, A_re, A_im, log_dt, B_re, B_im, C_re, C_im, D_skip, w_glu, b_glu,
           w_up_ssm, w_up_attn, w_gate, b_gate, w_out, norm_mlp, w_ff1, w_ff2, norm_final):
    bsz, seq, _ = x.shape
    tokens = bsz * seq
    bf = jnp.bfloat16
    assert norm_mix.shape[0] == 1, "single layer"
    assert seq % (SSM_CHUNK * 8) == 0 and seq % ATT_BLOCK == 0
    assert tokens % PROJ_ROWS == 0 and tokens % MERGE_ROWS == 0 and seq % min(PROJ_ROWS, seq) == 0

    x2 = x.reshape(tokens, D_MODEL)
    ussm, q, k, v, gates = _inproj_call(x2, norm_mix, w_in[0].astype(bf), w_gate[0].astype(bf),
                                        b_gate, bsz, seq)

    toep, inj, ro, a_tab = _ssm_tables(
        A_re[0], A_im[0], log_dt[0], B_re[0], B_im[0], C_re[0], C_im[0], D_skip[0])
    e_inj, e_ro = _expansion_matrices()
    ys = _ssm_call(ussm, toep, inj, ro, e_inj, e_ro, a_tab, bsz)

    ya = _attn_call(q.reshape(bsz, seq, D_ATTN), k.reshape(bsz, seq, D_ATTN),
                    v.reshape(bsz, seq, D_ATTN), _suffix_sum_matrix()).reshape(tokens, D_ATTN)

    out = _merge_call(x2, ys, ya, gates, w_glu[0].astype(bf), b_glu, w_up_ssm[0].astype(bf),
                      w_up_attn[0].astype(bf), w_out[0].astype(bf), norm_mlp,
                      w_ff1[0].astype(bf), w_ff2[0].astype(bf), norm_final.reshape(1, D_MODEL))
    return out.reshape(bsz, seq, D_MODEL)
```

```python
import functools

import jax
import jax.numpy as jnp
from jax import lax
from jax.experimental import pallas as pl
from jax.experimental.pallas import tpu as pltpu

D_MODEL = 1024
D_SSM = 512
SSM_GROUP = 16
N_GROUPS = 32
STATE = 64
N_HEADS = 8
HEAD_DIM = 64
D_ATTN = 512
D_FF = 4096
EPS = 1e-6

LANES = 128
MXU_TILE = 256
VMEM_LIMIT = 52 * 1024 * 1024

SSM_CHUNK = 16
N_LANE_BLOCKS = D_SSM // LANES
GROUPS_PER_BLOCK = LANES // SSM_GROUP
STATE_COLS = 2 * GROUPS_PER_BLOCK * STATE
CHUNK_COLS = SSM_CHUNK * LANES
HALF_LANES = LANES // 2
HALF_COLS = SSM_CHUNK * HALF_LANES
HALF_STATE = STATE_COLS // 2
SCAN_POWERS = (1, 2, 4, 8)
SCAN_PAD = 8
ATT_BLOCK = 128
ATT_SKIP_SUM = 64.0
ATT_MASKED = 1e30
LOG2_E = 1.4426950408889634
NT_DIMS = (((1,), (1,)), ((), ()))
GELU_C1 = 0.7978845608028654
GELU_C3 = GELU_C1 * 0.044715
PROJ_ROWS = 1024
MERGE_ROWS = 512
FF_CHUNK = 1024
ATT_ROWS = 1024


def _rmsnorm_f32(x, g):
    ms = jnp.mean(x * x, axis=-1, keepdims=True)
    return x * lax.rsqrt(ms + EPS) * g


def _inproj_kernel(x_ref, g_ref, w_ref, wg_ref, bg_ref,
                   ussm_ref, q_ref, k_ref, v_ref, gates_ref, pssm_scr):
    x = x_ref[...]
    u = _rmsnorm_f32(x, g_ref[...]).astype(jnp.bfloat16)
    rows = x.shape[0]
    chunk_rows = rows // SSM_CHUNK

    p_ssm = jnp.dot(u, w_ref[:, 0:D_SSM], preferred_element_type=jnp.float32)
    for blk in range(N_LANE_BLOCKS):
        pssm_scr[blk] = p_ssm[:, blk * LANES:(blk + 1) * LANES]
    low_lanes = lax.broadcasted_iota(jnp.int32, (chunk_rows, LANES), 1) < HALF_LANES
    for m in range(SSM_CHUNK // 2):
        for blk in range(N_LANE_BLOCKS):
            even = pssm_scr[blk, pl.ds(2 * m, chunk_rows, stride=SSM_CHUNK), :]
            odd = pssm_scr[blk, pl.ds(2 * m + 1, chunk_rows, stride=SSM_CHUNK), :]
            first = jnp.where(low_lanes, even, pltpu.roll(odd, HALF_LANES, 1))
            second = jnp.where(low_lanes, pltpu.roll(even, HALF_LANES, 1), odd)
            ussm_ref[blk, :, m * LANES:(m + 1) * LANES] = first.astype(jnp.bfloat16)
            ussm_ref[blk, :, HALF_COLS + m * LANES:HALF_COLS + (m + 1) * LANES] = (
                second.astype(jnp.bfloat16))

    p_q = jnp.dot(u, w_ref[:, D_SSM:D_SSM + D_ATTN], preferred_element_type=jnp.float32)
    q_ref[...] = (p_q * (HEAD_DIM ** -0.5)).astype(jnp.bfloat16)

    p_k = jnp.dot(u, w_ref[:, D_SSM + D_ATTN:D_SSM + 2 * D_ATTN],
                  preferred_element_type=jnp.float32)
    k_ref[...] = p_k.astype(jnp.bfloat16)

    p_v = jnp.dot(u, w_ref[:, D_SSM + 2 * D_ATTN:D_SSM + 3 * D_ATTN],
                  preferred_element_type=jnp.float32)
    v_ref[...] = p_v.astype(jnp.bfloat16)

    for c in range(2 * D_MODEL // 512):
        pg = jnp.dot(u, wg_ref[:, c * 512:(c + 1) * 512], preferred_element_type=jnp.float32)
        pg = pg + bg_ref[:, c * 512:(c + 1) * 512]
        gates_ref[:, c * 512:(c + 1) * 512] = jax.nn.sigmoid(pg).astype(jnp.bfloat16)


def _inproj_call(x2, g, w_in, w_gate, bg, bsz, seq):
    tokens = bsz * seq
    rows = min(PROJ_ROWS, seq)
    const = lambda i: (0, 0)
    return pl.pallas_call(
        _inproj_kernel,
        grid=(tokens // rows,),
        in_specs=[
            pl.BlockSpec((rows, D_MODEL), lambda i: (i, 0)),
            pl.BlockSpec((1, D_MODEL), const),
            pl.BlockSpec(w_in.shape, const),
            pl.BlockSpec(w_gate.shape, const),
            pl.BlockSpec((1, 2 * D_MODEL), const),
        ],
        out_specs=[
            pl.BlockSpec((N_LANE_BLOCKS, rows // SSM_CHUNK, CHUNK_COLS), lambda i: (0, i, 0)),
            pl.BlockSpec((rows, D_ATTN), lambda i: (i, 0)),
            pl.BlockSpec((rows, D_ATTN), lambda i: (i, 0)),
            pl.BlockSpec((rows, D_ATTN), lambda i: (i, 0)),
            pl.BlockSpec((rows, 2 * D_MODEL), lambda i: (i, 0)),
        ],
        out_shape=[
            jax.ShapeDtypeStruct((N_LANE_BLOCKS, tokens // SSM_CHUNK, CHUNK_COLS), jnp.bfloat16),
            jax.ShapeDtypeStruct((tokens, D_ATTN), jnp.bfloat16),
            jax.ShapeDtypeStruct((tokens, D_ATTN), jnp.bfloat16),
            jax.ShapeDtypeStruct((tokens, D_ATTN), jnp.bfloat16),
            jax.ShapeDtypeStruct((tokens, 2 * D_MODEL), jnp.bfloat16),
        ],
        scratch_shapes=[pltpu.VMEM((N_LANE_BLOCKS, rows, LANES), jnp.float32)],
        compiler_params=pltpu.CompilerParams(
            dimension_semantics=("arbitrary",), vmem_limit_bytes=VMEM_LIMIT),
        name="inproj",
    )(x2, g, w_in, w_gate, bg)


def _ssm_tables(A_re, A_im, log_dt, B_re, B_im, C_re, C_im, D_skip):
    f32 = jnp.float32
    L = SSM_CHUNK
    nb, gb = N_LANE_BLOCKS, GROUPS_PER_BLOCK
    ar, ai = A_re.astype(f32), A_im.astype(f32)
    dt = jnp.exp(log_dt.astype(f32))[:, None]
    tau = jnp.arange(L + 1, dtype=f32)[:, None, None]
    mag = jnp.exp(ar[None] * dt[None] * tau)
    ang = ai[None] * dt[None] * tau
    pw_re, pw_im = mag * jnp.cos(ang), mag * jnp.sin(ang)
    num_re, num_im = pw_re[1] - 1.0, pw_im[1]
    den = ar * ar + ai * ai
    cf_re = (num_re * ar + num_im * ai) / den
    cf_im = (num_im * ar - num_re * ai) / den
    br, bi = B_re.astype(f32), B_im.astype(f32)
    bb_re = cf_re[..., None] * br - cf_im[..., None] * bi
    bb_im = cf_re[..., None] * bi + cf_im[..., None] * br
    ab_re = pw_re[:L, :, :, None] * bb_re[None] - pw_im[:L, :, :, None] * bb_im[None]
    ab_im = pw_re[:L, :, :, None] * bb_im[None] + pw_im[:L, :, :, None] * bb_re[None]
    cr, ci = C_re.astype(f32), C_im.astype(f32)

    kern = jnp.einsum('gdp,tgpc->tgcd', cr, ab_re) - jnp.einsum('gdp,tgpc->tgcd', ci, ab_im)
    kern = kern.reshape(L, nb, LANES, SSM_GROUP).transpose(1, 0, 2, 3)
    lane_idx = jnp.arange(LANES)
    spread = (jnp.arange(SSM_GROUP)[:, None] == lane_idx[None, :] % SSM_GROUP).astype(f32)
    same_group = lane_idx[:, None] // SSM_GROUP == lane_idx[None, :] // SSM_GROUP
    toep = jnp.where(same_group, jnp.matmul(kern, spread), 0.0)
    d_diag = D_skip.astype(f32).reshape(nb, LANES)[:, :, None] * jnp.eye(LANES, dtype=f32)
    toep = toep.at[:, 0].add(d_diag)

    pw_cat = jnp.concatenate([pw_re[:L], pw_im[:L]], axis=-1)[::-1]
    pw_swp = jnp.concatenate([pw_im[:L], pw_re[:L]], axis=-1)[::-1]
    bbt_re, bbt_im = bb_re.transpose(0, 2, 1), bb_im.transpose(0, 2, 1)
    b_same = jnp.concatenate([bbt_re, bbt_re], axis=-1)
    b_cross = jnp.concatenate([-bbt_im, bbt_im], axis=-1)
    inj = pw_cat[:, :, None, :] * b_same[None] + pw_swp[:, :, None, :] * b_cross[None]
    inj = inj.reshape(L, nb, 2, HALF_LANES, 2 * STATE).transpose(1, 2, 0, 3, 4)
    inj = inj.reshape(nb, 2, HALF_COLS, 2 * STATE)

    n_gp = HALF_STATE // 2
    crt = cr.transpose(1, 0, 2).reshape(SSM_GROUP, nb, 2, n_gp)
    cit = ci.transpose(1, 0, 2).reshape(SSM_GROUP, nb, 2, n_gp)
    pwr = pw_re[1:L + 1].reshape(L, 1, nb, 2, n_gp)
    pwi = pw_im[1:L + 1].reshape(L, 1, nb, 2, n_gp)
    ro = jnp.concatenate([crt[None] * pwr - cit[None] * pwi,
                          -(crt[None] * pwi + cit[None] * pwr)], axis=-1)
    ro = ro.transpose(2, 3, 0, 1, 4).reshape(nb, 2, L * SSM_GROUP, HALF_STATE)

    t_half = jnp.stack([toep[:, :, h * HALF_LANES:(h + 1) * HALF_LANES,
                             h * HALF_LANES:(h + 1) * HALF_LANES] for h in range(2)], axis=1)
    t_pad = jnp.pad(t_half, ((0, 0), (0, 0), (1, 0), (0, 0), (0, 0)))
    n_d2 = L // 2
    top = jnp.concatenate([t_pad[:, :, 1::2][:, :, :n_d2], t_pad[:, :, 2::2][:, :, :n_d2]], axis=-1)
    bot = jnp.concatenate([t_pad[:, :, 0::2][:, :, :n_d2], t_pad[:, :, 1::2][:, :, :n_d2]], axis=-1)
    toep = jnp.concatenate([top, bot], axis=-2)

    steps = (L * jnp.array(SCAN_POWERS, f32))[:, None, None]
    sc_mag = jnp.exp(ar[None] * dt[None] * steps)
    sc_ang = ai[None] * dt[None] * steps
    a_tab = jnp.stack([sc_mag * jnp.cos(sc_ang), sc_mag * jnp.sin(sc_ang)], axis=1)
    a_tab = a_tab.reshape(2 * len(SCAN_POWERS), nb, gb * STATE).transpose(1, 0, 2)
    bf = jnp.bfloat16
    return toep.astype(bf), inj.astype(bf), ro.astype(bf), a_tab


def _expansion_matrices():
    gh = GROUPS_PER_BLOCK // 2
    src = jnp.arange(2 * STATE)[:, None]
    dst = jnp.arange(HALF_STATE)[None, :]
    e_inj = (src // STATE == dst // (gh * STATE)) & (src % STATE == dst % STATE)
    src = jnp.arange(SSM_CHUNK * SSM_GROUP)[None, :]
    dst = jnp.arange(HALF_COLS)[:, None]
    e_ro = (src // SSM_GROUP == dst // HALF_LANES) & (src % SSM_GROUP == dst % SSM_GROUP)
    return e_inj.astype(jnp.bfloat16), e_ro.astype(jnp.bfloat16)


def _gelu_tanh(y):
    half_y = 0.5 * y
    return half_y + half_y * jnp.tanh(y * (GELU_C1 + GELU_C3 * (y * y)))


def _ssm_kernel(x_ref, toep_ref, inj_ref, ro_ref, einj_ref, ero_ref, a_ref, y_ref,
                m_scr, p_scr, q_scr, z_scr, w_scr, hp_scr):
    gh = GROUPS_PER_BLOCK // 2
    half = STATE_COLS // 2
    quarter = HALF_STATE // 2
    n_pairs = SSM_CHUNK // 2
    n_rows = x_ref.shape[1]

    @pl.when(pl.program_id(1) == 0)
    def _():
        def group_of(shape, axis, width):
            idx = lax.broadcasted_iota(jnp.int32, shape, axis)
            return lax.shift_right_logical(idx, width.bit_length() - 1) & (gh - 1)

        m_scr[...] = jnp.zeros_like(m_scr)
        for hf in range(2):
            for sp in range(n_pairs):
                for tp in range(sp, n_pairs):
                    m_scr[hf, sp * LANES:(sp + 1) * LANES, tp * LANES:(tp + 1) * LANES] = (
                        toep_ref[0, hf, tp - sp])
            row_g = group_of((HALF_COLS, HALF_STATE), 0, SSM_GROUP)
            col_h = group_of((HALF_COLS, HALF_STATE), 1, STATE)
            p_full = jnp.dot(inj_ref[0, hf], einj_ref[...], preferred_element_type=jnp.float32)
            p_scr[hf] = jnp.where(row_g == col_h, p_full, 0.0).astype(jnp.bfloat16)
            row_h = group_of((HALF_COLS, HALF_STATE), 0, SSM_GROUP)
            col_g = group_of((HALF_COLS, HALF_STATE), 1, STATE)
            q_t = jnp.dot(ero_ref[...], ro_ref[0, hf], preferred_element_type=jnp.float32)
            q_scr[hf] = jnp.where(row_h == col_g, q_t, 0.0).T.astype(jnp.bfloat16)

    pad = SCAN_PAD
    re, im = slice(0, half), slice(half, STATE_COLS)
    for buf in (z_scr, w_scr, hp_scr):
        buf[0:pad, :] = jnp.zeros((pad, STATE_COLS), jnp.float32)
    for hf in range(2):
        z_h = jnp.dot(x_ref[0, :, hf * HALF_COLS:(hf + 1) * HALF_COLS], p_scr[hf],
                      preferred_element_type=jnp.float32)
        z_scr[pad:pad + n_rows, hf * quarter:(hf + 1) * quarter] = z_h[:, 0:quarter]
        z_scr[pad:pad + n_rows, half + hf * quarter:half + (hf + 1) * quarter] = z_h[:, quarter:]

    def coef(i):
        return a_ref[0, 2 * i:2 * i + 1, :], a_ref[0, 2 * i + 1:2 * i + 2, :]

    def doubling_pass(src, dst, shift, c_re, c_im):
        s_re = src[pad - shift:pad - shift + n_rows, re]
        s_im = src[pad - shift:pad - shift + n_rows, im]
        dst[pad:pad + n_rows, re] = src[pad:pad + n_rows, re] + c_re * s_re - c_im * s_im
        dst[pad:pad + n_rows, im] = src[pad:pad + n_rows, im] + c_re * s_im + c_im * s_re

    doubling_pass(z_scr, w_scr, 1, *coef(0))
    doubling_pass(w_scr, z_scr, 2, *coef(1))
    doubling_pass(z_scr, w_scr, 4, *coef(2))
    c8_re, c8_im = coef(3)

    def tile_step(m, h):
        h_re, h_im = h
        rows8 = pl.ds(pl.multiple_of(pad + 8 * m, 8), 8)
        n_re = w_scr[rows8, re] + c8_re * h_re - c8_im * h_im
        n_im = w_scr[rows8, im] + c8_re * h_im + c8_im * h_re
        hp_scr[rows8, re] = n_re
        hp_scr[rows8, im] = n_im
        return n_re, n_im

    zero = jnp.zeros((8, half), jnp.float32)
    lax.fori_loop(0, n_rows // 8, tile_step, (zero, zero), unroll=8)

    for hf in range(2):
        base = hf * HALF_COLS
        x_h = x_ref[0, :, base:base + HALF_COLS]
        hp_h = jnp.concatenate(
            [hp_scr[pad - 1:pad - 1 + n_rows, hf * quarter:(hf + 1) * quarter],
             hp_scr[pad - 1:pad - 1 + n_rows, half + hf * quarter:half + (hf + 1) * quarter]],
            axis=1).astype(jnp.bfloat16)
        for n in range(HALF_COLS // MXU_TILE):
            lo, hi = n * MXU_TILE, (n + 1) * MXU_TILE
            y = jnp.dot(x_h[:, 0:hi], m_scr[hf, 0:hi, lo:hi], preferred_element_type=jnp.float32)
            y = y + jnp.dot(hp_h, q_scr[hf, :, lo:hi], preferred_element_type=jnp.float32)
            y_ref[0, :, base + lo:base + hi] = _gelu_tanh(y).astype(jnp.bfloat16)


def _ssm_call(xc, toep, inj, ro, e_inj, e_ro, a_tab, bsz):
    nb, total_rows, _ = xc.shape
    n_rows = total_rows // bsz
    per_q = lambda q, b: (q, 0, 0)
    const = lambda q, b: (0, 0)
    return pl.pallas_call(
        _ssm_kernel,
        grid=(nb, bsz),
        in_specs=[
            pl.BlockSpec((1, n_rows, CHUNK_COLS), lambda q, b: (q, b, 0)),
            pl.BlockSpec((1, 2, SSM_CHUNK // 2, LANES, LANES), lambda q, b: (q, 0, 0, 0, 0)),
            pl.BlockSpec((1, 2, HALF_COLS, 2 * STATE), lambda q, b: (q, 0, 0, 0)),
            pl.BlockSpec((1, 2, SSM_CHUNK * SSM_GROUP, HALF_STATE), lambda q, b: (q, 0, 0, 0)),
            pl.BlockSpec(e_inj.shape, const),
            pl.BlockSpec(e_ro.shape, const),
            pl.BlockSpec((1, 2 * len(SCAN_POWERS), STATE_COLS // 2), per_q),
        ],
        out_specs=pl.BlockSpec((1, n_rows, CHUNK_COLS), lambda q, b: (q, b, 0)),
        out_shape=jax.ShapeDtypeStruct(xc.shape, jnp.bfloat16),
        scratch_shapes=[
            pltpu.VMEM((2, HALF_COLS, HALF_COLS), jnp.bfloat16),
            pltpu.VMEM((2, HALF_COLS, HALF_STATE), jnp.bfloat16),
            pltpu.VMEM((2, HALF_STATE, HALF_COLS), jnp.bfloat16),
            pltpu.VMEM((SCAN_PAD + n_rows, STATE_COLS), jnp.float32),
            pltpu.VMEM((SCAN_PAD + n_rows, STATE_COLS), jnp.float32),
            pltpu.VMEM((SCAN_PAD + n_rows, STATE_COLS), jnp.float32),
        ],
        compiler_params=pltpu.CompilerParams(
            dimension_semantics=("arbitrary", "arbitrary"), vmem_limit_bytes=VMEM_LIMIT),
        name="ssm",
    )(xc, toep, inj, ro, e_inj, e_ro, a_tab)


def _merge_mlp_stages(x_ref, ys_ref, ya, gates_ref, wglu_ref, bglu_ref, wus_ref, wua_ref,
                      wout_ref, gm_ref, w1_ref, w2_ref, gf_ref, o_ref, y_scr):
    f32, bf = jnp.float32, jnp.bfloat16
    chunk_rows = ys_ref.shape[1]
    st = {}

    def mix():
        gated_a = gates_ref[:, D_MODEL:2 * D_MODEL].astype(f32) * jnp.dot(
            ya, wua_ref[...], preferred_element_type=f32)
        low_lanes = lax.broadcasted_iota(jnp.int32, (chunk_rows, LANES), 1) < HALF_LANES
        for m in range(SSM_CHUNK // 2):
            for blk in range(N_LANE_BLOCKS):
                first = ys_ref[blk, :, m * LANES:(m + 1) * LANES].astype(f32)
                second = ys_ref[blk, :, HALF_COLS + m * LANES:HALF_COLS + (m + 1) * LANES].astype(f32)
                even = jnp.where(low_lanes, first, pltpu.roll(second, HALF_LANES, 1))
                odd = jnp.where(low_lanes, pltpu.roll(first, HALF_LANES, 1), second)
                y_scr[blk, pl.ds(2 * m, chunk_rows, stride=SSM_CHUNK), :] = even
                y_scr[blk, pl.ds(2 * m + 1, chunk_rows, stride=SSM_CHUNK), :] = odd
        y = jnp.concatenate([y_scr[blk] for blk in range(N_LANE_BLOCKS)], axis=-1)
        glu_arg = jnp.dot(y.astype(bf), wglu_ref[...], preferred_element_type=f32) + bglu_ref[...]
        y_ssm = (y * jax.nn.sigmoid(glu_arg)).astype(bf)
        up_s = jnp.dot(y_ssm, wus_ref[...], preferred_element_type=f32)
        g_s = gates_ref[:, 0:D_MODEL].astype(f32)
        st["merged"] = (g_s * up_s + gated_a).astype(bf)

    def project_out():
        h = x_ref[...] + jnp.dot(st["merged"], wout_ref[...], preferred_element_type=f32)
        st["h"] = h
        st["n"] = _rmsnorm_f32(h, gm_ref[...]).astype(bf)

    def mlp_chunk(c):
        cols = slice(c * FF_CHUNK, (c + 1) * FF_CHUNK)
        hid = jnp.maximum(jnp.dot(st["n"], w1_ref[:, cols], preferred_element_type=f32), 0.0)
        st["h"] = st["h"] + jnp.dot((hid * hid).astype(bf), w2_ref[cols, :],
                                    preferred_element_type=f32)

    def finish():
        o_ref[...] = _rmsnorm_f32(st["h"], gf_ref[...])

    chunks = [functools.partial(mlp_chunk, c) for c in range(D_FF // FF_CHUNK)]
    return [mix, project_out] + chunks + [finish]


def _attn_kernel(q_ref, kc_ref, kp_ref, vc_ref, vp_ref, k_hbm, v_hbm, tri_ref, o_ref,
                 qm_scr, carry_scr, acc_scr, k_buf, v_buf, dma_sem):
    blk = ATT_BLOCK
    n_qb = q_ref.shape[1] // blk
    n_pairs = N_HEADS // 2
    head_rows = N_HEADS * blk
    f32, bf = jnp.float32, jnp.bfloat16
    batch = pl.program_id(0)
    first_qb = pl.program_id(1) * n_qb

    lane = lax.broadcasted_iota(jnp.int32, (blk, LANES), 1)
    first_head = lane < HEAD_DIM
    tri_near = tri_ref[...]
    tri = tri_ref[blk:2 * blk, blk:2 * blk + LANES]

    for c in range(n_qb):
        for pair in range(n_pairs):
            q_pair = q_ref[0, c * blk:(c + 1) * blk, pair * LANES:(pair + 1) * LANES]
            zero = jnp.zeros_like(q_pair)
            lo = c * head_rows + 2 * pair * blk
            qm_scr[lo:lo + blk] = jnp.where(first_head, q_pair, zero)
            qm_scr[lo + blk:lo + 2 * blk] = jnp.where(first_head, zero, q_pair)

    def scores(c):
        base = c * head_rows
        return jnp.concatenate(
            [lax.dot_general(qm_scr[base + 2 * p * blk:base + (2 * p + 2) * blk],
                             k_buf[:, p * LANES:(p + 1) * LANES],
                             NT_DIMS, preferred_element_type=f32) for p in range(n_pairs)], axis=0)

    def stick(z, later):
        sp = jnp.maximum(z, 0.0) + jnp.log(1.0 + jnp.exp2(jnp.abs(z) * (-LOG2_E)))
        cs = jnp.dot(sp.astype(bf), tri, preferred_element_type=f32)
        log_w = z - sp - cs[:, 0:blk]
        if later is not None:
            log_w = log_w - later
        return jnp.exp2(log_w * LOG2_E).astype(bf), cs[:, blk:blk + LANES]

    def weighted_values(w, pair):
        o2 = jnp.dot(w, v_buf[:, pair * LANES:(pair + 1) * LANES],
                     preferred_element_type=f32)
        return jnp.where(first_head, o2[0:blk], o2[blk:2 * blk])

    row = lax.broadcasted_iota(jnp.int32, (blk, blk), 0)
    col = lax.broadcasted_iota(jnp.int32, (blk, blk), 1)
    pen_diag = jnp.where(col < row, 0.0, ATT_MASKED)

    def window(cur_ref, prev_ref, c, lanes):
        before = prev_ref[0, :, lanes] if c == 0 else cur_ref[0, (c - 1) * blk:c * blk, lanes]
        return jnp.concatenate([before, cur_ref[0, c * blk:(c + 1) * blk, lanes]], axis=0)

    def near_scores(c):
        base = c * head_rows
        return jnp.concatenate(
            [lax.dot_general(
                qm_scr[base + 2 * p * blk:base + (2 * p + 2) * blk],
                window(kc_ref, kp_ref, c, slice(p * LANES, (p + 1) * LANES)),
                NT_DIMS, preferred_element_type=f32) for p in range(n_pairs)], axis=0)

    def masked_near_scores(c):
        z = near_scores(c)
        z_prev = z[:, 0:blk]
        if c == 0:
            z_prev = z_prev - jnp.where(first_qb >= 1, 0.0, ATT_MASKED)
        z_diag = (z[:, blk:2 * blk].reshape(N_HEADS, blk, blk) - pen_diag[None]).reshape(head_rows, blk)
        return jnp.concatenate([z_prev, z_diag], axis=1)

    all_qbs = range(n_qb)
    least = [None] * n_qb

    def phase_scores(qbs):
        return jnp.concatenate([masked_near_scores(c) for c in qbs], axis=0)

    def phase_softplus(z):
        return jnp.maximum(z, 0.0) + jnp.log(1.0 + jnp.exp2(jnp.abs(z) * (-LOG2_E)))

    def phase_suffix(sp):
        return jnp.dot(sp.astype(bf), tri_near[:, 0:2 * blk], preferred_element_type=f32)

    def phase_weights(qbs, z, sp, cs):
        total = cs[:, 0:LANES] + sp[:, 0:LANES]
        carry_scr[qbs[0] * head_rows:(qbs[-1] + 1) * head_rows] = total
        for k, c in enumerate(qbs):
            least[c] = jnp.min(total[k * head_rows:(k + 1) * head_rows], axis=0, keepdims=True)[0, 0]
        return jnp.exp2((z - sp - cs) * LOG2_E).astype(bf)

    def phase_values(qbs, w):
        for k, c in enumerate(qbs):
            for pair in range(n_pairs):
                rows_w = slice(k * head_rows + 2 * pair * blk, k * head_rows + (2 * pair + 2) * blk)
                lanes = slice(pair * LANES, (pair + 1) * LANES)
                o2 = jnp.dot(w[rows_w], window(vc_ref, vp_ref, c, lanes),
                             preferred_element_type=f32)
                acc_scr[c * blk:(c + 1) * blk, lanes] = jnp.where(first_head, o2[0:blk], o2[blk:2 * blk])

    z_near = phase_scores(all_qbs)
    sp = phase_softplus(z_near)
    cs = phase_suffix(sp)
    phase_values(all_qbs, phase_weights(all_qbs, z_near, sp, cs))

    for c in range(n_qb):
        rows_c = slice(c * blk, (c + 1) * blk)
        base = c * head_rows

        def body(state, c=c, rows_c=rows_c, base=base):
            j, _ = state
            key_rows = pl.ds(pl.multiple_of(j * blk, blk), blk)
            k_copy = pltpu.make_async_copy(k_hbm.at[batch, key_rows], k_buf, dma_sem.at[0])
            v_copy = pltpu.make_async_copy(v_hbm.at[batch, key_rows], v_buf, dma_sem.at[1])
            k_copy.start()
            v_copy.start()
            k_copy.wait()
            v_copy.wait()
            carry = jnp.broadcast_to(carry_scr[base:base + head_rows, 0:1], (head_rows, LANES))
            w, sum_j = stick(scores(c), carry)
            carry = carry + sum_j
            carry_scr[base:base + head_rows] = carry
            for pair in range(n_pairs):
                acc_scr[rows_c, pair * LANES:(pair + 1) * LANES] += weighted_values(
                    w[2 * pair * blk:(2 * pair + 2) * blk], pair)
            return j - 1, jnp.min(carry)

        def cond(state):
            j, least_c = state
            return jnp.logical_and(j >= 0, least_c < ATT_SKIP_SUM)

        lax.while_loop(cond, body, (first_qb + (c - 2), least[c]))

    o_ref[0] = acc_scr[...].astype(o_ref.dtype)


def _attn_call(q3, k3, v3, tri):
    bsz, seq, _ = q3.shape
    blk = ATT_BLOCK
    rows = ATT_ROWS
    n_qb = rows // blk
    tile = lambda b, i: (b, i, 0)
    block_before = lambda b, i: (b, jnp.maximum(i * n_qb - 1, 0), 0)
    return pl.pallas_call(
        _attn_kernel,
        grid=(bsz, seq // rows),
        in_specs=[
            pl.BlockSpec((1, rows, D_ATTN), tile),
            pl.BlockSpec((1, rows, D_ATTN), tile),
            pl.BlockSpec((1, blk, D_ATTN), block_before),
            pl.BlockSpec((1, rows, D_ATTN), tile),
            pl.BlockSpec((1, blk, D_ATTN), block_before),
            pl.BlockSpec(memory_space=pl.ANY),
            pl.BlockSpec(memory_space=pl.ANY),
            pl.BlockSpec(tri.shape, lambda b, i: (0, 0)),
        ],
        out_specs=pl.BlockSpec((1, rows, D_ATTN), tile),
        out_shape=jax.ShapeDtypeStruct((bsz, seq, D_ATTN), jnp.bfloat16),
        scratch_shapes=[
            pltpu.VMEM((n_qb * N_HEADS * blk, LANES), jnp.bfloat16),
            pltpu.VMEM((n_qb * N_HEADS * blk, LANES), jnp.float32),
            pltpu.VMEM((rows, D_ATTN), jnp.float32),
            pltpu.VMEM((blk, D_ATTN), jnp.bfloat16),
            pltpu.VMEM((blk, D_ATTN), jnp.bfloat16),
            pltpu.SemaphoreType.DMA((2,)),
        ],
        compiler_params=pltpu.CompilerParams(
            dimension_semantics=("arbitrary", "arbitrary"), vmem_limit_bytes=VMEM_LIMIT),
        name="attn",
    )(q3, k3, k3, v3, v3, k3, v3, tri)


def _merge_kernel(x_ref, ys_ref, ya_ref, gates_ref, wglu_ref, bglu_ref, wus_ref, wua_ref,
                  wout_ref, gm_ref, w1_ref, w2_ref, gf_ref, o_ref, y_scr):
    for stage in _merge_mlp_stages(x_ref, ys_ref, ya_ref[...], gates_ref, wglu_ref, bglu_ref,
                                   wus_ref, wua_ref, wout_ref, gm_ref, w1_ref, w2_ref, gf_ref,
                                   o_ref, y_scr):
        stage()


def _merge_call(x2, ys, ya, gates, wglu, bglu, wus, wua, wout, gm, w1, w2, gf):
    tokens = x2.shape[0]
    rows = MERGE_ROWS
    const = lambda i: (0, 0)

    def resident(arr):
        return pl.BlockSpec(arr.shape, const, pipeline_mode=pl.Buffered(1))

    return pl.pallas_call(
        _merge_kernel,
        grid=(tokens // rows,),
        in_specs=[
            pl.BlockSpec((rows, D_MODEL), lambda i: (i, 0)),
            pl.BlockSpec((N_LANE_BLOCKS, rows // SSM_CHUNK, CHUNK_COLS), lambda i: (0, i, 0)),
            pl.BlockSpec((rows, D_ATTN), lambda i: (i, 0)),
            pl.BlockSpec((rows, 2 * D_MODEL), lambda i: (i, 0)),
            resident(wglu), resident(bglu), resident(wus), resident(wua), resident(wout),
            resident(gm), resident(w1), resident(w2), resident(gf),
        ],
        out_specs=pl.BlockSpec((rows, D_MODEL), lambda i: (i, 0)),
        out_shape=jax.ShapeDtypeStruct((tokens, D_MODEL), jnp.float32),
        scratch_shapes=[pltpu.VMEM((N_LANE_BLOCKS, rows, LANES), jnp.float32)],
        compiler_params=pltpu.CompilerParams(
            dimension_semantics=("arbitrary",), vmem_limit_bytes=VMEM_LIMIT),
        name="merge_mlp",
    )(x2, ys, ya, gates, wglu, bglu, wus, wua, wout, gm, w1, w2, gf)


def _suffix_sum_matrix():
    win = 2 * ATT_BLOCK
    r = jnp.arange(win)[:, None]
    c = jnp.arange(win + LANES)[None, :]
    return jnp.where((c >= win) | (r > c), 1.0, 0.0).astype(jnp.bfloat16)


def kernel(x, norm_mix, w_in, A_re, A_im, log_dt, B_re, B_im, C_re, C_im, D_skip, w_glu, b_glu,
           w_up_ssm, w_up_attn, w_gate, b_gate, w_out, norm_mlp, w_ff1, w_ff2, norm_final):
    bsz, seq, _ = x.shape
    tokens = bsz * seq
    bf = jnp.bfloat16
    assert norm_mix.shape[0] == 1, "single layer"
    assert seq % (SSM_CHUNK * 8) == 0 and seq % ATT_BLOCK == 0
    assert tokens % PROJ_ROWS == 0 and tokens % MERGE_ROWS == 0 and seq % min(PROJ_ROWS, seq) == 0

    x2 = x.reshape(tokens, D_MODEL)
    ussm, q, k, v, gates = _inproj_call(x2, norm_mix, w_in[0].astype(bf), w_gate[0].astype(bf),
                                        b_gate, bsz, seq)

    toep, inj, ro, a_tab = _ssm_tables(
        A_re[0], A_im[0], log_dt[0], B_re[0], B_im[0], C_re[0], C_im[0], D_skip[0])
    e_inj, e_ro = _expansion_matrices()
    ys = _ssm_call(ussm, toep, inj, ro, e_inj, e_ro, a_tab, bsz)

    ya = _attn_call(q.reshape(bsz, seq, D_ATTN), k.reshape(bsz, seq, D_ATTN),
                    v.reshape(bsz, seq, D_ATTN), _suffix_sum_matrix()).reshape(tokens, D_ATTN)

    out = _merge_call(x2, ys, ya, gates, w_glu[0].astype(bf), b_glu, w_up_ssm[0].astype(bf),
                      w_up_attn[0].astype(bf), w_out[0].astype(bf), norm_mlp,
                      w_ff1[0].astype(bf), w_ff2[0].astype(bf), norm_final.reshape(1, D_MODEL))
    return out.reshape(bsz, seq, D_MODEL)
```

```python
import functools

import jax
import jax.numpy as jnp
from jax import lax
from jax.experimental import pallas as pl
from jax.experimental.pallas import tpu as pltpu

D_MODEL = 1024
D_SSM = 512
SSM_GROUP = 16
N_GROUPS = 32
STATE = 64
N_HEADS = 8
HEAD_DIM = 64
D_ATTN = 512
D_FF = 4096
EPS = 1e-6

LANES = 128
MXU_TILE = 256
VMEM_LIMIT = 52 * 1024 * 1024

SSM_CHUNK = 16
N_LANE_BLOCKS = D_SSM // LANES
GROUPS_PER_BLOCK = LANES // SSM_GROUP
STATE_COLS = 2 * GROUPS_PER_BLOCK * STATE
CHUNK_COLS = SSM_CHUNK * LANES
N_PARTS = 4
PART_GROUPS = GROUPS_PER_BLOCK // N_PARTS
PART_LANES = LANES // N_PARTS
STEPS_PER_TILE = LANES // PART_LANES
PART_COLS = SSM_CHUNK * PART_LANES
PART_STATE = STATE_COLS // N_PARTS
SCAN_POWERS = (1, 2, 4, 8)
SCAN_PAD = 8
ATT_BLOCK = 128
ATT_SKIP_SUM = 64.0
ATT_MASKED = 1e30
LOG2_E = 1.4426950408889634
NT_DIMS = (((1,), (1,)), ((), ()))
GELU_C1 = 0.7978845608028654
GELU_C3 = GELU_C1 * 0.044715
PROJ_ROWS = 1024
MERGE_ROWS = 512
FF_CHUNK = 1024
ATT_ROWS = 1024


def _rmsnorm_f32(x, g):
    ms = jnp.mean(x * x, axis=-1, keepdims=True)
    return x * lax.rsqrt(ms + EPS) * g


def _lane_part(shape):
    lanes = lax.broadcasted_iota(jnp.int32, shape, len(shape) - 1)
    return lax.shift_right_logical(lanes, PART_LANES.bit_length() - 1)


def _roll_lanes(x, shift):
    shift %= LANES
    return x if shift == 0 else pltpu.roll(x, shift, x.ndim - 1)


def _merge_lane_parts(sources, lane_part):
    out = sources[0]
    for k in range(1, len(sources)):
        out = jnp.where(lane_part == k, sources[k], out)
    return out


def _inproj_kernel(x_ref, g_ref, w_ref, wg_ref, bg_ref,
                   ussm_ref, q_ref, k_ref, v_ref, gates_ref, pssm_scr):
    x = x_ref[...]
    u = _rmsnorm_f32(x, g_ref[...]).astype(jnp.bfloat16)
    rows = x.shape[0]
    chunk_rows = rows // SSM_CHUNK

    p_ssm = jnp.dot(u, w_ref[:, 0:D_SSM], preferred_element_type=jnp.float32)
    for blk in range(N_LANE_BLOCKS):
        pssm_scr[blk] = p_ssm[:, blk * LANES:(blk + 1) * LANES]
    lane_part = _lane_part((chunk_rows, LANES))
    for m in range(SSM_CHUNK // STEPS_PER_TILE):
        for blk in range(N_LANE_BLOCKS):
            steps = [pssm_scr[blk, pl.ds(m * STEPS_PER_TILE + a, chunk_rows, stride=SSM_CHUNK), :]
                     for a in range(STEPS_PER_TILE)]
            for p in range(N_PARTS):
                tile = _merge_lane_parts(
                    [_roll_lanes(steps[a], (a - p) * PART_LANES) for a in range(STEPS_PER_TILE)],
                    lane_part)
                ussm_ref[blk, :, p * PART_COLS + m * LANES:p * PART_COLS + (m + 1) * LANES] = (
                    tile.astype(jnp.bfloat16))

    p_q = jnp.dot(u, w_ref[:, D_SSM:D_SSM + D_ATTN], preferred_element_type=jnp.float32)
    q_ref[...] = (p_q * (HEAD_DIM ** -0.5)).astype(jnp.bfloat16)

    p_k = jnp.dot(u, w_ref[:, D_SSM + D_ATTN:D_SSM + 2 * D_ATTN],
                  preferred_element_type=jnp.float32)
    k_ref[...] = p_k.astype(jnp.bfloat16)

    p_v = jnp.dot(u, w_ref[:, D_SSM + 2 * D_ATTN:D_SSM + 3 * D_ATTN],
                  preferred_element_type=jnp.float32)
    v_ref[...] = p_v.astype(jnp.bfloat16)

    for c in range(2 * D_MODEL // 512):
        pg = jnp.dot(u, wg_ref[:, c * 512:(c + 1) * 512], preferred_element_type=jnp.float32)
        pg = pg + bg_ref[:, c * 512:(c + 1) * 512]
        gates_ref[:, c * 512:(c + 1) * 512] = jax.nn.sigmoid(pg).astype(jnp.bfloat16)


def _inproj_call(x2, g, w_in, w_gate, bg, bsz, seq):
    tokens = bsz * seq
    rows = min(PROJ_ROWS, seq)
    const = lambda i: (0, 0)
    return pl.pallas_call(
        _inproj_kernel,
        grid=(tokens // rows,),
        in_specs=[
            pl.BlockSpec((rows, D_MODEL), lambda i: (i, 0)),
            pl.BlockSpec((1, D_MODEL), const),
            pl.BlockSpec(w_in.shape, const),
            pl.BlockSpec(w_gate.shape, const),
            pl.BlockSpec((1, 2 * D_MODEL), const),
        ],
        out_specs=[
            pl.BlockSpec((N_LANE_BLOCKS, rows // SSM_CHUNK, CHUNK_COLS), lambda i: (0, i, 0)),
            pl.BlockSpec((rows, D_ATTN), lambda i: (i, 0)),
            pl.BlockSpec((rows, D_ATTN), lambda i: (i, 0)),
            pl.BlockSpec((rows, D_ATTN), lambda i: (i, 0)),
            pl.BlockSpec((rows, 2 * D_MODEL), lambda i: (i, 0)),
        ],
        out_shape=[
            jax.ShapeDtypeStruct((N_LANE_BLOCKS, tokens // SSM_CHUNK, CHUNK_COLS), jnp.bfloat16),
            jax.ShapeDtypeStruct((tokens, D_ATTN), jnp.bfloat16),
            jax.ShapeDtypeStruct((tokens, D_ATTN), jnp.bfloat16),
            jax.ShapeDtypeStruct((tokens, D_ATTN), jnp.bfloat16),
            jax.ShapeDtypeStruct((tokens, 2 * D_MODEL), jnp.bfloat16),
        ],
        scratch_shapes=[pltpu.VMEM((N_LANE_BLOCKS, rows, LANES), jnp.float32)],
        compiler_params=pltpu.CompilerParams(
            dimension_semantics=("arbitrary",), vmem_limit_bytes=VMEM_LIMIT),
        name="inproj",
    )(x2, g, w_in, w_gate, bg)


def _ssm_tables(A_re, A_im, log_dt, B_re, B_im, C_re, C_im, D_skip):
    f32 = jnp.float32
    L = SSM_CHUNK
    nb, gb = N_LANE_BLOCKS, GROUPS_PER_BLOCK
    ar, ai = A_re.astype(f32), A_im.astype(f32)
    dt = jnp.exp(log_dt.astype(f32))[:, None]
    tau = jnp.arange(L + 1, dtype=f32)[:, None, None]
    mag = jnp.exp(ar[None] * dt[None] * tau)
    ang = ai[None] * dt[None] * tau
    pw_re, pw_im = mag * jnp.cos(ang), mag * jnp.sin(ang)
    num_re, num_im = pw_re[1] - 1.0, pw_im[1]
    den = ar * ar + ai * ai
    cf_re = (num_re * ar + num_im * ai) / den
    cf_im = (num_im * ar - num_re * ai) / den
    br, bi = B_re.astype(f32), B_im.astype(f32)
    bb_re = cf_re[..., None] * br - cf_im[..., None] * bi
    bb_im = cf_re[..., None] * bi + cf_im[..., None] * br
    ab_re = pw_re[:L, :, :, None] * bb_re[None] - pw_im[:L, :, :, None] * bb_im[None]
    ab_im = pw_re[:L, :, :, None] * bb_im[None] + pw_im[:L, :, :, None] * bb_re[None]
    cr, ci = C_re.astype(f32), C_im.astype(f32)

    kern = jnp.einsum('gdp,tgpc->tgcd', cr, ab_re) - jnp.einsum('gdp,tgpc->tgcd', ci, ab_im)
    kern = kern.reshape(L, nb, LANES, SSM_GROUP).transpose(1, 0, 2, 3)
    lane_idx = jnp.arange(LANES)
    spread = (jnp.arange(SSM_GROUP)[:, None] == lane_idx[None, :] % SSM_GROUP).astype(f32)
    same_group = lane_idx[:, None] // SSM_GROUP == lane_idx[None, :] // SSM_GROUP
    toep = jnp.where(same_group, jnp.matmul(kern, spread), 0.0)
    d_diag = D_skip.astype(f32).reshape(nb, LANES)[:, :, None] * jnp.eye(LANES, dtype=f32)
    toep = toep.at[:, 0].add(d_diag)

    pw_cat = jnp.concatenate([pw_re[:L], pw_im[:L]], axis=-1)[::-1]
    pw_swp = jnp.concatenate([pw_im[:L], pw_re[:L]], axis=-1)[::-1]
    bbt_re, bbt_im = bb_re.transpose(0, 2, 1), bb_im.transpose(0, 2, 1)
    b_same = jnp.concatenate([bbt_re, bbt_re], axis=-1)
    b_cross = jnp.concatenate([-bbt_im, bbt_im], axis=-1)
    inj = pw_cat[:, :, None, :] * b_same[None] + pw_swp[:, :, None, :] * b_cross[None]
    inj = inj.reshape(L, nb, N_PARTS, PART_LANES, 2 * STATE).transpose(1, 2, 0, 3, 4)
    inj = inj.reshape(nb, N_PARTS, PART_COLS, 2 * STATE)

    n_gp = PART_STATE // 2
    crt = cr.transpose(1, 0, 2).reshape(SSM_GROUP, nb, N_PARTS, n_gp)
    cit = ci.transpose(1, 0, 2).reshape(SSM_GROUP, nb, N_PARTS, n_gp)
    pwr = pw_re[1:L + 1].reshape(L, 1, nb, N_PARTS, n_gp)
    pwi = pw_im[1:L + 1].reshape(L, 1, nb, N_PARTS, n_gp)
    ro = jnp.concatenate([crt[None] * pwr - cit[None] * pwi,
                          -(crt[None] * pwi + cit[None] * pwr)], axis=-1)
    ro = ro.transpose(2, 3, 0, 1, 4).reshape(nb, N_PARTS, L * SSM_GROUP, PART_STATE)

    n = STEPS_PER_TILE
    t_part = jnp.stack([toep[:, :, p * PART_LANES:(p + 1) * PART_LANES,
                             p * PART_LANES:(p + 1) * PART_LANES] for p in range(N_PARTS)], axis=1)
    t_pad = jnp.pad(t_part, ((0, 0), (0, 0), (n - 1, 0), (0, 0), (0, 0)))
    n_dt = L // n
    toep = jnp.concatenate(
        [jnp.concatenate([t_pad[:, :, (b - a + n - 1)::n][:, :, :n_dt] for b in range(n)], axis=-1)
         for a in range(n)], axis=-2)

    steps = (L * jnp.array(SCAN_POWERS, f32))[:, None, None]
    sc_mag = jnp.exp(ar[None] * dt[None] * steps)
    sc_ang = ai[None] * dt[None] * steps
    a_tab = jnp.stack([sc_mag * jnp.cos(sc_ang), sc_mag * jnp.sin(sc_ang)], axis=1)
    a_tab = a_tab.reshape(2 * len(SCAN_POWERS), nb, gb * STATE).transpose(1, 0, 2)
    bf = jnp.bfloat16
    return toep.astype(bf), inj.astype(bf), ro.astype(bf), a_tab


def _expansion_matrices():
    gh = PART_GROUPS
    src = jnp.arange(2 * STATE)[:, None]
    dst = jnp.arange(PART_STATE)[None, :]
    e_inj = (src // STATE == dst // (gh * STATE)) & (src % STATE == dst % STATE)
    src = jnp.arange(SSM_CHUNK * SSM_GROUP)[None, :]
    dst = jnp.arange(PART_COLS)[:, None]
    e_ro = (src // SSM_GROUP == dst // PART_LANES) & (src % SSM_GROUP == dst % SSM_GROUP)
    return e_inj.astype(jnp.bfloat16), e_ro.astype(jnp.bfloat16)


def _gelu_tanh(y):
    half_y = 0.5 * y
    return half_y + half_y * jnp.tanh(y * (GELU_C1 + GELU_C3 * (y * y)))


def _ssm_kernel(x_ref, toep_ref, inj_ref, ro_ref, einj_ref, ero_ref, a_ref, y_ref,
                m_scr, p_scr, q_scr, z_scr, w_scr, hp_scr):
    gh = PART_GROUPS
    half = STATE_COLS // 2
    quarter = PART_STATE // 2
    n_pairs = SSM_CHUNK // STEPS_PER_TILE
    n_rows = x_ref.shape[1]

    @pl.when(pl.program_id(1) == 0)
    def _():
        def group_of(shape, axis, width):
            idx = lax.broadcasted_iota(jnp.int32, shape, axis)
            return lax.shift_right_logical(idx, width.bit_length() - 1) & (gh - 1)

        m_scr[...] = jnp.zeros_like(m_scr)
        for hf in range(N_PARTS):
            for sp in range(n_pairs):
                for tp in range(sp, n_pairs):
                    m_scr[hf, sp * LANES:(sp + 1) * LANES, tp * LANES:(tp + 1) * LANES] = (
                        toep_ref[0, hf, tp - sp])
            row_g = group_of((PART_COLS, PART_STATE), 0, SSM_GROUP)
            col_h = group_of((PART_COLS, PART_STATE), 1, STATE)
            p_full = jnp.dot(inj_ref[0, hf], einj_ref[...], preferred_element_type=jnp.float32)
            p_scr[hf] = jnp.where(row_g == col_h, p_full, 0.0).astype(jnp.bfloat16)
            row_h = group_of((PART_COLS, PART_STATE), 0, SSM_GROUP)
            col_g = group_of((PART_COLS, PART_STATE), 1, STATE)
            q_t = jnp.dot(ero_ref[...], ro_ref[0, hf], preferred_element_type=jnp.float32)
            q_scr[hf] = jnp.where(row_h == col_g, q_t, 0.0).T.astype(jnp.bfloat16)

    pad = SCAN_PAD
    re, im = slice(0, half), slice(half, STATE_COLS)
    for buf in (z_scr, w_scr, hp_scr):
        buf[0:pad, :] = jnp.zeros((pad, STATE_COLS), jnp.float32)
    for hf in range(N_PARTS):
        z_h = jnp.dot(x_ref[0, :, hf * PART_COLS:(hf + 1) * PART_COLS], p_scr[hf],
                      preferred_element_type=jnp.float32)
        z_scr[pad:pad + n_rows, hf * quarter:(hf + 1) * quarter] = z_h[:, 0:quarter]
        z_scr[pad:pad + n_rows, half + hf * quarter:half + (hf + 1) * quarter] = z_h[:, quarter:]

    def coef(i):
        return a_ref[0, 2 * i:2 * i + 1, :], a_ref[0, 2 * i + 1:2 * i + 2, :]

    def doubling_pass(src, dst, shift, c_re, c_im):
        s_re = src[pad - shift:pad - shift + n_rows, re]
        s_im = src[pad - shift:pad - shift + n_rows, im]
        dst[pad:pad + n_rows, re] = src[pad:pad + n_rows, re] + c_re * s_re - c_im * s_im
        dst[pad:pad + n_rows, im] = src[pad:pad + n_rows, im] + c_re * s_im + c_im * s_re

    doubling_pass(z_scr, w_scr, 1, *coef(0))
    doubling_pass(w_scr, z_scr, 2, *coef(1))
    doubling_pass(z_scr, w_scr, 4, *coef(2))
    c8_re, c8_im = coef(3)

    def tile_step(m, h):
        h_re, h_im = h
        rows8 = pl.ds(pl.multiple_of(pad + 8 * m, 8), 8)
        n_re = w_scr[rows8, re] + c8_re * h_re - c8_im * h_im
        n_im = w_scr[rows8, im] + c8_re * h_im + c8_im * h_re
        hp_scr[rows8, re] = n_re
        hp_scr[rows8, im] = n_im
        return n_re, n_im

    zero = jnp.zeros((8, half), jnp.float32)
    lax.fori_loop(0, n_rows // 8, tile_step, (zero, zero), unroll=8)

    for hf in range(N_PARTS):
        base = hf * PART_COLS
        x_h = x_ref[0, :, base:base + PART_COLS]
        hp_h = jnp.concatenate(
            [hp_scr[pad - 1:pad - 1 + n_rows, hf * quarter:(hf + 1) * quarter],
             hp_scr[pad - 1:pad - 1 + n_rows, half + hf * quarter:half + (hf + 1) * quarter]],
            axis=1).astype(jnp.bfloat16)
        for n in range(PART_COLS // MXU_TILE):
            lo, hi = n * MXU_TILE, (n + 1) * MXU_TILE
            y = jnp.dot(x_h[:, 0:hi], m_scr[hf, 0:hi, lo:hi], preferred_element_type=jnp.float32)
            y = y + jnp.dot(hp_h, q_scr[hf, :, lo:hi], preferred_element_type=jnp.float32)
            y_ref[0, :, base + lo:base + hi] = _gelu_tanh(y).astype(jnp.bfloat16)


def _ssm_call(xc, toep, inj, ro, e_inj, e_ro, a_tab, bsz):
    nb, total_rows, _ = xc.shape
    n_rows = total_rows // bsz
    per_q = lambda q, b: (q, 0, 0)
    const = lambda q, b: (0, 0)
    return pl.pallas_call(
        _ssm_kernel,
        grid=(nb, bsz),
        in_specs=[
            pl.BlockSpec((1, n_rows, CHUNK_COLS), lambda q, b: (q, b, 0)),
            pl.BlockSpec((1, N_PARTS, SSM_CHUNK // STEPS_PER_TILE, LANES, LANES),
                         lambda q, b: (q, 0, 0, 0, 0)),
            pl.BlockSpec((1, N_PARTS, PART_COLS, 2 * STATE), lambda q, b: (q, 0, 0, 0)),
            pl.BlockSpec((1, N_PARTS, SSM_CHUNK * SSM_GROUP, PART_STATE), lambda q, b: (q, 0, 0, 0)),
            pl.BlockSpec(e_inj.shape, const),
            pl.BlockSpec(e_ro.shape, const),
            pl.BlockSpec((1, 2 * len(SCAN_POWERS), STATE_COLS // 2), per_q),
        ],
        out_specs=pl.BlockSpec((1, n_rows, CHUNK_COLS), lambda q, b: (q, b, 0)),
        out_shape=jax.ShapeDtypeStruct(xc.shape, jnp.bfloat16),
        scratch_shapes=[
            pltpu.VMEM((N_PARTS, PART_COLS, PART_COLS), jnp.bfloat16),
            pltpu.VMEM((N_PARTS, PART_COLS, PART_STATE), jnp.bfloat16),
            pltpu.VMEM((N_PARTS, PART_STATE, PART_COLS), jnp.bfloat16),
            pltpu.VMEM((SCAN_PAD + n_rows, STATE_COLS), jnp.float32),
            pltpu.VMEM((SCAN_PAD + n_rows, STATE_COLS), jnp.float32),
            pltpu.VMEM((SCAN_PAD + n_rows, STATE_COLS), jnp.float32),
        ],
        compiler_params=pltpu.CompilerParams(
            dimension_semantics=("arbitrary", "arbitrary"), vmem_limit_bytes=VMEM_LIMIT),
        name="ssm",
    )(xc, toep, inj, ro, e_inj, e_ro, a_tab)


def _merge_mlp_stages(x_ref, ys_ref, ya, gates_ref, wglu_ref, bglu_ref, wus_ref, wua_ref,
                      wout_ref, gm_ref, w1_ref, w2_ref, gf_ref, o_ref, y_scr):
    f32, bf = jnp.float32, jnp.bfloat16
    chunk_rows = ys_ref.shape[1]
    st = {}

    def mix():
        gated_a = gates_ref[:, D_MODEL:2 * D_MODEL].astype(f32) * jnp.dot(
            ya, wua_ref[...], preferred_element_type=f32)
        lane_part = _lane_part((chunk_rows, LANES))
        for m in range(SSM_CHUNK // STEPS_PER_TILE):
            for blk in range(N_LANE_BLOCKS):
                tiles = [ys_ref[blk, :, p * PART_COLS + m * LANES:p * PART_COLS + (m + 1) * LANES]
                         .astype(f32) for p in range(N_PARTS)]
                for a in range(STEPS_PER_TILE):
                    step = _merge_lane_parts(
                        [_roll_lanes(tiles[p], (p - a) * PART_LANES) for p in range(N_PARTS)], lane_part)
                    y_scr[blk, pl.ds(m * STEPS_PER_TILE + a, chunk_rows, stride=SSM_CHUNK), :] = step
        y = jnp.concatenate([y_scr[blk] for blk in range(N_LANE_BLOCKS)], axis=-1)
        glu_arg = jnp.dot(y.astype(bf), wglu_ref[...], preferred_element_type=f32) + bglu_ref[...]
        y_ssm = (y * jax.nn.sigmoid(glu_arg)).astype(bf)
        up_s = jnp.dot(y_ssm, wus_ref[...], preferred_element_type=f32)
        g_s = gates_ref[:, 0:D_MODEL].astype(f32)
        st["merged"] = (g_s * up_s + gated_a).astype(bf)

    def project_out():
        h = x_ref[...] + jnp.dot(st["merged"], wout_ref[...], preferred_element_type=f32)
        st["h"] = h
        st["n"] = _rmsnorm_f32(h, gm_ref[...]).astype(bf)

    def mlp_chunk(c):
        cols = slice(c * FF_CHUNK, (c + 1) * FF_CHUNK)
        hid = jnp.maximum(jnp.dot(st["n"], w1_ref[:, cols], preferred_element_type=f32), 0.0)
        st["h"] = st["h"] + jnp.dot((hid * hid).astype(bf), w2_ref[cols, :],
                                    preferred_element_type=f32)

    def finish():
        o_ref[...] = _rmsnorm_f32(st["h"], gf_ref[...])

    chunks = [functools.partial(mlp_chunk, c) for c in range(D_FF // FF_CHUNK)]
    return [mix, project_out] + chunks + [finish]


def _attn_kernel(q_ref, kc_ref, kp_ref, vc_ref, vp_ref, k_hbm, v_hbm, tri_ref, o_ref,
                 qm_scr, carry_scr, acc_scr, k_buf, v_buf, dma_sem):
    blk = ATT_BLOCK
    n_qb = q_ref.shape[1] // blk
    n_pairs = N_HEADS // 2
    head_rows = N_HEADS * blk
    f32, bf = jnp.float32, jnp.bfloat16
    batch = pl.program_id(0)
    first_qb = pl.program_id(1) * n_qb

    lane = lax.broadcasted_iota(jnp.int32, (blk, LANES), 1)
    first_head = lane < HEAD_DIM
    tri_near = tri_ref[...]
    tri = tri_ref[blk:2 * blk, blk:2 * blk + LANES]

    for c in range(n_qb):
        for pair in range(n_pairs):
            q_pair = q_ref[0, c * blk:(c + 1) * blk, pair * LANES:(pair + 1) * LANES]
            zero = jnp.zeros_like(q_pair)
            lo = c * head_rows + 2 * pair * blk
            qm_scr[lo:lo + blk] = jnp.where(first_head, q_pair, zero)
            qm_scr[lo + blk:lo + 2 * blk] = jnp.where(first_head, zero, q_pair)

    def scores(c):
        base = c * head_rows
        return jnp.concatenate(
            [lax.dot_general(qm_scr[base + 2 * p * blk:base + (2 * p + 2) * blk],
                             k_buf[:, p * LANES:(p + 1) * LANES],
                             NT_DIMS, preferred_element_type=f32) for p in range(n_pairs)], axis=0)

    def stick(z, later):
        sp = jnp.maximum(z, 0.0) + jnp.log(1.0 + jnp.exp2(jnp.abs(z) * (-LOG2_E)))
        cs = jnp.dot(sp.astype(bf), tri, preferred_element_type=f32)
        log_w = z - sp - cs[:, 0:blk]
        if later is not None:
            log_w = log_w - later
        return jnp.exp2(log_w * LOG2_E).astype(bf), cs[:, blk:blk + LANES]

    def weighted_values(w, pair):
        o2 = jnp.dot(w, v_buf[:, pair * LANES:(pair + 1) * LANES],
                     preferred_element_type=f32)
        return jnp.where(first_head, o2[0:blk], o2[blk:2 * blk])

    row = lax.broadcasted_iota(jnp.int32, (blk, blk), 0)
    col = lax.broadcasted_iota(jnp.int32, (blk, blk), 1)
    pen_diag = jnp.where(col < row, 0.0, ATT_MASKED)

    def window(cur_ref, prev_ref, c, lanes):
        before = prev_ref[0, :, lanes] if c == 0 else cur_ref[0, (c - 1) * blk:c * blk, lanes]
        return jnp.concatenate([before, cur_ref[0, c * blk:(c + 1) * blk, lanes]], axis=0)

    def near_scores(c):
        base = c * head_rows
        return jnp.concatenate(
            [lax.dot_general(
                qm_scr[base + 2 * p * blk:base + (2 * p + 2) * blk],
                window(kc_ref, kp_ref, c, slice(p * LANES, (p + 1) * LANES)),
                NT_DIMS, preferred_element_type=f32) for p in range(n_pairs)], axis=0)

    def masked_near_scores(c):
        z = near_scores(c)
        z_prev = z[:, 0:blk]
        if c == 0:
            z_prev = z_prev - jnp.where(first_qb >= 1, 0.0, ATT_MASKED)
        z_diag = (z[:, blk:2 * blk].reshape(N_HEADS, blk, blk) - pen_diag[None]).reshape(head_rows, blk)
        return jnp.concatenate([z_prev, z_diag], axis=1)

    all_qbs = range(n_qb)
    least = [None] * n_qb

    def phase_scores(qbs):
        return jnp.concatenate([masked_near_scores(c) for c in qbs], axis=0)

    def phase_softplus(z):
        return jnp.maximum(z, 0.0) + jnp.log(1.0 + jnp.exp2(jnp.abs(z) * (-LOG2_E)))

    def phase_suffix(sp):
        return jnp.dot(sp.astype(bf), tri_near[:, 0:2 * blk], preferred_element_type=f32)

    def phase_weights(qbs, z, sp, cs):
        total = cs[:, 0:LANES] + sp[:, 0:LANES]
        carry_scr[qbs[0] * head_rows:(qbs[-1] + 1) * head_rows] = total
        for k, c in enumerate(qbs):
            least[c] = jnp.min(total[k * head_rows:(k + 1) * head_rows], axis=0, keepdims=True)[0, 0]
        return jnp.exp2((z - sp - cs) * LOG2_E).astype(bf)

    def phase_values(qbs, w):
        for k, c in enumerate(qbs):
            for pair in range(n_pairs):
                rows_w = slice(k * head_rows + 2 * pair * blk, k * head_rows + (2 * pair + 2) * blk)
                lanes = slice(pair * LANES, (pair + 1) * LANES)
                o2 = jnp.dot(w[rows_w], window(vc_ref, vp_ref, c, lanes),
                             preferred_element_type=f32)
                acc_scr[c * blk:(c + 1) * blk, lanes] = jnp.where(first_head, o2[0:blk], o2[blk:2 * blk])

    z_near = phase_scores(all_qbs)
    sp = phase_softplus(z_near)
    cs = phase_suffix(sp)
    phase_values(all_qbs, phase_weights(all_qbs, z_near, sp, cs))

    for c in range(n_qb):
        rows_c = slice(c * blk, (c + 1) * blk)
        base = c * head_rows

        def body(state, c=c, rows_c=rows_c, base=base):
            j, _ = state
            key_rows = pl.ds(pl.multiple_of(j * blk, blk), blk)
            k_copy = pltpu.make_async_copy(k_hbm.at[batch, key_rows], k_buf, dma_sem.at[0])
            v_copy = pltpu.make_async_copy(v_hbm.at[batch, key_rows], v_buf, dma_sem.at[1])
            k_copy.start()
            v_copy.start()
            k_copy.wait()
            v_copy.wait()
            carry = jnp.broadcast_to(carry_scr[base:base + head_rows, 0:1], (head_rows, LANES))
            w, sum_j = stick(scores(c), carry)
            carry = carry + sum_j
            carry_scr[base:base + head_rows] = carry
            for pair in range(n_pairs):
                acc_scr[rows_c, pair * LANES:(pair + 1) * LANES] += weighted_values(
                    w[2 * pair * blk:(2 * pair + 2) * blk], pair)
            return j - 1, jnp.min(carry)

        def cond(state):
            j, least_c = state
            return jnp.logical_and(j >= 0, least_c < ATT_SKIP_SUM)

        lax.while_loop(cond, body, (first_qb + (c - 2), least[c]))

    o_ref[0] = acc_scr[...].astype(o_ref.dtype)


def _attn_call(q3, k3, v3, tri):
    bsz, seq, _ = q3.shape
    blk = ATT_BLOCK
    rows = ATT_ROWS
    n_qb = rows // blk
    tile = lambda b, i: (b, i, 0)
    block_before = lambda b, i: (b, jnp.maximum(i * n_qb - 1, 0), 0)
    return pl.pallas_call(
        _attn_kernel,
        grid=(bsz, seq // rows),
        in_specs=[
            pl.BlockSpec((1, rows, D_ATTN), tile),
            pl.BlockSpec((1, rows, D_ATTN), tile),
            pl.BlockSpec((1, blk, D_ATTN), block_before),
            pl.BlockSpec((1, rows, D_ATTN), tile),
            pl.BlockSpec((1, blk, D_ATTN), block_before),
            pl.BlockSpec(memory_space=pl.ANY),
            pl.BlockSpec(memory_space=pl.ANY),
            pl.BlockSpec(tri.shape, lambda b, i: (0, 0)),
        ],
        out_specs=pl.BlockSpec((1, rows, D_ATTN), tile),
        out_shape=jax.ShapeDtypeStruct((bsz, seq, D_ATTN), jnp.bfloat16),
        scratch_shapes=[
            pltpu.VMEM((n_qb * N_HEADS * blk, LANES), jnp.bfloat16),
            pltpu.VMEM((n_qb * N_HEADS * blk, LANES), jnp.float32),
            pltpu.VMEM((rows, D_ATTN), jnp.float32),
            pltpu.VMEM((blk, D_ATTN), jnp.bfloat16),
            pltpu.VMEM((blk, D_ATTN), jnp.bfloat16),
            pltpu.SemaphoreType.DMA((2,)),
        ],
        compiler_params=pltpu.CompilerParams(
            dimension_semantics=("arbitrary", "arbitrary"), vmem_limit_bytes=VMEM_LIMIT),
        name="attn",
    )(q3, k3, k3, v3, v3, k3, v3, tri)


def _merge_kernel(x_ref, ys_ref, ya_ref, gates_ref, wglu_ref, bglu_ref, wus_ref, wua_ref,
                  wout_ref, gm_ref, w1_ref, w2_ref, gf_ref, o_ref, y_scr):
    for stage in _merge_mlp_stages(x_ref, ys_ref, ya_ref[...], gates_ref, wglu_ref, bglu_ref,
                                   wus_ref, wua_ref, wout_ref, gm_ref, w1_ref, w2_ref, gf_ref,
                                   o_ref, y_scr):
        stage()


def _merge_call(x2, ys, ya, gates, wglu, bglu, wus, wua, wout, gm, w1, w2, gf):
    tokens = x2.shape[0]
    rows = MERGE_ROWS
    const = lambda i: (0, 0)

    def resident(arr):
        return pl.BlockSpec(arr.shape, const, pipeline_mode=pl.Buffered(1))

    return pl.pallas_call(
        _merge_kernel,
        grid=(tokens // rows,),
        in_specs=[
            pl.BlockSpec((rows, D_MODEL), lambda i: (i, 0)),
            pl.BlockSpec((N_LANE_BLOCKS, rows // SSM_CHUNK, CHUNK_COLS), lambda i: (0, i, 0)),
            pl.BlockSpec((rows, D_ATTN), lambda i: (i, 0)),
            pl.BlockSpec((rows, 2 * D_MODEL), lambda i: (i, 0)),
            resident(wglu), resident(bglu), resident(wus), resident(wua), resident(wout),
            resident(gm), resident(w1), resident(w2), resident(gf),
        ],
        out_specs=pl.BlockSpec((rows, D_MODEL), lambda i: (i, 0)),
        out_shape=jax.ShapeDtypeStruct((tokens, D_MODEL), jnp.float32),
        scratch_shapes=[pltpu.VMEM((N_LANE_BLOCKS, rows, LANES), jnp.float32)],
        compiler_params=pltpu.CompilerParams(
            dimension_semantics=("arbitrary",), vmem_limit_bytes=VMEM_LIMIT),
        name="merge_mlp",
    )(x2, ys, ya, gates, wglu, bglu, wus, wua, wout, gm, w1, w2, gf)


def _suffix_sum_matrix():
    win = 2 * ATT_BLOCK
    r = jnp.arange(win)[:, None]
    c = jnp.arange(win + LANES)[None, :]
    return jnp.where((c >= win) | (r > c), 1.0, 0.0).astype(jnp.bfloat16)


def kernel(x, norm_mix, w_in, A_re, A_im, log_dt, B_re, B_im, C_re, C_im, D_skip, w_glu, b_glu,
           w_up_ssm, w_up_attn, w_gate, b_gate, w_out, norm_mlp, w_ff1, w_ff2, norm_final):
    bsz, seq, _ = x.shape
    tokens = bsz * seq
    bf = jnp.bfloat16
    assert norm_mix.shape[0] == 1, "single layer"
    assert seq % (SSM_CHUNK * 8) == 0 and seq % ATT_BLOCK == 0
    assert tokens % PROJ_ROWS == 0 and tokens % MERGE_ROWS == 0 and seq % min(PROJ_ROWS, seq) == 0

    x2 = x.reshape(tokens, D_MODEL)
    ussm, q, k, v, gates = _inproj_call(x2, norm_mix, w_in[0].astype(bf), w_gate[0].astype(bf),
                                        b_gate, bsz, seq)

    toep, inj, ro, a_tab = _ssm_tables(
        A_re[0], A_im[0], log_dt[0], B_re[0], B_im[0], C_re[0], C_im[0], D_skip[0])
    e_inj, e_ro = _expansion_matrices()
    ys = _ssm_call(ussm, toep, inj, ro, e_inj, e_ro, a_tab, bsz)

    ya = _attn_call(q.reshape(bsz, seq, D_ATTN), k.reshape(bsz, seq, D_ATTN),
                    v.reshape(bsz, seq, D_ATTN), _suffix_sum_matrix()).reshape(tokens, D_ATTN)

    out = _merge_call(x2, ys, ya, gates, w_glu[0].astype(bf), b_glu, w_up_ssm[0].astype(bf),
                      w_up_attn[0].astype(bf), w_out[0].astype(bf), norm_mlp,
                      w_ff1[0].astype(bf), w_ff2[0].astype(bf), norm_final.reshape(1, D_MODEL))
    return out.reshape(bsz, seq, D_MODEL)
```

```python
import functools

import jax
import jax.numpy as jnp
from jax import lax
from jax.experimental import pallas as pl
from jax.experimental.pallas import tpu as pltpu

D_MODEL = 1024
D_SSM = 512
SSM_GROUP = 16
N_GROUPS = 32
STATE = 64
N_HEADS = 8
HEAD_DIM = 64
D_ATTN = 512
D_FF = 4096
EPS = 1e-6

LANES = 128
MXU_TILE = 256
VMEM_LIMIT = 52 * 1024 * 1024

SSM_CHUNK = 16
N_LANE_BLOCKS = D_SSM // LANES
GROUPS_PER_BLOCK = LANES // SSM_GROUP
STATE_COLS = 2 * GROUPS_PER_BLOCK * STATE
CHUNK_COLS = SSM_CHUNK * LANES
N_PARTS = 4
PART_GROUPS = GROUPS_PER_BLOCK // N_PARTS
PART_LANES = LANES // N_PARTS
STEPS_PER_TILE = LANES // PART_LANES
PART_COLS = SSM_CHUNK * PART_LANES
PART_STATE = STATE_COLS // N_PARTS
SCAN_POWERS = (1, 2, 4, 8)
SCAN_PAD = 8
ATT_BLOCK = 128
ATT_SKIP_SUM = 64.0
ATT_MASKED = 1e30
LOG2_E = 1.4426950408889634
NT_DIMS = (((1,), (1,)), ((), ()))
GELU_C1 = 0.7978845608028654
GELU_C3 = GELU_C1 * 0.044715
PROJ_ROWS = 1024
MERGE_ROWS = 512
FF_CHUNK = 1024
ATT_ROWS = 1024


def _rmsnorm_f32(x, g):
    ms = jnp.mean(x * x, axis=-1, keepdims=True)
    return x * lax.rsqrt(ms + EPS) * g


def _lane_part(shape):
    lanes = lax.broadcasted_iota(jnp.int32, shape, len(shape) - 1)
    return lax.shift_right_logical(lanes, PART_LANES.bit_length() - 1)


def _roll_lanes(x, shift):
    shift %= LANES
    return x if shift == 0 else pltpu.roll(x, shift, x.ndim - 1)


def _merge_lane_parts(sources, lane_part):
    out = sources[0]
    for k in range(1, len(sources)):
        out = jnp.where(lane_part == k, sources[k], out)
    return out


def _inproj_kernel(x_ref, g_ref, w_ref, wg_ref, bg_ref,
                   ussm_ref, q_ref, k_ref, v_ref, gates_ref, pssm_scr):
    x = x_ref[...]
    u = _rmsnorm_f32(x, g_ref[...]).astype(jnp.bfloat16)
    rows = x.shape[0]
    chunk_rows = rows // SSM_CHUNK

    p_ssm = jnp.dot(u, w_ref[:, 0:D_SSM], preferred_element_type=jnp.float32)
    for blk in range(N_LANE_BLOCKS):
        pssm_scr[blk] = p_ssm[:, blk * LANES:(blk + 1) * LANES]
    lane_part = _lane_part((chunk_rows, LANES))
    for m in range(SSM_CHUNK // STEPS_PER_TILE):
        for blk in range(N_LANE_BLOCKS):
            steps = [pssm_scr[blk, pl.ds(m * STEPS_PER_TILE + a, chunk_rows, stride=SSM_CHUNK), :]
                     for a in range(STEPS_PER_TILE)]
            for p in range(N_PARTS):
                tile = _merge_lane_parts(
                    [_roll_lanes(steps[a], (a - p) * PART_LANES) for a in range(STEPS_PER_TILE)],
                    lane_part)
                ussm_ref[blk, :, p * PART_COLS + m * LANES:p * PART_COLS + (m + 1) * LANES] = (
                    tile.astype(jnp.bfloat16))

    p_q = jnp.dot(u, w_ref[:, D_SSM:D_SSM + D_ATTN], preferred_element_type=jnp.float32)
    q_ref[...] = (p_q * (HEAD_DIM ** -0.5 * LOG2_E)).astype(jnp.bfloat16)

    p_k = jnp.dot(u, w_ref[:, D_SSM + D_ATTN:D_SSM + 2 * D_ATTN],
                  preferred_element_type=jnp.float32)
    k_ref[...] = p_k.astype(jnp.bfloat16)

    p_v = jnp.dot(u, w_ref[:, D_SSM + 2 * D_ATTN:D_SSM + 3 * D_ATTN],
                  preferred_element_type=jnp.float32)
    v_ref[...] = p_v.astype(jnp.bfloat16)

    for c in range(2 * D_MODEL // 512):
        pg = jnp.dot(u, wg_ref[:, c * 512:(c + 1) * 512], preferred_element_type=jnp.float32)
        pg = pg + bg_ref[:, c * 512:(c + 1) * 512]
        gates_ref[:, c * 512:(c + 1) * 512] = jax.nn.sigmoid(pg).astype(jnp.bfloat16)


def _inproj_call(x2, g, w_in, w_gate, bg, bsz, seq):
    tokens = bsz * seq
    rows = min(PROJ_ROWS, seq)
    const = lambda i: (0, 0)
    return pl.pallas_call(
        _inproj_kernel,
        grid=(tokens // rows,),
        in_specs=[
            pl.BlockSpec((rows, D_MODEL), lambda i: (i, 0)),
            pl.BlockSpec((1, D_MODEL), const),
            pl.BlockSpec(w_in.shape, const),
            pl.BlockSpec(w_gate.shape, const),
            pl.BlockSpec((1, 2 * D_MODEL), const),
        ],
        out_specs=[
            pl.BlockSpec((N_LANE_BLOCKS, rows // SSM_CHUNK, CHUNK_COLS), lambda i: (0, i, 0)),
            pl.BlockSpec((rows, D_ATTN), lambda i: (i, 0)),
            pl.BlockSpec((rows, D_ATTN), lambda i: (i, 0)),
            pl.BlockSpec((rows, D_ATTN), lambda i: (i, 0)),
            pl.BlockSpec((rows, 2 * D_MODEL), lambda i: (i, 0)),
        ],
        out_shape=[
            jax.ShapeDtypeStruct((N_LANE_BLOCKS, tokens // SSM_CHUNK, CHUNK_COLS), jnp.bfloat16),
            jax.ShapeDtypeStruct((tokens, D_ATTN), jnp.bfloat16),
            jax.ShapeDtypeStruct((tokens, D_ATTN), jnp.bfloat16),
            jax.ShapeDtypeStruct((tokens, D_ATTN), jnp.bfloat16),
            jax.ShapeDtypeStruct((tokens, 2 * D_MODEL), jnp.bfloat16),
        ],
        scratch_shapes=[pltpu.VMEM((N_LANE_BLOCKS, rows, LANES), jnp.float32)],
        compiler_params=pltpu.CompilerParams(
            dimension_semantics=("arbitrary",), vmem_limit_bytes=VMEM_LIMIT),
        name="inproj",
    )(x2, g, w_in, w_gate, bg)


def _ssm_tables(A_re, A_im, log_dt, B_re, B_im, C_re, C_im, D_skip):
    f32 = jnp.float32
    L = SSM_CHUNK
    nb, gb = N_LANE_BLOCKS, GROUPS_PER_BLOCK
    ar, ai = A_re.astype(f32), A_im.astype(f32)
    dt = jnp.exp(log_dt.astype(f32))[:, None]
    tau = jnp.arange(L + 1, dtype=f32)[:, None, None]
    mag = jnp.exp(ar[None] * dt[None] * tau)
    ang = ai[None] * dt[None] * tau
    pw_re, pw_im = mag * jnp.cos(ang), mag * jnp.sin(ang)
    num_re, num_im = pw_re[1] - 1.0, pw_im[1]
    den = ar * ar + ai * ai
    cf_re = (num_re * ar + num_im * ai) / den
    cf_im = (num_im * ar - num_re * ai) / den
    br, bi = B_re.astype(f32), B_im.astype(f32)
    bb_re = cf_re[..., None] * br - cf_im[..., None] * bi
    bb_im = cf_re[..., None] * bi + cf_im[..., None] * br
    ab_re = pw_re[:L, :, :, None] * bb_re[None] - pw_im[:L, :, :, None] * bb_im[None]
    ab_im = pw_re[:L, :, :, None] * bb_im[None] + pw_im[:L, :, :, None] * bb_re[None]
    cr, ci = C_re.astype(f32), C_im.astype(f32)

    kern = jnp.einsum('gdp,tgpc->tgcd', cr, ab_re) - jnp.einsum('gdp,tgpc->tgcd', ci, ab_im)
    kern = kern.reshape(L, nb, LANES, SSM_GROUP).transpose(1, 0, 2, 3)
    lane_idx = jnp.arange(LANES)
    spread = (jnp.arange(SSM_GROUP)[:, None] == lane_idx[None, :] % SSM_GROUP).astype(f32)
    same_group = lane_idx[:, None] // SSM_GROUP == lane_idx[None, :] // SSM_GROUP
    toep = jnp.where(same_group, jnp.matmul(kern, spread), 0.0)
    d_diag = D_skip.astype(f32).reshape(nb, LANES)[:, :, None] * jnp.eye(LANES, dtype=f32)
    toep = toep.at[:, 0].add(d_diag)

    pw_cat = jnp.concatenate([pw_re[:L], pw_im[:L]], axis=-1)[::-1]
    pw_swp = jnp.concatenate([pw_im[:L], pw_re[:L]], axis=-1)[::-1]
    bbt_re, bbt_im = bb_re.transpose(0, 2, 1), bb_im.transpose(0, 2, 1)
    b_same = jnp.concatenate([bbt_re, bbt_re], axis=-1)
    b_cross = jnp.concatenate([-bbt_im, bbt_im], axis=-1)
    inj = pw_cat[:, :, None, :] * b_same[None] + pw_swp[:, :, None, :] * b_cross[None]
    inj = inj.reshape(L, nb, N_PARTS, PART_LANES, 2 * STATE).transpose(1, 2, 0, 3, 4)
    inj = inj.reshape(nb, N_PARTS, PART_COLS, 2 * STATE)

    n_gp = PART_STATE // 2
    crt = cr.transpose(1, 0, 2).reshape(SSM_GROUP, nb, N_PARTS, n_gp)
    cit = ci.transpose(1, 0, 2).reshape(SSM_GROUP, nb, N_PARTS, n_gp)
    pwr = pw_re[1:L + 1].reshape(L, 1, nb, N_PARTS, n_gp)
    pwi = pw_im[1:L + 1].reshape(L, 1, nb, N_PARTS, n_gp)
    ro = jnp.concatenate([crt[None] * pwr - cit[None] * pwi,
                          -(crt[None] * pwi + cit[None] * pwr)], axis=-1)
    ro = ro.transpose(2, 3, 0, 1, 4).reshape(nb, N_PARTS, L * SSM_GROUP, PART_STATE)

    n = STEPS_PER_TILE
    t_part = jnp.stack([toep[:, :, p * PART_LANES:(p + 1) * PART_LANES,
                             p * PART_LANES:(p + 1) * PART_LANES] for p in range(N_PARTS)], axis=1)
    t_pad = jnp.pad(t_part, ((0, 0), (0, 0), (n - 1, 0), (0, 0), (0, 0)))
    n_dt = L // n
    toep = jnp.concatenate(
        [jnp.concatenate([t_pad[:, :, (b - a + n - 1)::n][:, :, :n_dt] for b in range(n)], axis=-1)
         for a in range(n)], axis=-2)

    steps = (L * jnp.array(SCAN_POWERS, f32))[:, None, None]
    sc_mag = jnp.exp(ar[None] * dt[None] * steps)
    sc_ang = ai[None] * dt[None] * steps
    a_tab = jnp.stack([sc_mag * jnp.cos(sc_ang), sc_mag * jnp.sin(sc_ang)], axis=1)
    a_tab = a_tab.reshape(2 * len(SCAN_POWERS), nb, gb * STATE).transpose(1, 0, 2)
    bf = jnp.bfloat16
    return toep.astype(bf), inj.astype(bf), ro.astype(bf), a_tab


def _expansion_matrices():
    gh = PART_GROUPS
    src = jnp.arange(2 * STATE)[:, None]
    dst = jnp.arange(PART_STATE)[None, :]
    e_inj = (src // STATE == dst // (gh * STATE)) & (src % STATE == dst % STATE)
    src = jnp.arange(SSM_CHUNK * SSM_GROUP)[None, :]
    dst = jnp.arange(PART_COLS)[:, None]
    e_ro = (src // SSM_GROUP == dst // PART_LANES) & (src % SSM_GROUP == dst % SSM_GROUP)
    return e_inj.astype(jnp.bfloat16), e_ro.astype(jnp.bfloat16)


def _gelu_tanh(y):
    half_y = 0.5 * y
    return half_y + half_y * jnp.tanh(y * (GELU_C1 + GELU_C3 * (y * y)))


def _ssm_kernel(x_ref, toep_ref, inj_ref, ro_ref, einj_ref, ero_ref, a_ref, y_ref,
                m_scr, p_scr, q_scr, z_scr, w_scr, hp_scr):
    gh = PART_GROUPS
    half = STATE_COLS // 2
    quarter = PART_STATE // 2
    n_pairs = SSM_CHUNK // STEPS_PER_TILE
    n_rows = x_ref.shape[1]

    @pl.when(pl.program_id(1) == 0)
    def _():
        def group_of(shape, axis, width):
            idx = lax.broadcasted_iota(jnp.int32, shape, axis)
            return lax.shift_right_logical(idx, width.bit_length() - 1) & (gh - 1)

        m_scr[...] = jnp.zeros_like(m_scr)
        for hf in range(N_PARTS):
            for sp in range(n_pairs):
                for tp in range(sp, n_pairs):
                    m_scr[hf, sp * LANES:(sp + 1) * LANES, tp * LANES:(tp + 1) * LANES] = (
                        toep_ref[0, hf, tp - sp])
            row_g = group_of((PART_COLS, PART_STATE), 0, SSM_GROUP)
            col_h = group_of((PART_COLS, PART_STATE), 1, STATE)
            p_full = jnp.dot(inj_ref[0, hf], einj_ref[...], preferred_element_type=jnp.float32)
            p_scr[hf] = jnp.where(row_g == col_h, p_full, 0.0).astype(jnp.bfloat16)
            row_h = group_of((PART_COLS, PART_STATE), 0, SSM_GROUP)
            col_g = group_of((PART_COLS, PART_STATE), 1, STATE)
            q_t = jnp.dot(ero_ref[...], ro_ref[0, hf], preferred_element_type=jnp.float32)
            q_scr[hf] = jnp.where(row_h == col_g, q_t, 0.0).T.astype(jnp.bfloat16)

    pad = SCAN_PAD
    re, im = slice(0, half), slice(half, STATE_COLS)
    for buf in (z_scr, w_scr, hp_scr):
        buf[0:pad, :] = jnp.zeros((pad, STATE_COLS), jnp.float32)
    for hf in range(N_PARTS):
        z_h = jnp.dot(x_ref[0, :, hf * PART_COLS:(hf + 1) * PART_COLS], p_scr[hf],
                      preferred_element_type=jnp.float32)
        z_scr[pad:pad + n_rows, hf * quarter:(hf + 1) * quarter] = z_h[:, 0:quarter]
        z_scr[pad:pad + n_rows, half + hf * quarter:half + (hf + 1) * quarter] = z_h[:, quarter:]

    def coef(i):
        return a_ref[0, 2 * i:2 * i + 1, :], a_ref[0, 2 * i + 1:2 * i + 2, :]

    def doubling_pass(src, dst, shift, c_re, c_im):
        s_re = src[pad - shift:pad - shift + n_rows, re]
        s_im = src[pad - shift:pad - shift + n_rows, im]
        dst[pad:pad + n_rows, re] = src[pad:pad + n_rows, re] + c_re * s_re - c_im * s_im
        dst[pad:pad + n_rows, im] = src[pad:pad + n_rows, im] + c_re * s_im + c_im * s_re

    doubling_pass(z_scr, w_scr, 1, *coef(0))
    doubling_pass(w_scr, z_scr, 2, *coef(1))
    doubling_pass(z_scr, w_scr, 4, *coef(2))
    c8_re, c8_im = coef(3)

    def tile_step(m, h):
        h_re, h_im = h
        rows8 = pl.ds(pl.multiple_of(pad + 8 * m, 8), 8)
        n_re = w_scr[rows8, re] + c8_re * h_re - c8_im * h_im
        n_im = w_scr[rows8, im] + c8_re * h_im + c8_im * h_re
        hp_scr[rows8, re] = n_re
        hp_scr[rows8, im] = n_im
        return n_re, n_im

    zero = jnp.zeros((8, half), jnp.float32)
    lax.fori_loop(0, n_rows // 8, tile_step, (zero, zero), unroll=8)

    for hf in range(N_PARTS):
        base = hf * PART_COLS
        x_h = x_ref[0, :, base:base + PART_COLS]
        hp_h = jnp.concatenate(
            [hp_scr[pad - 1:pad - 1 + n_rows, hf * quarter:(hf + 1) * quarter],
             hp_scr[pad - 1:pad - 1 + n_rows, half + hf * quarter:half + (hf + 1) * quarter]],
            axis=1).astype(jnp.bfloat16)
        for n in range(PART_COLS // MXU_TILE):
            lo, hi = n * MXU_TILE, (n + 1) * MXU_TILE
            y = jnp.dot(x_h[:, 0:hi], m_scr[hf, 0:hi, lo:hi], preferred_element_type=jnp.float32)
            y = y + jnp.dot(hp_h, q_scr[hf, :, lo:hi], preferred_element_type=jnp.float32)
            y_ref[0, :, base + lo:base + hi] = _gelu_tanh(y).astype(jnp.bfloat16)


def _ssm_call(xc, toep, inj, ro, e_inj, e_ro, a_tab, bsz):
    nb, total_rows, _ = xc.shape
    n_rows = total_rows // bsz
    per_q = lambda q, b: (q, 0, 0)
    const = lambda q, b: (0, 0)
    return pl.pallas_call(
        _ssm_kernel,
        grid=(nb, bsz),
        in_specs=[
            pl.BlockSpec((1, n_rows, CHUNK_COLS), lambda q, b: (q, b, 0)),
            pl.BlockSpec((1, N_PARTS, SSM_CHUNK // STEPS_PER_TILE, LANES, LANES),
                         lambda q, b: (q, 0, 0, 0, 0)),
            pl.BlockSpec((1, N_PARTS, PART_COLS, 2 * STATE), lambda q, b: (q, 0, 0, 0)),
            pl.BlockSpec((1, N_PARTS, SSM_CHUNK * SSM_GROUP, PART_STATE), lambda q, b: (q, 0, 0, 0)),
            pl.BlockSpec(e_inj.shape, const),
            pl.BlockSpec(e_ro.shape, const),
            pl.BlockSpec((1, 2 * len(SCAN_POWERS), STATE_COLS // 2), per_q),
        ],
        out_specs=pl.BlockSpec((1, n_rows, CHUNK_COLS), lambda q, b: (q, b, 0)),
        out_shape=jax.ShapeDtypeStruct(xc.shape, jnp.bfloat16),
        scratch_shapes=[
            pltpu.VMEM((N_PARTS, PART_COLS, PART_COLS), jnp.bfloat16),
            pltpu.VMEM((N_PARTS, PART_COLS, PART_STATE), jnp.bfloat16),
            pltpu.VMEM((N_PARTS, PART_STATE, PART_COLS), jnp.bfloat16),
            pltpu.VMEM((SCAN_PAD + n_rows, STATE_COLS), jnp.float32),
            pltpu.VMEM((SCAN_PAD + n_rows, STATE_COLS), jnp.float32),
            pltpu.VMEM((SCAN_PAD + n_rows, STATE_COLS), jnp.float32),
        ],
        compiler_params=pltpu.CompilerParams(
            dimension_semantics=("arbitrary", "arbitrary"), vmem_limit_bytes=VMEM_LIMIT),
        name="ssm",
    )(xc, toep, inj, ro, e_inj, e_ro, a_tab)


def _merge_mlp_stages(x_ref, ys_ref, ya, gates_ref, wglu_ref, bglu_ref, wus_ref, wua_ref,
                      wout_ref, gm_ref, w1_ref, w2_ref, gf_ref, o_ref, y_scr):
    f32, bf = jnp.float32, jnp.bfloat16
    chunk_rows = ys_ref.shape[1]
    st = {}

    def mix():
        gated_a = gates_ref[:, D_MODEL:2 * D_MODEL].astype(f32) * jnp.dot(
            ya, wua_ref[...], preferred_element_type=f32)
        lane_part = _lane_part((chunk_rows, LANES))
        for m in range(SSM_CHUNK // STEPS_PER_TILE):
            for blk in range(N_LANE_BLOCKS):
                tiles = [ys_ref[blk, :, p * PART_COLS + m * LANES:p * PART_COLS + (m + 1) * LANES]
                         .astype(f32) for p in range(N_PARTS)]
                for a in range(STEPS_PER_TILE):
                    step = _merge_lane_parts(
                        [_roll_lanes(tiles[p], (p - a) * PART_LANES) for p in range(N_PARTS)], lane_part)
                    y_scr[blk, pl.ds(m * STEPS_PER_TILE + a, chunk_rows, stride=SSM_CHUNK), :] = step
        y = jnp.concatenate([y_scr[blk] for blk in range(N_LANE_BLOCKS)], axis=-1)
        glu_arg = jnp.dot(y.astype(bf), wglu_ref[...], preferred_element_type=f32) + bglu_ref[...]
        y_ssm = (y * jax.nn.sigmoid(glu_arg)).astype(bf)
        up_s = jnp.dot(y_ssm, wus_ref[...], preferred_element_type=f32)
        g_s = gates_ref[:, 0:D_MODEL].astype(f32)
        st["merged"] = (g_s * up_s + gated_a).astype(bf)

    def project_out():
        h = x_ref[...] + jnp.dot(st["merged"], wout_ref[...], preferred_element_type=f32)
        st["h"] = h
        st["n"] = _rmsnorm_f32(h, gm_ref[...]).astype(bf)

    def mlp_chunk(c):
        cols = slice(c * FF_CHUNK, (c + 1) * FF_CHUNK)
        hid = jnp.maximum(jnp.dot(st["n"], w1_ref[:, cols], preferred_element_type=f32), 0.0)
        st["h"] = st["h"] + jnp.dot((hid * hid).astype(bf), w2_ref[cols, :],
                                    preferred_element_type=f32)

    def finish():
        o_ref[...] = _rmsnorm_f32(st["h"], gf_ref[...])

    chunks = [functools.partial(mlp_chunk, c) for c in range(D_FF // FF_CHUNK)]
    return [mix, project_out] + chunks + [finish]


def _attn_kernel(q_ref, kc_ref, kp_ref, vc_ref, vp_ref, k_hbm, v_hbm, tri_ref, o_ref,
                 qm_scr, carry_scr, acc_scr, k_buf, v_buf, dma_sem):
    blk = ATT_BLOCK
    n_qb = q_ref.shape[1] // blk
    n_pairs = N_HEADS // 2
    head_rows = N_HEADS * blk
    f32, bf = jnp.float32, jnp.bfloat16
    batch = pl.program_id(0)
    first_qb = pl.program_id(1) * n_qb

    lane = lax.broadcasted_iota(jnp.int32, (blk, LANES), 1)
    first_head = lane < HEAD_DIM
    tri_near = tri_ref[...]
    tri = tri_ref[blk:2 * blk, blk:2 * blk + LANES]

    for c in range(n_qb):
        for pair in range(n_pairs):
            q_pair = q_ref[0, c * blk:(c + 1) * blk, pair * LANES:(pair + 1) * LANES]
            zero = jnp.zeros_like(q_pair)
            lo = c * head_rows + 2 * pair * blk
            qm_scr[lo:lo + blk] = jnp.where(first_head, q_pair, zero)
            qm_scr[lo + blk:lo + 2 * blk] = jnp.where(first_head, zero, q_pair)

    def scores(c):
        base = c * head_rows
        return jnp.concatenate(
            [lax.dot_general(qm_scr[base + 2 * p * blk:base + (2 * p + 2) * blk],
                             k_buf[:, p * LANES:(p + 1) * LANES],
                             NT_DIMS, preferred_element_type=f32) for p in range(n_pairs)], axis=0)

    def softplus2(z):
        return jnp.maximum(z, 0.0) + jnp.log2(1.0 + jnp.exp2(-jnp.abs(z)))

    def stick(z, later):
        sp = softplus2(z)
        cs = jnp.dot(sp.astype(bf), tri, preferred_element_type=f32)
        log_w = z - sp - cs[:, 0:blk]
        if later is not None:
            log_w = log_w - later
        return jnp.exp2(log_w).astype(bf), cs[:, blk:blk + LANES]

    def weighted_values(w, pair):
        o2 = jnp.dot(w, v_buf[:, pair * LANES:(pair + 1) * LANES],
                     preferred_element_type=f32)
        return jnp.where(first_head, o2[0:blk], o2[blk:2 * blk])

    row = lax.broadcasted_iota(jnp.int32, (blk, blk), 0)
    col = lax.broadcasted_iota(jnp.int32, (blk, blk), 1)
    pen_diag = jnp.where(col < row, 0.0, ATT_MASKED)

    def window(cur_ref, prev_ref, c, lanes):
        before = prev_ref[0, :, lanes] if c == 0 else cur_ref[0, (c - 1) * blk:c * blk, lanes]
        return jnp.concatenate([before, cur_ref[0, c * blk:(c + 1) * blk, lanes]], axis=0)

    def near_scores(c):
        base = c * head_rows
        return jnp.concatenate(
            [lax.dot_general(
                qm_scr[base + 2 * p * blk:base + (2 * p + 2) * blk],
                window(kc_ref, kp_ref, c, slice(p * LANES, (p + 1) * LANES)),
                NT_DIMS, preferred_element_type=f32) for p in range(n_pairs)], axis=0)

    def masked_near_scores(c):
        z = near_scores(c)
        z_prev = z[:, 0:blk]
        if c == 0:
            z_prev = z_prev - jnp.where(first_qb >= 1, 0.0, ATT_MASKED)
        z_diag = (z[:, blk:2 * blk].reshape(N_HEADS, blk, blk) - pen_diag[None]).reshape(head_rows, blk)
        return jnp.concatenate([z_prev, z_diag], axis=1)

    all_qbs = range(n_qb)
    least = [None] * n_qb

    def phase_scores(qbs):
        return jnp.concatenate([masked_near_scores(c) for c in qbs], axis=0)

    def phase_softplus(z):
        return softplus2(z)

    def phase_suffix(sp):
        return jnp.dot(sp.astype(bf), tri_near[:, 0:2 * blk], preferred_element_type=f32)

    def phase_weights(qbs, z, sp, cs):
        total = cs[:, 0:LANES] + sp[:, 0:LANES]
        carry_scr[qbs[0] * head_rows:(qbs[-1] + 1) * head_rows] = total
        for k, c in enumerate(qbs):
            least[c] = jnp.min(total[k * head_rows:(k + 1) * head_rows], axis=0, keepdims=True)[0, 0]
        return jnp.exp2(z - sp - cs).astype(bf)

    def phase_values(qbs, w):
        for k, c in enumerate(qbs):
            for pair in range(n_pairs):
                rows_w = slice(k * head_rows + 2 * pair * blk, k * head_rows + (2 * pair + 2) * blk)
                lanes = slice(pair * LANES, (pair + 1) * LANES)
                o2 = jnp.dot(w[rows_w], window(vc_ref, vp_ref, c, lanes),
                             preferred_element_type=f32)
                acc_scr[c * blk:(c + 1) * blk, lanes] = jnp.where(first_head, o2[0:blk], o2[blk:2 * blk])

    z_near = phase_scores(all_qbs)
    sp = phase_softplus(z_near)
    cs = phase_suffix(sp)
    phase_values(all_qbs, phase_weights(all_qbs, z_near, sp, cs))

    for c in range(n_qb):
        rows_c = slice(c * blk, (c + 1) * blk)
        base = c * head_rows

        def body(state, c=c, rows_c=rows_c, base=base):
            j, _ = state
            key_rows = pl.ds(pl.multiple_of(j * blk, blk), blk)
            k_copy = pltpu.make_async_copy(k_hbm.at[batch, key_rows], k_buf, dma_sem.at[0])
            v_copy = pltpu.make_async_copy(v_hbm.at[batch, key_rows], v_buf, dma_sem.at[1])
            k_copy.start()
            v_copy.start()
            k_copy.wait()
            v_copy.wait()
            carry = jnp.broadcast_to(carry_scr[base:base + head_rows, 0:1], (head_rows, LANES))
            w, sum_j = stick(scores(c), carry)
            carry = carry + sum_j
            carry_scr[base:base + head_rows] = carry
            for pair in range(n_pairs):
                acc_scr[rows_c, pair * LANES:(pair + 1) * LANES] += weighted_values(
                    w[2 * pair * blk:(2 * pair + 2) * blk], pair)
            return j - 1, jnp.min(carry)

        def cond(state):
            j, least_c = state
            return jnp.logical_and(j >= 0, least_c < ATT_SKIP_SUM * LOG2_E)

        lax.while_loop(cond, body, (first_qb + (c - 2), least[c]))

    o_ref[0] = acc_scr[...].astype(o_ref.dtype)


def _attn_call(q3, k3, v3, tri):
    bsz, seq, _ = q3.shape
    blk = ATT_BLOCK
    rows = ATT_ROWS
    n_qb = rows // blk
    tile = lambda b, i: (b, i, 0)
    block_before = lambda b, i: (b, jnp.maximum(i * n_qb - 1, 0), 0)
    return pl.pallas_call(
        _attn_kernel,
        grid=(bsz, seq // rows),
        in_specs=[
            pl.BlockSpec((1, rows, D_ATTN), tile),
            pl.BlockSpec((1, rows, D_ATTN), tile),
            pl.BlockSpec((1, blk, D_ATTN), block_before),
            pl.BlockSpec((1, rows, D_ATTN), tile),
            pl.BlockSpec((1, blk, D_ATTN), block_before),
            pl.BlockSpec(memory_space=pl.ANY),
            pl.BlockSpec(memory_space=pl.ANY),
            pl.BlockSpec(tri.shape, lambda b, i: (0, 0)),
        ],
        out_specs=pl.BlockSpec((1, rows, D_ATTN), tile),
        out_shape=jax.ShapeDtypeStruct((bsz, seq, D_ATTN), jnp.bfloat16),
        scratch_shapes=[
            pltpu.VMEM((n_qb * N_HEADS * blk, LANES), jnp.bfloat16),
            pltpu.VMEM((n_qb * N_HEADS * blk, LANES), jnp.float32),
            pltpu.VMEM((rows, D_ATTN), jnp.float32),
            pltpu.VMEM((blk, D_ATTN), jnp.bfloat16),
            pltpu.VMEM((blk, D_ATTN), jnp.bfloat16),
            pltpu.SemaphoreType.DMA((2,)),
        ],
        compiler_params=pltpu.CompilerParams(
            dimension_semantics=("arbitrary", "arbitrary"), vmem_limit_bytes=VMEM_LIMIT),
        name="attn",
    )(q3, k3, k3, v3, v3, k3, v3, tri)


def _merge_kernel(x_ref, ys_ref, ya_ref, gates_ref, wglu_ref, bglu_ref, wus_ref, wua_ref,
                  wout_ref, gm_ref, w1_ref, w2_ref, gf_ref, o_ref, y_scr):
    for stage in _merge_mlp_stages(x_ref, ys_ref, ya_ref[...], gates_ref, wglu_ref, bglu_ref,
                                   wus_ref, wua_ref, wout_ref, gm_ref, w1_ref, w2_ref, gf_ref,
                                   o_ref, y_scr):
        stage()


def _merge_call(x2, ys, ya, gates, wglu, bglu, wus, wua, wout, gm, w1, w2, gf):
    tokens = x2.shape[0]
    rows = MERGE_ROWS
    const = lambda i: (0, 0)

    def resident(arr):
        return pl.BlockSpec(arr.shape, const, pipeline_mode=pl.Buffered(1))

    return pl.pallas_call(
        _merge_kernel,
        grid=(tokens // rows,),
        in_specs=[
            pl.BlockSpec((rows, D_MODEL), lambda i: (i, 0)),
            pl.BlockSpec((N_LANE_BLOCKS, rows // SSM_CHUNK, CHUNK_COLS), lambda i: (0, i, 0)),
            pl.BlockSpec((rows, D_ATTN), lambda i: (i, 0)),
            pl.BlockSpec((rows, 2 * D_MODEL), lambda i: (i, 0)),
            resident(wglu), resident(bglu), resident(wus), resident(wua), resident(wout),
            resident(gm), resident(w1), resident(w2), resident(gf),
        ],
        out_specs=pl.BlockSpec((rows, D_MODEL), lambda i: (i, 0)),
        out_shape=jax.ShapeDtypeStruct((tokens, D_MODEL), jnp.float32),
        scratch_shapes=[pltpu.VMEM((N_LANE_BLOCKS, rows, LANES), jnp.float32)],
        compiler_params=pltpu.CompilerParams(
            dimension_semantics=("arbitrary",), vmem_limit_bytes=VMEM_LIMIT),
        name="merge_mlp",
    )(x2, ys, ya, gates, wglu, bglu, wus, wua, wout, gm, w1, w2, gf)


def _suffix_sum_matrix():
    win = 2 * ATT_BLOCK
    r = jnp.arange(win)[:, None]
    c = jnp.arange(win + LANES)[None, :]
    return jnp.where((c >= win) | (r > c), 1.0, 0.0).astype(jnp.bfloat16)


def kernel(x, norm_mix, w_in, A_re, A_im, log_dt, B_re, B_im, C_re, C_im, D_skip, w_glu, b_glu,
           w_up_ssm, w_up_attn, w_gate, b_gate, w_out, norm_mlp, w_ff1, w_ff2, norm_final):
    bsz, seq, _ = x.shape
    tokens = bsz * seq
    bf = jnp.bfloat16
    assert norm_mix.shape[0] == 1, "single layer"
    assert seq % (SSM_CHUNK * 8) == 0 and seq % ATT_BLOCK == 0
    assert tokens % PROJ_ROWS == 0 and tokens % MERGE_ROWS == 0 and seq % min(PROJ_ROWS, seq) == 0

    x2 = x.reshape(tokens, D_MODEL)
    ussm, q, k, v, gates = _inproj_call(x2, norm_mix, w_in[0].astype(bf), w_gate[0].astype(bf),
                                        b_gate, bsz, seq)

    toep, inj, ro, a_tab = _ssm_tables(
        A_re[0], A_im[0], log_dt[0], B_re[0], B_im[0], C_re[0], C_im[0], D_skip[0])
    e_inj, e_ro = _expansion_matrices()
    ys = _ssm_call(ussm, toep, inj, ro, e_inj, e_ro, a_tab, bsz)

    ya = _attn_call(q.reshape(bsz, seq, D_ATTN), k.reshape(bsz, seq, D_ATTN),
                    v.reshape(bsz, seq, D_ATTN), _suffix_sum_matrix()).reshape(tokens, D_ATTN)

    out = _merge_call(x2, ys, ya, gates, w_glu[0].astype(bf), b_glu, w_up_ssm[0].astype(bf),
                      w_up_attn[0].astype(bf), w_out[0].astype(bf), norm_mlp,
                      w_ff1[0].astype(bf), w_ff2[0].astype(bf), norm_final.reshape(1, D_MODEL))
    return out.reshape(bsz, seq, D_MODEL)
```

```python
import functools

import jax
import jax.numpy as jnp
import numpy as np
from jax import lax
from jax.experimental import pallas as pl
from jax.experimental.pallas import tpu as pltpu

D_MODEL = 1024
D_SSM = 512
SSM_GROUP = 16
STATE = 64
N_HEADS = 8
HEAD_DIM = 64
D_ATTN = 512
D_FF = 4096
EPS = 1e-6

LANES = 128
MXU_TILE = 256
VMEM_LIMIT = 52 * 1024 * 1024

SSM_CHUNK = 16
N_LANE_BLOCKS = D_SSM // LANES
GROUPS_PER_BLOCK = LANES // SSM_GROUP
STATE_COLS = 2 * GROUPS_PER_BLOCK * STATE
CHUNK_COLS = SSM_CHUNK * LANES
N_PARTS = 4
PART_GROUPS = GROUPS_PER_BLOCK // N_PARTS
PART_LANES = LANES // N_PARTS
STEPS_PER_TILE = LANES // PART_LANES
PART_COLS = SSM_CHUNK * PART_LANES
PART_STATE = STATE_COLS // N_PARTS
SCAN_POWERS = (1, 2, 4, 8)
SCAN_PAD = 8
ATT_BLOCK = 128
ATT_SKIP_SUM = 64.0
ATT_MASKED = 1e30
LOG2_E = 1.4426950408889634
NT_DIMS = (((1,), (1,)), ((), ()))
GELU_C1 = 0.7978845608028654
GELU_C3 = GELU_C1 * 0.044715
PROJ_ROWS = 1024
MERGE_ROWS = 512
FF_CHUNK = 1024
ATT_ROWS = 1024


def _rmsnorm_f32(x, g):
    ms = jnp.mean(x * x, axis=-1, keepdims=True)
    return x * lax.rsqrt(ms + EPS) * g


def _lane_part(shape):
    lanes = lax.broadcasted_iota(jnp.int32, shape, len(shape) - 1)
    return lax.shift_right_logical(lanes, PART_LANES.bit_length() - 1)


def _roll_lanes(x, shift):
    shift %= LANES
    return x if shift == 0 else pltpu.roll(x, shift, x.ndim - 1)


def _merge_lane_parts(sources, lane_part):
    out = sources[0]
    for k in range(1, len(sources)):
        out = jnp.where(lane_part == k, sources[k], out)
    return out


def _inproj_kernel(x_ref, g_ref, w_ref, wg_ref, bg_ref,
                   ussm_ref, q_ref, k_ref, v_ref, gates_ref, pssm_scr):
    x = x_ref[...]
    u = _rmsnorm_f32(x, g_ref[...]).astype(jnp.bfloat16)
    rows = x.shape[0]
    chunk_rows = rows // SSM_CHUNK

    p_ssm = jnp.dot(u, w_ref[:, 0:D_SSM], preferred_element_type=jnp.float32)
    for blk in range(N_LANE_BLOCKS):
        pssm_scr[blk] = p_ssm[:, blk * LANES:(blk + 1) * LANES]
    lane_part = _lane_part((chunk_rows, LANES))
    for m in range(SSM_CHUNK // STEPS_PER_TILE):
        for blk in range(N_LANE_BLOCKS):
            steps = [pssm_scr[blk, pl.ds(m * STEPS_PER_TILE + a, chunk_rows, stride=SSM_CHUNK), :]
                     for a in range(STEPS_PER_TILE)]
            for p in range(N_PARTS):
                tile = _merge_lane_parts(
                    [_roll_lanes(steps[a], (a - p) * PART_LANES) for a in range(STEPS_PER_TILE)],
                    lane_part)
                ussm_ref[blk, :, p * PART_COLS + m * LANES:p * PART_COLS + (m + 1) * LANES] = (
                    tile.astype(jnp.bfloat16))

    p_q = jnp.dot(u, w_ref[:, D_SSM:D_SSM + D_ATTN], preferred_element_type=jnp.float32)
    q_ref[...] = (p_q * (HEAD_DIM ** -0.5 * LOG2_E)).astype(jnp.bfloat16)

    p_k = jnp.dot(u, w_ref[:, D_SSM + D_ATTN:D_SSM + 2 * D_ATTN],
                  preferred_element_type=jnp.float32)
    k_ref[...] = p_k.astype(jnp.bfloat16)

    p_v = jnp.dot(u, w_ref[:, D_SSM + 2 * D_ATTN:D_SSM + 3 * D_ATTN],
                  preferred_element_type=jnp.float32)
    v_ref[...] = p_v.astype(jnp.bfloat16)

    for c in range(2 * D_MODEL // 512):
        pg = jnp.dot(u, wg_ref[:, c * 512:(c + 1) * 512], preferred_element_type=jnp.float32)
        pg = pg + bg_ref[:, c * 512:(c + 1) * 512]
        gates_ref[:, c * 512:(c + 1) * 512] = jax.nn.sigmoid(pg).astype(jnp.bfloat16)


def _inproj_call(x2, g, w_in, w_gate, bg, bsz, seq):
    tokens = bsz * seq
    rows = min(PROJ_ROWS, seq)
    const = lambda i: (0, 0)
    return pl.pallas_call(
        _inproj_kernel,
        grid=(tokens // rows,),
        in_specs=[
            pl.BlockSpec((rows, D_MODEL), lambda i: (i, 0)),
            pl.BlockSpec((1, D_MODEL), const),
            pl.BlockSpec(w_in.shape, const),
            pl.BlockSpec(w_gate.shape, const),
            pl.BlockSpec((1, 2 * D_MODEL), const),
        ],
        out_specs=[
            pl.BlockSpec((N_LANE_BLOCKS, rows // SSM_CHUNK, CHUNK_COLS), lambda i: (0, i, 0)),
            pl.BlockSpec((rows, D_ATTN), lambda i: (i, 0)),
            pl.BlockSpec((rows, D_ATTN), lambda i: (i, 0)),
            pl.BlockSpec((rows, D_ATTN), lambda i: (i, 0)),
            pl.BlockSpec((rows, 2 * D_MODEL), lambda i: (i, 0)),
        ],
        out_shape=[
            jax.ShapeDtypeStruct((N_LANE_BLOCKS, tokens // SSM_CHUNK, CHUNK_COLS), jnp.bfloat16),
            jax.ShapeDtypeStruct((tokens, D_ATTN), jnp.bfloat16),
            jax.ShapeDtypeStruct((tokens, D_ATTN), jnp.bfloat16),
            jax.ShapeDtypeStruct((tokens, D_ATTN), jnp.bfloat16),
            jax.ShapeDtypeStruct((tokens, 2 * D_MODEL), jnp.bfloat16),
        ],
        scratch_shapes=[pltpu.VMEM((N_LANE_BLOCKS, rows, LANES), jnp.float32)],
        compiler_params=pltpu.CompilerParams(
            dimension_semantics=("arbitrary",), vmem_limit_bytes=VMEM_LIMIT),
        name="inproj",
    )(x2, g, w_in, w_gate, bg)


def _ssm_tables(A_re, A_im, log_dt, B_re, B_im, C_re, C_im, D_skip):
    f32 = jnp.float32
    L = SSM_CHUNK
    nb, gb = N_LANE_BLOCKS, GROUPS_PER_BLOCK
    ar, ai = A_re.astype(f32), A_im.astype(f32)
    dt = jnp.exp(log_dt.astype(f32))[:, None]
    tau = np.arange(L + 1, dtype=np.float32)[:, None, None]
    mag = jnp.exp(ar[None] * dt[None] * tau)
    ang = ai[None] * dt[None] * tau
    pw_re, pw_im = mag * jnp.cos(ang), mag * jnp.sin(ang)
    num_re, num_im = pw_re[1] - 1.0, pw_im[1]
    den = ar * ar + ai * ai
    cf_re = (num_re * ar + num_im * ai) / den
    cf_im = (num_im * ar - num_re * ai) / den
    br, bi = B_re.astype(f32), B_im.astype(f32)
    bb_re = cf_re[..., None] * br - cf_im[..., None] * bi
    bb_im = cf_re[..., None] * bi + cf_im[..., None] * br
    ab_re = pw_re[:L, :, :, None] * bb_re[None] - pw_im[:L, :, :, None] * bb_im[None]
    ab_im = pw_re[:L, :, :, None] * bb_im[None] + pw_im[:L, :, :, None] * bb_re[None]
    cr, ci = C_re.astype(f32), C_im.astype(f32)

    kern = jnp.einsum('gdp,tgpc->tgcd', cr, ab_re) - jnp.einsum('gdp,tgpc->tgcd', ci, ab_im)
    kern = kern.reshape(L, nb, LANES, SSM_GROUP).transpose(1, 0, 2, 3)
    lane_idx = np.arange(LANES)
    spread = (np.arange(SSM_GROUP)[:, None] == lane_idx[None, :] % SSM_GROUP).astype(np.float32)
    same_group = lane_idx[:, None] // SSM_GROUP == lane_idx[None, :] // SSM_GROUP
    toep = jnp.where(same_group, jnp.matmul(kern, spread), 0.0)
    d_diag = D_skip.astype(f32).reshape(nb, LANES)[:, :, None] * np.eye(LANES, dtype=np.float32)
    toep = toep.at[:, 0].add(d_diag)

    pw_cat = jnp.concatenate([pw_re[:L], pw_im[:L]], axis=-1)[::-1]
    pw_swp = jnp.concatenate([pw_im[:L], pw_re[:L]], axis=-1)[::-1]
    bbt_re, bbt_im = bb_re.transpose(0, 2, 1), bb_im.transpose(0, 2, 1)
    b_same = jnp.concatenate([bbt_re, bbt_re], axis=-1)
    b_cross = jnp.concatenate([-bbt_im, bbt_im], axis=-1)
    inj = pw_cat[:, :, None, :] * b_same[None] + pw_swp[:, :, None, :] * b_cross[None]
    inj = inj.reshape(L, nb, N_PARTS, PART_LANES, 2 * STATE).transpose(1, 2, 0, 3, 4)
    inj = inj.reshape(nb, N_PARTS, PART_COLS, 2 * STATE)

    n_gp = PART_STATE // 2
    crt = cr.transpose(1, 0, 2).reshape(SSM_GROUP, nb, N_PARTS, n_gp)
    cit = ci.transpose(1, 0, 2).reshape(SSM_GROUP, nb, N_PARTS, n_gp)
    pwr = pw_re[1:L + 1].reshape(L, 1, nb, N_PARTS, n_gp)
    pwi = pw_im[1:L + 1].reshape(L, 1, nb, N_PARTS, n_gp)
    ro = jnp.concatenate([crt[None] * pwr - cit[None] * pwi,
                          -(crt[None] * pwi + cit[None] * pwr)], axis=-1)
    ro = ro.transpose(2, 3, 0, 1, 4).reshape(nb, N_PARTS, L * SSM_GROUP, PART_STATE)

    n = STEPS_PER_TILE
    t_part = jnp.stack([toep[:, :, p * PART_LANES:(p + 1) * PART_LANES,
                             p * PART_LANES:(p + 1) * PART_LANES] for p in range(N_PARTS)], axis=1)
    t_pad = jnp.pad(t_part, ((0, 0), (0, 0), (n - 1, 0), (0, 0), (0, 0)))
    n_dt = L // n
    toep = jnp.concatenate(
        [jnp.concatenate([t_pad[:, :, (b - a + n - 1)::n][:, :, :n_dt] for b in range(n)], axis=-1)
         for a in range(n)], axis=-2)

    steps = (L * np.array(SCAN_POWERS, np.float32))[:, None, None]
    sc_mag = jnp.exp(ar[None] * dt[None] * steps)
    sc_ang = ai[None] * dt[None] * steps
    a_tab = jnp.stack([sc_mag * jnp.cos(sc_ang), sc_mag * jnp.sin(sc_ang)], axis=1)
    a_tab = a_tab.reshape(2 * len(SCAN_POWERS), nb, gb * STATE).transpose(1, 0, 2)
    bf = jnp.bfloat16
    return toep.astype(bf), inj.astype(bf), ro.astype(bf), a_tab


def _expansion_matrices():
    gh = PART_GROUPS
    src = np.arange(2 * STATE)[:, None]
    dst = np.arange(PART_STATE)[None, :]
    e_inj = (src // STATE == dst // (gh * STATE)) & (src % STATE == dst % STATE)
    src = np.arange(SSM_CHUNK * SSM_GROUP)[None, :]
    dst = np.arange(PART_COLS)[:, None]
    e_ro = (src // SSM_GROUP == dst // PART_LANES) & (src % SSM_GROUP == dst % SSM_GROUP)
    return e_inj.astype(jnp.bfloat16), e_ro.astype(jnp.bfloat16)


def _gelu_tanh(y):
    half_y = 0.5 * y
    return half_y + half_y * jnp.tanh(y * (GELU_C1 + GELU_C3 * (y * y)))


def _ssm_kernel(x_ref, toep_ref, inj_ref, ro_ref, einj_ref, ero_ref, a_ref, y_ref,
                m_scr, p_scr, q_scr, z_scr, w_scr, hp_scr):
    gh = PART_GROUPS
    half = STATE_COLS // 2
    quarter = PART_STATE // 2
    n_pairs = SSM_CHUNK // STEPS_PER_TILE
    n_rows = x_ref.shape[1]

    @pl.when(pl.program_id(1) == 0)
    def _():
        def group_of(shape, axis, width):
            idx = lax.broadcasted_iota(jnp.int32, shape, axis)
            return lax.shift_right_logical(idx, width.bit_length() - 1) & (gh - 1)

        m_scr[...] = jnp.zeros_like(m_scr)
        for hf in range(N_PARTS):
            for sp in range(n_pairs):
                for tp in range(sp, n_pairs):
                    m_scr[hf, sp * LANES:(sp + 1) * LANES, tp * LANES:(tp + 1) * LANES] = (
                        toep_ref[0, hf, tp - sp])
            row_g = group_of((PART_COLS, PART_STATE), 0, SSM_GROUP)
            col_h = group_of((PART_COLS, PART_STATE), 1, STATE)
            p_full = jnp.dot(inj_ref[0, hf], einj_ref[...], preferred_element_type=jnp.float32)
            p_scr[hf] = jnp.where(row_g == col_h, p_full, 0.0).astype(jnp.bfloat16)
            row_h = group_of((PART_COLS, PART_STATE), 0, SSM_GROUP)
            col_g = group_of((PART_COLS, PART_STATE), 1, STATE)
            q_t = jnp.dot(ero_ref[...], ro_ref[0, hf], preferred_element_type=jnp.float32)
            q_scr[hf] = jnp.where(row_h == col_g, q_t, 0.0).T.astype(jnp.bfloat16)

    pad = SCAN_PAD
    re, im = slice(0, half), slice(half, STATE_COLS)
    for buf in (z_scr, w_scr, hp_scr):
        buf[0:pad, :] = jnp.zeros((pad, STATE_COLS), jnp.float32)
    for hf in range(N_PARTS):
        z_h = jnp.dot(x_ref[0, :, hf * PART_COLS:(hf + 1) * PART_COLS], p_scr[hf],
                      preferred_element_type=jnp.float32)
        z_scr[pad:pad + n_rows, hf * quarter:(hf + 1) * quarter] = z_h[:, 0:quarter]
        z_scr[pad:pad + n_rows, half + hf * quarter:half + (hf + 1) * quarter] = z_h[:, quarter:]

    def coef(i):
        return a_ref[0, 2 * i:2 * i + 1, :], a_ref[0, 2 * i + 1:2 * i + 2, :]

    def doubling_pass(src, dst, shift, c_re, c_im):
        s_re = src[pad - shift:pad - shift + n_rows, re]
        s_im = src[pad - shift:pad - shift + n_rows, im]
        dst[pad:pad + n_rows, re] = src[pad:pad + n_rows, re] + c_re * s_re - c_im * s_im
        dst[pad:pad + n_rows, im] = src[pad:pad + n_rows, im] + c_re * s_im + c_im * s_re

    doubling_pass(z_scr, w_scr, 1, *coef(0))
    doubling_pass(w_scr, z_scr, 2, *coef(1))
    doubling_pass(z_scr, w_scr, 4, *coef(2))
    c8_re, c8_im = coef(3)

    def tile_step(m, h):
        h_re, h_im = h
        rows8 = pl.ds(pl.multiple_of(pad + 8 * m, 8), 8)
        n_re = w_scr[rows8, re] + c8_re * h_re - c8_im * h_im
        n_im = w_scr[rows8, im] + c8_re * h_im + c8_im * h_re
        hp_scr[rows8, re] = n_re
        hp_scr[rows8, im] = n_im
        return n_re, n_im

    zero = jnp.zeros((8, half), jnp.float32)
    lax.fori_loop(0, n_rows // 8, tile_step, (zero, zero), unroll=8)

    for hf in range(N_PARTS):
        base = hf * PART_COLS
        x_h = x_ref[0, :, base:base + PART_COLS]
        hp_h = jnp.concatenate(
            [hp_scr[pad - 1:pad - 1 + n_rows, hf * quarter:(hf + 1) * quarter],
             hp_scr[pad - 1:pad - 1 + n_rows, half + hf * quarter:half + (hf + 1) * quarter]],
            axis=1).astype(jnp.bfloat16)
        for n in range(PART_COLS // MXU_TILE):
            lo, hi = n * MXU_TILE, (n + 1) * MXU_TILE
            y = jnp.dot(x_h[:, 0:hi], m_scr[hf, 0:hi, lo:hi], preferred_element_type=jnp.float32)
            y = y + jnp.dot(hp_h, q_scr[hf, :, lo:hi], preferred_element_type=jnp.float32)
            y_ref[0, :, base + lo:base + hi] = _gelu_tanh(y).astype(jnp.bfloat16)


def _ssm_call(xc, toep, inj, ro, e_inj, e_ro, a_tab, bsz):
    nb, total_rows, _ = xc.shape
    n_rows = total_rows // bsz
    per_q = lambda q, b: (q, 0, 0)
    const = lambda q, b: (0, 0)
    return pl.pallas_call(
        _ssm_kernel,
        grid=(nb, bsz),
        in_specs=[
            pl.BlockSpec((1, n_rows, CHUNK_COLS), lambda q, b: (q, b, 0)),
            pl.BlockSpec((1, N_PARTS, SSM_CHUNK // STEPS_PER_TILE, LANES, LANES),
                         lambda q, b: (q, 0, 0, 0, 0)),
            pl.BlockSpec((1, N_PARTS, PART_COLS, 2 * STATE), lambda q, b: (q, 0, 0, 0)),
            pl.BlockSpec((1, N_PARTS, SSM_CHUNK * SSM_GROUP, PART_STATE), lambda q, b: (q, 0, 0, 0)),
            pl.BlockSpec(e_inj.shape, const),
            pl.BlockSpec(e_ro.shape, const),
            pl.BlockSpec((1, 2 * len(SCAN_POWERS), STATE_COLS // 2), per_q),
        ],
        out_specs=pl.BlockSpec((1, n_rows, CHUNK_COLS), lambda q, b: (q, b, 0)),
        out_shape=jax.ShapeDtypeStruct(xc.shape, jnp.bfloat16),
        scratch_shapes=[
            pltpu.VMEM((N_PARTS, PART_COLS, PART_COLS), jnp.bfloat16),
            pltpu.VMEM((N_PARTS, PART_COLS, PART_STATE), jnp.bfloat16),
            pltpu.VMEM((N_PARTS, PART_STATE, PART_COLS), jnp.bfloat16),
            pltpu.VMEM((SCAN_PAD + n_rows, STATE_COLS), jnp.float32),
            pltpu.VMEM((SCAN_PAD + n_rows, STATE_COLS), jnp.float32),
            pltpu.VMEM((SCAN_PAD + n_rows, STATE_COLS), jnp.float32),
        ],
        compiler_params=pltpu.CompilerParams(
            dimension_semantics=("arbitrary", "arbitrary"), vmem_limit_bytes=VMEM_LIMIT),
        name="ssm",
    )(xc, toep, inj, ro, e_inj, e_ro, a_tab)


def _merge_mlp_stages(x_ref, ys_ref, ya, gates_ref, wglu_ref, bglu_ref, wus_ref, wua_ref,
                      wout_ref, gm_ref, w1_ref, w2_ref, gf_ref, o_ref, y_scr):
    f32, bf = jnp.float32, jnp.bfloat16
    chunk_rows = ys_ref.shape[1]
    st = {}

    def mix():
        gated_a = gates_ref[:, D_MODEL:2 * D_MODEL].astype(f32) * jnp.dot(
            ya, wua_ref[...], preferred_element_type=f32)
        lane_part = _lane_part((chunk_rows, LANES))
        for m in range(SSM_CHUNK // STEPS_PER_TILE):
            for blk in range(N_LANE_BLOCKS):
                tiles = [ys_ref[blk, :, p * PART_COLS + m * LANES:p * PART_COLS + (m + 1) * LANES]
                         .astype(f32) for p in range(N_PARTS)]
                for a in range(STEPS_PER_TILE):
                    step = _merge_lane_parts(
                        [_roll_lanes(tiles[p], (p - a) * PART_LANES) for p in range(N_PARTS)], lane_part)
                    y_scr[blk, pl.ds(m * STEPS_PER_TILE + a, chunk_rows, stride=SSM_CHUNK), :] = step
        y = jnp.concatenate([y_scr[blk] for blk in range(N_LANE_BLOCKS)], axis=-1)
        glu_arg = jnp.dot(y.astype(bf), wglu_ref[...], preferred_element_type=f32) + bglu_ref[...]
        y_ssm = (y * jax.nn.sigmoid(glu_arg)).astype(bf)
        up_s = jnp.dot(y_ssm, wus_ref[...], preferred_element_type=f32)
        g_s = gates_ref[:, 0:D_MODEL].astype(f32)
        st["merged"] = (g_s * up_s + gated_a).astype(bf)

    def project_out():
        h = x_ref[...] + jnp.dot(st["merged"], wout_ref[...], preferred_element_type=f32)
        st["h"] = h
        st["n"] = _rmsnorm_f32(h, gm_ref[...]).astype(bf)

    def mlp_chunk(c):
        cols = slice(c * FF_CHUNK, (c + 1) * FF_CHUNK)
        hid = jnp.maximum(jnp.dot(st["n"], w1_ref[:, cols], preferred_element_type=f32), 0.0)
        st["h"] = st["h"] + jnp.dot((hid * hid).astype(bf), w2_ref[cols, :],
                                    preferred_element_type=f32)

    def finish():
        o_ref[...] = _rmsnorm_f32(st["h"], gf_ref[...])

    chunks = [functools.partial(mlp_chunk, c) for c in range(D_FF // FF_CHUNK)]
    return [mix, project_out] + chunks + [finish]


def _attn_kernel(q_ref, kc_ref, kp_ref, vc_ref, vp_ref, k_hbm, v_hbm, tri_ref, o_ref,
                 qm_scr, carry_scr, acc_scr, k_buf, v_buf, dma_sem):
    blk = ATT_BLOCK
    n_qb = q_ref.shape[1] // blk
    n_pairs = N_HEADS // 2
    head_rows = N_HEADS * blk
    f32, bf = jnp.float32, jnp.bfloat16
    batch = pl.program_id(0)
    first_qb = pl.program_id(1) * n_qb

    lane = lax.broadcasted_iota(jnp.int32, (blk, LANES), 1)
    first_head = lane < HEAD_DIM
    tri_near = tri_ref[...]
    tri = tri_ref[blk:2 * blk, blk:2 * blk + LANES]

    for c in range(n_qb):
        for pair in range(n_pairs):
            q_pair = q_ref[0, c * blk:(c + 1) * blk, pair * LANES:(pair + 1) * LANES]
            zero = jnp.zeros_like(q_pair)
            lo = c * head_rows + 2 * pair * blk
            qm_scr[lo:lo + blk] = jnp.where(first_head, q_pair, zero)
            qm_scr[lo + blk:lo + 2 * blk] = jnp.where(first_head, zero, q_pair)

    def scores(c):
        base = c * head_rows
        return jnp.concatenate(
            [lax.dot_general(qm_scr[base + 2 * p * blk:base + (2 * p + 2) * blk],
                             k_buf[:, p * LANES:(p + 1) * LANES],
                             NT_DIMS, preferred_element_type=f32) for p in range(n_pairs)], axis=0)

    def softplus2(z):
        return jnp.maximum(z, 0.0) + jnp.log2(1.0 + jnp.exp2(-jnp.abs(z)))

    def stick(z, later):
        sp = softplus2(z)
        cs = jnp.dot(sp.astype(bf), tri, preferred_element_type=f32)
        log_w = z - sp - cs[:, 0:blk]
        if later is not None:
            log_w = log_w - later
        return jnp.exp2(log_w).astype(bf), cs[:, blk:blk + LANES]

    def weighted_values(w, pair):
        o2 = jnp.dot(w, v_buf[:, pair * LANES:(pair + 1) * LANES],
                     preferred_element_type=f32)
        return jnp.where(first_head, o2[0:blk], o2[blk:2 * blk])

    row = lax.broadcasted_iota(jnp.int32, (blk, blk), 0)
    col = lax.broadcasted_iota(jnp.int32, (blk, blk), 1)
    pen_diag = jnp.where(col < row, 0.0, ATT_MASKED)

    def window(cur_ref, prev_ref, c, lanes):
        before = prev_ref[0, :, lanes] if c == 0 else cur_ref[0, (c - 1) * blk:c * blk, lanes]
        return jnp.concatenate([before, cur_ref[0, c * blk:(c + 1) * blk, lanes]], axis=0)

    def near_scores(c):
        base = c * head_rows
        return jnp.concatenate(
            [lax.dot_general(
                qm_scr[base + 2 * p * blk:base + (2 * p + 2) * blk],
                window(kc_ref, kp_ref, c, slice(p * LANES, (p + 1) * LANES)),
                NT_DIMS, preferred_element_type=f32) for p in range(n_pairs)], axis=0)

    def masked_near_scores(c):
        z = near_scores(c)
        z_prev = z[:, 0:blk]
        if c == 0:
            z_prev = z_prev - jnp.where(first_qb >= 1, 0.0, ATT_MASKED)
        z_diag = (z[:, blk:2 * blk].reshape(N_HEADS, blk, blk) - pen_diag[None]).reshape(head_rows, blk)
        return jnp.concatenate([z_prev, z_diag], axis=1)

    all_qbs = range(n_qb)
    least = [None] * n_qb

    def phase_scores(qbs):
        return jnp.concatenate([masked_near_scores(c) for c in qbs], axis=0)

    def phase_softplus(z):
        return softplus2(z)

    def phase_suffix(sp):
        return jnp.dot(sp.astype(bf), tri_near[:, 0:2 * blk], preferred_element_type=f32)

    def phase_weights(qbs, z, sp, cs):
        total = cs[:, 0:LANES] + sp[:, 0:LANES]
        carry_scr[qbs[0] * head_rows:(qbs[-1] + 1) * head_rows] = total
        for k, c in enumerate(qbs):
            least[c] = jnp.min(total[k * head_rows:(k + 1) * head_rows], axis=0, keepdims=True)
        return jnp.exp2(z - sp - cs).astype(bf)

    def phase_values(qbs, w):
        for k, c in enumerate(qbs):
            for pair in range(n_pairs):
                rows_w = slice(k * head_rows + 2 * pair * blk, k * head_rows + (2 * pair + 2) * blk)
                lanes = slice(pair * LANES, (pair + 1) * LANES)
                o2 = jnp.dot(w[rows_w], window(vc_ref, vp_ref, c, lanes),
                             preferred_element_type=f32)
                acc_scr[c * blk:(c + 1) * blk, lanes] = jnp.where(first_head, o2[0:blk], o2[blk:2 * blk])

    z_near = phase_scores(all_qbs)
    sp = phase_softplus(z_near)
    cs = phase_suffix(sp)
    phase_values(all_qbs, phase_weights(all_qbs, z_near, sp, cs))

    skip_at = ATT_SKIP_SUM * LOG2_E
    least_of_step = functools.reduce(jnp.minimum, least)[0, 0]

    def earlier_blocks(c):
        rows_c = slice(c * blk, (c + 1) * blk)
        base = c * head_rows

        def body(state):
            j, _ = state
            key_rows = pl.ds(pl.multiple_of(j * blk, blk), blk)
            k_copy = pltpu.make_async_copy(k_hbm.at[batch, key_rows], k_buf, dma_sem.at[0])
            v_copy = pltpu.make_async_copy(v_hbm.at[batch, key_rows], v_buf, dma_sem.at[1])
            k_copy.start()
            v_copy.start()
            k_copy.wait()
            v_copy.wait()
            carry = jnp.broadcast_to(carry_scr[base:base + head_rows, 0:1], (head_rows, LANES))
            w, sum_j = stick(scores(c), carry)
            carry = carry + sum_j
            carry_scr[base:base + head_rows] = carry
            for pair in range(n_pairs):
                acc_scr[rows_c, pair * LANES:(pair + 1) * LANES] += weighted_values(
                    w[2 * pair * blk:(2 * pair + 2) * blk], pair)
            return j - 1, jnp.min(carry)

        def cond(state):
            j, least_c = state
            return jnp.logical_and(j >= 0, least_c < skip_at)

        lax.while_loop(cond, body, (first_qb + (c - 2), least[c][0, 0]))

    @pl.when(least_of_step < skip_at)
    def _():
        for c in range(n_qb):
            earlier_blocks(c)

    o_ref[0] = acc_scr[...].astype(o_ref.dtype)


def _attn_call(q3, k3, v3, tri):
    bsz, seq, _ = q3.shape
    blk = ATT_BLOCK
    rows = ATT_ROWS
    n_qb = rows // blk
    tile = lambda b, i: (b, i, 0)
    block_before = lambda b, i: (b, jnp.maximum(i * n_qb - 1, 0), 0)
    return pl.pallas_call(
        _attn_kernel,
        grid=(bsz, seq // rows),
        in_specs=[
            pl.BlockSpec((1, rows, D_ATTN), tile),
            pl.BlockSpec((1, rows, D_ATTN), tile),
            pl.BlockSpec((1, blk, D_ATTN), block_before),
            pl.BlockSpec((1, rows, D_ATTN), tile),
            pl.BlockSpec((1, blk, D_ATTN), block_before),
            pl.BlockSpec(memory_space=pl.ANY),
            pl.BlockSpec(memory_space=pl.ANY),
            pl.BlockSpec(tri.shape, lambda b, i: (0, 0)),
        ],
        out_specs=pl.BlockSpec((1, rows, D_ATTN), tile),
        out_shape=jax.ShapeDtypeStruct((bsz, seq, D_ATTN), jnp.bfloat16),
        scratch_shapes=[
            pltpu.VMEM((n_qb * N_HEADS * blk, LANES), jnp.bfloat16),
            pltpu.VMEM((n_qb * N_HEADS * blk, LANES), jnp.float32),
            pltpu.VMEM((rows, D_ATTN), jnp.float32),
            pltpu.VMEM((blk, D_ATTN), jnp.bfloat16),
            pltpu.VMEM((blk, D_ATTN), jnp.bfloat16),
            pltpu.SemaphoreType.DMA((2,)),
        ],
        compiler_params=pltpu.CompilerParams(
            dimension_semantics=("arbitrary", "arbitrary"), vmem_limit_bytes=VMEM_LIMIT),
        name="attn",
    )(q3, k3, k3, v3, v3, k3, v3, tri)


def _merge_kernel(x_ref, ys_ref, ya_ref, gates_ref, wglu_ref, bglu_ref, wus_ref, wua_ref,
                  wout_ref, gm_ref, w1_ref, w2_ref, gf_ref, o_ref, y_scr):
    for stage in _merge_mlp_stages(x_ref, ys_ref, ya_ref[...], gates_ref, wglu_ref, bglu_ref,
                                   wus_ref, wua_ref, wout_ref, gm_ref, w1_ref, w2_ref, gf_ref,
                                   o_ref, y_scr):
        stage()


def _merge_call(x2, ys, ya, gates, wglu, bglu, wus, wua, wout, gm, w1, w2, gf):
    tokens = x2.shape[0]
    rows = MERGE_ROWS
    const = lambda i: (0, 0)

    def resident(arr):
        return pl.BlockSpec(arr.shape, const, pipeline_mode=pl.Buffered(1))

    return pl.pallas_call(
        _merge_kernel,
        grid=(tokens // rows,),
        in_specs=[
            pl.BlockSpec((rows, D_MODEL), lambda i: (i, 0)),
            pl.BlockSpec((N_LANE_BLOCKS, rows // SSM_CHUNK, CHUNK_COLS), lambda i: (0, i, 0)),
            pl.BlockSpec((rows, D_ATTN), lambda i: (i, 0)),
            pl.BlockSpec((rows, 2 * D_MODEL), lambda i: (i, 0)),
            resident(wglu), resident(bglu), resident(wus), resident(wua), resident(wout),
            resident(gm), resident(w1), resident(w2), resident(gf),
        ],
        out_specs=pl.BlockSpec((rows, D_MODEL), lambda i: (i, 0)),
        out_shape=jax.ShapeDtypeStruct((tokens, D_MODEL), jnp.float32),
        scratch_shapes=[pltpu.VMEM((N_LANE_BLOCKS, rows, LANES), jnp.float32)],
        compiler_params=pltpu.CompilerParams(
            dimension_semantics=("arbitrary",), vmem_limit_bytes=VMEM_LIMIT),
        name="merge_mlp",
    )(x2, ys, ya, gates, wglu, bglu, wus, wua, wout, gm, w1, w2, gf)


def _suffix_sum_matrix():
    win = 2 * ATT_BLOCK
    r = np.arange(win)[:, None]
    c = np.arange(win + LANES)[None, :]
    return ((c >= win) | (r > c)).astype(jnp.bfloat16)


def kernel(x, norm_mix, w_in, A_re, A_im, log_dt, B_re, B_im, C_re, C_im, D_skip, w_glu, b_glu,
           w_up_ssm, w_up_attn, w_gate, b_gate, w_out, norm_mlp, w_ff1, w_ff2, norm_final):
    bsz, seq, _ = x.shape
    tokens = bsz * seq
    bf = jnp.bfloat16
    assert norm_mix.shape[0] == 1, "single layer"
    assert seq % (SSM_CHUNK * 8) == 0 and seq % ATT_BLOCK == 0
    assert tokens % PROJ_ROWS == 0 and tokens % MERGE_ROWS == 0 and seq % min(PROJ_ROWS, seq) == 0

    x2 = x.reshape(tokens, D_MODEL)
    ussm, q, k, v, gates = _inproj_call(x2, norm_mix, w_in[0].astype(bf), w_gate[0].astype(bf),
                                        b_gate, bsz, seq)

    toep, inj, ro, a_tab = _ssm_tables(
        A_re[0], A_im[0], log_dt[0], B_re[0], B_im[0], C_re[0], C_im[0], D_skip[0])
    e_inj, e_ro = _expansion_matrices()
    ys = _ssm_call(ussm, toep, inj, ro, e_inj, e_ro, a_tab, bsz)

    ya = _attn_call(q.reshape(bsz, seq, D_ATTN), k.reshape(bsz, seq, D_ATTN),
                    v.reshape(bsz, seq, D_ATTN), _suffix_sum_matrix()).reshape(tokens, D_ATTN)

    out = _merge_call(x2, ys, ya, gates, w_glu[0].astype(bf), b_glu, w_up_ssm[0].astype(bf),
                      w_up_attn[0].astype(bf), w_out[0].astype(bf), norm_mlp,
                      w_ff1[0].astype(bf), w_ff2[0].astype(bf), norm_final.reshape(1, D_MODEL))
    return out.reshape(bsz, seq, D_MODEL)
```

```python
import functools

import jax
import jax.numpy as jnp
from jax import lax
from jax.experimental import pallas as pl
from jax.experimental.pallas import tpu as pltpu

D_MODEL = 1024
D_SSM = 512
SSM_GROUP = 16
STATE = 64
N_HEADS = 8
HEAD_DIM = 64
D_ATTN = 512
D_FF = 4096
EPS = 1e-6

LANES = 128
MXU_TILE = 256
VMEM_LIMIT = 52 * 1024 * 1024

SSM_CHUNK = 16
N_LANE_BLOCKS = D_SSM // LANES
GROUPS_PER_BLOCK = LANES // SSM_GROUP
STATE_COLS = 2 * GROUPS_PER_BLOCK * STATE
CHUNK_COLS = SSM_CHUNK * LANES
N_PARTS = 4
PART_GROUPS = GROUPS_PER_BLOCK // N_PARTS
PART_LANES = LANES // N_PARTS
STEPS_PER_TILE = LANES // PART_LANES
PART_COLS = SSM_CHUNK * PART_LANES
PART_STATE = STATE_COLS // N_PARTS
SCAN_POWERS = (1, 2, 4, 8)
SCAN_PAD = 8
ATT_BLOCK = 128
ATT_SKIP_SUM = 64.0
ATT_MASKED = 1e30
LOG2_E = 1.4426950408889634
NT_DIMS = (((1,), (1,)), ((), ()))
GELU_C1 = 0.7978845608028654
GELU_C3 = GELU_C1 * 0.044715
PROJ_ROWS = 1024
MERGE_ROWS = 512
FF_CHUNK = 1024
ATT_ROWS = 1024


def _rmsnorm_f32(x, g):
    ms = jnp.mean(x * x, axis=-1, keepdims=True)
    return x * lax.rsqrt(ms + EPS) * g


def _lane_part(shape):
    lanes = lax.broadcasted_iota(jnp.int32, shape, len(shape) - 1)
    return lax.shift_right_logical(lanes, PART_LANES.bit_length() - 1)


def _roll_lanes(x, shift):
    shift %= LANES
    return x if shift == 0 else pltpu.roll(x, shift, x.ndim - 1)


def _merge_lane_parts(sources, lane_part):
    out = sources[0]
    for k in range(1, len(sources)):
        out = jnp.where(lane_part == k, sources[k], out)
    return out


def _inproj_kernel(x_ref, g_ref, w_ref, wg_ref, bg_ref,
                   ussm_ref, q_ref, k_ref, v_ref, gates_ref, pssm_scr):
    x = x_ref[...]
    u = _rmsnorm_f32(x, g_ref[...]).astype(jnp.bfloat16)
    rows = x.shape[0]
    chunk_rows = rows // SSM_CHUNK

    for c in range(2 * D_MODEL // 512):
        pg = jnp.dot(u, wg_ref[:, c * 512:(c + 1) * 512], preferred_element_type=jnp.float32)
        pg = pg + bg_ref[:, c * 512:(c + 1) * 512]
        gates_ref[:, c * 512:(c + 1) * 512] = jax.nn.sigmoid(pg).astype(jnp.bfloat16)

    p_ssm = jnp.dot(u, w_ref[:, 0:D_SSM], preferred_element_type=jnp.float32)
    for blk in range(N_LANE_BLOCKS):
        pssm_scr[blk] = p_ssm[:, blk * LANES:(blk + 1) * LANES]
    lane_part = _lane_part((chunk_rows, LANES))
    for m in range(SSM_CHUNK // STEPS_PER_TILE):
        for blk in range(N_LANE_BLOCKS):
            steps = [pssm_scr[blk, pl.ds(m * STEPS_PER_TILE + a, chunk_rows, stride=SSM_CHUNK), :]
                     for a in range(STEPS_PER_TILE)]
            for p in range(N_PARTS):
                tile = _merge_lane_parts(
                    [_roll_lanes(steps[a], (a - p) * PART_LANES) for a in range(STEPS_PER_TILE)],
                    lane_part)
                ussm_ref[blk, :, p * PART_COLS + m * LANES:p * PART_COLS + (m + 1) * LANES] = (
                    tile.astype(jnp.bfloat16))

    p_q = jnp.dot(u, w_ref[:, D_SSM:D_SSM + D_ATTN], preferred_element_type=jnp.float32)
    q_ref[...] = (p_q * (HEAD_DIM ** -0.5 * LOG2_E)).astype(jnp.bfloat16)

    p_k = jnp.dot(u, w_ref[:, D_SSM + D_ATTN:D_SSM + 2 * D_ATTN],
                  preferred_element_type=jnp.float32)
    k_ref[...] = p_k.astype(jnp.bfloat16)

    p_v = jnp.dot(u, w_ref[:, D_SSM + 2 * D_ATTN:D_SSM + 3 * D_ATTN],
                  preferred_element_type=jnp.float32)
    v_ref[...] = p_v.astype(jnp.bfloat16)


def _inproj_call(x2, g, w_in, w_gate, bg, bsz, seq):
    tokens = bsz * seq
    rows = min(PROJ_ROWS, seq)
    const = lambda i: (0, 0)
    return pl.pallas_call(
        _inproj_kernel,
        grid=(tokens // rows,),
        in_specs=[
            pl.BlockSpec((rows, D_MODEL), lambda i: (i, 0)),
            pl.BlockSpec((1, D_MODEL), const),
            pl.BlockSpec(w_in.shape, const),
            pl.BlockSpec(w_gate.shape, const),
            pl.BlockSpec((1, 2 * D_MODEL), const),
        ],
        out_specs=[
            pl.BlockSpec((N_LANE_BLOCKS, rows // SSM_CHUNK, CHUNK_COLS), lambda i: (0, i, 0)),
            pl.BlockSpec((rows, D_ATTN), lambda i: (i, 0)),
            pl.BlockSpec((rows, D_ATTN), lambda i: (i, 0)),
            pl.BlockSpec((rows, D_ATTN), lambda i: (i, 0)),
            pl.BlockSpec((rows, 2 * D_MODEL), lambda i: (i, 0)),
        ],
        out_shape=[
            jax.ShapeDtypeStruct((N_LANE_BLOCKS, tokens // SSM_CHUNK, CHUNK_COLS), jnp.bfloat16),
            jax.ShapeDtypeStruct((tokens, D_ATTN), jnp.bfloat16),
            jax.ShapeDtypeStruct((tokens, D_ATTN), jnp.bfloat16),
            jax.ShapeDtypeStruct((tokens, D_ATTN), jnp.bfloat16),
            jax.ShapeDtypeStruct((tokens, 2 * D_MODEL), jnp.bfloat16),
        ],
        scratch_shapes=[pltpu.VMEM((N_LANE_BLOCKS, rows, LANES), jnp.float32)],
        compiler_params=pltpu.CompilerParams(
            dimension_semantics=("arbitrary",), vmem_limit_bytes=VMEM_LIMIT),
        name="inproj",
    )(x2, g, w_in, w_gate, bg)


def _ssm_tables(A_re, A_im, log_dt, B_re, B_im, C_re, C_im, D_skip):
    f32 = jnp.float32
    L = SSM_CHUNK
    nb, gb = N_LANE_BLOCKS, GROUPS_PER_BLOCK
    ar, ai = A_re.astype(f32), A_im.astype(f32)
    dt = jnp.exp(log_dt.astype(f32))[:, None]
    tau = jnp.arange(L + 1, dtype=f32)[:, None, None]
    mag = jnp.exp(ar[None] * dt[None] * tau)
    ang = ai[None] * dt[None] * tau
    pw_re, pw_im = mag * jnp.cos(ang), mag * jnp.sin(ang)
    num_re, num_im = pw_re[1] - 1.0, pw_im[1]
    den = ar * ar + ai * ai
    cf_re = (num_re * ar + num_im * ai) / den
    cf_im = (num_im * ar - num_re * ai) / den
    br, bi = B_re.astype(f32), B_im.astype(f32)
    bb_re = cf_re[..., None] * br - cf_im[..., None] * bi
    bb_im = cf_re[..., None] * bi + cf_im[..., None] * br
    ab_re = pw_re[:L, :, :, None] * bb_re[None] - pw_im[:L, :, :, None] * bb_im[None]
    ab_im = pw_re[:L, :, :, None] * bb_im[None] + pw_im[:L, :, :, None] * bb_re[None]
    cr, ci = C_re.astype(f32), C_im.astype(f32)

    kern = jnp.einsum('gdp,tgpc->tgcd', cr, ab_re) - jnp.einsum('gdp,tgpc->tgcd', ci, ab_im)
    kern = kern.reshape(L, nb, LANES, SSM_GROUP).transpose(1, 0, 2, 3)
    lane_idx = jnp.arange(LANES)
    spread = (jnp.arange(SSM_GROUP)[:, None] == lane_idx[None, :] % SSM_GROUP).astype(f32)
    same_group = lane_idx[:, None] // SSM_GROUP == lane_idx[None, :] // SSM_GROUP
    toep = jnp.where(same_group, jnp.matmul(kern, spread), 0.0)
    d_diag = D_skip.astype(f32).reshape(nb, LANES)[:, :, None] * jnp.eye(LANES, dtype=f32)
    toep = toep.at[:, 0].add(d_diag)

    pw_cat = jnp.concatenate([pw_re[:L], pw_im[:L]], axis=-1)[::-1]
    pw_swp = jnp.concatenate([pw_im[:L], pw_re[:L]], axis=-1)[::-1]
    bbt_re, bbt_im = bb_re.transpose(0, 2, 1), bb_im.transpose(0, 2, 1)
    b_same = jnp.concatenate([bbt_re, bbt_re], axis=-1)
    b_cross = jnp.concatenate([-bbt_im, bbt_im], axis=-1)
    inj = pw_cat[:, :, None, :] * b_same[None] + pw_swp[:, :, None, :] * b_cross[None]
    inj = inj.reshape(L, nb, N_PARTS, PART_LANES, 2 * STATE).transpose(1, 2, 0, 3, 4)
    inj = inj.reshape(nb, N_PARTS, PART_COLS, 2 * STATE)

    n_gp = PART_STATE // 2
    crt = cr.transpose(1, 0, 2).reshape(SSM_GROUP, nb, N_PARTS, n_gp)
    cit = ci.transpose(1, 0, 2).reshape(SSM_GROUP, nb, N_PARTS, n_gp)
    pwr = pw_re[1:L + 1].reshape(L, 1, nb, N_PARTS, n_gp)
    pwi = pw_im[1:L + 1].reshape(L, 1, nb, N_PARTS, n_gp)
    ro = jnp.concatenate([crt[None] * pwr - cit[None] * pwi,
                          -(crt[None] * pwi + cit[None] * pwr)], axis=-1)
    ro = ro.transpose(2, 3, 0, 1, 4).reshape(nb, N_PARTS, L * SSM_GROUP, PART_STATE)

    n = STEPS_PER_TILE
    t_part = jnp.stack([toep[:, :, p * PART_LANES:(p + 1) * PART_LANES,
                             p * PART_LANES:(p + 1) * PART_LANES] for p in range(N_PARTS)], axis=1)
    t_pad = jnp.pad(t_part, ((0, 0), (0, 0), (n - 1, 0), (0, 0), (0, 0)))
    n_dt = L // n
    toep = jnp.concatenate(
        [jnp.concatenate([t_pad[:, :, (b - a + n - 1)::n][:, :, :n_dt] for b in range(n)], axis=-1)
         for a in range(n)], axis=-2)

    steps = (L * jnp.array(SCAN_POWERS, f32))[:, None, None]
    sc_mag = jnp.exp(ar[None] * dt[None] * steps)
    sc_ang = ai[None] * dt[None] * steps
    a_tab = jnp.stack([sc_mag * jnp.cos(sc_ang), sc_mag * jnp.sin(sc_ang)], axis=1)
    a_tab = a_tab.reshape(2 * len(SCAN_POWERS), nb, gb * STATE).transpose(1, 0, 2)
    bf = jnp.bfloat16
    return toep.astype(bf), inj.astype(bf), ro.astype(bf), a_tab


def _expansion_matrices():
    gh = PART_GROUPS
    src = jnp.arange(2 * STATE)[:, None]
    dst = jnp.arange(PART_STATE)[None, :]
    e_inj = (src // STATE == dst // (gh * STATE)) & (src % STATE == dst % STATE)
    src = jnp.arange(SSM_CHUNK * SSM_GROUP)[None, :]
    dst = jnp.arange(PART_COLS)[:, None]
    e_ro = (src // SSM_GROUP == dst // PART_LANES) & (src % SSM_GROUP == dst % SSM_GROUP)
    return e_inj.astype(jnp.bfloat16), e_ro.astype(jnp.bfloat16)


def _gelu_tanh(y):
    half_y = 0.5 * y
    return half_y + half_y * jnp.tanh(y * (GELU_C1 + GELU_C3 * (y * y)))


def _ssm_kernel(x_ref, toep_ref, inj_ref, ro_ref, einj_ref, ero_ref, a_ref, y_ref,
                m_scr, p_scr, q_scr, z_scr, w_scr, hp_scr):
    gh = PART_GROUPS
    half = STATE_COLS // 2
    quarter = PART_STATE // 2
    n_pairs = SSM_CHUNK // STEPS_PER_TILE
    n_rows = x_ref.shape[1]

    @pl.when(pl.program_id(1) == 0)
    def _():
        def group_of(shape, axis, width):
            idx = lax.broadcasted_iota(jnp.int32, shape, axis)
            return lax.shift_right_logical(idx, width.bit_length() - 1) & (gh - 1)

        m_scr[...] = jnp.zeros_like(m_scr)
        for hf in range(N_PARTS):
            for sp in range(n_pairs):
                for tp in range(sp, n_pairs):
                    m_scr[hf, sp * LANES:(sp + 1) * LANES, tp * LANES:(tp + 1) * LANES] = (
                        toep_ref[0, hf, tp - sp])
            row_g = group_of((PART_COLS, PART_STATE), 0, SSM_GROUP)
            col_h = group_of((PART_COLS, PART_STATE), 1, STATE)
            p_full = jnp.dot(inj_ref[0, hf], einj_ref[...], preferred_element_type=jnp.float32)
            p_scr[hf] = jnp.where(row_g == col_h, p_full, 0.0).astype(jnp.bfloat16)
            row_h = group_of((PART_COLS, PART_STATE), 0, SSM_GROUP)
            col_g = group_of((PART_COLS, PART_STATE), 1, STATE)
            q_t = jnp.dot(ero_ref[...], ro_ref[0, hf], preferred_element_type=jnp.float32)
            q_scr[hf] = jnp.where(row_h == col_g, q_t, 0.0).T.astype(jnp.bfloat16)

    pad = SCAN_PAD
    re, im = slice(0, half), slice(half, STATE_COLS)
    for buf in (z_scr, w_scr, hp_scr):
        buf[0:pad, :] = jnp.zeros((pad, STATE_COLS), jnp.float32)
    for hf in range(N_PARTS):
        z_h = jnp.dot(x_ref[0, :, hf * PART_COLS:(hf + 1) * PART_COLS], p_scr[hf],
                      preferred_element_type=jnp.float32)
        z_scr[pad:pad + n_rows, hf * quarter:(hf + 1) * quarter] = z_h[:, 0:quarter]
        z_scr[pad:pad + n_rows, half + hf * quarter:half + (hf + 1) * quarter] = z_h[:, quarter:]

    def coef(i):
        return a_ref[0, 2 * i:2 * i + 1, :], a_ref[0, 2 * i + 1:2 * i + 2, :]

    def doubling_pass(src, dst, shift, c_re, c_im):
        s_re = src[pad - shift:pad - shift + n_rows, re]
        s_im = src[pad - shift:pad - shift + n_rows, im]
        dst[pad:pad + n_rows, re] = src[pad:pad + n_rows, re] + c_re * s_re - c_im * s_im
        dst[pad:pad + n_rows, im] = src[pad:pad + n_rows, im] + c_re * s_im + c_im * s_re

    doubling_pass(z_scr, w_scr, 1, *coef(0))
    doubling_pass(w_scr, z_scr, 2, *coef(1))
    doubling_pass(z_scr, w_scr, 4, *coef(2))
    c8_re, c8_im = coef(3)

    def tile_step(m, h):
        h_re, h_im = h
        rows8 = pl.ds(pl.multiple_of(pad + 8 * m, 8), 8)
        n_re = w_scr[rows8, re] + c8_re * h_re - c8_im * h_im
        n_im = w_scr[rows8, im] + c8_re * h_im + c8_im * h_re
        hp_scr[rows8, re] = n_re
        hp_scr[rows8, im] = n_im
        return n_re, n_im

    zero = jnp.zeros((8, half), jnp.float32)
    lax.fori_loop(0, n_rows // 8, tile_step, (zero, zero), unroll=8)

    for hf in range(N_PARTS):
        base = hf * PART_COLS
        x_h = x_ref[0, :, base:base + PART_COLS]
        hp_h = jnp.concatenate(
            [hp_scr[pad - 1:pad - 1 + n_rows, hf * quarter:(hf + 1) * quarter],
             hp_scr[pad - 1:pad - 1 + n_rows, half + hf * quarter:half + (hf + 1) * quarter]],
            axis=1).astype(jnp.bfloat16)
        for n in range(PART_COLS // MXU_TILE):
            lo, hi = n * MXU_TILE, (n + 1) * MXU_TILE
            y = jnp.dot(x_h[:, 0:hi], m_scr[hf, 0:hi, lo:hi], preferred_element_type=jnp.float32)
            y = y + jnp.dot(hp_h, q_scr[hf, :, lo:hi], preferred_element_type=jnp.float32)
            y_ref[0, :, base + lo:base + hi] = _gelu_tanh(y).astype(jnp.bfloat16)


def _ssm_call(xc, toep, inj, ro, e_inj, e_ro, a_tab, bsz):
    nb, total_rows, _ = xc.shape
    n_rows = total_rows // bsz
    per_q = lambda q, b: (q, 0, 0)
    const = lambda q, b: (0, 0)
    return pl.pallas_call(
        _ssm_kernel,
        grid=(nb, bsz),
        in_specs=[
            pl.BlockSpec((1, n_rows, CHUNK_COLS), lambda q, b: (q, b, 0)),
            pl.BlockSpec((1, N_PARTS, SSM_CHUNK // STEPS_PER_TILE, LANES, LANES),
                         lambda q, b: (q, 0, 0, 0, 0)),
            pl.BlockSpec((1, N_PARTS, PART_COLS, 2 * STATE), lambda q, b: (q, 0, 0, 0)),
            pl.BlockSpec((1, N_PARTS, SSM_CHUNK * SSM_GROUP, PART_STATE), lambda q, b: (q, 0, 0, 0)),
            pl.BlockSpec(e_inj.shape, const),
            pl.BlockSpec(e_ro.shape, const),
            pl.BlockSpec((1, 2 * len(SCAN_POWERS), STATE_COLS // 2), per_q),
        ],
        out_specs=pl.BlockSpec((1, n_rows, CHUNK_COLS), lambda q, b: (q, b, 0)),
        out_shape=jax.ShapeDtypeStruct(xc.shape, jnp.bfloat16),
        scratch_shapes=[
            pltpu.VMEM((N_PARTS, PART_COLS, PART_COLS), jnp.bfloat16),
            pltpu.VMEM((N_PARTS, PART_COLS, PART_STATE), jnp.bfloat16),
            pltpu.VMEM((N_PARTS, PART_STATE, PART_COLS), jnp.bfloat16),
            pltpu.VMEM((SCAN_PAD + n_rows, STATE_COLS), jnp.float32),
            pltpu.VMEM((SCAN_PAD + n_rows, STATE_COLS), jnp.float32),
            pltpu.VMEM((SCAN_PAD + n_rows, STATE_COLS), jnp.float32),
        ],
        compiler_params=pltpu.CompilerParams(
            dimension_semantics=("arbitrary", "arbitrary"), vmem_limit_bytes=VMEM_LIMIT),
        name="ssm",
    )(xc, toep, inj, ro, e_inj, e_ro, a_tab)


def _merge_mlp_stages(x_ref, ys_ref, ya, gates_ref, wglu_ref, bglu_ref, wus_ref, wua_ref,
                      wout_ref, gm_ref, w1_ref, w2_ref, gf_ref, o_ref, y_scr):
    f32, bf = jnp.float32, jnp.bfloat16
    chunk_rows = ys_ref.shape[1]
    st = {}

    def mix():
        gated_a = gates_ref[:, D_MODEL:2 * D_MODEL].astype(f32) * jnp.dot(
            ya, wua_ref[...], preferred_element_type=f32)
        lane_part = _lane_part((chunk_rows, LANES))
        for m in range(SSM_CHUNK // STEPS_PER_TILE):
            for blk in range(N_LANE_BLOCKS):
                tiles = [ys_ref[blk, :, p * PART_COLS + m * LANES:p * PART_COLS + (m + 1) * LANES]
                         .astype(f32) for p in range(N_PARTS)]
                for a in range(STEPS_PER_TILE):
                    step = _merge_lane_parts(
                        [_roll_lanes(tiles[p], (p - a) * PART_LANES) for p in range(N_PARTS)], lane_part)
                    y_scr[blk, pl.ds(m * STEPS_PER_TILE + a, chunk_rows, stride=SSM_CHUNK), :] = step
        y = jnp.concatenate([y_scr[blk] for blk in range(N_LANE_BLOCKS)], axis=-1)
        glu_arg = jnp.dot(y.astype(bf), wglu_ref[...], preferred_element_type=f32) + bglu_ref[...]
        y_ssm = (y * jax.nn.sigmoid(glu_arg)).astype(bf)
        up_s = jnp.dot(y_ssm, wus_ref[...], preferred_element_type=f32)
        g_s = gates_ref[:, 0:D_MODEL].astype(f32)
        st["merged"] = (g_s * up_s + gated_a).astype(bf)

    def project_out():
        h = x_ref[...] + jnp.dot(st["merged"], wout_ref[...], preferred_element_type=f32)
        st["h"] = h
        st["n"] = _rmsnorm_f32(h, gm_ref[...]).astype(bf)

    def mlp_chunk(c):
        cols = slice(c * FF_CHUNK, (c + 1) * FF_CHUNK)
        hid = jnp.maximum(jnp.dot(st["n"], w1_ref[:, cols], preferred_element_type=f32), 0.0)
        st["h"] = st["h"] + jnp.dot((hid * hid).astype(bf), w2_ref[cols, :],
                                    preferred_element_type=f32)

    def finish():
        o_ref[...] = _rmsnorm_f32(st["h"], gf_ref[...])

    chunks = [functools.partial(mlp_chunk, c) for c in range(D_FF // FF_CHUNK)]
    return [mix, project_out] + chunks + [finish]


def _attn_kernel(q_ref, kc_ref, kp_ref, vc_ref, vp_ref, k_hbm, v_hbm, tri_ref, o_ref,
                 qm_scr, carry_scr, acc_scr, k_buf, v_buf, dma_sem):
    blk = ATT_BLOCK
    n_qb = q_ref.shape[1] // blk
    n_pairs = N_HEADS // 2
    head_rows = N_HEADS * blk
    f32, bf = jnp.float32, jnp.bfloat16
    batch = pl.program_id(0)
    first_qb = pl.program_id(1) * n_qb

    lane = lax.broadcasted_iota(jnp.int32, (blk, LANES), 1)
    first_head = lane < HEAD_DIM
    tri_near = tri_ref[...]
    tri = tri_ref[blk:2 * blk, blk:2 * blk + LANES]

    for c in range(n_qb):
        for pair in range(n_pairs):
            q_pair = q_ref[0, c * blk:(c + 1) * blk, pair * LANES:(pair + 1) * LANES]
            zero = jnp.zeros_like(q_pair)
            lo = c * head_rows + 2 * pair * blk
            qm_scr[lo:lo + blk] = jnp.where(first_head, q_pair, zero)
            qm_scr[lo + blk:lo + 2 * blk] = jnp.where(first_head, zero, q_pair)

    def scores(c):
        base = c * head_rows
        return jnp.concatenate(
            [lax.dot_general(qm_scr[base + 2 * p * blk:base + (2 * p + 2) * blk],
                             k_buf[:, p * LANES:(p + 1) * LANES],
                             NT_DIMS, preferred_element_type=f32) for p in range(n_pairs)], axis=0)

    def softplus2(z):
        return jnp.maximum(z, 0.0) + jnp.log2(1.0 + jnp.exp2(-jnp.abs(z)))

    def stick(z, later):
        sp = softplus2(z)
        cs = jnp.dot(sp.astype(bf), tri, preferred_element_type=f32)
        log_w = z - sp - cs[:, 0:blk]
        if later is not None:
            log_w = log_w - later
        return jnp.exp2(log_w).astype(bf), cs[:, blk:blk + LANES]

    def weighted_values(w, pair):
        o2 = jnp.dot(w, v_buf[:, pair * LANES:(pair + 1) * LANES],
                     preferred_element_type=f32)
        return jnp.where(first_head, o2[0:blk], o2[blk:2 * blk])

    row = lax.broadcasted_iota(jnp.int32, (blk, blk), 0)
    col = lax.broadcasted_iota(jnp.int32, (blk, blk), 1)
    pen_diag = jnp.where(col < row, 0.0, ATT_MASKED)

    def window(cur_ref, prev_ref, c, lanes):
        before = prev_ref[0, :, lanes] if c == 0 else cur_ref[0, (c - 1) * blk:c * blk, lanes]
        return jnp.concatenate([before, cur_ref[0, c * blk:(c + 1) * blk, lanes]], axis=0)

    def near_scores(c):
        base = c * head_rows
        return jnp.concatenate(
            [lax.dot_general(
                qm_scr[base + 2 * p * blk:base + (2 * p + 2) * blk],
                window(kc_ref, kp_ref, c, slice(p * LANES, (p + 1) * LANES)),
                NT_DIMS, preferred_element_type=f32) for p in range(n_pairs)], axis=0)

    def masked_near_scores(c):
        z = near_scores(c)
        z_prev = z[:, 0:blk]
        if c == 0:
            z_prev = z_prev - jnp.where(first_qb >= 1, 0.0, ATT_MASKED)
        z_diag = (z[:, blk:2 * blk].reshape(N_HEADS, blk, blk) - pen_diag[None]).reshape(head_rows, blk)
        return jnp.concatenate([z_prev, z_diag], axis=1)

    all_qbs = range(n_qb)
    least = [None] * n_qb

    def phase_scores(qbs):
        return jnp.concatenate([masked_near_scores(c) for c in qbs], axis=0)

    def phase_softplus(z):
        return softplus2(z)

    def phase_suffix(sp):
        return jnp.dot(sp.astype(bf), tri_near[:, 0:2 * blk], preferred_element_type=f32)

    def phase_weights(qbs, z, sp, cs):
        total = cs[:, 0:LANES] + sp[:, 0:LANES]
        carry_scr[qbs[0] * head_rows:(qbs[-1] + 1) * head_rows] = total
        for k, c in enumerate(qbs):
            least[c] = jnp.min(total[k * head_rows:(k + 1) * head_rows], axis=0, keepdims=True)
        return jnp.exp2(z - sp - cs).astype(bf)

    def phase_values(qbs, w):
        for k, c in enumerate(qbs):
            for pair in range(n_pairs):
                rows_w = slice(k * head_rows + 2 * pair * blk, k * head_rows + (2 * pair + 2) * blk)
                lanes = slice(pair * LANES, (pair + 1) * LANES)
                o2 = jnp.dot(w[rows_w], window(vc_ref, vp_ref, c, lanes),
                             preferred_element_type=f32)
                acc_scr[c * blk:(c + 1) * blk, lanes] = jnp.where(first_head, o2[0:blk], o2[blk:2 * blk])

    z_near = phase_scores(all_qbs)
    sp = phase_softplus(z_near)
    cs = phase_suffix(sp)
    phase_values(all_qbs, phase_weights(all_qbs, z_near, sp, cs))

    skip_at = ATT_SKIP_SUM * LOG2_E
    least_of_step = functools.reduce(jnp.minimum, least)[0, 0]

    def earlier_blocks(c):
        rows_c = slice(c * blk, (c + 1) * blk)
        base = c * head_rows

        def body(state):
            j, _ = state
            key_rows = pl.ds(pl.multiple_of(j * blk, blk), blk)
            k_copy = pltpu.make_async_copy(k_hbm.at[batch, key_rows], k_buf, dma_sem.at[0])
            v_copy = pltpu.make_async_copy(v_hbm.at[batch, key_rows], v_buf, dma_sem.at[1])
            k_copy.start()
            v_copy.start()
            k_copy.wait()
            v_copy.wait()
            carry = jnp.broadcast_to(carry_scr[base:base + head_rows, 0:1], (head_rows, LANES))
            w, sum_j = stick(scores(c), carry)
            carry = carry + sum_j
            carry_scr[base:base + head_rows] = carry
            for pair in range(n_pairs):
                acc_scr[rows_c, pair * LANES:(pair + 1) * LANES] += weighted_values(
                    w[2 * pair * blk:(2 * pair + 2) * blk], pair)
            return j - 1, jnp.min(carry)

        def cond(state):
            j, least_c = state
            return jnp.logical_and(j >= 0, least_c < skip_at)

        lax.while_loop(cond, body, (first_qb + (c - 2), least[c][0, 0]))

    @pl.when(least_of_step < skip_at)
    def _():
        for c in range(n_qb):
            earlier_blocks(c)

    o_ref[0] = acc_scr[...].astype(o_ref.dtype)


def _attn_call(q3, k3, v3, tri):
    bsz, seq, _ = q3.shape
    blk = ATT_BLOCK
    rows = ATT_ROWS
    n_qb = rows // blk
    tile = lambda b, i: (b, i, 0)
    block_before = lambda b, i: (b, jnp.maximum(i * n_qb - 1, 0), 0)
    return pl.pallas_call(
        _attn_kernel,
        grid=(bsz, seq // rows),
        in_specs=[
            pl.BlockSpec((1, rows, D_ATTN), tile),
            pl.BlockSpec((1, rows, D_ATTN), tile),
            pl.BlockSpec((1, blk, D_ATTN), block_before),
            pl.BlockSpec((1, rows, D_ATTN), tile),
            pl.BlockSpec((1, blk, D_ATTN), block_before),
            pl.BlockSpec(memory_space=pl.ANY),
            pl.BlockSpec(memory_space=pl.ANY),
            pl.BlockSpec(tri.shape, lambda b, i: (0, 0)),
        ],
        out_specs=pl.BlockSpec((1, rows, D_ATTN), tile),
        out_shape=jax.ShapeDtypeStruct((bsz, seq, D_ATTN), jnp.bfloat16),
        scratch_shapes=[
            pltpu.VMEM((n_qb * N_HEADS * blk, LANES), jnp.bfloat16),
            pltpu.VMEM((n_qb * N_HEADS * blk, LANES), jnp.float32),
            pltpu.VMEM((rows, D_ATTN), jnp.float32),
            pltpu.VMEM((blk, D_ATTN), jnp.bfloat16),
            pltpu.VMEM((blk, D_ATTN), jnp.bfloat16),
            pltpu.SemaphoreType.DMA((2,)),
        ],
        compiler_params=pltpu.CompilerParams(
            dimension_semantics=("arbitrary", "arbitrary"), vmem_limit_bytes=VMEM_LIMIT),
        name="attn",
    )(q3, k3, k3, v3, v3, k3, v3, tri)


def _merge_kernel(x_ref, ys_ref, ya_ref, gates_ref, wglu_ref, bglu_ref, wus_ref, wua_ref,
                  wout_ref, gm_ref, w1_ref, w2_ref, gf_ref, o_ref, y_scr):
    for stage in _merge_mlp_stages(x_ref, ys_ref, ya_ref[...], gates_ref, wglu_ref, bglu_ref,
                                   wus_ref, wua_ref, wout_ref, gm_ref, w1_ref, w2_ref, gf_ref,
                                   o_ref, y_scr):
        stage()


def _merge_call(x2, ys, ya, gates, wglu, bglu, wus, wua, wout, gm, w1, w2, gf):
    tokens = x2.shape[0]
    rows = MERGE_ROWS
    const = lambda i: (0, 0)

    def resident(arr):
        return pl.BlockSpec(arr.shape, const, pipeline_mode=pl.Buffered(1))

    return pl.pallas_call(
        _merge_kernel,
        grid=(tokens // rows,),
        in_specs=[
            pl.BlockSpec((rows, D_MODEL), lambda i: (i, 0)),
            pl.BlockSpec((N_LANE_BLOCKS, rows // SSM_CHUNK, CHUNK_COLS), lambda i: (0, i, 0)),
            pl.BlockSpec((rows, D_ATTN), lambda i: (i, 0)),
            pl.BlockSpec((rows, 2 * D_MODEL), lambda i: (i, 0)),
            resident(wglu), resident(bglu), resident(wus), resident(wua), resident(wout),
            resident(gm), resident(w1), resident(w2), resident(gf),
        ],
        out_specs=pl.BlockSpec((rows, D_MODEL), lambda i: (i, 0)),
        out_shape=jax.ShapeDtypeStruct((tokens, D_MODEL), jnp.float32),
        scratch_shapes=[pltpu.VMEM((N_LANE_BLOCKS, rows, LANES), jnp.float32)],
        compiler_params=pltpu.CompilerParams(
            dimension_semantics=("arbitrary",), vmem_limit_bytes=VMEM_LIMIT),
        name="merge_mlp",
    )(x2, ys, ya, gates, wglu, bglu, wus, wua, wout, gm, w1, w2, gf)


def _suffix_sum_matrix():
    win = 2 * ATT_BLOCK
    r = jnp.arange(win)[:, None]
    c = jnp.arange(win + LANES)[None, :]
    return jnp.where((c >= win) | (r > c), 1.0, 0.0).astype(jnp.bfloat16)


def kernel(x, norm_mix, w_in, A_re, A_im, log_dt, B_re, B_im, C_re, C_im, D_skip, w_glu, b_glu,
           w_up_ssm, w_up_attn, w_gate, b_gate, w_out, norm_mlp, w_ff1, w_ff2, norm_final):
    bsz, seq, _ = x.shape
    tokens = bsz * seq
    bf = jnp.bfloat16
    assert norm_mix.shape[0] == 1, "single layer"
    assert seq % (SSM_CHUNK * 8) == 0 and seq % ATT_BLOCK == 0
    assert tokens % PROJ_ROWS == 0 and tokens % MERGE_ROWS == 0 and seq % min(PROJ_ROWS, seq) == 0

    x2 = x.reshape(tokens, D_MODEL)
    ussm, q, k, v, gates = _inproj_call(x2, norm_mix, w_in[0].astype(bf), w_gate[0].astype(bf),
                                        b_gate, bsz, seq)

    toep, inj, ro, a_tab = _ssm_tables(
        A_re[0], A_im[0], log_dt[0], B_re[0], B_im[0], C_re[0], C_im[0], D_skip[0])
    e_inj, e_ro = _expansion_matrices()
    ys = _ssm_call(ussm, toep, inj, ro, e_inj, e_ro, a_tab, bsz)

    ya = _attn_call(q.reshape(bsz, seq, D_ATTN), k.reshape(bsz, seq, D_ATTN),
                    v.reshape(bsz, seq, D_ATTN), _suffix_sum_matrix()).reshape(tokens, D_ATTN)

    out = _merge_call(x2, ys, ya, gates, w_glu[0].astype(bf), b_glu, w_up_ssm[0].astype(bf),
                      w_up_attn[0].astype(bf), w_out[0].astype(bf), norm_mlp,
                      w_ff1[0].astype(bf), w_ff2[0].astype(bf), norm_final.reshape(1, D_MODEL))
    return out.reshape(bsz, seq, D_MODEL)
```

```python
import functools

import jax
import jax.numpy as jnp
from jax import lax
from jax.experimental import pallas as pl
from jax.experimental.pallas import tpu as pltpu

D_MODEL = 1024
D_SSM = 512
SSM_GROUP = 16
STATE = 64
N_HEADS = 8
HEAD_DIM = 64
D_ATTN = 512
D_FF = 4096
EPS = 1e-6

LANES = 128
MXU_TILE = 256
VMEM_LIMIT = 52 * 1024 * 1024

SSM_CHUNK = 16
N_LANE_BLOCKS = D_SSM // LANES
GROUPS_PER_BLOCK = LANES // SSM_GROUP
STATE_COLS = 2 * GROUPS_PER_BLOCK * STATE
CHUNK_COLS = SSM_CHUNK * LANES
N_PARTS = 4
PART_GROUPS = GROUPS_PER_BLOCK // N_PARTS
PART_LANES = LANES // N_PARTS
STEPS_PER_TILE = LANES // PART_LANES
PART_COLS = SSM_CHUNK * PART_LANES
PART_STATE = STATE_COLS // N_PARTS
SCAN_POWERS = (1, 2, 4, 8)
SCAN_PAD = 8
ATT_BLOCK = 128
ATT_SKIP_SUM = 64.0
ATT_MASKED = 1e30
LOG2_E = 1.4426950408889634
NT_DIMS = (((1,), (1,)), ((), ()))
GELU_C1 = 0.7978845608028654
GELU_C3 = GELU_C1 * 0.044715
PROJ_ROWS = 1024
MERGE_ROWS = 512
FF_CHUNK = 1024
ATT_ROWS = 1024


def _rmsnorm_f32(x, g):
    ms = jnp.mean(x * x, axis=-1, keepdims=True)
    return x * lax.rsqrt(ms + EPS) * g


def _lane_part(shape):
    lanes = lax.broadcasted_iota(jnp.int32, shape, len(shape) - 1)
    return lax.shift_right_logical(lanes, PART_LANES.bit_length() - 1)


def _roll_lanes(x, shift):
    shift %= LANES
    return x if shift == 0 else pltpu.roll(x, shift, x.ndim - 1)


def _merge_lane_parts(sources, lane_part):
    out = sources[0]
    for k in range(1, len(sources)):
        out = jnp.where(lane_part == k, sources[k], out)
    return out


def _inproj_kernel(x_ref, g_ref, w_ref, wg_ref, bg_ref,
                   ussm_ref, q_ref, k_ref, v_ref, gates_ref, pssm_scr):
    x = x_ref[...]
    u = _rmsnorm_f32(x, g_ref[...]).astype(jnp.bfloat16)
    rows = x.shape[0]
    chunk_rows = rows // SSM_CHUNK

    for c in range(2 * D_MODEL // 512):
        pg = jnp.dot(u, wg_ref[:, c * 512:(c + 1) * 512], preferred_element_type=jnp.float32)
        pg = pg + bg_ref[:, c * 512:(c + 1) * 512]
        gates_ref[:, c * 512:(c + 1) * 512] = jax.nn.sigmoid(pg).astype(jnp.bfloat16)

    p_ssm = jnp.dot(u, w_ref[:, 0:D_SSM], preferred_element_type=jnp.float32)
    for blk in range(N_LANE_BLOCKS):
        pssm_scr[blk] = p_ssm[:, blk * LANES:(blk + 1) * LANES]
    lane_part = _lane_part((chunk_rows, LANES))
    for m in range(SSM_CHUNK // STEPS_PER_TILE):
        for blk in range(N_LANE_BLOCKS):
            steps = [pssm_scr[blk, pl.ds(m * STEPS_PER_TILE + a, chunk_rows, stride=SSM_CHUNK), :]
                     for a in range(STEPS_PER_TILE)]
            for p in range(N_PARTS):
                tile = _merge_lane_parts(
                    [_roll_lanes(steps[a], (a - p) * PART_LANES) for a in range(STEPS_PER_TILE)],
                    lane_part)
                ussm_ref[blk, :, p * PART_COLS + m * LANES:p * PART_COLS + (m + 1) * LANES] = (
                    tile.astype(jnp.bfloat16))

    p_q = jnp.dot(u, w_ref[:, D_SSM:D_SSM + D_ATTN], preferred_element_type=jnp.float32)
    q_ref[...] = (p_q * (HEAD_DIM ** -0.5 * LOG2_E)).astype(jnp.bfloat16)

    p_k = jnp.dot(u, w_ref[:, D_SSM + D_ATTN:D_SSM + 2 * D_ATTN],
                  preferred_element_type=jnp.float32)
    k_ref[...] = p_k.astype(jnp.bfloat16)

    p_v = jnp.dot(u, w_ref[:, D_SSM + 2 * D_ATTN:D_SSM + 3 * D_ATTN],
                  preferred_element_type=jnp.float32)
    v_ref[...] = p_v.astype(jnp.bfloat16)


def _inproj_call(x2, g, w_in, w_gate, bg, bsz, seq):
    tokens = bsz * seq
    rows = min(PROJ_ROWS, seq)
    const = lambda i: (0, 0)
    return pl.pallas_call(
        _inproj_kernel,
        grid=(tokens // rows,),
        in_specs=[
            pl.BlockSpec((rows, D_MODEL), lambda i: (i, 0)),
            pl.BlockSpec((1, D_MODEL), const),
            pl.BlockSpec(w_in.shape, const),
            pl.BlockSpec(w_gate.shape, const),
            pl.BlockSpec((1, 2 * D_MODEL), const),
        ],
        out_specs=[
            pl.BlockSpec((N_LANE_BLOCKS, rows // SSM_CHUNK, CHUNK_COLS), lambda i: (0, i, 0)),
            pl.BlockSpec((rows, D_ATTN), lambda i: (i, 0)),
            pl.BlockSpec((rows, D_ATTN), lambda i: (i, 0)),
            pl.BlockSpec((rows, D_ATTN), lambda i: (i, 0)),
            pl.BlockSpec((rows, 2 * D_MODEL), lambda i: (i, 0)),
        ],
        out_shape=[
            jax.ShapeDtypeStruct((N_LANE_BLOCKS, tokens // SSM_CHUNK, CHUNK_COLS), jnp.bfloat16),
            jax.ShapeDtypeStruct((tokens, D_ATTN), jnp.bfloat16),
            jax.ShapeDtypeStruct((tokens, D_ATTN), jnp.bfloat16),
            jax.ShapeDtypeStruct((tokens, D_ATTN), jnp.bfloat16),
            jax.ShapeDtypeStruct((tokens, 2 * D_MODEL), jnp.bfloat16),
        ],
        scratch_shapes=[pltpu.VMEM((N_LANE_BLOCKS, rows, LANES), jnp.float32)],
        compiler_params=pltpu.CompilerParams(
            dimension_semantics=("arbitrary",), vmem_limit_bytes=VMEM_LIMIT),
        name="inproj",
    )(x2, g, w_in, w_gate, bg)


def _ssm_tables(A_re, A_im, log_dt, B_re, B_im, C_re, C_im, D_skip):
    f32 = jnp.float32
    L = SSM_CHUNK
    nb, gb = N_LANE_BLOCKS, GROUPS_PER_BLOCK
    ar, ai = A_re.astype(f32), A_im.astype(f32)
    dt = jnp.exp(log_dt.astype(f32))[:, None]
    tau = jnp.arange(L + 1, dtype=f32)[:, None, None]
    mag = jnp.exp(ar[None] * dt[None] * tau)
    ang = ai[None] * dt[None] * tau
    pw_re, pw_im = mag * jnp.cos(ang), mag * jnp.sin(ang)
    num_re, num_im = pw_re[1] - 1.0, pw_im[1]
    den = ar * ar + ai * ai
    cf_re = (num_re * ar + num_im * ai) / den
    cf_im = (num_im * ar - num_re * ai) / den
    br, bi = B_re.astype(f32), B_im.astype(f32)
    bb_re = cf_re[..., None] * br - cf_im[..., None] * bi
    bb_im = cf_re[..., None] * bi + cf_im[..., None] * br
    ab_re = pw_re[:L, :, :, None] * bb_re[None] - pw_im[:L, :, :, None] * bb_im[None]
    ab_im = pw_re[:L, :, :, None] * bb_im[None] + pw_im[:L, :, :, None] * bb_re[None]
    cr, ci = C_re.astype(f32), C_im.astype(f32)

    kern = jnp.einsum('gdp,tgpc->tgcd', cr, ab_re) - jnp.einsum('gdp,tgpc->tgcd', ci, ab_im)
    kern = kern.reshape(L, nb, LANES, SSM_GROUP).transpose(1, 0, 2, 3)
    lane_idx = jnp.arange(LANES)
    spread = (jnp.arange(SSM_GROUP)[:, None] == lane_idx[None, :] % SSM_GROUP).astype(f32)
    same_group = lane_idx[:, None] // SSM_GROUP == lane_idx[None, :] // SSM_GROUP
    toep = jnp.where(same_group, jnp.matmul(kern, spread), 0.0)
    d_diag = D_skip.astype(f32).reshape(nb, LANES)[:, :, None] * jnp.eye(LANES, dtype=f32)
    toep = toep.at[:, 0].add(d_diag)

    pw_cat = jnp.concatenate([pw_re[:L], pw_im[:L]], axis=-1)[::-1]
    pw_swp = jnp.concatenate([pw_im[:L], pw_re[:L]], axis=-1)[::-1]
    bbt_re, bbt_im = bb_re.transpose(0, 2, 1), bb_im.transpose(0, 2, 1)
    b_same = jnp.concatenate([bbt_re, bbt_re], axis=-1)
    b_cross = jnp.concatenate([-bbt_im, bbt_im], axis=-1)
    inj = pw_cat[:, :, None, :] * b_same[None] + pw_swp[:, :, None, :] * b_cross[None]
    inj = inj.reshape(L, nb, N_PARTS, PART_LANES, 2 * STATE).transpose(1, 2, 0, 3, 4)
    inj = inj.reshape(nb, N_PARTS, PART_COLS, 2 * STATE)

    n_gp = PART_STATE // 2
    crt = cr.transpose(1, 0, 2).reshape(SSM_GROUP, nb, N_PARTS, n_gp)
    cit = ci.transpose(1, 0, 2).reshape(SSM_GROUP, nb, N_PARTS, n_gp)
    pwr = pw_re[1:L + 1].reshape(L, 1, nb, N_PARTS, n_gp)
    pwi = pw_im[1:L + 1].reshape(L, 1, nb, N_PARTS, n_gp)
    ro = jnp.concatenate([crt[None] * pwr - cit[None] * pwi,
                          -(crt[None] * pwi + cit[None] * pwr)], axis=-1)
    ro = ro.transpose(2, 3, 0, 1, 4).reshape(nb, N_PARTS, L * SSM_GROUP, PART_STATE)

    n = STEPS_PER_TILE
    t_part = jnp.stack([toep[:, :, p * PART_LANES:(p + 1) * PART_LANES,
                             p * PART_LANES:(p + 1) * PART_LANES] for p in range(N_PARTS)], axis=1)
    t_pad = jnp.pad(t_part, ((0, 0), (0, 0), (n - 1, 0), (0, 0), (0, 0)))
    n_dt = L // n
    toep = jnp.concatenate(
        [jnp.concatenate([t_pad[:, :, (b - a + n - 1)::n][:, :, :n_dt] for b in range(n)], axis=-1)
         for a in range(n)], axis=-2)

    steps = (L * jnp.array(SCAN_POWERS, f32))[:, None, None]
    sc_mag = jnp.exp(ar[None] * dt[None] * steps)
    sc_ang = ai[None] * dt[None] * steps
    a_tab = jnp.stack([sc_mag * jnp.cos(sc_ang), sc_mag * jnp.sin(sc_ang)], axis=1)
    a_tab = a_tab.reshape(2 * len(SCAN_POWERS), nb, gb * STATE).transpose(1, 0, 2)
    bf = jnp.bfloat16
    return toep.astype(bf), inj.astype(bf), ro.astype(bf), a_tab


def _expansion_matrices():
    gh = PART_GROUPS
    src = jnp.arange(2 * STATE)[:, None]
    dst = jnp.arange(PART_STATE)[None, :]
    e_inj = (src // STATE == dst // (gh * STATE)) & (src % STATE == dst % STATE)
    src = jnp.arange(SSM_CHUNK * SSM_GROUP)[None, :]
    dst = jnp.arange(PART_COLS)[:, None]
    e_ro = (src // SSM_GROUP == dst // PART_LANES) & (src % SSM_GROUP == dst % SSM_GROUP)
    return e_inj.astype(jnp.bfloat16), e_ro.astype(jnp.bfloat16)


def _gelu_tanh(y):
    half_y = 0.5 * y
    return half_y + half_y * jnp.tanh(y * (GELU_C1 + GELU_C3 * (y * y)))


def _ssm_kernel(x_ref, toep_ref, inj_ref, ro_ref, einj_ref, ero_ref, a_ref, y_ref,
                m_scr, p_scr, q_scr, z_scr, w_scr, hp_scr):
    gh = PART_GROUPS
    half = STATE_COLS // 2
    part_re = PART_STATE // 2
    n_tiles = SSM_CHUNK // STEPS_PER_TILE
    n_rows = x_ref.shape[1]

    @pl.when(pl.program_id(1) == 0)
    def _():
        def group_of(shape, axis, width):
            idx = lax.broadcasted_iota(jnp.int32, shape, axis)
            return lax.shift_right_logical(idx, width.bit_length() - 1) & (gh - 1)

        m_scr[...] = jnp.zeros_like(m_scr)
        for part in range(N_PARTS):
            for sp in range(n_tiles):
                for tp in range(sp, n_tiles):
                    m_scr[part, sp * LANES:(sp + 1) * LANES, tp * LANES:(tp + 1) * LANES] = (
                        toep_ref[0, part, tp - sp])
            row_g = group_of((PART_COLS, PART_STATE), 0, SSM_GROUP)
            col_h = group_of((PART_COLS, PART_STATE), 1, STATE)
            p_full = jnp.dot(inj_ref[0, part], einj_ref[...], preferred_element_type=jnp.float32)
            p_scr[part] = jnp.where(row_g == col_h, p_full, 0.0).astype(jnp.bfloat16)
            row_h = group_of((PART_COLS, PART_STATE), 0, SSM_GROUP)
            col_g = group_of((PART_COLS, PART_STATE), 1, STATE)
            q_t = jnp.dot(ero_ref[...], ro_ref[0, part], preferred_element_type=jnp.float32)
            q_scr[part] = jnp.where(row_h == col_g, q_t, 0.0).T.astype(jnp.bfloat16)

    pad = SCAN_PAD
    re, im = slice(0, half), slice(half, STATE_COLS)
    for buf in (z_scr, w_scr, hp_scr):
        buf[0:pad, :] = jnp.zeros((pad, STATE_COLS), jnp.float32)
    for part in range(N_PARTS):
        z_h = jnp.dot(x_ref[0, :, part * PART_COLS:(part + 1) * PART_COLS], p_scr[part],
                      preferred_element_type=jnp.float32)
        z_scr[pad:pad + n_rows, part * part_re:(part + 1) * part_re] = z_h[:, 0:part_re]
        z_scr[pad:pad + n_rows, half + part * part_re:half + (part + 1) * part_re] = z_h[:, part_re:]

    def coef(i):
        return a_ref[0, 2 * i:2 * i + 1, :], a_ref[0, 2 * i + 1:2 * i + 2, :]

    def doubling_pass(src, dst, shift, c_re, c_im):
        s_re = src[pad - shift:pad - shift + n_rows, re]
        s_im = src[pad - shift:pad - shift + n_rows, im]
        dst[pad:pad + n_rows, re] = src[pad:pad + n_rows, re] + c_re * s_re - c_im * s_im
        dst[pad:pad + n_rows, im] = src[pad:pad + n_rows, im] + c_re * s_im + c_im * s_re

    doubling_pass(z_scr, w_scr, 1, *coef(0))
    doubling_pass(w_scr, z_scr, 2, *coef(1))
    doubling_pass(z_scr, w_scr, 4, *coef(2))
    c8_re, c8_im = coef(3)

    def tile_step(m, h):
        h_re, h_im = h
        rows8 = pl.ds(pl.multiple_of(pad + 8 * m, 8), 8)
        n_re = w_scr[rows8, re] + c8_re * h_re - c8_im * h_im
        n_im = w_scr[rows8, im] + c8_re * h_im + c8_im * h_re
        hp_scr[rows8, re] = n_re
        hp_scr[rows8, im] = n_im
        return n_re, n_im

    zero = jnp.zeros((8, half), jnp.float32)
    lax.fori_loop(0, n_rows // 8, tile_step, (zero, zero), unroll=8)

    for part in range(N_PARTS):
        base = part * PART_COLS
        x_h = x_ref[0, :, base:base + PART_COLS]
        hp_h = jnp.concatenate(
            [hp_scr[pad - 1:pad - 1 + n_rows, part * part_re:(part + 1) * part_re],
             hp_scr[pad - 1:pad - 1 + n_rows, half + part * part_re:half + (part + 1) * part_re]],
            axis=1).astype(jnp.bfloat16)
        for n in range(PART_COLS // MXU_TILE):
            lo, hi = n * MXU_TILE, (n + 1) * MXU_TILE
            y = jnp.dot(x_h[:, 0:hi], m_scr[part, 0:hi, lo:hi], preferred_element_type=jnp.float32)
            y = y + jnp.dot(hp_h, q_scr[part, :, lo:hi], preferred_element_type=jnp.float32)
            y_ref[0, :, base + lo:base + hi] = _gelu_tanh(y).astype(jnp.bfloat16)


def _ssm_call(xc, toep, inj, ro, e_inj, e_ro, a_tab, bsz):
    nb, total_rows, _ = xc.shape
    n_rows = total_rows // bsz
    per_q = lambda q, b: (q, 0, 0)
    const = lambda q, b: (0, 0)
    return pl.pallas_call(
        _ssm_kernel,
        grid=(nb, bsz),
        in_specs=[
            pl.BlockSpec((1, n_rows, CHUNK_COLS), lambda q, b: (q, b, 0)),
            pl.BlockSpec((1, N_PARTS, SSM_CHUNK // STEPS_PER_TILE, LANES, LANES),
                         lambda q, b: (q, 0, 0, 0, 0)),
            pl.BlockSpec((1, N_PARTS, PART_COLS, 2 * STATE), lambda q, b: (q, 0, 0, 0)),
            pl.BlockSpec((1, N_PARTS, SSM_CHUNK * SSM_GROUP, PART_STATE), lambda q, b: (q, 0, 0, 0)),
            pl.BlockSpec(e_inj.shape, const),
            pl.BlockSpec(e_ro.shape, const),
            pl.BlockSpec((1, 2 * len(SCAN_POWERS), STATE_COLS // 2), per_q),
        ],
        out_specs=pl.BlockSpec((1, n_rows, CHUNK_COLS), lambda q, b: (q, b, 0)),
        out_shape=jax.ShapeDtypeStruct(xc.shape, jnp.bfloat16),
        scratch_shapes=[
            pltpu.VMEM((N_PARTS, PART_COLS, PART_COLS), jnp.bfloat16),
            pltpu.VMEM((N_PARTS, PART_COLS, PART_STATE), jnp.bfloat16),
            pltpu.VMEM((N_PARTS, PART_STATE, PART_COLS), jnp.bfloat16),
            pltpu.VMEM((SCAN_PAD + n_rows, STATE_COLS), jnp.float32),
            pltpu.VMEM((SCAN_PAD + n_rows, STATE_COLS), jnp.float32),
            pltpu.VMEM((SCAN_PAD + n_rows, STATE_COLS), jnp.float32),
        ],
        compiler_params=pltpu.CompilerParams(
            dimension_semantics=("arbitrary", "arbitrary"), vmem_limit_bytes=VMEM_LIMIT),
        name="ssm",
    )(xc, toep, inj, ro, e_inj, e_ro, a_tab)


def _merge_mlp_stages(x_ref, ys_ref, ya, gates_ref, wglu_ref, bglu_ref, wus_ref, wua_ref,
                      wout_ref, gm_ref, w1_ref, w2_ref, gf_ref, o_ref, y_scr):
    f32, bf = jnp.float32, jnp.bfloat16
    chunk_rows = ys_ref.shape[1]
    st = {}

    def mix():
        gated_a = gates_ref[:, D_MODEL:2 * D_MODEL].astype(f32) * jnp.dot(
            ya, wua_ref[...], preferred_element_type=f32)
        lane_part = _lane_part((chunk_rows, LANES))
        for m in range(SSM_CHUNK // STEPS_PER_TILE):
            for blk in range(N_LANE_BLOCKS):
                tiles = [ys_ref[blk, :, p * PART_COLS + m * LANES:p * PART_COLS + (m + 1) * LANES]
                         .astype(f32) for p in range(N_PARTS)]
                for a in range(STEPS_PER_TILE):
                    step = _merge_lane_parts(
                        [_roll_lanes(tiles[p], (p - a) * PART_LANES) for p in range(N_PARTS)], lane_part)
                    y_scr[blk, pl.ds(m * STEPS_PER_TILE + a, chunk_rows, stride=SSM_CHUNK), :] = step
        y = jnp.concatenate([y_scr[blk] for blk in range(N_LANE_BLOCKS)], axis=-1)
        glu_arg = jnp.dot(y.astype(bf), wglu_ref[...], preferred_element_type=f32) + bglu_ref[...]
        y_ssm = (y * jax.nn.sigmoid(glu_arg)).astype(bf)
        up_s = jnp.dot(y_ssm, wus_ref[...], preferred_element_type=f32)
        g_s = gates_ref[:, 0:D_MODEL].astype(f32)
        st["merged"] = (g_s * up_s + gated_a).astype(bf)

    def project_out():
        h = x_ref[...] + jnp.dot(st["merged"], wout_ref[...], preferred_element_type=f32)
        st["h"] = h
        st["n"] = _rmsnorm_f32(h, gm_ref[...]).astype(bf)

    def mlp_chunk(c):
        cols = slice(c * FF_CHUNK, (c + 1) * FF_CHUNK)
        hid = jnp.maximum(jnp.dot(st["n"], w1_ref[:, cols], preferred_element_type=f32), 0.0)
        st["h"] = st["h"] + jnp.dot((hid * hid).astype(bf), w2_ref[cols, :],
                                    preferred_element_type=f32)

    def finish():
        o_ref[...] = _rmsnorm_f32(st["h"], gf_ref[...])

    chunks = [functools.partial(mlp_chunk, c) for c in range(D_FF // FF_CHUNK)]
    return [mix, project_out] + chunks + [finish]


def _attn_kernel(q_ref, kc_ref, kp_ref, vc_ref, vp_ref, k_hbm, v_hbm, tri_ref, o_ref,
                 qm_scr, carry_scr, acc_scr, k_buf, v_buf, dma_sem):
    blk = ATT_BLOCK
    n_qb = q_ref.shape[1] // blk
    n_pairs = N_HEADS // 2
    head_rows = N_HEADS * blk
    f32, bf = jnp.float32, jnp.bfloat16
    batch = pl.program_id(0)
    first_qb = pl.program_id(1) * n_qb

    lane = lax.broadcasted_iota(jnp.int32, (blk, LANES), 1)
    first_head = lane < HEAD_DIM
    tri_near = tri_ref[...]
    tri = tri_ref[blk:2 * blk, blk:2 * blk + LANES]

    for c in range(n_qb):
        for pair in range(n_pairs):
            q_pair = q_ref[0, c * blk:(c + 1) * blk, pair * LANES:(pair + 1) * LANES]
            zero = jnp.zeros_like(q_pair)
            lo = c * head_rows + 2 * pair * blk
            qm_scr[lo:lo + blk] = jnp.where(first_head, q_pair, zero)
            qm_scr[lo + blk:lo + 2 * blk] = jnp.where(first_head, zero, q_pair)

    def scores(c):
        base = c * head_rows
        return jnp.concatenate(
            [lax.dot_general(qm_scr[base + 2 * p * blk:base + (2 * p + 2) * blk],
                             k_buf[:, p * LANES:(p + 1) * LANES],
                             NT_DIMS, preferred_element_type=f32) for p in range(n_pairs)], axis=0)

    def softplus2(z):
        return jnp.maximum(z, 0.0) + jnp.log2(1.0 + jnp.exp2(-jnp.abs(z)))

    def stick(z, later):
        sp = softplus2(z)
        cs = jnp.dot(sp.astype(bf), tri, preferred_element_type=f32)
        log_w = z - sp - cs[:, 0:blk]
        if later is not None:
            log_w = log_w - later
        return jnp.exp2(log_w).astype(bf), cs[:, blk:blk + LANES]

    def weighted_values(w, pair):
        o2 = jnp.dot(w, v_buf[:, pair * LANES:(pair + 1) * LANES],
                     preferred_element_type=f32)
        return jnp.where(first_head, o2[0:blk], o2[blk:2 * blk])

    row = lax.broadcasted_iota(jnp.int32, (blk, blk), 0)
    col = lax.broadcasted_iota(jnp.int32, (blk, blk), 1)
    pen_diag = jnp.where(col < row, 0.0, ATT_MASKED)

    def window(cur_ref, prev_ref, c, lanes):
        before = prev_ref[0, :, lanes] if c == 0 else cur_ref[0, (c - 1) * blk:c * blk, lanes]
        return jnp.concatenate([before, cur_ref[0, c * blk:(c + 1) * blk, lanes]], axis=0)

    def near_scores(c):
        base = c * head_rows
        return jnp.concatenate(
            [lax.dot_general(
                qm_scr[base + 2 * p * blk:base + (2 * p + 2) * blk],
                window(kc_ref, kp_ref, c, slice(p * LANES, (p + 1) * LANES)),
                NT_DIMS, preferred_element_type=f32) for p in range(n_pairs)], axis=0)

    def masked_near_scores(c):
        z = near_scores(c)
        z_prev = z[:, 0:blk]
        if c == 0:
            z_prev = z_prev - jnp.where(first_qb >= 1, 0.0, ATT_MASKED)
        z_diag = (z[:, blk:2 * blk].reshape(N_HEADS, blk, blk) - pen_diag[None]).reshape(head_rows, blk)
        return jnp.concatenate([z_prev, z_diag], axis=1)

    all_qbs = range(n_qb)
    least = [None] * n_qb

    def phase_scores(qbs):
        return jnp.concatenate([masked_near_scores(c) for c in qbs], axis=0)

    def phase_softplus(z):
        return softplus2(z)

    def phase_suffix(sp):
        return jnp.dot(sp.astype(bf), tri_near[:, 0:2 * blk], preferred_element_type=f32)

    def phase_weights(qbs, z, sp, cs):
        total = cs[:, 0:LANES] + sp[:, 0:LANES]
        carry_scr[qbs[0] * head_rows:(qbs[-1] + 1) * head_rows] = total
        for k, c in enumerate(qbs):
            least[c] = jnp.min(total[k * head_rows:(k + 1) * head_rows], axis=0, keepdims=True)
        return jnp.exp2(z - sp - cs).astype(bf)

    def phase_values(qbs, w):
        for k, c in enumerate(qbs):
            for pair in range(n_pairs):
                rows_w = slice(k * head_rows + 2 * pair * blk, k * head_rows + (2 * pair + 2) * blk)
                lanes = slice(pair * LANES, (pair + 1) * LANES)
                o2 = jnp.dot(w[rows_w], window(vc_ref, vp_ref, c, lanes),
                             preferred_element_type=f32)
                acc_scr[c * blk:(c + 1) * blk, lanes] = jnp.where(first_head, o2[0:blk], o2[blk:2 * blk])

    z_near = phase_scores(all_qbs)
    sp = phase_softplus(z_near)
    cs = phase_suffix(sp)
    phase_values(all_qbs, phase_weights(all_qbs, z_near, sp, cs))

    skip_at = ATT_SKIP_SUM * LOG2_E
    least_of_step = functools.reduce(jnp.minimum, least)[0, 0]

    def earlier_blocks(c):
        rows_c = slice(c * blk, (c + 1) * blk)
        base = c * head_rows

        def body(state):
            j, _ = state
            key_rows = pl.ds(pl.multiple_of(j * blk, blk), blk)
            k_copy = pltpu.make_async_copy(k_hbm.at[batch, key_rows], k_buf, dma_sem.at[0])
            v_copy = pltpu.make_async_copy(v_hbm.at[batch, key_rows], v_buf, dma_sem.at[1])
            k_copy.start()
            v_copy.start()
            k_copy.wait()
            v_copy.wait()
            carry = jnp.broadcast_to(carry_scr[base:base + head_rows, 0:1], (head_rows, LANES))
            w, sum_j = stick(scores(c), carry)
            carry = carry + sum_j
            carry_scr[base:base + head_rows] = carry
            for pair in range(n_pairs):
                acc_scr[rows_c, pair * LANES:(pair + 1) * LANES] += weighted_values(
                    w[2 * pair * blk:(2 * pair + 2) * blk], pair)
            return j - 1, jnp.min(carry)

        def cond(state):
            j, least_c = state
            return jnp.logical_and(j >= 0, least_c < skip_at)

        lax.while_loop(cond, body, (first_qb + (c - 2), least[c][0, 0]))

    @pl.when(least_of_step < skip_at)
    def _():
        for c in range(n_qb):
            earlier_blocks(c)

    o_ref[0] = acc_scr[...].astype(o_ref.dtype)


def _attn_call(q3, k3, v3, tri):
    bsz, seq, _ = q3.shape
    blk = ATT_BLOCK
    rows = ATT_ROWS
    n_qb = rows // blk
    tile = lambda b, i: (b, i, 0)
    block_before = lambda b, i: (b, jnp.maximum(i * n_qb - 1, 0), 0)
    return pl.pallas_call(
        _attn_kernel,
        grid=(bsz, seq // rows),
        in_specs=[
            pl.BlockSpec((1, rows, D_ATTN), tile),
            pl.BlockSpec((1, rows, D_ATTN), tile),
            pl.BlockSpec((1, blk, D_ATTN), block_before),
            pl.BlockSpec((1, rows, D_ATTN), tile),
            pl.BlockSpec((1, blk, D_ATTN), block_before),
            pl.BlockSpec(memory_space=pl.ANY),
            pl.BlockSpec(memory_space=pl.ANY),
            pl.BlockSpec(tri.shape, lambda b, i: (0, 0)),
        ],
        out_specs=pl.BlockSpec((1, rows, D_ATTN), tile),
        out_shape=jax.ShapeDtypeStruct((bsz, seq, D_ATTN), jnp.bfloat16),
        scratch_shapes=[
            pltpu.VMEM((n_qb * N_HEADS * blk, LANES), jnp.bfloat16),
            pltpu.VMEM((n_qb * N_HEADS * blk, LANES), jnp.float32),
            pltpu.VMEM((rows, D_ATTN), jnp.float32),
            pltpu.VMEM((blk, D_ATTN), jnp.bfloat16),
            pltpu.VMEM((blk, D_ATTN), jnp.bfloat16),
            pltpu.SemaphoreType.DMA((2,)),
        ],
        compiler_params=pltpu.CompilerParams(
            dimension_semantics=("arbitrary", "arbitrary"), vmem_limit_bytes=VMEM_LIMIT),
        name="attn",
    )(q3, k3, k3, v3, v3, k3, v3, tri)


def _merge_kernel(x_ref, ys_ref, ya_ref, gates_ref, wglu_ref, bglu_ref, wus_ref, wua_ref,
                  wout_ref, gm_ref, w1_ref, w2_ref, gf_ref, o_ref, y_scr):
    for stage in _merge_mlp_stages(x_ref, ys_ref, ya_ref[...], gates_ref, wglu_ref, bglu_ref,
                                   wus_ref, wua_ref, wout_ref, gm_ref, w1_ref, w2_ref, gf_ref,
                                   o_ref, y_scr):
        stage()


def _merge_call(x2, ys, ya, gates, wglu, bglu, wus, wua, wout, gm, w1, w2, gf):
    tokens = x2.shape[0]
    rows = MERGE_ROWS
    const = lambda i: (0, 0)

    def resident(arr):
        return pl.BlockSpec(arr.shape, const, pipeline_mode=pl.Buffered(1))

    return pl.pallas_call(
        _merge_kernel,
        grid=(tokens // rows,),
        in_specs=[
            pl.BlockSpec((rows, D_MODEL), lambda i: (i, 0)),
            pl.BlockSpec((N_LANE_BLOCKS, rows // SSM_CHUNK, CHUNK_COLS), lambda i: (0, i, 0)),
            pl.BlockSpec((rows, D_ATTN), lambda i: (i, 0)),
            pl.BlockSpec((rows, 2 * D_MODEL), lambda i: (i, 0)),
            resident(wglu), resident(bglu), resident(wus), resident(wua), resident(wout),
            resident(gm), resident(w1), resident(w2), resident(gf),
        ],
        out_specs=pl.BlockSpec((rows, D_MODEL), lambda i: (i, 0)),
        out_shape=jax.ShapeDtypeStruct((tokens, D_MODEL), jnp.float32),
        scratch_shapes=[pltpu.VMEM((N_LANE_BLOCKS, rows, LANES), jnp.float32)],
        compiler_params=pltpu.CompilerParams(
            dimension_semantics=("arbitrary",), vmem_limit_bytes=VMEM_LIMIT),
        name="merge_mlp",
    )(x2, ys, ya, gates, wglu, bglu, wus, wua, wout, gm, w1, w2, gf)


def _suffix_sum_matrix():
    win = 2 * ATT_BLOCK
    r = jnp.arange(win)[:, None]
    c = jnp.arange(win + LANES)[None, :]
    return jnp.where((c >= win) | (r > c), 1.0, 0.0).astype(jnp.bfloat16)


def kernel(x, norm_mix, w_in, A_re, A_im, log_dt, B_re, B_im, C_re, C_im, D_skip, w_glu, b_glu,
           w_up_ssm, w_up_attn, w_gate, b_gate, w_out, norm_mlp, w_ff1, w_ff2, norm_final):
    bsz, seq, _ = x.shape
    tokens = bsz * seq
    bf = jnp.bfloat16
    assert norm_mix.shape[0] == 1, "single layer"
    assert seq % (SSM_CHUNK * 8) == 0 and seq % ATT_BLOCK == 0
    assert tokens % PROJ_ROWS == 0 and tokens % MERGE_ROWS == 0 and seq % min(PROJ_ROWS, seq) == 0

    x2 = x.reshape(tokens, D_MODEL)
    ussm, q, k, v, gates = _inproj_call(x2, norm_mix, w_in[0].astype(bf), w_gate[0].astype(bf),
                                        b_gate, bsz, seq)

    toep, inj, ro, a_tab = _ssm_tables(
        A_re[0], A_im[0], log_dt[0], B_re[0], B_im[0], C_re[0], C_im[0], D_skip[0])
    e_inj, e_ro = _expansion_matrices()
    ys = _ssm_call(ussm, toep, inj, ro, e_inj, e_ro, a_tab, bsz)

    ya = _attn_call(q.reshape(bsz, seq, D_ATTN), k.reshape(bsz, seq, D_ATTN),
                    v.reshape(bsz, seq, D_ATTN), _suffix_sum_matrix()).reshape(tokens, D_ATTN)

    out = _merge_call(x2, ys, ya, gates, w_glu[0].astype(bf), b_glu, w_up_ssm[0].astype(bf),
                      w_up_attn[0].astype(bf), w_out[0].astype(bf), norm_mlp,
                      w_ff1[0].astype(bf), w_ff2[0].astype(bf), norm_final.reshape(1, D_MODEL))
    return out.reshape(bsz, seq, D_MODEL)
```

```python
import functools

import jax
import jax.numpy as jnp
from jax import lax
from jax.experimental import pallas as pl
from jax.experimental.pallas import tpu as pltpu

D_MODEL = 1024
D_SSM = 512
SSM_GROUP = 16
STATE = 64
N_HEADS = 8
HEAD_DIM = 64
D_ATTN = 512
D_FF = 4096
EPS = 1e-6

LANES = 128
MXU_TILE = 256
VMEM_LIMIT = 52 * 1024 * 1024

SSM_CHUNK = 16
N_LANE_BLOCKS = D_SSM // LANES
GROUPS_PER_BLOCK = LANES // SSM_GROUP
STATE_COLS = 2 * GROUPS_PER_BLOCK * STATE
CHUNK_COLS = SSM_CHUNK * LANES
N_PARTS = 4
PART_GROUPS = GROUPS_PER_BLOCK // N_PARTS
PART_LANES = LANES // N_PARTS
STEPS_PER_TILE = LANES // PART_LANES
PART_COLS = SSM_CHUNK * PART_LANES
PART_STATE = STATE_COLS // N_PARTS
SCAN_POWERS = (1, 2, 4, 8)
SCAN_PAD = 8
ATT_BLOCK = 128
ATT_SKIP_SUM = 64.0
ATT_MASKED = 1e30
LOG2_E = 1.4426950408889634
NT_DIMS = (((1,), (1,)), ((), ()))
GELU_C1 = 0.7978845608028654
GELU_C3 = GELU_C1 * 0.044715
PROJ_ROWS = 1024
MERGE_ROWS = 512
FF_CHUNK = 1024
ATT_ROWS = 1024


def _rmsnorm_f32(x, g):
    ms = jnp.mean(x * x, axis=-1, keepdims=True)
    return x * lax.rsqrt(ms + EPS) * g


def _lane_part(shape):
    lanes = lax.broadcasted_iota(jnp.int32, shape, len(shape) - 1)
    return lax.shift_right_logical(lanes, PART_LANES.bit_length() - 1)


def _roll_lanes(x, shift):
    shift %= LANES
    return x if shift == 0 else pltpu.roll(x, shift, x.ndim - 1)


def _merge_lane_parts(sources, lane_part):
    out = sources[0]
    for k in range(1, len(sources)):
        out = jnp.where(lane_part == k, sources[k], out)
    return out


def _inproj_kernel(x_ref, g_ref, w_ref, wg_ref, bg_ref, *rest):
    n_cast = (len(rest) - 6) // 2
    cast_srcs = rest[:n_cast]
    ussm_ref, q_ref, k_ref, v_ref, gates_ref = rest[n_cast:n_cast + 5]
    cast_dsts = rest[n_cast + 5:2 * n_cast + 5]
    pssm_scr = rest[-1]
    for src, dst in zip(cast_srcs, cast_dsts):
        dst[...] = src[...].astype(jnp.bfloat16)

    x = x_ref[...]
    u = _rmsnorm_f32(x, g_ref[...]).astype(jnp.bfloat16)
    rows = x.shape[0]
    chunk_rows = rows // SSM_CHUNK

    for c in range(2 * D_MODEL // 512):
        pg = jnp.dot(u, wg_ref[:, c * 512:(c + 1) * 512], preferred_element_type=jnp.float32)
        pg = pg + bg_ref[:, c * 512:(c + 1) * 512]
        gates_ref[:, c * 512:(c + 1) * 512] = jax.nn.sigmoid(pg).astype(jnp.bfloat16)

    p_ssm = jnp.dot(u, w_ref[:, 0:D_SSM], preferred_element_type=jnp.float32)
    for blk in range(N_LANE_BLOCKS):
        pssm_scr[blk] = p_ssm[:, blk * LANES:(blk + 1) * LANES]
    lane_part = _lane_part((chunk_rows, LANES))
    for m in range(SSM_CHUNK // STEPS_PER_TILE):
        for blk in range(N_LANE_BLOCKS):
            steps = [pssm_scr[blk, pl.ds(m * STEPS_PER_TILE + a, chunk_rows, stride=SSM_CHUNK), :]
                     for a in range(STEPS_PER_TILE)]
            for p in range(N_PARTS):
                tile = _merge_lane_parts(
                    [_roll_lanes(steps[a], (a - p) * PART_LANES) for a in range(STEPS_PER_TILE)],
                    lane_part)
                ussm_ref[blk, :, p * PART_COLS + m * LANES:p * PART_COLS + (m + 1) * LANES] = (
                    tile.astype(jnp.bfloat16))

    p_q = jnp.dot(u, w_ref[:, D_SSM:D_SSM + D_ATTN], preferred_element_type=jnp.float32)
    q_ref[...] = (p_q * (HEAD_DIM ** -0.5 * LOG2_E)).astype(jnp.bfloat16)

    p_k = jnp.dot(u, w_ref[:, D_SSM + D_ATTN:D_SSM + 2 * D_ATTN],
                  preferred_element_type=jnp.float32)
    k_ref[...] = p_k.astype(jnp.bfloat16)

    p_v = jnp.dot(u, w_ref[:, D_SSM + 2 * D_ATTN:D_SSM + 3 * D_ATTN],
                  preferred_element_type=jnp.float32)
    v_ref[...] = p_v.astype(jnp.bfloat16)


def _inproj_call(x2, g, w_in, w_gate, bg, later_weights, bsz, seq):
    tokens = bsz * seq
    rows = min(PROJ_ROWS, seq)
    n_steps = tokens // rows
    const = lambda i: (0, 0)
    row_slice = lambda i: (i, 0)
    slices = []
    for w in later_weights:
        slice_rows = w.shape[0] // n_steps
        assert slice_rows * n_steps == w.shape[0] and slice_rows % 16 == 0, w.shape
        slices.append((slice_rows, w.shape[1]))
    return pl.pallas_call(
        _inproj_kernel,
        grid=(n_steps,),
        in_specs=[
            pl.BlockSpec((rows, D_MODEL), lambda i: (i, 0)),
            pl.BlockSpec((1, D_MODEL), const),
            pl.BlockSpec(w_in.shape, const),
            pl.BlockSpec(w_gate.shape, const),
            pl.BlockSpec((1, 2 * D_MODEL), const),
        ] + [pl.BlockSpec(blk, row_slice) for blk in slices],
        out_specs=[
            pl.BlockSpec((N_LANE_BLOCKS, rows // SSM_CHUNK, CHUNK_COLS), lambda i: (0, i, 0)),
            pl.BlockSpec((rows, D_ATTN), lambda i: (i, 0)),
            pl.BlockSpec((rows, D_ATTN), lambda i: (i, 0)),
            pl.BlockSpec((rows, D_ATTN), lambda i: (i, 0)),
            pl.BlockSpec((rows, 2 * D_MODEL), lambda i: (i, 0)),
        ] + [pl.BlockSpec(blk, row_slice) for blk in slices],
        out_shape=[
            jax.ShapeDtypeStruct((N_LANE_BLOCKS, tokens // SSM_CHUNK, CHUNK_COLS), jnp.bfloat16),
            jax.ShapeDtypeStruct((tokens, D_ATTN), jnp.bfloat16),
            jax.ShapeDtypeStruct((tokens, D_ATTN), jnp.bfloat16),
            jax.ShapeDtypeStruct((tokens, D_ATTN), jnp.bfloat16),
            jax.ShapeDtypeStruct((tokens, 2 * D_MODEL), jnp.bfloat16),
        ] + [jax.ShapeDtypeStruct(w.shape, jnp.bfloat16) for w in later_weights],
        scratch_shapes=[pltpu.VMEM((N_LANE_BLOCKS, rows, LANES), jnp.float32)],
        compiler_params=pltpu.CompilerParams(
            dimension_semantics=("arbitrary",), vmem_limit_bytes=VMEM_LIMIT),
        name="inproj",
    )(x2, g, w_in, w_gate, bg, *later_weights)


def _ssm_tables(A_re, A_im, log_dt, B_re, B_im, C_re, C_im, D_skip):
    f32 = jnp.float32
    L = SSM_CHUNK
    nb, gb = N_LANE_BLOCKS, GROUPS_PER_BLOCK
    ar, ai = A_re.astype(f32), A_im.astype(f32)
    dt = jnp.exp(log_dt.astype(f32))[:, None]
    tau = jnp.arange(L + 1, dtype=f32)[:, None, None]
    mag = jnp.exp(ar[None] * dt[None] * tau)
    ang = ai[None] * dt[None] * tau
    pw_re, pw_im = mag * jnp.cos(ang), mag * jnp.sin(ang)
    num_re, num_im = pw_re[1] - 1.0, pw_im[1]
    den = ar * ar + ai * ai
    cf_re = (num_re * ar + num_im * ai) / den
    cf_im = (num_im * ar - num_re * ai) / den
    br, bi = B_re.astype(f32), B_im.astype(f32)
    bb_re = cf_re[..., None] * br - cf_im[..., None] * bi
    bb_im = cf_re[..., None] * bi + cf_im[..., None] * br
    ab_re = pw_re[:L, :, :, None] * bb_re[None] - pw_im[:L, :, :, None] * bb_im[None]
    ab_im = pw_re[:L, :, :, None] * bb_im[None] + pw_im[:L, :, :, None] * bb_re[None]
    cr, ci = C_re.astype(f32), C_im.astype(f32)

    kern = jnp.einsum('gdp,tgpc->tgcd', cr, ab_re) - jnp.einsum('gdp,tgpc->tgcd', ci, ab_im)
    kern = kern.reshape(L, nb, LANES, SSM_GROUP).transpose(1, 0, 2, 3)
    lane_idx = jnp.arange(LANES)
    spread = (jnp.arange(SSM_GROUP)[:, None] == lane_idx[None, :] % SSM_GROUP).astype(f32)
    same_group = lane_idx[:, None] // SSM_GROUP == lane_idx[None, :] // SSM_GROUP
    toep = jnp.where(same_group, jnp.matmul(kern, spread), 0.0)
    d_diag = D_skip.astype(f32).reshape(nb, LANES)[:, :, None] * jnp.eye(LANES, dtype=f32)
    toep = toep.at[:, 0].add(d_diag)

    pw_cat = jnp.concatenate([pw_re[:L], pw_im[:L]], axis=-1)[::-1]
    pw_swp = jnp.concatenate([pw_im[:L], pw_re[:L]], axis=-1)[::-1]
    bbt_re, bbt_im = bb_re.transpose(0, 2, 1), bb_im.transpose(0, 2, 1)
    b_same = jnp.concatenate([bbt_re, bbt_re], axis=-1)
    b_cross = jnp.concatenate([-bbt_im, bbt_im], axis=-1)
    inj = pw_cat[:, :, None, :] * b_same[None] + pw_swp[:, :, None, :] * b_cross[None]
    inj = inj.reshape(L, nb, N_PARTS, PART_LANES, 2 * STATE).transpose(1, 2, 0, 3, 4)
    inj = inj.reshape(nb, N_PARTS, PART_COLS, 2 * STATE)

    n_gp = PART_STATE // 2
    crt = cr.transpose(1, 0, 2).reshape(SSM_GROUP, nb, N_PARTS, n_gp)
    cit = ci.transpose(1, 0, 2).reshape(SSM_GROUP, nb, N_PARTS, n_gp)
    pwr = pw_re[1:L + 1].reshape(L, 1, nb, N_PARTS, n_gp)
    pwi = pw_im[1:L + 1].reshape(L, 1, nb, N_PARTS, n_gp)
    ro = jnp.concatenate([crt[None] * pwr - cit[None] * pwi,
                          -(crt[None] * pwi + cit[None] * pwr)], axis=-1)
    ro = ro.transpose(2, 3, 0, 1, 4).reshape(nb, N_PARTS, L * SSM_GROUP, PART_STATE)

    n = STEPS_PER_TILE
    t_part = jnp.stack([toep[:, :, p * PART_LANES:(p + 1) * PART_LANES,
                             p * PART_LANES:(p + 1) * PART_LANES] for p in range(N_PARTS)], axis=1)
    t_pad = jnp.pad(t_part, ((0, 0), (0, 0), (n - 1, 0), (0, 0), (0, 0)))
    n_dt = L // n
    toep = jnp.concatenate(
        [jnp.concatenate([t_pad[:, :, (b - a + n - 1)::n][:, :, :n_dt] for b in range(n)], axis=-1)
         for a in range(n)], axis=-2)

    steps = (L * jnp.array(SCAN_POWERS, f32))[:, None, None]
    sc_mag = jnp.exp(ar[None] * dt[None] * steps)
    sc_ang = ai[None] * dt[None] * steps
    a_tab = jnp.stack([sc_mag * jnp.cos(sc_ang), sc_mag * jnp.sin(sc_ang)], axis=1)
    a_tab = a_tab.reshape(2 * len(SCAN_POWERS), nb, gb * STATE).transpose(1, 0, 2)
    bf = jnp.bfloat16
    return toep.astype(bf), inj.astype(bf), ro.astype(bf), a_tab


def _expansion_matrices():
    gh = PART_GROUPS
    src = jnp.arange(2 * STATE)[:, None]
    dst = jnp.arange(PART_STATE)[None, :]
    e_inj = (src // STATE == dst // (gh * STATE)) & (src % STATE == dst % STATE)
    src = jnp.arange(SSM_CHUNK * SSM_GROUP)[None, :]
    dst = jnp.arange(PART_COLS)[:, None]
    e_ro = (src // SSM_GROUP == dst // PART_LANES) & (src % SSM_GROUP == dst % SSM_GROUP)
    return e_inj.astype(jnp.bfloat16), e_ro.astype(jnp.bfloat16)


def _gelu_tanh(y):
    half_y = 0.5 * y
    return half_y + half_y * jnp.tanh(y * (GELU_C1 + GELU_C3 * (y * y)))


def _ssm_kernel(x_ref, toep_ref, inj_ref, ro_ref, einj_ref, ero_ref, a_ref, y_ref,
                m_scr, p_scr, q_scr, z_scr, w_scr, hp_scr):
    gh = PART_GROUPS
    half = STATE_COLS // 2
    part_re = PART_STATE // 2
    n_tiles = SSM_CHUNK // STEPS_PER_TILE
    n_rows = x_ref.shape[1]

    @pl.when(pl.program_id(1) == 0)
    def _():
        def group_of(shape, axis, width):
            idx = lax.broadcasted_iota(jnp.int32, shape, axis)
            return lax.shift_right_logical(idx, width.bit_length() - 1) & (gh - 1)

        m_scr[...] = jnp.zeros_like(m_scr)
        for part in range(N_PARTS):
            for sp in range(n_tiles):
                for tp in range(sp, n_tiles):
                    m_scr[part, sp * LANES:(sp + 1) * LANES, tp * LANES:(tp + 1) * LANES] = (
                        toep_ref[0, part, tp - sp])
            row_g = group_of((PART_COLS, PART_STATE), 0, SSM_GROUP)
            col_h = group_of((PART_COLS, PART_STATE), 1, STATE)
            p_full = jnp.dot(inj_ref[0, part], einj_ref[...], preferred_element_type=jnp.float32)
            p_scr[part] = jnp.where(row_g == col_h, p_full, 0.0).astype(jnp.bfloat16)
            row_h = group_of((PART_COLS, PART_STATE), 0, SSM_GROUP)
            col_g = group_of((PART_COLS, PART_STATE), 1, STATE)
            q_t = jnp.dot(ero_ref[...], ro_ref[0, part], preferred_element_type=jnp.float32)
            q_scr[part] = jnp.where(row_h == col_g, q_t, 0.0).T.astype(jnp.bfloat16)

    pad = SCAN_PAD
    re, im = slice(0, half), slice(half, STATE_COLS)
    for buf in (z_scr, w_scr, hp_scr):
        buf[0:pad, :] = jnp.zeros((pad, STATE_COLS), jnp.float32)
    for part in range(N_PARTS):
        z_h = jnp.dot(x_ref[0, :, part * PART_COLS:(part + 1) * PART_COLS], p_scr[part],
                      preferred_element_type=jnp.float32)
        z_scr[pad:pad + n_rows, part * part_re:(part + 1) * part_re] = z_h[:, 0:part_re]
        z_scr[pad:pad + n_rows, half + part * part_re:half + (part + 1) * part_re] = z_h[:, part_re:]

    def coef(i):
        return a_ref[0, 2 * i:2 * i + 1, :], a_ref[0, 2 * i + 1:2 * i + 2, :]

    def doubling_pass(src, dst, shift, c_re, c_im):
        s_re = src[pad - shift:pad - shift + n_rows, re]
        s_im = src[pad - shift:pad - shift + n_rows, im]
        dst[pad:pad + n_rows, re] = src[pad:pad + n_rows, re] + c_re * s_re - c_im * s_im
        dst[pad:pad + n_rows, im] = src[pad:pad + n_rows, im] + c_re * s_im + c_im * s_re

    doubling_pass(z_scr, w_scr, 1, *coef(0))
    doubling_pass(w_scr, z_scr, 2, *coef(1))
    doubling_pass(z_scr, w_scr, 4, *coef(2))
    c8_re, c8_im = coef(3)

    def tile_step(m, h):
        h_re, h_im = h
        rows8 = pl.ds(pl.multiple_of(pad + 8 * m, 8), 8)
        n_re = w_scr[rows8, re] + c8_re * h_re - c8_im * h_im
        n_im = w_scr[rows8, im] + c8_re * h_im + c8_im * h_re
        hp_scr[rows8, re] = n_re
        hp_scr[rows8, im] = n_im
        return n_re, n_im

    zero = jnp.zeros((8, half), jnp.float32)
    lax.fori_loop(0, n_rows // 8, tile_step, (zero, zero), unroll=8)

    for part in range(N_PARTS):
        base = part * PART_COLS
        x_h = x_ref[0, :, base:base + PART_COLS]
        hp_h = jnp.concatenate(
            [hp_scr[pad - 1:pad - 1 + n_rows, part * part_re:(part + 1) * part_re],
             hp_scr[pad - 1:pad - 1 + n_rows, half + part * part_re:half + (part + 1) * part_re]],
            axis=1).astype(jnp.bfloat16)
        for n in range(PART_COLS // MXU_TILE):
            lo, hi = n * MXU_TILE, (n + 1) * MXU_TILE
            y = jnp.dot(x_h[:, 0:hi], m_scr[part, 0:hi, lo:hi], preferred_element_type=jnp.float32)
            y = y + jnp.dot(hp_h, q_scr[part, :, lo:hi], preferred_element_type=jnp.float32)
            y_ref[0, :, base + lo:base + hi] = _gelu_tanh(y).astype(jnp.bfloat16)


def _ssm_call(xc, toep, inj, ro, e_inj, e_ro, a_tab, bsz):
    nb, total_rows, _ = xc.shape
    n_rows = total_rows // bsz
    per_q = lambda q, b: (q, 0, 0)
    const = lambda q, b: (0, 0)
    return pl.pallas_call(
        _ssm_kernel,
        grid=(nb, bsz),
        in_specs=[
            pl.BlockSpec((1, n_rows, CHUNK_COLS), lambda q, b: (q, b, 0)),
            pl.BlockSpec((1, N_PARTS, SSM_CHUNK // STEPS_PER_TILE, LANES, LANES),
                         lambda q, b: (q, 0, 0, 0, 0)),
            pl.BlockSpec((1, N_PARTS, PART_COLS, 2 * STATE), lambda q, b: (q, 0, 0, 0)),
            pl.BlockSpec((1, N_PARTS, SSM_CHUNK * SSM_GROUP, PART_STATE), lambda q, b: (q, 0, 0, 0)),
            pl.BlockSpec(e_inj.shape, const),
            pl.BlockSpec(e_ro.shape, const),
            pl.BlockSpec((1, 2 * len(SCAN_POWERS), STATE_COLS // 2), per_q),
        ],
        out_specs=pl.BlockSpec((1, n_rows, CHUNK_COLS), lambda q, b: (q, b, 0)),
        out_shape=jax.ShapeDtypeStruct(xc.shape, jnp.bfloat16),
        scratch_shapes=[
            pltpu.VMEM((N_PARTS, PART_COLS, PART_COLS), jnp.bfloat16),
            pltpu.VMEM((N_PARTS, PART_COLS, PART_STATE), jnp.bfloat16),
            pltpu.VMEM((N_PARTS, PART_STATE, PART_COLS), jnp.bfloat16),
            pltpu.VMEM((SCAN_PAD + n_rows, STATE_COLS), jnp.float32),
            pltpu.VMEM((SCAN_PAD + n_rows, STATE_COLS), jnp.float32),
            pltpu.VMEM((SCAN_PAD + n_rows, STATE_COLS), jnp.float32),
        ],
        compiler_params=pltpu.CompilerParams(
            dimension_semantics=("arbitrary", "arbitrary"), vmem_limit_bytes=VMEM_LIMIT),
        name="ssm",
    )(xc, toep, inj, ro, e_inj, e_ro, a_tab)


def _merge_mlp_stages(x_ref, ys_ref, ya, gates_ref, wglu_ref, bglu_ref, wus_ref, wua_ref,
                      wout_ref, gm_ref, w1_ref, w2_ref, gf_ref, o_ref, y_scr):
    f32, bf = jnp.float32, jnp.bfloat16
    chunk_rows = ys_ref.shape[1]
    st = {}

    def mix():
        gated_a = gates_ref[:, D_MODEL:2 * D_MODEL].astype(f32) * jnp.dot(
            ya, wua_ref[...], preferred_element_type=f32)
        lane_part = _lane_part((chunk_rows, LANES))
        for m in range(SSM_CHUNK // STEPS_PER_TILE):
            for blk in range(N_LANE_BLOCKS):
                tiles = [ys_ref[blk, :, p * PART_COLS + m * LANES:p * PART_COLS + (m + 1) * LANES]
                         .astype(f32) for p in range(N_PARTS)]
                for a in range(STEPS_PER_TILE):
                    step = _merge_lane_parts(
                        [_roll_lanes(tiles[p], (p - a) * PART_LANES) for p in range(N_PARTS)], lane_part)
                    y_scr[blk, pl.ds(m * STEPS_PER_TILE + a, chunk_rows, stride=SSM_CHUNK), :] = step
        y = jnp.concatenate([y_scr[blk] for blk in range(N_LANE_BLOCKS)], axis=-1)
        glu_arg = jnp.dot(y.astype(bf), wglu_ref[...], preferred_element_type=f32) + bglu_ref[...]
        y_ssm = (y * jax.nn.sigmoid(glu_arg)).astype(bf)
        up_s = jnp.dot(y_ssm, wus_ref[...], preferred_element_type=f32)
        g_s = gates_ref[:, 0:D_MODEL].astype(f32)
        st["merged"] = (g_s * up_s + gated_a).astype(bf)

    def project_out():
        h = x_ref[...] + jnp.dot(st["merged"], wout_ref[...], preferred_element_type=f32)
        st["h"] = h
        st["n"] = _rmsnorm_f32(h, gm_ref[...]).astype(bf)

    def mlp_chunk(c):
        cols = slice(c * FF_CHUNK, (c + 1) * FF_CHUNK)
        hid = jnp.maximum(jnp.dot(st["n"], w1_ref[:, cols], preferred_element_type=f32), 0.0)
        st["h"] = st["h"] + jnp.dot((hid * hid).astype(bf), w2_ref[cols, :],
                                    preferred_element_type=f32)

    def finish():
        o_ref[...] = _rmsnorm_f32(st["h"], gf_ref[...])

    chunks = [functools.partial(mlp_chunk, c) for c in range(D_FF // FF_CHUNK)]
    return [mix, project_out] + chunks + [finish]


def _attn_kernel(q_ref, kc_ref, kp_ref, vc_ref, vp_ref, k_hbm, v_hbm, tri_ref, o_ref,
                 qm_scr, carry_scr, acc_scr, k_buf, v_buf, dma_sem):
    blk = ATT_BLOCK
    n_qb = q_ref.shape[1] // blk
    n_pairs = N_HEADS // 2
    head_rows = N_HEADS * blk
    f32, bf = jnp.float32, jnp.bfloat16
    batch = pl.program_id(0)
    first_qb = pl.program_id(1) * n_qb

    lane = lax.broadcasted_iota(jnp.int32, (blk, LANES), 1)
    first_head = lane < HEAD_DIM
    tri_near = tri_ref[...]
    tri = tri_ref[blk:2 * blk, blk:2 * blk + LANES]

    for c in range(n_qb):
        for pair in range(n_pairs):
            q_pair = q_ref[0, c * blk:(c + 1) * blk, pair * LANES:(pair + 1) * LANES]
            zero = jnp.zeros_like(q_pair)
            lo = c * head_rows + 2 * pair * blk
            qm_scr[lo:lo + blk] = jnp.where(first_head, q_pair, zero)
            qm_scr[lo + blk:lo + 2 * blk] = jnp.where(first_head, zero, q_pair)

    def scores(c):
        base = c * head_rows
        return jnp.concatenate(
            [lax.dot_general(qm_scr[base + 2 * p * blk:base + (2 * p + 2) * blk],
                             k_buf[:, p * LANES:(p + 1) * LANES],
                             NT_DIMS, preferred_element_type=f32) for p in range(n_pairs)], axis=0)

    def softplus2(z):
        return jnp.maximum(z, 0.0) + jnp.log2(1.0 + jnp.exp2(-jnp.abs(z)))

    def stick(z, later):
        sp = softplus2(z)
        cs = jnp.dot(sp.astype(bf), tri, preferred_element_type=f32)
        log_w = z - sp - cs[:, 0:blk]
        if later is not None:
            log_w = log_w - later
        return jnp.exp2(log_w).astype(bf), cs[:, blk:blk + LANES]

    def weighted_values(w, pair):
        o2 = jnp.dot(w, v_buf[:, pair * LANES:(pair + 1) * LANES],
                     preferred_element_type=f32)
        return jnp.where(first_head, o2[0:blk], o2[blk:2 * blk])

    row = lax.broadcasted_iota(jnp.int32, (blk, blk), 0)
    col = lax.broadcasted_iota(jnp.int32, (blk, blk), 1)
    pen_diag = jnp.where(col < row, 0.0, ATT_MASKED)

    def window(cur_ref, prev_ref, c, lanes):
        before = prev_ref[0, :, lanes] if c == 0 else cur_ref[0, (c - 1) * blk:c * blk, lanes]
        return jnp.concatenate([before, cur_ref[0, c * blk:(c + 1) * blk, lanes]], axis=0)

    def near_scores(c):
        base = c * head_rows
        return jnp.concatenate(
            [lax.dot_general(
                qm_scr[base + 2 * p * blk:base + (2 * p + 2) * blk],
                window(kc_ref, kp_ref, c, slice(p * LANES, (p + 1) * LANES)),
                NT_DIMS, preferred_element_type=f32) for p in range(n_pairs)], axis=0)

    def masked_near_scores(c):
        z = near_scores(c)
        z_prev = z[:, 0:blk]
        if c == 0:
            z_prev = z_prev - jnp.where(first_qb >= 1, 0.0, ATT_MASKED)
        z_diag = (z[:, blk:2 * blk].reshape(N_HEADS, blk, blk) - pen_diag[None]).reshape(head_rows, blk)
        return jnp.concatenate([z_prev, z_diag], axis=1)

    all_qbs = range(n_qb)
    least = [None] * n_qb

    def phase_scores(qbs):
        return jnp.concatenate([masked_near_scores(c) for c in qbs], axis=0)

    def phase_softplus(z):
        return softplus2(z)

    def phase_suffix(sp):
        return jnp.dot(sp.astype(bf), tri_near[:, 0:2 * blk], preferred_element_type=f32)

    def phase_weights(qbs, z, sp, cs):
        total = cs[:, 0:LANES] + sp[:, 0:LANES]
        carry_scr[qbs[0] * head_rows:(qbs[-1] + 1) * head_rows] = total
        for k, c in enumerate(qbs):
            least[c] = jnp.min(total[k * head_rows:(k + 1) * head_rows], axis=0, keepdims=True)
        return jnp.exp2(z - sp - cs).astype(bf)

    def phase_values(qbs, w):
        for k, c in enumerate(qbs):
            for pair in range(n_pairs):
                rows_w = slice(k * head_rows + 2 * pair * blk, k * head_rows + (2 * pair + 2) * blk)
                lanes = slice(pair * LANES, (pair + 1) * LANES)
                o2 = jnp.dot(w[rows_w], window(vc_ref, vp_ref, c, lanes),
                             preferred_element_type=f32)
                acc_scr[c * blk:(c + 1) * blk, lanes] = jnp.where(first_head, o2[0:blk], o2[blk:2 * blk])

    z_near = phase_scores(all_qbs)
    sp = phase_softplus(z_near)
    cs = phase_suffix(sp)
    phase_values(all_qbs, phase_weights(all_qbs, z_near, sp, cs))

    skip_at = ATT_SKIP_SUM * LOG2_E
    least_of_step = functools.reduce(jnp.minimum, least)[0, 0]

    def earlier_blocks(c):
        rows_c = slice(c * blk, (c + 1) * blk)
        base = c * head_rows

        def body(state):
            j, _ = state
            key_rows = pl.ds(pl.multiple_of(j * blk, blk), blk)
            k_copy = pltpu.make_async_copy(k_hbm.at[batch, key_rows], k_buf, dma_sem.at[0])
            v_copy = pltpu.make_async_copy(v_hbm.at[batch, key_rows], v_buf, dma_sem.at[1])
            k_copy.start()
            v_copy.start()
            k_copy.wait()
            v_copy.wait()
            carry = jnp.broadcast_to(carry_scr[base:base + head_rows, 0:1], (head_rows, LANES))
            w, sum_j = stick(scores(c), carry)
            carry = carry + sum_j
            carry_scr[base:base + head_rows] = carry
            for pair in range(n_pairs):
                acc_scr[rows_c, pair * LANES:(pair + 1) * LANES] += weighted_values(
                    w[2 * pair * blk:(2 * pair + 2) * blk], pair)
            return j - 1, jnp.min(carry)

        def cond(state):
            j, least_c = state
            return jnp.logical_and(j >= 0, least_c < skip_at)

        lax.while_loop(cond, body, (first_qb + (c - 2), least[c][0, 0]))

    @pl.when(least_of_step < skip_at)
    def _():
        for c in range(n_qb):
            earlier_blocks(c)

    o_ref[0] = acc_scr[...].astype(o_ref.dtype)


def _attn_call(q3, k3, v3, tri):
    bsz, seq, _ = q3.shape
    blk = ATT_BLOCK
    rows = ATT_ROWS
    n_qb = rows // blk
    tile = lambda b, i: (b, i, 0)
    block_before = lambda b, i: (b, jnp.maximum(i * n_qb - 1, 0), 0)
    return pl.pallas_call(
        _attn_kernel,
        grid=(bsz, seq // rows),
        in_specs=[
            pl.BlockSpec((1, rows, D_ATTN), tile),
            pl.BlockSpec((1, rows, D_ATTN), tile),
            pl.BlockSpec((1, blk, D_ATTN), block_before),
            pl.BlockSpec((1, rows, D_ATTN), tile),
            pl.BlockSpec((1, blk, D_ATTN), block_before),
            pl.BlockSpec(memory_space=pl.ANY),
            pl.BlockSpec(memory_space=pl.ANY),
            pl.BlockSpec(tri.shape, lambda b, i: (0, 0)),
        ],
        out_specs=pl.BlockSpec((1, rows, D_ATTN), tile),
        out_shape=jax.ShapeDtypeStruct((bsz, seq, D_ATTN), jnp.bfloat16),
        scratch_shapes=[
            pltpu.VMEM((n_qb * N_HEADS * blk, LANES), jnp.bfloat16),
            pltpu.VMEM((n_qb * N_HEADS * blk, LANES), jnp.float32),
            pltpu.VMEM((rows, D_ATTN), jnp.float32),
            pltpu.VMEM((blk, D_ATTN), jnp.bfloat16),
            pltpu.VMEM((blk, D_ATTN), jnp.bfloat16),
            pltpu.SemaphoreType.DMA((2,)),
        ],
        compiler_params=pltpu.CompilerParams(
            dimension_semantics=("arbitrary", "arbitrary"), vmem_limit_bytes=VMEM_LIMIT),
        name="attn",
    )(q3, k3, k3, v3, v3, k3, v3, tri)


def _merge_kernel(x_ref, ys_ref, ya_ref, gates_ref, wglu_ref, bglu_ref, wus_ref, wua_ref,
                  wout_ref, gm_ref, w1_ref, w2_ref, gf_ref, o_ref, y_scr):
    for stage in _merge_mlp_stages(x_ref, ys_ref, ya_ref[...], gates_ref, wglu_ref, bglu_ref,
                                   wus_ref, wua_ref, wout_ref, gm_ref, w1_ref, w2_ref, gf_ref,
                                   o_ref, y_scr):
        stage()


def _merge_call(x2, ys, ya, gates, wglu, bglu, wus, wua, wout, gm, w1, w2, gf):
    tokens = x2.shape[0]
    rows = MERGE_ROWS
    const = lambda i: (0, 0)

    def resident(arr):
        return pl.BlockSpec(arr.shape, const, pipeline_mode=pl.Buffered(1))

    return pl.pallas_call(
        _merge_kernel,
        grid=(tokens // rows,),
        in_specs=[
            pl.BlockSpec((rows, D_MODEL), lambda i: (i, 0)),
            pl.BlockSpec((N_LANE_BLOCKS, rows // SSM_CHUNK, CHUNK_COLS), lambda i: (0, i, 0)),
            pl.BlockSpec((rows, D_ATTN), lambda i: (i, 0)),
            pl.BlockSpec((rows, 2 * D_MODEL), lambda i: (i, 0)),
            resident(wglu), resident(bglu), resident(wus), resident(wua), resident(wout),
            resident(gm), resident(w1), resident(w2), resident(gf),
        ],
        out_specs=pl.BlockSpec((rows, D_MODEL), lambda i: (i, 0)),
        out_shape=jax.ShapeDtypeStruct((tokens, D_MODEL), jnp.float32),
        scratch_shapes=[pltpu.VMEM((N_LANE_BLOCKS, rows, LANES), jnp.float32)],
        compiler_params=pltpu.CompilerParams(
            dimension_semantics=("arbitrary",), vmem_limit_bytes=VMEM_LIMIT),
        name="merge_mlp",
    )(x2, ys, ya, gates, wglu, bglu, wus, wua, wout, gm, w1, w2, gf)


def _suffix_sum_matrix():
    win = 2 * ATT_BLOCK
    r = jnp.arange(win)[:, None]
    c = jnp.arange(win + LANES)[None, :]
    return jnp.where((c >= win) | (r > c), 1.0, 0.0).astype(jnp.bfloat16)


def kernel(x, norm_mix, w_in, A_re, A_im, log_dt, B_re, B_im, C_re, C_im, D_skip, w_glu, b_glu,
           w_up_ssm, w_up_attn, w_gate, b_gate, w_out, norm_mlp, w_ff1, w_ff2, norm_final):
    bsz, seq, _ = x.shape
    tokens = bsz * seq
    bf = jnp.bfloat16
    assert norm_mix.shape[0] == 1, "single layer"
    assert seq % (SSM_CHUNK * 8) == 0 and seq % ATT_BLOCK == 0
    assert tokens % PROJ_ROWS == 0 and tokens % MERGE_ROWS == 0 and seq % min(PROJ_ROWS, seq) == 0

    x2 = x.reshape(tokens, D_MODEL)
    later_weights = [w_glu[0], w_up_ssm[0], w_up_attn[0], w_out[0], w_ff1[0], w_ff2[0]]
    ussm, q, k, v, gates, wglu_b, wus_b, wua_b, wout_b, w1_b, w2_b = _inproj_call(
        x2, norm_mix, w_in[0].astype(bf), w_gate[0].astype(bf), b_gate, later_weights, bsz, seq)

    toep, inj, ro, a_tab = _ssm_tables(
        A_re[0], A_im[0], log_dt[0], B_re[0], B_im[0], C_re[0], C_im[0], D_skip[0])
    e_inj, e_ro = _expansion_matrices()
    ys = _ssm_call(ussm, toep, inj, ro, e_inj, e_ro, a_tab, bsz)

    ya = _attn_call(q.reshape(bsz, seq, D_ATTN), k.reshape(bsz, seq, D_ATTN),
                    v.reshape(bsz, seq, D_ATTN), _suffix_sum_matrix()).reshape(tokens, D_ATTN)

    out = _merge_call(x2, ys, ya, gates, wglu_b, b_glu, wus_b, wua_b, wout_b, norm_mlp,
                      w1_b, w2_b, norm_final.reshape(1, D_MODEL))
    return out.reshape(bsz, seq, D_MODEL)
```

```python
import functools

import jax
import jax.numpy as jnp
from jax import lax
from jax.experimental import pallas as pl
from jax.experimental.pallas import tpu as pltpu

D_MODEL = 1024
D_SSM = 512
SSM_GROUP = 16
STATE = 64
N_HEADS = 8
HEAD_DIM = 64
D_ATTN = 512
D_FF = 4096
EPS = 1e-6

LANES = 128
MXU_TILE = 256
VMEM_LIMIT = 52 * 1024 * 1024

SSM_CHUNK = 16
N_LANE_BLOCKS = D_SSM // LANES
GROUPS_PER_BLOCK = LANES // SSM_GROUP
STATE_COLS = 2 * GROUPS_PER_BLOCK * STATE
CHUNK_COLS = SSM_CHUNK * LANES
N_PARTS = 4
PART_GROUPS = GROUPS_PER_BLOCK // N_PARTS
PART_LANES = LANES // N_PARTS
STEPS_PER_TILE = LANES // PART_LANES
PART_COLS = SSM_CHUNK * PART_LANES
PART_STATE = STATE_COLS // N_PARTS
SCAN_POWERS = (1, 2, 4, 8)
SCAN_PAD = 8
ATT_BLOCK = 128
ATT_SKIP_SUM = 64.0
ATT_MASKED = 1e30
LOG2_E = 1.4426950408889634
NT_DIMS = (((1,), (1,)), ((), ()))
GELU_C1 = 0.7978845608028654
GELU_C3 = GELU_C1 * 0.044715
PROJ_ROWS = 1024
MERGE_ROWS = 512
FF_CHUNK = 1024
ATT_ROWS = 1024


def _rmsnorm_f32(x, g):
    ms = jnp.mean(x * x, axis=-1, keepdims=True)
    return x * lax.rsqrt(ms + EPS) * g


def _lane_part(shape):
    lanes = lax.broadcasted_iota(jnp.int32, shape, len(shape) - 1)
    return lax.shift_right_logical(lanes, PART_LANES.bit_length() - 1)


def _roll_lanes(x, shift):
    shift %= LANES
    return x if shift == 0 else pltpu.roll(x, shift, x.ndim - 1)


def _merge_lane_parts(sources, lane_part):
    out = sources[0]
    for k in range(1, len(sources)):
        out = jnp.where(lane_part == k, sources[k], out)
    return out


def _inproj_kernel(x_ref, g_ref, w_ref, wg_ref, bg_ref, *rest):
    n_cast = (len(rest) - 6) // 2
    cast_srcs = rest[:n_cast]
    ussm_ref, q_ref, k_ref, v_ref, gates_ref = rest[n_cast:n_cast + 5]
    cast_dsts = rest[n_cast + 5:2 * n_cast + 5]
    pssm_scr = rest[-1]
    for src, dst in zip(cast_srcs, cast_dsts):
        dst[...] = src[...].astype(jnp.bfloat16)

    x = x_ref[...]
    u = _rmsnorm_f32(x, g_ref[...]).astype(jnp.bfloat16)
    rows = x.shape[0]
    chunk_rows = rows // SSM_CHUNK

    for c in range(2 * D_MODEL // 512):
        pg = jnp.dot(u, wg_ref[:, c * 512:(c + 1) * 512], preferred_element_type=jnp.float32)
        pg = pg + bg_ref[:, c * 512:(c + 1) * 512]
        gates_ref[:, c * 512:(c + 1) * 512] = jax.nn.sigmoid(pg).astype(jnp.bfloat16)

    p_ssm = jnp.dot(u, w_ref[:, 0:D_SSM], preferred_element_type=jnp.float32)
    for blk in range(N_LANE_BLOCKS):
        pssm_scr[blk] = p_ssm[:, blk * LANES:(blk + 1) * LANES]
    lane_part = _lane_part((chunk_rows, LANES))
    for m in range(SSM_CHUNK // STEPS_PER_TILE):
        for blk in range(N_LANE_BLOCKS):
            steps = [pssm_scr[blk, pl.ds(m * STEPS_PER_TILE + a, chunk_rows, stride=SSM_CHUNK), :]
                     for a in range(STEPS_PER_TILE)]
            for p in range(N_PARTS):
                tile = _merge_lane_parts(
                    [_roll_lanes(steps[a], (a - p) * PART_LANES) for a in range(STEPS_PER_TILE)],
                    lane_part)
                ussm_ref[blk, :, p * PART_COLS + m * LANES:p * PART_COLS + (m + 1) * LANES] = (
                    tile.astype(jnp.bfloat16))

    p_q = jnp.dot(u, w_ref[:, D_SSM:D_SSM + D_ATTN], preferred_element_type=jnp.float32)
    q_ref[...] = (p_q * (HEAD_DIM ** -0.5 * LOG2_E)).astype(jnp.bfloat16)

    p_k = jnp.dot(u, w_ref[:, D_SSM + D_ATTN:D_SSM + 2 * D_ATTN],
                  preferred_element_type=jnp.float32)
    k_ref[...] = p_k.astype(jnp.bfloat16)

    p_v = jnp.dot(u, w_ref[:, D_SSM + 2 * D_ATTN:D_SSM + 3 * D_ATTN],
                  preferred_element_type=jnp.float32)
    v_ref[...] = p_v.astype(jnp.bfloat16)


def _inproj_call(x2, g, w_in, w_gate, bg, later_weights, bsz, seq):
    tokens = bsz * seq
    rows = min(PROJ_ROWS, seq)
    n_steps = tokens // rows
    const = lambda i: (0, 0)
    row_slice = lambda i: (i, 0)
    slices = []
    for w in later_weights:
        slice_rows = w.shape[0] // n_steps
        assert slice_rows * n_steps == w.shape[0] and slice_rows % 16 == 0, w.shape
        slices.append((slice_rows, w.shape[1]))
    return pl.pallas_call(
        _inproj_kernel,
        grid=(n_steps,),
        in_specs=[
            pl.BlockSpec((rows, D_MODEL), lambda i: (i, 0)),
            pl.BlockSpec((1, D_MODEL), const),
            pl.BlockSpec(w_in.shape, const),
            pl.BlockSpec(w_gate.shape, const),
            pl.BlockSpec((1, 2 * D_MODEL), const),
        ] + [pl.BlockSpec(blk, row_slice) for blk in slices],
        out_specs=[
            pl.BlockSpec((N_LANE_BLOCKS, rows // SSM_CHUNK, CHUNK_COLS), lambda i: (0, i, 0)),
            pl.BlockSpec((rows, D_ATTN), lambda i: (i, 0)),
            pl.BlockSpec((rows, D_ATTN), lambda i: (i, 0)),
            pl.BlockSpec((rows, D_ATTN), lambda i: (i, 0)),
            pl.BlockSpec((rows, 2 * D_MODEL), lambda i: (i, 0)),
        ] + [pl.BlockSpec(blk, row_slice) for blk in slices],
        out_shape=[
            jax.ShapeDtypeStruct((N_LANE_BLOCKS, tokens // SSM_CHUNK, CHUNK_COLS), jnp.bfloat16),
            jax.ShapeDtypeStruct((tokens, D_ATTN), jnp.bfloat16),
            jax.ShapeDtypeStruct((tokens, D_ATTN), jnp.bfloat16),
            jax.ShapeDtypeStruct((tokens, D_ATTN), jnp.bfloat16),
            jax.ShapeDtypeStruct((tokens, 2 * D_MODEL), jnp.bfloat16),
        ] + [jax.ShapeDtypeStruct(w.shape, jnp.bfloat16) for w in later_weights],
        scratch_shapes=[pltpu.VMEM((N_LANE_BLOCKS, rows, LANES), jnp.float32)],
        compiler_params=pltpu.CompilerParams(
            dimension_semantics=("arbitrary",), vmem_limit_bytes=VMEM_LIMIT),
        name="inproj",
    )(x2, g, w_in, w_gate, bg, *later_weights)


def _ssm_tables(A_re, A_im, log_dt, B_re, B_im, C_re, C_im, D_skip):
    f32 = jnp.float32
    L = SSM_CHUNK
    nb, gb = N_LANE_BLOCKS, GROUPS_PER_BLOCK
    ar, ai = A_re.astype(f32), A_im.astype(f32)
    dt = jnp.exp(log_dt.astype(f32))[:, None]
    tau = jnp.arange(L + 1, dtype=f32)[:, None, None]
    mag = jnp.exp(ar[None] * dt[None] * tau)
    ang = ai[None] * dt[None] * tau
    pw_re, pw_im = mag * jnp.cos(ang), mag * jnp.sin(ang)
    num_re, num_im = pw_re[1] - 1.0, pw_im[1]
    den = ar * ar + ai * ai
    cf_re = (num_re * ar + num_im * ai) / den
    cf_im = (num_im * ar - num_re * ai) / den
    br, bi = B_re.astype(f32), B_im.astype(f32)
    bb_re = cf_re[..., None] * br - cf_im[..., None] * bi
    bb_im = cf_re[..., None] * bi + cf_im[..., None] * br
    ab_re = pw_re[:L, :, :, None] * bb_re[None] - pw_im[:L, :, :, None] * bb_im[None]
    ab_im = pw_re[:L, :, :, None] * bb_im[None] + pw_im[:L, :, :, None] * bb_re[None]
    cr, ci = C_re.astype(f32), C_im.astype(f32)

    kern = jnp.einsum('gdp,tgpc->tgcd', cr, ab_re) - jnp.einsum('gdp,tgpc->tgcd', ci, ab_im)
    kern = kern.reshape(L, nb, LANES, SSM_GROUP).transpose(1, 0, 2, 3)
    lane_idx = jnp.arange(LANES)
    spread = (jnp.arange(SSM_GROUP)[:, None] == lane_idx[None, :] % SSM_GROUP).astype(f32)
    same_group = lane_idx[:, None] // SSM_GROUP == lane_idx[None, :] // SSM_GROUP
    toep = jnp.where(same_group, jnp.matmul(kern, spread), 0.0)
    d_diag = D_skip.astype(f32).reshape(nb, LANES)[:, :, None] * jnp.eye(LANES, dtype=f32)
    toep = toep.at[:, 0].add(d_diag)

    pw_cat = jnp.concatenate([pw_re[:L], pw_im[:L]], axis=-1)[::-1]
    pw_swp = jnp.concatenate([pw_im[:L], pw_re[:L]], axis=-1)[::-1]
    bbt_re, bbt_im = bb_re.transpose(0, 2, 1), bb_im.transpose(0, 2, 1)
    b_same = jnp.concatenate([bbt_re, bbt_re], axis=-1)
    b_cross = jnp.concatenate([-bbt_im, bbt_im], axis=-1)
    inj = pw_cat[:, :, None, :] * b_same[None] + pw_swp[:, :, None, :] * b_cross[None]
    inj = inj.reshape(L, nb, N_PARTS, PART_LANES, 2 * STATE).transpose(1, 2, 0, 3, 4)
    inj = inj.reshape(nb, N_PARTS, PART_COLS, 2 * STATE)

    n_gp = PART_STATE // 2
    crt = cr.transpose(1, 0, 2).reshape(SSM_GROUP, nb, N_PARTS, n_gp)
    cit = ci.transpose(1, 0, 2).reshape(SSM_GROUP, nb, N_PARTS, n_gp)
    pwr = pw_re[1:L + 1].reshape(L, 1, nb, N_PARTS, n_gp)
    pwi = pw_im[1:L + 1].reshape(L, 1, nb, N_PARTS, n_gp)
    ro = jnp.concatenate([crt[None] * pwr - cit[None] * pwi,
                          -(crt[None] * pwi + cit[None] * pwr)], axis=-1)
    ro = ro.transpose(2, 3, 0, 1, 4).reshape(nb, N_PARTS, L * SSM_GROUP, PART_STATE)

    n = STEPS_PER_TILE
    t_part = jnp.stack([toep[:, :, p * PART_LANES:(p + 1) * PART_LANES,
                             p * PART_LANES:(p + 1) * PART_LANES] for p in range(N_PARTS)], axis=1)
    t_pad = jnp.pad(t_part, ((0, 0), (0, 0), (n - 1, 0), (0, 0), (0, 0)))
    n_dt = L // n
    toep = jnp.concatenate(
        [jnp.concatenate([t_pad[:, :, (b - a + n - 1)::n][:, :, :n_dt] for b in range(n)], axis=-1)
         for a in range(n)], axis=-2)

    steps = (L * jnp.array(SCAN_POWERS, f32))[:, None, None]
    sc_mag = jnp.exp(ar[None] * dt[None] * steps)
    sc_ang = ai[None] * dt[None] * steps
    a_tab = jnp.stack([sc_mag * jnp.cos(sc_ang), sc_mag * jnp.sin(sc_ang)], axis=1)
    a_tab = a_tab.reshape(2 * len(SCAN_POWERS), nb, gb * STATE).transpose(1, 0, 2)
    bf = jnp.bfloat16
    return toep.astype(bf), inj.astype(bf), ro.astype(bf), a_tab


def _expansion_matrices():
    gh = PART_GROUPS
    src = jnp.arange(2 * STATE)[:, None]
    dst = jnp.arange(PART_STATE)[None, :]
    e_inj = (src // STATE == dst // (gh * STATE)) & (src % STATE == dst % STATE)
    src = jnp.arange(SSM_CHUNK * SSM_GROUP)[None, :]
    dst = jnp.arange(PART_COLS)[:, None]
    e_ro = (src // SSM_GROUP == dst // PART_LANES) & (src % SSM_GROUP == dst % SSM_GROUP)
    return e_inj.astype(jnp.bfloat16), e_ro.astype(jnp.bfloat16)


def _gelu_tanh(y):
    half_y = 0.5 * y
    return half_y + half_y * jnp.tanh(y * (GELU_C1 + GELU_C3 * (y * y)))


def _ssm_kernel(x_ref, toep_ref, inj_ref, ro_ref, einj_ref, ero_ref, a_ref, y_ref,
                m_scr, p_scr, q_scr, z_scr, w_scr, hp_scr):
    gh = PART_GROUPS
    half = STATE_COLS // 2
    part_re = PART_STATE // 2
    n_tiles = SSM_CHUNK // STEPS_PER_TILE
    n_rows = x_ref.shape[1]

    @pl.when(pl.program_id(1) == 0)
    def _():
        def group_of(shape, axis, width):
            idx = lax.broadcasted_iota(jnp.int32, shape, axis)
            return lax.shift_right_logical(idx, width.bit_length() - 1) & (gh - 1)

        m_scr[...] = jnp.zeros_like(m_scr)
        for part in range(N_PARTS):
            for sp in range(n_tiles):
                for tp in range(sp, n_tiles):
                    m_scr[part, sp * LANES:(sp + 1) * LANES, tp * LANES:(tp + 1) * LANES] = (
                        toep_ref[0, part, tp - sp])
            row_g = group_of((PART_COLS, PART_STATE), 0, SSM_GROUP)
            col_h = group_of((PART_COLS, PART_STATE), 1, STATE)
            p_full = jnp.dot(inj_ref[0, part], einj_ref[...], preferred_element_type=jnp.float32)
            p_scr[part] = jnp.where(row_g == col_h, p_full, 0.0).astype(jnp.bfloat16)
            row_h = group_of((PART_COLS, PART_STATE), 0, SSM_GROUP)
            col_g = group_of((PART_COLS, PART_STATE), 1, STATE)
            q_t = jnp.dot(ero_ref[...], ro_ref[0, part], preferred_element_type=jnp.float32)
            q_scr[part] = jnp.where(row_h == col_g, q_t, 0.0).T.astype(jnp.bfloat16)

    pad = SCAN_PAD
    re, im = slice(0, half), slice(half, STATE_COLS)
    for buf in (z_scr, w_scr, hp_scr):
        buf[0:pad, :] = jnp.zeros((pad, STATE_COLS), jnp.float32)
    for part in range(N_PARTS):
        z_h = jnp.dot(x_ref[0, :, part * PART_COLS:(part + 1) * PART_COLS], p_scr[part],
                      preferred_element_type=jnp.float32)
        z_scr[pad:pad + n_rows, part * part_re:(part + 1) * part_re] = z_h[:, 0:part_re]
        z_scr[pad:pad + n_rows, half + part * part_re:half + (part + 1) * part_re] = z_h[:, part_re:]

    def coef(i):
        return a_ref[0, 2 * i:2 * i + 1, :], a_ref[0, 2 * i + 1:2 * i + 2, :]

    def doubling_pass(src, dst, shift, c_re, c_im):
        s_re = src[pad - shift:pad - shift + n_rows, re]
        s_im = src[pad - shift:pad - shift + n_rows, im]
        dst[pad:pad + n_rows, re] = src[pad:pad + n_rows, re] + c_re * s_re - c_im * s_im
        dst[pad:pad + n_rows, im] = src[pad:pad + n_rows, im] + c_re * s_im + c_im * s_re

    doubling_pass(z_scr, w_scr, 1, *coef(0))
    doubling_pass(w_scr, z_scr, 2, *coef(1))
    doubling_pass(z_scr, w_scr, 4, *coef(2))
    c8_re, c8_im = coef(3)

    def tile_step(m, h):
        h_re, h_im = h
        rows8 = pl.ds(pl.multiple_of(pad + 8 * m, 8), 8)
        n_re = w_scr[rows8, re] + c8_re * h_re - c8_im * h_im
        n_im = w_scr[rows8, im] + c8_re * h_im + c8_im * h_re
        hp_scr[rows8, re] = n_re
        hp_scr[rows8, im] = n_im
        return n_re, n_im

    zero = jnp.zeros((8, half), jnp.float32)
    lax.fori_loop(0, n_rows // 8, tile_step, (zero, zero), unroll=8)

    for part in range(N_PARTS):
        base = part * PART_COLS
        x_h = x_ref[0, :, base:base + PART_COLS]
        hp_h = jnp.concatenate(
            [hp_scr[pad - 1:pad - 1 + n_rows, part * part_re:(part + 1) * part_re],
             hp_scr[pad - 1:pad - 1 + n_rows, half + part * part_re:half + (part + 1) * part_re]],
            axis=1).astype(jnp.bfloat16)
        for n in range(PART_COLS // MXU_TILE):
            lo, hi = n * MXU_TILE, (n + 1) * MXU_TILE
            y = jnp.dot(x_h[:, 0:hi], m_scr[part, 0:hi, lo:hi], preferred_element_type=jnp.float32)
            y = y + jnp.dot(hp_h, q_scr[part, :, lo:hi], preferred_element_type=jnp.float32)
            y_ref[0, :, base + lo:base + hi] = _gelu_tanh(y).astype(jnp.bfloat16)


def _ssm_call(xc, toep, inj, ro, e_inj, e_ro, a_tab, bsz):
    nb, total_rows, _ = xc.shape
    n_rows = total_rows // bsz
    per_q = lambda q, b: (q, 0, 0)
    const = lambda q, b: (0, 0)
    return pl.pallas_call(
        _ssm_kernel,
        grid=(nb, bsz),
        in_specs=[
            pl.BlockSpec((1, n_rows, CHUNK_COLS), lambda q, b: (q, b, 0)),
            pl.BlockSpec((1, N_PARTS, SSM_CHUNK // STEPS_PER_TILE, LANES, LANES),
                         lambda q, b: (q, 0, 0, 0, 0)),
            pl.BlockSpec((1, N_PARTS, PART_COLS, 2 * STATE), lambda q, b: (q, 0, 0, 0)),
            pl.BlockSpec((1, N_PARTS, SSM_CHUNK * SSM_GROUP, PART_STATE), lambda q, b: (q, 0, 0, 0)),
            pl.BlockSpec(e_inj.shape, const),
            pl.BlockSpec(e_ro.shape, const),
            pl.BlockSpec((1, 2 * len(SCAN_POWERS), STATE_COLS // 2), per_q),
        ],
        out_specs=pl.BlockSpec((1, n_rows, CHUNK_COLS), lambda q, b: (q, b, 0)),
        out_shape=jax.ShapeDtypeStruct(xc.shape, jnp.bfloat16),
        scratch_shapes=[
            pltpu.VMEM((N_PARTS, PART_COLS, PART_COLS), jnp.bfloat16),
            pltpu.VMEM((N_PARTS, PART_COLS, PART_STATE), jnp.bfloat16),
            pltpu.VMEM((N_PARTS, PART_STATE, PART_COLS), jnp.bfloat16),
            pltpu.VMEM((SCAN_PAD + n_rows, STATE_COLS), jnp.float32),
            pltpu.VMEM((SCAN_PAD + n_rows, STATE_COLS), jnp.float32),
            pltpu.VMEM((SCAN_PAD + n_rows, STATE_COLS), jnp.float32),
        ],
        compiler_params=pltpu.CompilerParams(
            dimension_semantics=("arbitrary", "arbitrary"), vmem_limit_bytes=VMEM_LIMIT),
        name="ssm",
    )(xc, toep, inj, ro, e_inj, e_ro, a_tab)


def _merge_mlp_stages(x_ref, ys_ref, ya, gates_ref, wglu_ref, bglu_ref, wus_ref, wua_ref,
                      wout_ref, gm_ref, w1_ref, w2_ref, gf_ref, o_ref, y_scr):
    f32, bf = jnp.float32, jnp.bfloat16
    chunk_rows = ys_ref.shape[1]
    st = {}

    def mix():
        gated_a = gates_ref[:, D_MODEL:2 * D_MODEL].astype(f32) * jnp.dot(
            ya, wua_ref[...], preferred_element_type=f32)
        lane_part = _lane_part((chunk_rows, LANES))
        for m in range(SSM_CHUNK // STEPS_PER_TILE):
            for blk in range(N_LANE_BLOCKS):
                tiles = [ys_ref[blk, :, p * PART_COLS + m * LANES:p * PART_COLS + (m + 1) * LANES]
                         .astype(f32) for p in range(N_PARTS)]
                for a in range(STEPS_PER_TILE):
                    step = _merge_lane_parts(
                        [_roll_lanes(tiles[p], (p - a) * PART_LANES) for p in range(N_PARTS)], lane_part)
                    y_scr[blk, pl.ds(m * STEPS_PER_TILE + a, chunk_rows, stride=SSM_CHUNK), :] = step
        y = jnp.concatenate([y_scr[blk] for blk in range(N_LANE_BLOCKS)], axis=-1)
        n_rows = y.shape[0]
        halves = [slice(0, n_rows // 2), slice(n_rows // 2, n_rows)]
        glu_arg = [jnp.dot(y[h].astype(bf), wglu_ref[...], preferred_element_type=f32) + bglu_ref[...]
                   for h in halves]
        y_ssm = [(y[h] * jax.nn.sigmoid(a)).astype(bf) for h, a in zip(halves, glu_arg)]
        up_s = [jnp.dot(v, wus_ref[...], preferred_element_type=f32) for v in y_ssm]
        st["merged"] = jnp.concatenate(
            [(gates_ref[h, 0:D_MODEL].astype(f32) * u + gated_a[h]).astype(bf)
             for h, u in zip(halves, up_s)], axis=0)

    def project_out():
        h = x_ref[...] + jnp.dot(st["merged"], wout_ref[...], preferred_element_type=f32)
        st["h"] = h
        st["n"] = _rmsnorm_f32(h, gm_ref[...]).astype(bf)

    def mlp_chunk(c):
        cols = slice(c * FF_CHUNK, (c + 1) * FF_CHUNK)
        hid = jnp.maximum(jnp.dot(st["n"], w1_ref[:, cols], preferred_element_type=f32), 0.0)
        st["h"] = st["h"] + jnp.dot((hid * hid).astype(bf), w2_ref[cols, :],
                                    preferred_element_type=f32)

    def finish():
        o_ref[...] = _rmsnorm_f32(st["h"], gf_ref[...])

    chunks = [functools.partial(mlp_chunk, c) for c in range(D_FF // FF_CHUNK)]
    return [mix, project_out] + chunks + [finish]


def _attn_kernel(q_ref, kc_ref, kp_ref, vc_ref, vp_ref, k_hbm, v_hbm, tri_ref, o_ref,
                 qm_scr, carry_scr, acc_scr, k_buf, v_buf, dma_sem):
    blk = ATT_BLOCK
    n_qb = q_ref.shape[1] // blk
    n_pairs = N_HEADS // 2
    head_rows = N_HEADS * blk
    f32, bf = jnp.float32, jnp.bfloat16
    batch = pl.program_id(0)
    first_qb = pl.program_id(1) * n_qb

    lane = lax.broadcasted_iota(jnp.int32, (blk, LANES), 1)
    first_head = lane < HEAD_DIM
    tri_near = tri_ref[...]
    tri = tri_ref[blk:2 * blk, blk:2 * blk + LANES]

    for c in range(n_qb):
        for pair in range(n_pairs):
            q_pair = q_ref[0, c * blk:(c + 1) * blk, pair * LANES:(pair + 1) * LANES]
            zero = jnp.zeros_like(q_pair)
            lo = c * head_rows + 2 * pair * blk
            qm_scr[lo:lo + blk] = jnp.where(first_head, q_pair, zero)
            qm_scr[lo + blk:lo + 2 * blk] = jnp.where(first_head, zero, q_pair)

    def scores(c):
        base = c * head_rows
        return jnp.concatenate(
            [lax.dot_general(qm_scr[base + 2 * p * blk:base + (2 * p + 2) * blk],
                             k_buf[:, p * LANES:(p + 1) * LANES],
                             NT_DIMS, preferred_element_type=f32) for p in range(n_pairs)], axis=0)

    def softplus2(z):
        return jnp.maximum(z, 0.0) + jnp.log2(1.0 + jnp.exp2(-jnp.abs(z)))

    def stick(z, later):
        sp = softplus2(z)
        cs = jnp.dot(sp.astype(bf), tri, preferred_element_type=f32)
        log_w = z - sp - cs[:, 0:blk]
        if later is not None:
            log_w = log_w - later
        return jnp.exp2(log_w).astype(bf), cs[:, blk:blk + LANES]

    def weighted_values(w, pair):
        o2 = jnp.dot(w, v_buf[:, pair * LANES:(pair + 1) * LANES],
                     preferred_element_type=f32)
        return jnp.where(first_head, o2[0:blk], o2[blk:2 * blk])

    row = lax.broadcasted_iota(jnp.int32, (blk, blk), 0)
    col = lax.broadcasted_iota(jnp.int32, (blk, blk), 1)
    pen_diag = jnp.where(col < row, 0.0, ATT_MASKED)

    def window(cur_ref, prev_ref, c, lanes):
        before = prev_ref[0, :, lanes] if c == 0 else cur_ref[0, (c - 1) * blk:c * blk, lanes]
        return jnp.concatenate([before, cur_ref[0, c * blk:(c + 1) * blk, lanes]], axis=0)

    def near_scores(c):
        base = c * head_rows
        return jnp.concatenate(
            [lax.dot_general(
                qm_scr[base + 2 * p * blk:base + (2 * p + 2) * blk],
                window(kc_ref, kp_ref, c, slice(p * LANES, (p + 1) * LANES)),
                NT_DIMS, preferred_element_type=f32) for p in range(n_pairs)], axis=0)

    def masked_near_scores(c):
        z = near_scores(c)
        z_prev = z[:, 0:blk]
        if c == 0:
            z_prev = z_prev - jnp.where(first_qb >= 1, 0.0, ATT_MASKED)
        z_diag = (z[:, blk:2 * blk].reshape(N_HEADS, blk, blk) - pen_diag[None]).reshape(head_rows, blk)
        return jnp.concatenate([z_prev, z_diag], axis=1)

    all_qbs = range(n_qb)
    least = [None] * n_qb

    def phase_scores(qbs):
        return jnp.concatenate([masked_near_scores(c) for c in qbs], axis=0)

    def phase_softplus(z):
        return softplus2(z)

    def phase_suffix(sp):
        return jnp.dot(sp.astype(bf), tri_near[:, 0:2 * blk], preferred_element_type=f32)

    def phase_weights(qbs, z, sp, cs):
        total = cs[:, 0:LANES] + sp[:, 0:LANES]
        carry_scr[qbs[0] * head_rows:(qbs[-1] + 1) * head_rows] = total
        for k, c in enumerate(qbs):
            least[c] = jnp.min(total[k * head_rows:(k + 1) * head_rows], axis=0, keepdims=True)
        return jnp.exp2(z - sp - cs).astype(bf)

    def phase_values(qbs, w):
        for k, c in enumerate(qbs):
            for pair in range(n_pairs):
                rows_w = slice(k * head_rows + 2 * pair * blk, k * head_rows + (2 * pair + 2) * blk)
                lanes = slice(pair * LANES, (pair + 1) * LANES)
                o2 = jnp.dot(w[rows_w], window(vc_ref, vp_ref, c, lanes),
                             preferred_element_type=f32)
                acc_scr[c * blk:(c + 1) * blk, lanes] = jnp.where(first_head, o2[0:blk], o2[blk:2 * blk])

    z_near = phase_scores(all_qbs)
    sp = phase_softplus(z_near)
    cs = phase_suffix(sp)
    phase_values(all_qbs, phase_weights(all_qbs, z_near, sp, cs))

    skip_at = ATT_SKIP_SUM * LOG2_E
    least_of_step = functools.reduce(jnp.minimum, least)[0, 0]

    def earlier_blocks(c):
        rows_c = slice(c * blk, (c + 1) * blk)
        base = c * head_rows

        def body(state):
            j, _ = state
            key_rows = pl.ds(pl.multiple_of(j * blk, blk), blk)
            k_copy = pltpu.make_async_copy(k_hbm.at[batch, key_rows], k_buf, dma_sem.at[0])
            v_copy = pltpu.make_async_copy(v_hbm.at[batch, key_rows], v_buf, dma_sem.at[1])
            k_copy.start()
            v_copy.start()
            k_copy.wait()
            v_copy.wait()
            carry = jnp.broadcast_to(carry_scr[base:base + head_rows, 0:1], (head_rows, LANES))
            w, sum_j = stick(scores(c), carry)
            carry = carry + sum_j
            carry_scr[base:base + head_rows] = carry
            for pair in range(n_pairs):
                acc_scr[rows_c, pair * LANES:(pair + 1) * LANES] += weighted_values(
                    w[2 * pair * blk:(2 * pair + 2) * blk], pair)
            return j - 1, jnp.min(carry)

        def cond(state):
            j, least_c = state
            return jnp.logical_and(j >= 0, least_c < skip_at)

        lax.while_loop(cond, body, (first_qb + (c - 2), least[c][0, 0]))

    @pl.when(least_of_step < skip_at)
    def _():
        for c in range(n_qb):
            earlier_blocks(c)

    o_ref[0] = acc_scr[...].astype(o_ref.dtype)


def _attn_call(q3, k3, v3, tri):
    bsz, seq, _ = q3.shape
    blk = ATT_BLOCK
    rows = ATT_ROWS
    n_qb = rows // blk
    tile = lambda b, i: (b, i, 0)
    block_before = lambda b, i: (b, jnp.maximum(i * n_qb - 1, 0), 0)
    return pl.pallas_call(
        _attn_kernel,
        grid=(bsz, seq // rows),
        in_specs=[
            pl.BlockSpec((1, rows, D_ATTN), tile),
            pl.BlockSpec((1, rows, D_ATTN), tile),
            pl.BlockSpec((1, blk, D_ATTN), block_before),
            pl.BlockSpec((1, rows, D_ATTN), tile),
            pl.BlockSpec((1, blk, D_ATTN), block_before),
            pl.BlockSpec(memory_space=pl.ANY),
            pl.BlockSpec(memory_space=pl.ANY),
            pl.BlockSpec(tri.shape, lambda b, i: (0, 0)),
        ],
        out_specs=pl.BlockSpec((1, rows, D_ATTN), tile),
        out_shape=jax.ShapeDtypeStruct((bsz, seq, D_ATTN), jnp.bfloat16),
        scratch_shapes=[
            pltpu.VMEM((n_qb * N_HEADS * blk, LANES), jnp.bfloat16),
            pltpu.VMEM((n_qb * N_HEADS * blk, LANES), jnp.float32),
            pltpu.VMEM((rows, D_ATTN), jnp.float32),
            pltpu.VMEM((blk, D_ATTN), jnp.bfloat16),
            pltpu.VMEM((blk, D_ATTN), jnp.bfloat16),
            pltpu.SemaphoreType.DMA((2,)),
        ],
        compiler_params=pltpu.CompilerParams(
            dimension_semantics=("arbitrary", "arbitrary"), vmem_limit_bytes=VMEM_LIMIT),
        name="attn",
    )(q3, k3, k3, v3, v3, k3, v3, tri)


def _merge_kernel(x_ref, ys_ref, ya_ref, gates_ref, wglu_ref, bglu_ref, wus_ref, wua_ref,
                  wout_ref, gm_ref, w1_ref, w2_ref, gf_ref, o_ref, y_scr):
    for stage in _merge_mlp_stages(x_ref, ys_ref, ya_ref[...], gates_ref, wglu_ref, bglu_ref,
                                   wus_ref, wua_ref, wout_ref, gm_ref, w1_ref, w2_ref, gf_ref,
                                   o_ref, y_scr):
        stage()


def _merge_call(x2, ys, ya, gates, wglu, bglu, wus, wua, wout, gm, w1, w2, gf):
    tokens = x2.shape[0]
    rows = MERGE_ROWS
    const = lambda i: (0, 0)

    def resident(arr):
        return pl.BlockSpec(arr.shape, const, pipeline_mode=pl.Buffered(1))

    return pl.pallas_call(
        _merge_kernel,
        grid=(tokens // rows,),
        in_specs=[
            pl.BlockSpec((rows, D_MODEL), lambda i: (i, 0)),
            pl.BlockSpec((N_LANE_BLOCKS, rows // SSM_CHUNK, CHUNK_COLS), lambda i: (0, i, 0)),
            pl.BlockSpec((rows, D_ATTN), lambda i: (i, 0)),
            pl.BlockSpec((rows, 2 * D_MODEL), lambda i: (i, 0)),
            resident(wglu), resident(bglu), resident(wus), resident(wua), resident(wout),
            resident(gm), resident(w1), resident(w2), resident(gf),
        ],
        out_specs=pl.BlockSpec((rows, D_MODEL), lambda i: (i, 0)),
        out_shape=jax.ShapeDtypeStruct((tokens, D_MODEL), jnp.float32),
        scratch_shapes=[pltpu.VMEM((N_LANE_BLOCKS, rows, LANES), jnp.float32)],
        compiler_params=pltpu.CompilerParams(
            dimension_semantics=("arbitrary",), vmem_limit_bytes=VMEM_LIMIT),
        name="merge_mlp",
    )(x2, ys, ya, gates, wglu, bglu, wus, wua, wout, gm, w1, w2, gf)


def _suffix_sum_matrix():
    win = 2 * ATT_BLOCK
    r = jnp.arange(win)[:, None]
    c = jnp.arange(win + LANES)[None, :]
    return jnp.where((c >= win) | (r > c), 1.0, 0.0).astype(jnp.bfloat16)


def kernel(x, norm_mix, w_in, A_re, A_im, log_dt, B_re, B_im, C_re, C_im, D_skip, w_glu, b_glu,
           w_up_ssm, w_up_attn, w_gate, b_gate, w_out, norm_mlp, w_ff1, w_ff2, norm_final):
    bsz, seq, _ = x.shape
    tokens = bsz * seq
    bf = jnp.bfloat16
    assert norm_mix.shape[0] == 1, "single layer"
    assert seq % (SSM_CHUNK * 8) == 0 and seq % ATT_BLOCK == 0
    assert tokens % PROJ_ROWS == 0 and tokens % MERGE_ROWS == 0 and seq % min(PROJ_ROWS, seq) == 0

    x2 = x.reshape(tokens, D_MODEL)
    later_weights = [w_glu[0], w_up_ssm[0], w_up_attn[0], w_out[0], w_ff1[0], w_ff2[0]]
    ussm, q, k, v, gates, wglu_b, wus_b, wua_b, wout_b, w1_b, w2_b = _inproj_call(
        x2, norm_mix, w_in[0].astype(bf), w_gate[0].astype(bf), b_gate, later_weights, bsz, seq)

    toep, inj, ro, a_tab = _ssm_tables(
        A_re[0], A_im[0], log_dt[0], B_re[0], B_im[0], C_re[0], C_im[0], D_skip[0])
    e_inj, e_ro = _expansion_matrices()
    ys = _ssm_call(ussm, toep, inj, ro, e_inj, e_ro, a_tab, bsz)

    ya = _attn_call(q.reshape(bsz, seq, D_ATTN), k.reshape(bsz, seq, D_ATTN),
                    v.reshape(bsz, seq, D_ATTN), _suffix_sum_matrix()).reshape(tokens, D_ATTN)

    out = _merge_call(x2, ys, ya, gates, wglu_b, b_glu, wus_b, wua_b, wout_b, norm_mlp,
                      w1_b, w2_b, norm_final.reshape(1, D_MODEL))
    return out.reshape(bsz, seq, D_MODEL)
```

```python
import functools

import jax
import jax.numpy as jnp
from jax import lax
from jax.experimental import pallas as pl
from jax.experimental.pallas import tpu as pltpu

D_MODEL = 1024
D_SSM = 512
SSM_GROUP = 16
STATE = 64
N_HEADS = 8
HEAD_DIM = 64
D_ATTN = 512
D_FF = 4096
EPS = 1e-6

LANES = 128
MXU_TILE = 256
VMEM_LIMIT = 52 * 1024 * 1024

SSM_CHUNK = 16
N_LANE_BLOCKS = D_SSM // LANES
GROUPS_PER_BLOCK = LANES // SSM_GROUP
STATE_COLS = 2 * GROUPS_PER_BLOCK * STATE
CHUNK_COLS = SSM_CHUNK * LANES
N_PARTS = 4
PART_GROUPS = GROUPS_PER_BLOCK // N_PARTS
PART_LANES = LANES // N_PARTS
STEPS_PER_TILE = LANES // PART_LANES
PART_COLS = SSM_CHUNK * PART_LANES
PART_STATE = STATE_COLS // N_PARTS
SCAN_POWERS = (1, 2, 4, 8)
SCAN_PAD = 8
ATT_BLOCK = 128
ATT_SKIP_SUM = 64.0
ATT_MASKED = 1e30
LOG2_E = 1.4426950408889634
NT_DIMS = (((1,), (1,)), ((), ()))
GELU_C1 = 0.7978845608028654
GELU_C3 = GELU_C1 * 0.044715
PROJ_ROWS = 1024
MERGE_ROWS = 512
FF_CHUNK = 1024
ATT_ROWS = 1024


def _rmsnorm_f32(x, g):
    ms = jnp.mean(x * x, axis=-1, keepdims=True)
    return x * lax.rsqrt(ms + EPS) * g


def _lane_part(shape):
    lanes = lax.broadcasted_iota(jnp.int32, shape, len(shape) - 1)
    return lax.shift_right_logical(lanes, PART_LANES.bit_length() - 1)


def _roll_lanes(x, shift):
    shift %= LANES
    return x if shift == 0 else pltpu.roll(x, shift, x.ndim - 1)


def _merge_lane_parts(sources, lane_part):
    out = sources[0]
    for k in range(1, len(sources)):
        out = jnp.where(lane_part == k, sources[k], out)
    return out


def _inproj_kernel(x_ref, g_ref, w_ref, wg_ref, bg_ref, *rest):
    n_cast = (len(rest) - 6) // 2
    cast_srcs = rest[:n_cast]
    ussm_ref, q_ref, k_ref, v_ref, gates_ref = rest[n_cast:n_cast + 5]
    cast_dsts = rest[n_cast + 5:2 * n_cast + 5]
    pssm_scr = rest[-1]
    for src, dst in zip(cast_srcs, cast_dsts):
        dst[...] = src[...].astype(jnp.bfloat16)

    x = x_ref[...]
    u = _rmsnorm_f32(x, g_ref[...]).astype(jnp.bfloat16)
    rows = x.shape[0]
    chunk_rows = rows // SSM_CHUNK

    for c in range(2 * D_MODEL // 512):
        pg = jnp.dot(u, wg_ref[:, c * 512:(c + 1) * 512], preferred_element_type=jnp.float32)
        pg = pg + bg_ref[:, c * 512:(c + 1) * 512]
        gates_ref[:, c * 512:(c + 1) * 512] = jax.nn.sigmoid(pg).astype(jnp.bfloat16)

    p_ssm = jnp.dot(u, w_ref[:, 0:D_SSM], preferred_element_type=jnp.float32)
    for blk in range(N_LANE_BLOCKS):
        pssm_scr[blk] = p_ssm[:, blk * LANES:(blk + 1) * LANES]
    lane_part = _lane_part((chunk_rows, LANES))
    for m in range(SSM_CHUNK // STEPS_PER_TILE):
        for blk in range(N_LANE_BLOCKS):
            steps = [pssm_scr[blk, pl.ds(m * STEPS_PER_TILE + a, chunk_rows, stride=SSM_CHUNK), :]
                     for a in range(STEPS_PER_TILE)]
            for p in range(N_PARTS):
                tile = _merge_lane_parts(
                    [_roll_lanes(steps[a], (a - p) * PART_LANES) for a in range(STEPS_PER_TILE)],
                    lane_part)
                ussm_ref[blk, :, p * PART_COLS + m * LANES:p * PART_COLS + (m + 1) * LANES] = (
                    tile.astype(jnp.bfloat16))

    p_q = jnp.dot(u, w_ref[:, D_SSM:D_SSM + D_ATTN], preferred_element_type=jnp.float32)
    q_ref[...] = (p_q * (HEAD_DIM ** -0.5 * LOG2_E)).astype(jnp.bfloat16)

    p_k = jnp.dot(u, w_ref[:, D_SSM + D_ATTN:D_SSM + 2 * D_ATTN],
                  preferred_element_type=jnp.float32)
    k_ref[...] = p_k.astype(jnp.bfloat16)

    p_v = jnp.dot(u, w_ref[:, D_SSM + 2 * D_ATTN:D_SSM + 3 * D_ATTN],
                  preferred_element_type=jnp.float32)
    v_ref[...] = p_v.astype(jnp.bfloat16)


def _inproj_call(x2, g, w_in, w_gate, bg, later_weights, bsz, seq):
    tokens = bsz * seq
    rows = min(PROJ_ROWS, seq)
    n_steps = tokens // rows
    const = lambda i: (0, 0)
    row_slice = lambda i: (i, 0)
    slices = []
    for w in later_weights:
        slice_rows = w.shape[0] // n_steps
        assert slice_rows * n_steps == w.shape[0] and slice_rows % 16 == 0, w.shape
        slices.append((slice_rows, w.shape[1]))
    return pl.pallas_call(
        _inproj_kernel,
        grid=(n_steps,),
        in_specs=[
            pl.BlockSpec((rows, D_MODEL), lambda i: (i, 0)),
            pl.BlockSpec((1, D_MODEL), const),
            pl.BlockSpec(w_in.shape, const),
            pl.BlockSpec(w_gate.shape, const),
            pl.BlockSpec((1, 2 * D_MODEL), const),
        ] + [pl.BlockSpec(blk, row_slice) for blk in slices],
        out_specs=[
            pl.BlockSpec((N_LANE_BLOCKS, rows // SSM_CHUNK, CHUNK_COLS), lambda i: (0, i, 0)),
            pl.BlockSpec((rows, D_ATTN), lambda i: (i, 0)),
            pl.BlockSpec((rows, D_ATTN), lambda i: (i, 0)),
            pl.BlockSpec((rows, D_ATTN), lambda i: (i, 0)),
            pl.BlockSpec((rows, 2 * D_MODEL), lambda i: (i, 0)),
        ] + [pl.BlockSpec(blk, row_slice) for blk in slices],
        out_shape=[
            jax.ShapeDtypeStruct((N_LANE_BLOCKS, tokens // SSM_CHUNK, CHUNK_COLS), jnp.bfloat16),
            jax.ShapeDtypeStruct((tokens, D_ATTN), jnp.bfloat16),
            jax.ShapeDtypeStruct((tokens, D_ATTN), jnp.bfloat16),
            jax.ShapeDtypeStruct((tokens, D_ATTN), jnp.bfloat16),
            jax.ShapeDtypeStruct((tokens, 2 * D_MODEL), jnp.bfloat16),
        ] + [jax.ShapeDtypeStruct(w.shape, jnp.bfloat16) for w in later_weights],
        scratch_shapes=[pltpu.VMEM((N_LANE_BLOCKS, rows, LANES), jnp.float32)],
        compiler_params=pltpu.CompilerParams(
            dimension_semantics=("arbitrary",), vmem_limit_bytes=VMEM_LIMIT),
        name="inproj",
    )(x2, g, w_in, w_gate, bg, *later_weights)


def _ssm_tables(A_re, A_im, log_dt, B_re, B_im, C_re, C_im, D_skip):
    f32 = jnp.float32
    L = SSM_CHUNK
    nb, gb = N_LANE_BLOCKS, GROUPS_PER_BLOCK
    ar, ai = A_re.astype(f32), A_im.astype(f32)
    dt = jnp.exp(log_dt.astype(f32))[:, None]
    tau = jnp.arange(L + 1, dtype=f32)[:, None, None]
    mag = jnp.exp(ar[None] * dt[None] * tau)
    ang = ai[None] * dt[None] * tau
    pw_re, pw_im = mag * jnp.cos(ang), mag * jnp.sin(ang)
    num_re, num_im = pw_re[1] - 1.0, pw_im[1]
    den = ar * ar + ai * ai
    cf_re = (num_re * ar + num_im * ai) / den
    cf_im = (num_im * ar - num_re * ai) / den
    br, bi = B_re.astype(f32), B_im.astype(f32)
    bb_re = cf_re[..., None] * br - cf_im[..., None] * bi
    bb_im = cf_re[..., None] * bi + cf_im[..., None] * br
    ab_re = pw_re[:L, :, :, None] * bb_re[None] - pw_im[:L, :, :, None] * bb_im[None]
    ab_im = pw_re[:L, :, :, None] * bb_im[None] + pw_im[:L, :, :, None] * bb_re[None]
    cr, ci = C_re.astype(f32), C_im.astype(f32)

    kern = jnp.einsum('gdp,tgpc->tgcd', cr, ab_re) - jnp.einsum('gdp,tgpc->tgcd', ci, ab_im)
    kern = kern.reshape(L, nb, LANES, SSM_GROUP).transpose(1, 0, 2, 3)
    lane_idx = jnp.arange(LANES)
    spread = (jnp.arange(SSM_GROUP)[:, None] == lane_idx[None, :] % SSM_GROUP).astype(f32)
    same_group = lane_idx[:, None] // SSM_GROUP == lane_idx[None, :] // SSM_GROUP
    toep = jnp.where(same_group, jnp.matmul(kern, spread), 0.0)
    d_diag = D_skip.astype(f32).reshape(nb, LANES)[:, :, None] * jnp.eye(LANES, dtype=f32)
    toep = toep.at[:, 0].add(d_diag)

    pw_cat = jnp.concatenate([pw_re[:L], pw_im[:L]], axis=-1)[::-1]
    pw_swp = jnp.concatenate([pw_im[:L], pw_re[:L]], axis=-1)[::-1]
    bbt_re, bbt_im = bb_re.transpose(0, 2, 1), bb_im.transpose(0, 2, 1)
    b_same = jnp.concatenate([bbt_re, bbt_re], axis=-1)
    b_cross = jnp.concatenate([-bbt_im, bbt_im], axis=-1)
    inj = pw_cat[:, :, None, :] * b_same[None] + pw_swp[:, :, None, :] * b_cross[None]
    inj = inj.reshape(L, nb, N_PARTS, PART_LANES, 2 * STATE).transpose(1, 2, 0, 3, 4)
    inj = inj.reshape(nb, N_PARTS, PART_COLS, 2 * STATE)

    n_gp = PART_STATE // 2
    crt = cr.transpose(1, 0, 2).reshape(SSM_GROUP, nb, N_PARTS, n_gp)
    cit = ci.transpose(1, 0, 2).reshape(SSM_GROUP, nb, N_PARTS, n_gp)
    pwr = pw_re[1:L + 1].reshape(L, 1, nb, N_PARTS, n_gp)
    pwi = pw_im[1:L + 1].reshape(L, 1, nb, N_PARTS, n_gp)
    ro = jnp.concatenate([crt[None] * pwr - cit[None] * pwi,
                          -(crt[None] * pwi + cit[None] * pwr)], axis=-1)
    ro = ro.transpose(2, 3, 0, 1, 4).reshape(nb, N_PARTS, L * SSM_GROUP, PART_STATE)

    n = STEPS_PER_TILE
    t_part = jnp.stack([toep[:, :, p * PART_LANES:(p + 1) * PART_LANES,
                             p * PART_LANES:(p + 1) * PART_LANES] for p in range(N_PARTS)], axis=1)
    t_pad = jnp.pad(t_part, ((0, 0), (0, 0), (n - 1, 0), (0, 0), (0, 0)))
    n_dt = L // n
    toep = jnp.concatenate(
        [jnp.concatenate([t_pad[:, :, (b - a + n - 1)::n][:, :, :n_dt] for b in range(n)], axis=-1)
         for a in range(n)], axis=-2)

    steps = (L * jnp.array(SCAN_POWERS, f32))[:, None, None]
    sc_mag = jnp.exp(ar[None] * dt[None] * steps)
    sc_ang = ai[None] * dt[None] * steps
    a_tab = jnp.stack([sc_mag * jnp.cos(sc_ang), sc_mag * jnp.sin(sc_ang)], axis=1)
    a_tab = a_tab.reshape(2 * len(SCAN_POWERS), nb, gb * STATE).transpose(1, 0, 2)
    bf = jnp.bfloat16
    return toep.astype(bf), inj.astype(bf), ro.astype(bf), a_tab


def _expansion_matrices():
    gh = PART_GROUPS
    src = jnp.arange(2 * STATE)[:, None]
    dst = jnp.arange(PART_STATE)[None, :]
    e_inj = (src // STATE == dst // (gh * STATE)) & (src % STATE == dst % STATE)
    src = jnp.arange(SSM_CHUNK * SSM_GROUP)[None, :]
    dst = jnp.arange(PART_COLS)[:, None]
    e_ro = (src // SSM_GROUP == dst // PART_LANES) & (src % SSM_GROUP == dst % SSM_GROUP)
    return e_inj.astype(jnp.bfloat16), e_ro.astype(jnp.bfloat16)


def _gelu_tanh(y):
    half_y = 0.5 * y
    return half_y + half_y * jnp.tanh(y * (GELU_C1 + GELU_C3 * (y * y)))


def _ssm_kernel(x_ref, toep_ref, inj_ref, ro_ref, einj_ref, ero_ref, a_ref, y_ref,
                m_scr, p_scr, q_scr, z_scr, w_scr, hp_scr):
    gh = PART_GROUPS
    half = STATE_COLS // 2
    part_re = PART_STATE // 2
    n_tiles = SSM_CHUNK // STEPS_PER_TILE
    n_rows = x_ref.shape[1]

    @pl.when(pl.program_id(1) == 0)
    def _():
        def group_of(shape, axis, width):
            idx = lax.broadcasted_iota(jnp.int32, shape, axis)
            return lax.shift_right_logical(idx, width.bit_length() - 1) & (gh - 1)

        m_scr[...] = jnp.zeros_like(m_scr)
        for part in range(N_PARTS):
            for sp in range(n_tiles):
                for tp in range(sp, n_tiles):
                    m_scr[part, sp * LANES:(sp + 1) * LANES, tp * LANES:(tp + 1) * LANES] = (
                        toep_ref[0, part, tp - sp])
            row_g = group_of((PART_COLS, PART_STATE), 0, SSM_GROUP)
            col_h = group_of((PART_COLS, PART_STATE), 1, STATE)
            p_full = jnp.dot(inj_ref[0, part], einj_ref[...], preferred_element_type=jnp.float32)
            p_scr[part] = jnp.where(row_g == col_h, p_full, 0.0).astype(jnp.bfloat16)
            row_h = group_of((PART_COLS, PART_STATE), 0, SSM_GROUP)
            col_g = group_of((PART_COLS, PART_STATE), 1, STATE)
            q_t = jnp.dot(ero_ref[...], ro_ref[0, part], preferred_element_type=jnp.float32)
            q_scr[part] = jnp.where(row_h == col_g, q_t, 0.0).T.astype(jnp.bfloat16)

    pad = SCAN_PAD
    re, im = slice(0, half), slice(half, STATE_COLS)
    for buf in (z_scr, w_scr, hp_scr):
        buf[0:pad, :] = jnp.zeros((pad, STATE_COLS), jnp.float32)
    for part in range(N_PARTS):
        z_h = jnp.dot(x_ref[0, :, part * PART_COLS:(part + 1) * PART_COLS], p_scr[part],
                      preferred_element_type=jnp.float32)
        z_scr[pad:pad + n_rows, part * part_re:(part + 1) * part_re] = z_h[:, 0:part_re]
        z_scr[pad:pad + n_rows, half + part * part_re:half + (part + 1) * part_re] = z_h[:, part_re:]

    def coef(i):
        return a_ref[0, 2 * i:2 * i + 1, :], a_ref[0, 2 * i + 1:2 * i + 2, :]

    def doubling_pass(src, dst, shift, c_re, c_im):
        s_re = src[pad - shift:pad - shift + n_rows, re]
        s_im = src[pad - shift:pad - shift + n_rows, im]
        dst[pad:pad + n_rows, re] = src[pad:pad + n_rows, re] + c_re * s_re - c_im * s_im
        dst[pad:pad + n_rows, im] = src[pad:pad + n_rows, im] + c_re * s_im + c_im * s_re

    doubling_pass(z_scr, w_scr, 1, *coef(0))
    doubling_pass(w_scr, z_scr, 2, *coef(1))
    doubling_pass(z_scr, w_scr, 4, *coef(2))
    c8_re, c8_im = coef(3)

    def tile_step(m, h):
        h_re, h_im = h
        rows8 = pl.ds(pl.multiple_of(pad + 8 * m, 8), 8)
        n_re = w_scr[rows8, re] + c8_re * h_re - c8_im * h_im
        n_im = w_scr[rows8, im] + c8_re * h_im + c8_im * h_re
        hp_scr[rows8, re] = n_re
        hp_scr[rows8, im] = n_im
        return n_re, n_im

    zero = jnp.zeros((8, half), jnp.float32)
    lax.fori_loop(0, n_rows // 8, tile_step, (zero, zero), unroll=8)

    for part in range(N_PARTS):
        base = part * PART_COLS
        x_h = x_ref[0, :, base:base + PART_COLS]
        hp_h = jnp.concatenate(
            [hp_scr[pad - 1:pad - 1 + n_rows, part * part_re:(part + 1) * part_re],
             hp_scr[pad - 1:pad - 1 + n_rows, half + part * part_re:half + (part + 1) * part_re]],
            axis=1).astype(jnp.bfloat16)
        for n in range(PART_COLS // MXU_TILE):
            lo, hi = n * MXU_TILE, (n + 1) * MXU_TILE
            y = jnp.dot(x_h[:, 0:hi], m_scr[part, 0:hi, lo:hi], preferred_element_type=jnp.float32)
            y = y + jnp.dot(hp_h, q_scr[part, :, lo:hi], preferred_element_type=jnp.float32)
            y_ref[0, :, base + lo:base + hi] = _gelu_tanh(y).astype(jnp.bfloat16)


def _ssm_call(xc, toep, inj, ro, e_inj, e_ro, a_tab, bsz):
    nb, total_rows, _ = xc.shape
    n_rows = total_rows // bsz
    per_q = lambda q, b: (q, 0, 0)
    const = lambda q, b: (0, 0)
    return pl.pallas_call(
        _ssm_kernel,
        grid=(nb, bsz),
        in_specs=[
            pl.BlockSpec((1, n_rows, CHUNK_COLS), lambda q, b: (q, b, 0)),
            pl.BlockSpec((1, N_PARTS, SSM_CHUNK // STEPS_PER_TILE, LANES, LANES),
                         lambda q, b: (q, 0, 0, 0, 0)),
            pl.BlockSpec((1, N_PARTS, PART_COLS, 2 * STATE), lambda q, b: (q, 0, 0, 0)),
            pl.BlockSpec((1, N_PARTS, SSM_CHUNK * SSM_GROUP, PART_STATE), lambda q, b: (q, 0, 0, 0)),
            pl.BlockSpec(e_inj.shape, const),
            pl.BlockSpec(e_ro.shape, const),
            pl.BlockSpec((1, 2 * len(SCAN_POWERS), STATE_COLS // 2), per_q),
        ],
        out_specs=pl.BlockSpec((1, n_rows, CHUNK_COLS), lambda q, b: (q, b, 0)),
        out_shape=jax.ShapeDtypeStruct(xc.shape, jnp.bfloat16),
        scratch_shapes=[
            pltpu.VMEM((N_PARTS, PART_COLS, PART_COLS), jnp.bfloat16),
            pltpu.VMEM((N_PARTS, PART_COLS, PART_STATE), jnp.bfloat16),
            pltpu.VMEM((N_PARTS, PART_STATE, PART_COLS), jnp.bfloat16),
            pltpu.VMEM((SCAN_PAD + n_rows, STATE_COLS), jnp.float32),
            pltpu.VMEM((SCAN_PAD + n_rows, STATE_COLS), jnp.float32),
            pltpu.VMEM((SCAN_PAD + n_rows, STATE_COLS), jnp.float32),
        ],
        compiler_params=pltpu.CompilerParams(
            dimension_semantics=("arbitrary", "arbitrary"), vmem_limit_bytes=VMEM_LIMIT),
        name="ssm",
    )(xc, toep, inj, ro, e_inj, e_ro, a_tab)


def _merge_mlp_stages(x_ref, ys_ref, ya, gates_ref, wglu_ref, bglu_ref, wus_ref, wua_ref,
                      wout_ref, gm_ref, w1_ref, w2_ref, gf_ref, o_ref, y_scr):
    f32, bf = jnp.float32, jnp.bfloat16
    chunk_rows = ys_ref.shape[1]
    st = {}

    def mix():
        gated_a = gates_ref[:, D_MODEL:2 * D_MODEL].astype(f32) * jnp.dot(
            ya, wua_ref[...], preferred_element_type=f32)
        lane_part = _lane_part((chunk_rows, LANES))
        for m in range(SSM_CHUNK // STEPS_PER_TILE):
            for blk in range(N_LANE_BLOCKS):
                tiles = [ys_ref[blk, :, p * PART_COLS + m * LANES:p * PART_COLS + (m + 1) * LANES]
                         .astype(f32) for p in range(N_PARTS)]
                for a in range(STEPS_PER_TILE):
                    step = _merge_lane_parts(
                        [_roll_lanes(tiles[p], (p - a) * PART_LANES) for p in range(N_PARTS)], lane_part)
                    y_scr[blk, pl.ds(m * STEPS_PER_TILE + a, chunk_rows, stride=SSM_CHUNK), :] = step
        y = jnp.concatenate([y_scr[blk] for blk in range(N_LANE_BLOCKS)], axis=-1)
        n_rows = y.shape[0]
        halves = [slice(0, n_rows // 2), slice(n_rows // 2, n_rows)]
        glu_arg = [jnp.dot(y[h].astype(bf), wglu_ref[...], preferred_element_type=f32) + bglu_ref[...]
                   for h in halves]
        y_ssm = [(y[h] * jax.nn.sigmoid(a)).astype(bf) for h, a in zip(halves, glu_arg)]
        up_s = [jnp.dot(v, wus_ref[...], preferred_element_type=f32) for v in y_ssm]
        st["merged"] = jnp.concatenate(
            [(gates_ref[h, 0:D_MODEL].astype(f32) * u + gated_a[h]).astype(bf)
             for h, u in zip(halves, up_s)], axis=0)

    def project_out():
        h = x_ref[...] + jnp.dot(st["merged"], wout_ref[...], preferred_element_type=f32)
        st["h"] = h
        st["n"] = _rmsnorm_f32(h, gm_ref[...]).astype(bf)

    def mlp_chunk(c):
        cols = slice(c * FF_CHUNK, (c + 1) * FF_CHUNK)
        hid = jnp.maximum(jnp.dot(st["n"], w1_ref[:, cols], preferred_element_type=f32), 0.0)
        st["h"] = st["h"] + jnp.dot((hid * hid).astype(bf), w2_ref[cols, :],
                                    preferred_element_type=f32)

    def finish():
        o_ref[...] = _rmsnorm_f32(st["h"], gf_ref[...])

    chunks = [functools.partial(mlp_chunk, c) for c in range(D_FF // FF_CHUNK)]
    return [mix, project_out] + chunks + [finish]


def _attn_kernel(q_ref, kc_ref, kp_ref, vc_ref, vp_ref, k_hbm, v_hbm, tri_ref, o_ref,
                 qm_scr, carry_scr, acc_scr, k_buf, v_buf, dma_sem):
    blk = ATT_BLOCK
    n_qb = q_ref.shape[1] // blk
    n_pairs = N_HEADS // 2
    head_rows = N_HEADS * blk
    f32, bf = jnp.float32, jnp.bfloat16
    batch = pl.program_id(0)
    first_qb = pl.program_id(1) * n_qb

    lane = lax.broadcasted_iota(jnp.int32, (blk, LANES), 1)
    first_head = lane < HEAD_DIM
    tri_near = tri_ref[...]
    tri = tri_ref[blk:2 * blk, blk:2 * blk + LANES]

    for c in range(n_qb):
        for pair in range(n_pairs):
            q_pair = q_ref[0, c * blk:(c + 1) * blk, pair * LANES:(pair + 1) * LANES]
            zero = jnp.zeros_like(q_pair)
            lo = c * head_rows + 2 * pair * blk
            qm_scr[lo:lo + blk] = jnp.where(first_head, q_pair, zero)
            qm_scr[lo + blk:lo + 2 * blk] = jnp.where(first_head, zero, q_pair)

    def scores(c):
        base = c * head_rows
        return jnp.concatenate(
            [lax.dot_general(qm_scr[base + 2 * p * blk:base + (2 * p + 2) * blk],
                             k_buf[:, p * LANES:(p + 1) * LANES],
                             NT_DIMS, preferred_element_type=f32) for p in range(n_pairs)], axis=0)

    def softplus2(z):
        return jnp.maximum(z, 0.0) + jnp.log2(1.0 + jnp.exp2(-jnp.abs(z)))

    def stick(z, later):
        sp = softplus2(z)
        cs = jnp.dot(sp.astype(bf), tri, preferred_element_type=f32)
        log_w = z - cs[:, 0:blk]
        if later is not None:
            log_w = log_w - later
        return jnp.exp2(log_w).astype(bf), cs[:, blk:blk + LANES]

    def weighted_values(w, pair):
        o2 = jnp.dot(w, v_buf[:, pair * LANES:(pair + 1) * LANES],
                     preferred_element_type=f32)
        return jnp.where(first_head, o2[0:blk], o2[blk:2 * blk])

    row = lax.broadcasted_iota(jnp.int32, (blk, blk), 0)
    col = lax.broadcasted_iota(jnp.int32, (blk, blk), 1)
    pen_diag = jnp.where(col < row, 0.0, ATT_MASKED)

    def window(cur_ref, prev_ref, c, lanes):
        before = prev_ref[0, :, lanes] if c == 0 else cur_ref[0, (c - 1) * blk:c * blk, lanes]
        return jnp.concatenate([before, cur_ref[0, c * blk:(c + 1) * blk, lanes]], axis=0)

    def near_scores(c):
        base = c * head_rows
        return jnp.concatenate(
            [lax.dot_general(
                qm_scr[base + 2 * p * blk:base + (2 * p + 2) * blk],
                window(kc_ref, kp_ref, c, slice(p * LANES, (p + 1) * LANES)),
                NT_DIMS, preferred_element_type=f32) for p in range(n_pairs)], axis=0)

    def masked_near_scores(c):
        z = near_scores(c)
        z_prev = z[:, 0:blk]
        if c == 0:
            z_prev = z_prev - jnp.where(first_qb >= 1, 0.0, ATT_MASKED)
        z_diag = (z[:, blk:2 * blk].reshape(N_HEADS, blk, blk) - pen_diag[None]).reshape(head_rows, blk)
        return jnp.concatenate([z_prev, z_diag], axis=1)

    all_qbs = range(n_qb)
    least = [None] * n_qb

    def phase_scores(qbs):
        return jnp.concatenate([masked_near_scores(c) for c in qbs], axis=0)

    def phase_softplus(z):
        return softplus2(z)

    def phase_suffix(sp):
        return jnp.dot(sp.astype(bf), tri_near[:, 0:2 * blk], preferred_element_type=f32)

    def phase_weights(qbs, z, cs):
        total = cs[:, 0:LANES]
        carry_scr[qbs[0] * head_rows:(qbs[-1] + 1) * head_rows] = total
        for k, c in enumerate(qbs):
            least[c] = jnp.min(total[k * head_rows:(k + 1) * head_rows], axis=0, keepdims=True)
        return jnp.exp2(z - cs).astype(bf)

    def phase_values(qbs, w):
        for k, c in enumerate(qbs):
            for pair in range(n_pairs):
                rows_w = slice(k * head_rows + 2 * pair * blk, k * head_rows + (2 * pair + 2) * blk)
                lanes = slice(pair * LANES, (pair + 1) * LANES)
                o2 = jnp.dot(w[rows_w], window(vc_ref, vp_ref, c, lanes),
                             preferred_element_type=f32)
                acc_scr[c * blk:(c + 1) * blk, lanes] = jnp.where(first_head, o2[0:blk], o2[blk:2 * blk])

    z_near = phase_scores(all_qbs)
    sp = phase_softplus(z_near)
    cs = phase_suffix(sp)
    phase_values(all_qbs, phase_weights(all_qbs, z_near, cs))

    skip_at = ATT_SKIP_SUM * LOG2_E
    least_of_step = functools.reduce(jnp.minimum, least)[0, 0]

    def earlier_blocks(c):
        rows_c = slice(c * blk, (c + 1) * blk)
        base = c * head_rows

        def body(state):
            j, _ = state
            key_rows = pl.ds(pl.multiple_of(j * blk, blk), blk)
            k_copy = pltpu.make_async_copy(k_hbm.at[batch, key_rows], k_buf, dma_sem.at[0])
            v_copy = pltpu.make_async_copy(v_hbm.at[batch, key_rows], v_buf, dma_sem.at[1])
            k_copy.start()
            v_copy.start()
            k_copy.wait()
            v_copy.wait()
            carry = jnp.broadcast_to(carry_scr[base:base + head_rows, 0:1], (head_rows, LANES))
            w, sum_j = stick(scores(c), carry)
            carry = carry + sum_j
            carry_scr[base:base + head_rows] = carry
            for pair in range(n_pairs):
                acc_scr[rows_c, pair * LANES:(pair + 1) * LANES] += weighted_values(
                    w[2 * pair * blk:(2 * pair + 2) * blk], pair)
            return j - 1, jnp.min(carry)

        def cond(state):
            j, least_c = state
            return jnp.logical_and(j >= 0, least_c < skip_at)

        lax.while_loop(cond, body, (first_qb + (c - 2), least[c][0, 0]))

    @pl.when(least_of_step < skip_at)
    def _():
        for c in range(n_qb):
            earlier_blocks(c)

    o_ref[0] = acc_scr[...].astype(o_ref.dtype)


def _attn_call(q3, k3, v3, tri):
    bsz, seq, _ = q3.shape
    blk = ATT_BLOCK
    rows = ATT_ROWS
    n_qb = rows // blk
    tile = lambda b, i: (b, i, 0)
    block_before = lambda b, i: (b, jnp.maximum(i * n_qb - 1, 0), 0)
    return pl.pallas_call(
        _attn_kernel,
        grid=(bsz, seq // rows),
        in_specs=[
            pl.BlockSpec((1, rows, D_ATTN), tile),
            pl.BlockSpec((1, rows, D_ATTN), tile),
            pl.BlockSpec((1, blk, D_ATTN), block_before),
            pl.BlockSpec((1, rows, D_ATTN), tile),
            pl.BlockSpec((1, blk, D_ATTN), block_before),
            pl.BlockSpec(memory_space=pl.ANY),
            pl.BlockSpec(memory_space=pl.ANY),
            pl.BlockSpec(tri.shape, lambda b, i: (0, 0)),
        ],
        out_specs=pl.BlockSpec((1, rows, D_ATTN), tile),
        out_shape=jax.ShapeDtypeStruct((bsz, seq, D_ATTN), jnp.bfloat16),
        scratch_shapes=[
            pltpu.VMEM((n_qb * N_HEADS * blk, LANES), jnp.bfloat16),
            pltpu.VMEM((n_qb * N_HEADS * blk, LANES), jnp.float32),
            pltpu.VMEM((rows, D_ATTN), jnp.float32),
            pltpu.VMEM((blk, D_ATTN), jnp.bfloat16),
            pltpu.VMEM((blk, D_ATTN), jnp.bfloat16),
            pltpu.SemaphoreType.DMA((2,)),
        ],
        compiler_params=pltpu.CompilerParams(
            dimension_semantics=("arbitrary", "arbitrary"), vmem_limit_bytes=VMEM_LIMIT),
        name="attn",
    )(q3, k3, k3, v3, v3, k3, v3, tri)


def _merge_kernel(x_ref, ys_ref, ya_ref, gates_ref, wglu_ref, bglu_ref, wus_ref, wua_ref,
                  wout_ref, gm_ref, w1_ref, w2_ref, gf_ref, o_ref, y_scr):
    for stage in _merge_mlp_stages(x_ref, ys_ref, ya_ref[...], gates_ref, wglu_ref, bglu_ref,
                                   wus_ref, wua_ref, wout_ref, gm_ref, w1_ref, w2_ref, gf_ref,
                                   o_ref, y_scr):
        stage()


def _merge_call(x2, ys, ya, gates, wglu, bglu, wus, wua, wout, gm, w1, w2, gf):
    tokens = x2.shape[0]
    rows = MERGE_ROWS
    const = lambda i: (0, 0)

    def resident(arr):
        return pl.BlockSpec(arr.shape, const, pipeline_mode=pl.Buffered(1))

    return pl.pallas_call(
        _merge_kernel,
        grid=(tokens // rows,),
        in_specs=[
            pl.BlockSpec((rows, D_MODEL), lambda i: (i, 0)),
            pl.BlockSpec((N_LANE_BLOCKS, rows // SSM_CHUNK, CHUNK_COLS), lambda i: (0, i, 0)),
            pl.BlockSpec((rows, D_ATTN), lambda i: (i, 0)),
            pl.BlockSpec((rows, 2 * D_MODEL), lambda i: (i, 0)),
            resident(wglu), resident(bglu), resident(wus), resident(wua), resident(wout),
            resident(gm), resident(w1), resident(w2), resident(gf),
        ],
        out_specs=pl.BlockSpec((rows, D_MODEL), lambda i: (i, 0)),
        out_shape=jax.ShapeDtypeStruct((tokens, D_MODEL), jnp.float32),
        scratch_shapes=[pltpu.VMEM((N_LANE_BLOCKS, rows, LANES), jnp.float32)],
        compiler_params=pltpu.CompilerParams(
            dimension_semantics=("arbitrary",), vmem_limit_bytes=VMEM_LIMIT),
        name="merge_mlp",
    )(x2, ys, ya, gates, wglu, bglu, wus, wua, wout, gm, w1, w2, gf)


def _suffix_sum_matrix():
    win = 2 * ATT_BLOCK
    r = jnp.arange(win)[:, None]
    c = jnp.arange(win + LANES)[None, :]
    return jnp.where((c >= win) | (r >= c), 1.0, 0.0).astype(jnp.bfloat16)


def kernel(x, norm_mix, w_in, A_re, A_im, log_dt, B_re, B_im, C_re, C_im, D_skip, w_glu, b_glu,
           w_up_ssm, w_up_attn, w_gate, b_gate, w_out, norm_mlp, w_ff1, w_ff2, norm_final):
    bsz, seq, _ = x.shape
    tokens = bsz * seq
    bf = jnp.bfloat16
    assert norm_mix.shape[0] == 1, "single layer"
    assert seq % (SSM_CHUNK * 8) == 0 and seq % ATT_BLOCK == 0
    assert tokens % PROJ_ROWS == 0 and tokens % MERGE_ROWS == 0 and seq % min(PROJ_ROWS, seq) == 0

    x2 = x.reshape(tokens, D_MODEL)
    later_weights = [w_glu[0], w_up_ssm[0], w_up_attn[0], w_out[0], w_ff1[0], w_ff2[0]]
    ussm, q, k, v, gates, wglu_b, wus_b, wua_b, wout_b, w1_b, w2_b = _inproj_call(
        x2, norm_mix, w_in[0].astype(bf), w_gate[0].astype(bf), b_gate, later_weights, bsz, seq)

    toep, inj, ro, a_tab = _ssm_tables(
        A_re[0], A_im[0], log_dt[0], B_re[0], B_im[0], C_re[0], C_im[0], D_skip[0])
    e_inj, e_ro = _expansion_matrices()
    ys = _ssm_call(ussm, toep, inj, ro, e_inj, e_ro, a_tab, bsz)

    ya = _attn_call(q.reshape(bsz, seq, D_ATTN), k.reshape(bsz, seq, D_ATTN),
                    v.reshape(bsz, seq, D_ATTN), _suffix_sum_matrix()).reshape(tokens, D_ATTN)

    out = _merge_call(x2, ys, ya, gates, wglu_b, b_glu, wus_b, wua_b, wout_b, norm_mlp,
                      w1_b, w2_b, norm_final.reshape(1, D_MODEL))
    return out.reshape(bsz, seq, D_MODEL)
```

```python
import functools

import jax
import jax.numpy as jnp
from jax import lax
from jax.experimental import pallas as pl
from jax.experimental.pallas import tpu as pltpu

D_MODEL = 1024
D_SSM = 512
SSM_GROUP = 16
STATE = 64
N_HEADS = 8
HEAD_DIM = 64
D_ATTN = 512
D_FF = 4096
EPS = 1e-6

LANES = 128
MXU_TILE = 256
VMEM_LIMIT = 52 * 1024 * 1024

SSM_CHUNK = 16
N_LANE_BLOCKS = D_SSM // LANES
GROUPS_PER_BLOCK = LANES // SSM_GROUP
STATE_COLS = 2 * GROUPS_PER_BLOCK * STATE
CHUNK_COLS = SSM_CHUNK * LANES
N_PARTS = 4
PART_GROUPS = GROUPS_PER_BLOCK // N_PARTS
PART_LANES = LANES // N_PARTS
STEPS_PER_TILE = LANES // PART_LANES
PART_COLS = SSM_CHUNK * PART_LANES
PART_STATE = STATE_COLS // N_PARTS
SCAN_POWERS = (1, 2, 4, 8)
SCAN_PAD = 8
ATT_BLOCK = 128
ATT_SKIP_SUM = 64.0
ATT_MASKED = 1e30
LOG2_E = 1.4426950408889634
NT_DIMS = (((1,), (1,)), ((), ()))
GELU_C1 = 0.7978845608028654
GELU_C3 = GELU_C1 * 0.044715
PROJ_ROWS = 1024
MERGE_ROWS = 512
FF_CHUNK = 1024
ATT_ROWS = 1024


def _rmsnorm_f32(x, g):
    ms = jnp.mean(x * x, axis=-1, keepdims=True)
    return x * lax.rsqrt(ms + EPS) * g


def _lane_part(shape):
    lanes = lax.broadcasted_iota(jnp.int32, shape, len(shape) - 1)
    return lax.shift_right_logical(lanes, PART_LANES.bit_length() - 1)


def _roll_lanes(x, shift):
    shift %= LANES
    return x if shift == 0 else pltpu.roll(x, shift, x.ndim - 1)


def _merge_lane_parts(sources, lane_part):
    out = sources[0]
    for k in range(1, len(sources)):
        out = jnp.where(lane_part == k, sources[k], out)
    return out


def _inproj_kernel(x_ref, g_ref, w_ref, wg_ref, bg_ref, *rest):
    n_cast = (len(rest) - 6) // 2
    cast_srcs = rest[:n_cast]
    ussm_ref, q_ref, k_ref, v_ref, gates_ref = rest[n_cast:n_cast + 5]
    cast_dsts = rest[n_cast + 5:2 * n_cast + 5]
    pssm_scr = rest[-1]
    for src, dst in zip(cast_srcs, cast_dsts):
        dst[...] = src[...].astype(jnp.bfloat16)

    x = x_ref[...]
    u = _rmsnorm_f32(x, g_ref[...]).astype(jnp.bfloat16)
    rows = x.shape[0]
    chunk_rows = rows // SSM_CHUNK

    for c in range(2 * D_MODEL // 512):
        pg = jnp.dot(u, wg_ref[:, c * 512:(c + 1) * 512], preferred_element_type=jnp.float32)
        pg = pg + bg_ref[:, c * 512:(c + 1) * 512]
        gates_ref[:, c * 512:(c + 1) * 512] = jax.nn.sigmoid(pg).astype(jnp.bfloat16)

    p_ssm = jnp.dot(u, w_ref[:, 0:D_SSM], preferred_element_type=jnp.float32)
    for blk in range(N_LANE_BLOCKS):
        pssm_scr[blk] = p_ssm[:, blk * LANES:(blk + 1) * LANES]
    lane_part = _lane_part((chunk_rows, LANES))
    for m in range(SSM_CHUNK // STEPS_PER_TILE):
        for blk in range(N_LANE_BLOCKS):
            steps = [pssm_scr[blk, pl.ds(m * STEPS_PER_TILE + a, chunk_rows, stride=SSM_CHUNK), :]
                     for a in range(STEPS_PER_TILE)]
            for p in range(N_PARTS):
                tile = _merge_lane_parts(
                    [_roll_lanes(steps[a], (a - p) * PART_LANES) for a in range(STEPS_PER_TILE)],
                    lane_part)
                ussm_ref[blk, :, p * PART_COLS + m * LANES:p * PART_COLS + (m + 1) * LANES] = (
                    tile.astype(jnp.bfloat16))

    p_q = jnp.dot(u, w_ref[:, D_SSM:D_SSM + D_ATTN], preferred_element_type=jnp.float32)
    q_ref[...] = (p_q * (HEAD_DIM ** -0.5 * LOG2_E)).astype(jnp.bfloat16)

    p_k = jnp.dot(u, w_ref[:, D_SSM + D_ATTN:D_SSM + 2 * D_ATTN],
                  preferred_element_type=jnp.float32)
    k_ref[...] = p_k.astype(jnp.bfloat16)

    p_v = jnp.dot(u, w_ref[:, D_SSM + 2 * D_ATTN:D_SSM + 3 * D_ATTN],
                  preferred_element_type=jnp.float32)
    v_ref[...] = p_v.astype(jnp.bfloat16)


def _inproj_call(x2, g, w_in, w_gate, bg, later_weights, bsz, seq):
    tokens = bsz * seq
    rows = min(PROJ_ROWS, seq)
    n_steps = tokens // rows
    const = lambda i: (0, 0)
    row_slice = lambda i: (i, 0)
    slices = []
    for w in later_weights:
        slice_rows = w.shape[0] // n_steps
        assert slice_rows * n_steps == w.shape[0] and slice_rows % 16 == 0, w.shape
        slices.append((slice_rows, w.shape[1]))
    return pl.pallas_call(
        _inproj_kernel,
        grid=(n_steps,),
        in_specs=[
            pl.BlockSpec((rows, D_MODEL), lambda i: (i, 0)),
            pl.BlockSpec((1, D_MODEL), const),
            pl.BlockSpec(w_in.shape, const),
            pl.BlockSpec(w_gate.shape, const),
            pl.BlockSpec((1, 2 * D_MODEL), const),
        ] + [pl.BlockSpec(blk, row_slice) for blk in slices],
        out_specs=[
            pl.BlockSpec((N_LANE_BLOCKS, rows // SSM_CHUNK, CHUNK_COLS), lambda i: (0, i, 0)),
            pl.BlockSpec((rows, D_ATTN), lambda i: (i, 0)),
            pl.BlockSpec((rows, D_ATTN), lambda i: (i, 0)),
            pl.BlockSpec((rows, D_ATTN), lambda i: (i, 0)),
            pl.BlockSpec((rows, 2 * D_MODEL), lambda i: (i, 0)),
        ] + [pl.BlockSpec(blk, row_slice) for blk in slices],
        out_shape=[
            jax.ShapeDtypeStruct((N_LANE_BLOCKS, tokens // SSM_CHUNK, CHUNK_COLS), jnp.bfloat16),
            jax.ShapeDtypeStruct((tokens, D_ATTN), jnp.bfloat16),
            jax.ShapeDtypeStruct((tokens, D_ATTN), jnp.bfloat16),
            jax.ShapeDtypeStruct((tokens, D_ATTN), jnp.bfloat16),
            jax.ShapeDtypeStruct((tokens, 2 * D_MODEL), jnp.bfloat16),
        ] + [jax.ShapeDtypeStruct(w.shape, jnp.bfloat16) for w in later_weights],
        scratch_shapes=[pltpu.VMEM((N_LANE_BLOCKS, rows, LANES), jnp.float32)],
        compiler_params=pltpu.CompilerParams(
            dimension_semantics=("arbitrary",), vmem_limit_bytes=VMEM_LIMIT),
        name="inproj",
    )(x2, g, w_in, w_gate, bg, *later_weights)


def _ssm_tables(A_re, A_im, log_dt, B_re, B_im, C_re, C_im, D_skip):
    f32 = jnp.float32
    L = SSM_CHUNK
    nb, gb = N_LANE_BLOCKS, GROUPS_PER_BLOCK
    ar, ai = A_re.astype(f32), A_im.astype(f32)
    dt = jnp.exp(log_dt.astype(f32))[:, None]
    tau = jnp.arange(L + 1, dtype=f32)[:, None, None]
    mag = jnp.exp(ar[None] * dt[None] * tau)
    ang = ai[None] * dt[None] * tau
    pw_re, pw_im = mag * jnp.cos(ang), mag * jnp.sin(ang)
    num_re, num_im = pw_re[1] - 1.0, pw_im[1]
    den = ar * ar + ai * ai
    cf_re = (num_re * ar + num_im * ai) / den
    cf_im = (num_im * ar - num_re * ai) / den
    br, bi = B_re.astype(f32), B_im.astype(f32)
    bb_re = cf_re[..., None] * br - cf_im[..., None] * bi
    bb_im = cf_re[..., None] * bi + cf_im[..., None] * br
    ab_re = pw_re[:L, :, :, None] * bb_re[None] - pw_im[:L, :, :, None] * bb_im[None]
    ab_im = pw_re[:L, :, :, None] * bb_im[None] + pw_im[:L, :, :, None] * bb_re[None]
    cr, ci = C_re.astype(f32), C_im.astype(f32)

    kern = jnp.einsum('gdp,tgpc->tgcd', cr, ab_re) - jnp.einsum('gdp,tgpc->tgcd', ci, ab_im)
    kern = kern.reshape(L, nb, LANES, SSM_GROUP).transpose(1, 0, 2, 3)
    lane_idx = jnp.arange(LANES)
    spread = (jnp.arange(SSM_GROUP)[:, None] == lane_idx[None, :] % SSM_GROUP).astype(f32)
    same_group = lane_idx[:, None] // SSM_GROUP == lane_idx[None, :] // SSM_GROUP
    toep = jnp.where(same_group, jnp.matmul(kern, spread), 0.0)
    d_diag = D_skip.astype(f32).reshape(nb, LANES)[:, :, None] * jnp.eye(LANES, dtype=f32)
    toep = toep.at[:, 0].add(d_diag)

    pw_cat = jnp.concatenate([pw_re[:L], pw_im[:L]], axis=-1)[::-1]
    pw_swp = jnp.concatenate([pw_im[:L], pw_re[:L]], axis=-1)[::-1]
    bbt_re, bbt_im = bb_re.transpose(0, 2, 1), bb_im.transpose(0, 2, 1)
    b_same = jnp.concatenate([bbt_re, bbt_re], axis=-1)
    b_cross = jnp.concatenate([-bbt_im, bbt_im], axis=-1)
    inj = pw_cat[:, :, None, :] * b_same[None] + pw_swp[:, :, None, :] * b_cross[None]
    inj = inj.reshape(L, nb, N_PARTS, PART_LANES, 2 * STATE).transpose(1, 2, 0, 3, 4)
    inj = inj.reshape(nb, N_PARTS, PART_COLS, 2 * STATE)

    n_gp = PART_STATE // 2
    crt = cr.transpose(1, 0, 2).reshape(SSM_GROUP, nb, N_PARTS, n_gp)
    cit = ci.transpose(1, 0, 2).reshape(SSM_GROUP, nb, N_PARTS, n_gp)
    pwr = pw_re[1:L + 1].reshape(L, 1, nb, N_PARTS, n_gp)
    pwi = pw_im[1:L + 1].reshape(L, 1, nb, N_PARTS, n_gp)
    ro = jnp.concatenate([crt[None] * pwr - cit[None] * pwi,
                          -(crt[None] * pwi + cit[None] * pwr)], axis=-1)
    ro = ro.transpose(2, 3, 0, 1, 4).reshape(nb, N_PARTS, L * SSM_GROUP, PART_STATE)

    n = STEPS_PER_TILE
    t_part = jnp.stack([toep[:, :, p * PART_LANES:(p + 1) * PART_LANES,
                             p * PART_LANES:(p + 1) * PART_LANES] for p in range(N_PARTS)], axis=1)
    t_pad = jnp.pad(t_part, ((0, 0), (0, 0), (n - 1, 0), (0, 0), (0, 0)))
    n_dt = L // n
    toep = jnp.concatenate(
        [jnp.concatenate([t_pad[:, :, (b - a + n - 1)::n][:, :, :n_dt] for b in range(n)], axis=-1)
         for a in range(n)], axis=-2)

    steps = (L * jnp.array(SCAN_POWERS, f32))[:, None, None]
    sc_mag = jnp.exp(ar[None] * dt[None] * steps)
    sc_ang = ai[None] * dt[None] * steps
    a_tab = jnp.stack([sc_mag * jnp.cos(sc_ang), sc_mag * jnp.sin(sc_ang)], axis=1)
    a_tab = a_tab.reshape(2 * len(SCAN_POWERS), nb, gb * STATE).transpose(1, 0, 2)
    bf = jnp.bfloat16
    return toep.astype(bf), inj.astype(bf), ro.astype(bf), a_tab


def _expansion_matrices():
    gh = PART_GROUPS
    src = jnp.arange(2 * STATE)[:, None]
    dst = jnp.arange(PART_STATE)[None, :]
    e_inj = (src // STATE == dst // (gh * STATE)) & (src % STATE == dst % STATE)
    src = jnp.arange(SSM_CHUNK * SSM_GROUP)[None, :]
    dst = jnp.arange(PART_COLS)[:, None]
    e_ro = (src // SSM_GROUP == dst // PART_LANES) & (src % SSM_GROUP == dst % SSM_GROUP)
    return e_inj.astype(jnp.bfloat16), e_ro.astype(jnp.bfloat16)


def _gelu_tanh(y):
    half_y = 0.5 * y
    return half_y + half_y * jnp.tanh(y * (GELU_C1 + GELU_C3 * (y * y)))


def _ssm_kernel(x_ref, toep_ref, inj_ref, ro_ref, einj_ref, ero_ref, a_ref, y_ref,
                m_scr, p_scr, q_scr, z_scr, w_scr, hp_scr):
    gh = PART_GROUPS
    half = STATE_COLS // 2
    part_re = PART_STATE // 2
    n_tiles = SSM_CHUNK // STEPS_PER_TILE
    n_rows = x_ref.shape[1]

    @pl.when(pl.program_id(1) == 0)
    def _():
        def group_of(shape, axis, width):
            idx = lax.broadcasted_iota(jnp.int32, shape, axis)
            return lax.shift_right_logical(idx, width.bit_length() - 1) & (gh - 1)

        m_scr[...] = jnp.zeros_like(m_scr)
        for part in range(N_PARTS):
            for sp in range(n_tiles):
                for tp in range(sp, n_tiles):
                    m_scr[part, sp * LANES:(sp + 1) * LANES, tp * LANES:(tp + 1) * LANES] = (
                        toep_ref[0, part, tp - sp])
            row_g = group_of((PART_COLS, PART_STATE), 0, SSM_GROUP)
            col_h = group_of((PART_COLS, PART_STATE), 1, STATE)
            p_full = jnp.dot(inj_ref[0, part], einj_ref[...], preferred_element_type=jnp.float32)
            p_scr[part] = jnp.where(row_g == col_h, p_full, 0.0).astype(jnp.bfloat16)
            row_h = group_of((PART_COLS, PART_STATE), 0, SSM_GROUP)
            col_g = group_of((PART_COLS, PART_STATE), 1, STATE)
            q_t = jnp.dot(ero_ref[...], ro_ref[0, part], preferred_element_type=jnp.float32)
            q_scr[part] = jnp.where(row_h == col_g, q_t, 0.0).T.astype(jnp.bfloat16)

    pad = SCAN_PAD
    re, im = slice(0, half), slice(half, STATE_COLS)
    for buf in (z_scr, w_scr, hp_scr):
        buf[0:pad, :] = jnp.zeros((pad, STATE_COLS), jnp.float32)
    for part in range(N_PARTS):
        z_h = jnp.dot(x_ref[0, :, part * PART_COLS:(part + 1) * PART_COLS], p_scr[part],
                      preferred_element_type=jnp.float32)
        z_scr[pad:pad + n_rows, part * part_re:(part + 1) * part_re] = z_h[:, 0:part_re]
        z_scr[pad:pad + n_rows, half + part * part_re:half + (part + 1) * part_re] = z_h[:, part_re:]

    def coef(i):
        return a_ref[0, 2 * i:2 * i + 1, :], a_ref[0, 2 * i + 1:2 * i + 2, :]

    def doubling_pass(src, dst, shift, c_re, c_im):
        s_re = src[pad - shift:pad - shift + n_rows, re]
        s_im = src[pad - shift:pad - shift + n_rows, im]
        dst[pad:pad + n_rows, re] = src[pad:pad + n_rows, re] + c_re * s_re - c_im * s_im
        dst[pad:pad + n_rows, im] = src[pad:pad + n_rows, im] + c_re * s_im + c_im * s_re

    doubling_pass(z_scr, w_scr, 1, *coef(0))
    doubling_pass(w_scr, z_scr, 2, *coef(1))
    doubling_pass(z_scr, w_scr, 4, *coef(2))
    c8_re, c8_im = coef(3)

    def tile_step(m, h):
        h_re, h_im = h
        rows8 = pl.ds(pl.multiple_of(pad + 8 * m, 8), 8)
        n_re = w_scr[rows8, re] + c8_re * h_re - c8_im * h_im
        n_im = w_scr[rows8, im] + c8_re * h_im + c8_im * h_re
        hp_scr[rows8, re] = n_re
        hp_scr[rows8, im] = n_im
        return n_re, n_im

    zero = jnp.zeros((8, half), jnp.float32)
    lax.fori_loop(0, n_rows // 8, tile_step, (zero, zero), unroll=8)

    for part in range(N_PARTS):
        base = part * PART_COLS
        x_h = x_ref[0, :, base:base + PART_COLS]
        hp_h = jnp.concatenate(
            [hp_scr[pad - 1:pad - 1 + n_rows, part * part_re:(part + 1) * part_re],
             hp_scr[pad - 1:pad - 1 + n_rows, half + part * part_re:half + (part + 1) * part_re]],
            axis=1).astype(jnp.bfloat16)
        for n in range(PART_COLS // MXU_TILE):
            lo, hi = n * MXU_TILE, (n + 1) * MXU_TILE
            y = jnp.dot(x_h[:, 0:hi], m_scr[part, 0:hi, lo:hi], preferred_element_type=jnp.float32)
            y = y + jnp.dot(hp_h, q_scr[part, :, lo:hi], preferred_element_type=jnp.float32)
            y_ref[0, :, base + lo:base + hi] = _gelu_tanh(y).astype(jnp.bfloat16)


def _ssm_call(xc, toep, inj, ro, e_inj, e_ro, a_tab, bsz):
    nb, total_rows, _ = xc.shape
    n_rows = total_rows // bsz
    per_q = lambda q, b: (q, 0, 0)
    const = lambda q, b: (0, 0)
    return pl.pallas_call(
        _ssm_kernel,
        grid=(nb, bsz),
        in_specs=[
            pl.BlockSpec((1, n_rows, CHUNK_COLS), lambda q, b: (q, b, 0)),
            pl.BlockSpec((1, N_PARTS, SSM_CHUNK // STEPS_PER_TILE, LANES, LANES),
                         lambda q, b: (q, 0, 0, 0, 0)),
            pl.BlockSpec((1, N_PARTS, PART_COLS, 2 * STATE), lambda q, b: (q, 0, 0, 0)),
            pl.BlockSpec((1, N_PARTS, SSM_CHUNK * SSM_GROUP, PART_STATE), lambda q, b: (q, 0, 0, 0)),
            pl.BlockSpec(e_inj.shape, const),
            pl.BlockSpec(e_ro.shape, const),
            pl.BlockSpec((1, 2 * len(SCAN_POWERS), STATE_COLS // 2), per_q),
        ],
        out_specs=pl.BlockSpec((1, n_rows, CHUNK_COLS), lambda q, b: (q, b, 0)),
        out_shape=jax.ShapeDtypeStruct(xc.shape, jnp.bfloat16),
        scratch_shapes=[
            pltpu.VMEM((N_PARTS, PART_COLS, PART_COLS), jnp.bfloat16),
            pltpu.VMEM((N_PARTS, PART_COLS, PART_STATE), jnp.bfloat16),
            pltpu.VMEM((N_PARTS, PART_STATE, PART_COLS), jnp.bfloat16),
            pltpu.VMEM((SCAN_PAD + n_rows, STATE_COLS), jnp.float32),
            pltpu.VMEM((SCAN_PAD + n_rows, STATE_COLS), jnp.float32),
            pltpu.VMEM((SCAN_PAD + n_rows, STATE_COLS), jnp.float32),
        ],
        compiler_params=pltpu.CompilerParams(
            dimension_semantics=("arbitrary", "arbitrary"), vmem_limit_bytes=VMEM_LIMIT),
        name="ssm",
    )(xc, toep, inj, ro, e_inj, e_ro, a_tab)


def _merge_mlp_stages(x_ref, ys_ref, ya, gates_ref, wglu_ref, bglu_ref, wus_ref, wua_ref,
                      wout_ref, gm_ref, w1_ref, w2_ref, gf_ref, o_ref, y_scr):
    f32, bf = jnp.float32, jnp.bfloat16
    chunk_rows = ys_ref.shape[1]
    st = {}

    def mix():
        gated_a = gates_ref[:, D_MODEL:2 * D_MODEL].astype(f32) * jnp.dot(
            ya, wua_ref[...], preferred_element_type=f32)
        lane_part = _lane_part((chunk_rows, LANES))
        for m in range(SSM_CHUNK // STEPS_PER_TILE):
            for blk in range(N_LANE_BLOCKS):
                tiles = [ys_ref[blk, :, p * PART_COLS + m * LANES:p * PART_COLS + (m + 1) * LANES]
                         .astype(f32) for p in range(N_PARTS)]
                for a in range(STEPS_PER_TILE):
                    step = _merge_lane_parts(
                        [_roll_lanes(tiles[p], (p - a) * PART_LANES) for p in range(N_PARTS)], lane_part)
                    y_scr[blk, pl.ds(m * STEPS_PER_TILE + a, chunk_rows, stride=SSM_CHUNK), :] = step
        y = jnp.concatenate([y_scr[blk] for blk in range(N_LANE_BLOCKS)], axis=-1)
        n_rows = y.shape[0]
        halves = [slice(0, n_rows // 2), slice(n_rows // 2, n_rows)]
        glu_arg = [jnp.dot(y[h].astype(bf), wglu_ref[...], preferred_element_type=f32) + bglu_ref[...]
                   for h in halves]
        y_ssm = [(y[h] * jax.nn.sigmoid(a)).astype(bf) for h, a in zip(halves, glu_arg)]
        up_s = [jnp.dot(v, wus_ref[...], preferred_element_type=f32) for v in y_ssm]
        st["merged"] = jnp.concatenate(
            [(gates_ref[h, 0:D_MODEL].astype(f32) * u + gated_a[h]).astype(bf)
             for h, u in zip(halves, up_s)], axis=0)

    def project_out():
        h = x_ref[...] + jnp.dot(st["merged"], wout_ref[...], preferred_element_type=f32)
        st["h"] = h
        st["n"] = _rmsnorm_f32(h, gm_ref[...]).astype(bf)

    def mlp_chunk(c):
        cols = slice(c * FF_CHUNK, (c + 1) * FF_CHUNK)
        hid = jnp.maximum(jnp.dot(st["n"], w1_ref[:, cols], preferred_element_type=f32), 0.0)
        st["h"] = st["h"] + jnp.dot((hid * hid).astype(bf), w2_ref[cols, :],
                                    preferred_element_type=f32)

    def finish():
        o_ref[...] = _rmsnorm_f32(st["h"], gf_ref[...])

    chunks = [functools.partial(mlp_chunk, c) for c in range(D_FF // FF_CHUNK)]
    return [mix, project_out] + chunks + [finish]


def _attn_kernel(q_ref, kc_ref, kp_ref, vc_ref, vp_ref, k_hbm, v_hbm, tri_ref, o_ref,
                 qm_scr, carry_scr, acc_scr, k_buf, v_buf, dma_sem):
    blk = ATT_BLOCK
    n_qb = q_ref.shape[1] // blk
    n_pairs = N_HEADS // 2
    head_rows = N_HEADS * blk
    f32, bf = jnp.float32, jnp.bfloat16
    batch = pl.program_id(0)
    first_qb = pl.program_id(1) * n_qb

    lane = lax.broadcasted_iota(jnp.int32, (blk, LANES), 1)
    first_head = lane < HEAD_DIM
    tri_near = tri_ref[...]
    tri = tri_ref[blk:2 * blk, blk:2 * blk + LANES]

    for c in range(n_qb):
        for pair in range(n_pairs):
            q_pair = q_ref[0, c * blk:(c + 1) * blk, pair * LANES:(pair + 1) * LANES]
            zero = jnp.zeros_like(q_pair)
            lo = c * head_rows + 2 * pair * blk
            qm_scr[lo:lo + blk] = jnp.where(first_head, q_pair, zero)
            qm_scr[lo + blk:lo + 2 * blk] = jnp.where(first_head, zero, q_pair)

    def scores(c):
        base = c * head_rows
        return jnp.concatenate(
            [lax.dot_general(qm_scr[base + 2 * p * blk:base + (2 * p + 2) * blk],
                             k_buf[:, p * LANES:(p + 1) * LANES],
                             NT_DIMS, preferred_element_type=f32) for p in range(n_pairs)], axis=0)

    def softplus2(z):
        return jnp.where(z > 64.0, z, jnp.log2(1.0 + jnp.exp2(z)))

    def stick(z, later):
        sp = softplus2(z)
        cs = jnp.dot(sp.astype(bf), tri, preferred_element_type=f32)
        log_w = z - cs[:, 0:blk]
        if later is not None:
            log_w = log_w - later
        return jnp.exp2(log_w).astype(bf), cs[:, blk:blk + LANES]

    def weighted_values(w, pair):
        o2 = jnp.dot(w, v_buf[:, pair * LANES:(pair + 1) * LANES],
                     preferred_element_type=f32)
        return jnp.where(first_head, o2[0:blk], o2[blk:2 * blk])

    row = lax.broadcasted_iota(jnp.int32, (blk, blk), 0)
    col = lax.broadcasted_iota(jnp.int32, (blk, blk), 1)
    pen_diag = jnp.where(col < row, 0.0, ATT_MASKED)

    def window(cur_ref, prev_ref, c, lanes):
        before = prev_ref[0, :, lanes] if c == 0 else cur_ref[0, (c - 1) * blk:c * blk, lanes]
        return jnp.concatenate([before, cur_ref[0, c * blk:(c + 1) * blk, lanes]], axis=0)

    def near_scores(c):
        base = c * head_rows
        return jnp.concatenate(
            [lax.dot_general(
                qm_scr[base + 2 * p * blk:base + (2 * p + 2) * blk],
                window(kc_ref, kp_ref, c, slice(p * LANES, (p + 1) * LANES)),
                NT_DIMS, preferred_element_type=f32) for p in range(n_pairs)], axis=0)

    def masked_near_scores(c):
        z = near_scores(c)
        z_prev = z[:, 0:blk]
        if c == 0:
            z_prev = z_prev - jnp.where(first_qb >= 1, 0.0, ATT_MASKED)
        z_diag = (z[:, blk:2 * blk].reshape(N_HEADS, blk, blk) - pen_diag[None]).reshape(head_rows, blk)
        return jnp.concatenate([z_prev, z_diag], axis=1)

    all_qbs = range(n_qb)
    least = [None] * n_qb

    def phase_scores(qbs):
        return jnp.concatenate([masked_near_scores(c) for c in qbs], axis=0)

    def phase_softplus(z):
        return softplus2(z)

    def phase_suffix(sp):
        return jnp.dot(sp.astype(bf), tri_near[:, 0:2 * blk], preferred_element_type=f32)

    def phase_weights(qbs, z, cs):
        total = cs[:, 0:LANES]
        carry_scr[qbs[0] * head_rows:(qbs[-1] + 1) * head_rows] = total
        for k, c in enumerate(qbs):
            least[c] = jnp.min(total[k * head_rows:(k + 1) * head_rows], axis=0, keepdims=True)
        return jnp.exp2(z - cs).astype(bf)

    def phase_values(qbs, w):
        for k, c in enumerate(qbs):
            for pair in range(n_pairs):
                rows_w = slice(k * head_rows + 2 * pair * blk, k * head_rows + (2 * pair + 2) * blk)
                lanes = slice(pair * LANES, (pair + 1) * LANES)
                o2 = jnp.dot(w[rows_w], window(vc_ref, vp_ref, c, lanes),
                             preferred_element_type=f32)
                acc_scr[c * blk:(c + 1) * blk, lanes] = jnp.where(first_head, o2[0:blk], o2[blk:2 * blk])

    z_near = phase_scores(all_qbs)
    sp = phase_softplus(z_near)
    cs = phase_suffix(sp)
    phase_values(all_qbs, phase_weights(all_qbs, z_near, cs))

    skip_at = ATT_SKIP_SUM * LOG2_E
    least_of_step = functools.reduce(jnp.minimum, least)[0, 0]

    def earlier_blocks(c):
        rows_c = slice(c * blk, (c + 1) * blk)
        base = c * head_rows

        def body(state):
            j, _ = state
            key_rows = pl.ds(pl.multiple_of(j * blk, blk), blk)
            k_copy = pltpu.make_async_copy(k_hbm.at[batch, key_rows], k_buf, dma_sem.at[0])
            v_copy = pltpu.make_async_copy(v_hbm.at[batch, key_rows], v_buf, dma_sem.at[1])
            k_copy.start()
            v_copy.start()
            k_copy.wait()
            v_copy.wait()
            carry = jnp.broadcast_to(carry_scr[base:base + head_rows, 0:1], (head_rows, LANES))
            w, sum_j = stick(scores(c), carry)
            carry = carry + sum_j
            carry_scr[base:base + head_rows] = carry
            for pair in range(n_pairs):
                acc_scr[rows_c, pair * LANES:(pair + 1) * LANES] += weighted_values(
                    w[2 * pair * blk:(2 * pair + 2) * blk], pair)
            return j - 1, jnp.min(carry)

        def cond(state):
            j, least_c = state
            return jnp.logical_and(j >= 0, least_c < skip_at)

        lax.while_loop(cond, body, (first_qb + (c - 2), least[c][0, 0]))

    @pl.when(least_of_step < skip_at)
    def _():
        for c in range(n_qb):
            earlier_blocks(c)

    o_ref[0] = acc_scr[...].astype(o_ref.dtype)


def _attn_call(q3, k3, v3, tri):
    bsz, seq, _ = q3.shape
    blk = ATT_BLOCK
    rows = ATT_ROWS
    n_qb = rows // blk
    tile = lambda b, i: (b, i, 0)
    block_before = lambda b, i: (b, jnp.maximum(i * n_qb - 1, 0), 0)
    return pl.pallas_call(
        _attn_kernel,
        grid=(bsz, seq // rows),
        in_specs=[
            pl.BlockSpec((1, rows, D_ATTN), tile),
            pl.BlockSpec((1, rows, D_ATTN), tile),
            pl.BlockSpec((1, blk, D_ATTN), block_before),
            pl.BlockSpec((1, rows, D_ATTN), tile),
            pl.BlockSpec((1, blk, D_ATTN), block_before),
            pl.BlockSpec(memory_space=pl.ANY),
            pl.BlockSpec(memory_space=pl.ANY),
            pl.BlockSpec(tri.shape, lambda b, i: (0, 0)),
        ],
        out_specs=pl.BlockSpec((1, rows, D_ATTN), tile),
        out_shape=jax.ShapeDtypeStruct((bsz, seq, D_ATTN), jnp.bfloat16),
        scratch_shapes=[
            pltpu.VMEM((n_qb * N_HEADS * blk, LANES), jnp.bfloat16),
            pltpu.VMEM((n_qb * N_HEADS * blk, LANES), jnp.float32),
            pltpu.VMEM((rows, D_ATTN), jnp.float32),
            pltpu.VMEM((blk, D_ATTN), jnp.bfloat16),
            pltpu.VMEM((blk, D_ATTN), jnp.bfloat16),
            pltpu.SemaphoreType.DMA((2,)),
        ],
        compiler_params=pltpu.CompilerParams(
            dimension_semantics=("arbitrary", "arbitrary"), vmem_limit_bytes=VMEM_LIMIT),
        name="attn",
    )(q3, k3, k3, v3, v3, k3, v3, tri)


def _merge_kernel(x_ref, ys_ref, ya_ref, gates_ref, wglu_ref, bglu_ref, wus_ref, wua_ref,
                  wout_ref, gm_ref, w1_ref, w2_ref, gf_ref, o_ref, y_scr):
    for stage in _merge_mlp_stages(x_ref, ys_ref, ya_ref[...], gates_ref, wglu_ref, bglu_ref,
                                   wus_ref, wua_ref, wout_ref, gm_ref, w1_ref, w2_ref, gf_ref,
                                   o_ref, y_scr):
        stage()


def _merge_call(x2, ys, ya, gates, wglu, bglu, wus, wua, wout, gm, w1, w2, gf):
    tokens = x2.shape[0]
    rows = MERGE_ROWS
    const = lambda i: (0, 0)

    def resident(arr):
        return pl.BlockSpec(arr.shape, const, pipeline_mode=pl.Buffered(1))

    return pl.pallas_call(
        _merge_kernel,
        grid=(tokens // rows,),
        in_specs=[
            pl.BlockSpec((rows, D_MODEL), lambda i: (i, 0)),
            pl.BlockSpec((N_LANE_BLOCKS, rows // SSM_CHUNK, CHUNK_COLS), lambda i: (0, i, 0)),
            pl.BlockSpec((rows, D_ATTN), lambda i: (i, 0)),
            pl.BlockSpec((rows, 2 * D_MODEL), lambda i: (i, 0)),
            resident(wglu), resident(bglu), resident(wus), resident(wua), resident(wout),
            resident(gm), resident(w1), resident(w2), resident(gf),
        ],
        out_specs=pl.BlockSpec((rows, D_MODEL), lambda i: (i, 0)),
        out_shape=jax.ShapeDtypeStruct((tokens, D_MODEL), jnp.float32),
        scratch_shapes=[pltpu.VMEM((N_LANE_BLOCKS, rows, LANES), jnp.float32)],
        compiler_params=pltpu.CompilerParams(
            dimension_semantics=("arbitrary",), vmem_limit_bytes=VMEM_LIMIT),
        name="merge_mlp",
    )(x2, ys, ya, gates, wglu, bglu, wus, wua, wout, gm, w1, w2, gf)


def _suffix_sum_matrix():
    win = 2 * ATT_BLOCK
    r = jnp.arange(win)[:, None]
    c = jnp.arange(win + LANES)[None, :]
    return jnp.where((c >= win) | (r >= c), 1.0, 0.0).astype(jnp.bfloat16)


def kernel(x, norm_mix, w_in, A_re, A_im, log_dt, B_re, B_im, C_re, C_im, D_skip, w_glu, b_glu,
           w_up_ssm, w_up_attn, w_gate, b_gate, w_out, norm_mlp, w_ff1, w_ff2, norm_final):
    bsz, seq, _ = x.shape
    tokens = bsz * seq
    bf = jnp.bfloat16
    assert norm_mix.shape[0] == 1, "single layer"
    assert seq % (SSM_CHUNK * 8) == 0 and seq % ATT_BLOCK == 0
    assert tokens % PROJ_ROWS == 0 and tokens % MERGE_ROWS == 0 and seq % min(PROJ_ROWS, seq) == 0

    x2 = x.reshape(tokens, D_MODEL)
    later_weights = [w_glu[0], w_up_ssm[0], w_up_attn[0], w_out[0], w_ff1[0], w_ff2[0]]
    ussm, q, k, v, gates, wglu_b, wus_b, wua_b, wout_b, w1_b, w2_b = _inproj_call(
        x2, norm_mix, w_in[0].astype(bf), w_gate[0].astype(bf), b_gate, later_weights, bsz, seq)

    toep, inj, ro, a_tab = _ssm_tables(
        A_re[0], A_im[0], log_dt[0], B_re[0], B_im[0], C_re[0], C_im[0], D_skip[0])
    e_inj, e_ro = _expansion_matrices()
    ys = _ssm_call(ussm, toep, inj, ro, e_inj, e_ro, a_tab, bsz)

    ya = _attn_call(q.reshape(bsz, seq, D_ATTN), k.reshape(bsz, seq, D_ATTN),
                    v.reshape(bsz, seq, D_ATTN), _suffix_sum_matrix()).reshape(tokens, D_ATTN)

    out = _merge_call(x2, ys, ya, gates, wglu_b, b_glu, wus_b, wua_b, wout_b, norm_mlp,
                      w1_b, w2_b, norm_final.reshape(1, D_MODEL))
    return out.reshape(bsz, seq, D_MODEL)
```
